```python
import math
import jax
import jax.numpy as jnp
from jax import lax
import numpy as np

D_MODEL = 1024
BATCH = 8
SEQ = 8192
DEPTH = 4

N_EVEN = (DEPTH + 1) // 2
N_ODD = DEPTH // 2
RMS_EPS = 1e-6

GDN_HEADS = 4
GDN_DK = 128
GDN_DV = 128
GDN_QK_W = GDN_HEADS * GDN_DK
GDN_V_W = GDN_HEADS * GDN_DV
GDN_CONV = 4
GDN_CHUNK = 64

POOL_WINDOWS = (2, 4, 8, 16)
POOL_GROUPS = len(POOL_WINDOWS)
POOL_GROUP_W = 128
POOL_W = POOL_GROUPS * POOL_GROUP_W

EVEN_QKV = 2 * GDN_QK_W + GDN_V_W
EVEN_IN = EVEN_QKV + GDN_V_W + 2 * GDN_HEADS + POOL_W
EVEN_OUT = GDN_V_W + POOL_W

DIL_PATTERNS = ((128, 1), (512, 4), (2048, 16))
ATT_GROUPS = len(DIL_PATTERNS)
ATT_HEADS = 8
ATT_DH = 128
ATT_W = ATT_HEADS * ATT_DH
ODD_IN = ATT_GROUPS * 3 * ATT_W
ATT_BLOCK = 128

D_FF = 2816
FFN_CONV = 3

kernel_name = "hybrid_gdn_pool_dilated_alibi_convffn"


def rms_norm(x, gain):
    xf = x.astype(jnp.float32)
    y = xf * lax.rsqrt(jnp.mean(xf * xf, axis=-1, keepdims=True) + RMS_EPS)
    return (y * gain.astype(jnp.float32)).astype(x.dtype)


def l2_norm(x):
    xf = x.astype(jnp.float32)
    return xf * lax.rsqrt(jnp.sum(xf * xf, axis=-1, keepdims=True) + RMS_EPS)


def causal_dwconv(x, w):
    K, C = w.shape
    return lax.conv_general_dilated(
        x, w[:, None, :].astype(x.dtype), window_strides=(1,), padding=[(K - 1, 0)],
        dimension_numbers=("NWC", "WIO", "NWC"), feature_group_count=C)


def gated_delta_rule(q, k, v, g, beta):
    B, T, H, dk = q.shape
    dv = v.shape[-1]
    C = GDN_CHUNK
    N = T // C
    f32 = jnp.float32

    def chunks(t):
        return t.astype(f32).reshape(B, N, C, H, -1).transpose(0, 3, 1, 2, 4)

    q = chunks(q) * (dk ** -0.5)
    k = chunks(k)
    v = chunks(v)
    g = g.astype(f32).reshape(B, N, C, H).transpose(0, 3, 1, 2)
    beta = beta.astype(f32).reshape(B, N, C, H).transpose(0, 3, 1, 2)
    gc = jnp.cumsum(g, axis=-1)

    idx = jnp.arange(C)
    causal = idx[:, None] >= idx[None, :]
    strict = idx[:, None] > idx[None, :]
    diff = gc[..., :, None] - gc[..., None, :]
    decay = jnp.where(causal, jnp.exp(jnp.where(causal, diff, 0.0)), 0.0)

    kb = k * beta[..., None]
    L = jnp.where(strict, jnp.einsum("bhncd,bhnsd->bhncs", kb, k) * decay, 0.0)
    rhs = jnp.concatenate([v * beta[..., None], kb * jnp.exp(gc)[..., None]], axis=-1)
    uw = lax.linalg.triangular_solve(L, rhs, left_side=True, lower=True,
                                     unit_diagonal=True)
    u, w = uw[..., :dv], uw[..., dv:]

    intra = jnp.where(causal, jnp.einsum("bhncd,bhnsd->bhncs", q, k) * decay, 0.0)
    qg = q * jnp.exp(gc)[..., None]
    kdec = k * jnp.exp(gc[..., -1:] - gc)[..., None]
    glast = jnp.exp(gc[..., -1])

    def step(S, xs):
        qg_i, kdec_i, u_i, w_i, intra_i, gl_i = xs
        v_new = u_i - jnp.einsum("bhck,bhkv->bhcv", w_i, S)
        o_i = (jnp.einsum("bhck,bhkv->bhcv", qg_i, S)
               + jnp.einsum("bhcs,bhsv->bhcv", intra_i, v_new))
        S = S * gl_i[..., None, None] + jnp.einsum("bhck,bhcv->bhkv", kdec_i, v_new)
        return S, o_i

    xs = tuple(jnp.moveaxis(t, 2, 0) for t in (qg, kdec, u, w, intra, glast))
    S0 = jnp.zeros((B, H, dk, dv), f32)
    _, o = lax.scan(step, S0, xs)
    return o.transpose(1, 0, 3, 2, 4).reshape(B, T, H, dv)


def multiscale_pool(p, pool_w, pool_scale):
    B, T, _ = p.shape
    pf = p.astype(jnp.float32).reshape(B, T, POOL_GROUPS, POOL_GROUP_W)
    csum = jnp.cumsum(pf, axis=1)
    t1 = jnp.arange(1, T + 1, dtype=jnp.float32)
    pooled = []
    for gi, win in enumerate(POOL_WINDOWS):
        cg = csum[:, :, gi]
        lag = jnp.pad(cg, ((0, 0), (win, 0), (0, 0)))[:, :T]
        cnt = jnp.minimum(t1, float(win))[None, :, None]
        pooled.append((cg - lag) / cnt)
    pooled = jnp.stack(pooled, axis=2) - pf
    y = jnp.einsum("btgc,gcd->btgd", pooled, pool_w.astype(jnp.float32))
    return (y.reshape(B, T, POOL_W) * pool_scale.astype(jnp.float32)).astype(p.dtype)


def even_mixer(h, w_in, w_out, conv_w, a_log, dt_bias, gdn_norm, pool_w, pool_scale):
    B, T, _ = h.shape
    proj = h @ w_in
    i1 = EVEN_QKV
    i2 = i1 + GDN_V_W
    i3 = i2 + GDN_HEADS
    i4 = i3 + GDN_HEADS
    qkv, z, b_raw, a_raw, pool_in = jnp.split(proj, [i1, i2, i3, i4], axis=-1)

    qkv = jax.nn.silu(causal_dwconv(qkv, conv_w))
    q, k, v = jnp.split(qkv, [GDN_QK_W, 2 * GDN_QK_W], axis=-1)
    q = l2_norm(q.reshape(B, T, GDN_HEADS, GDN_DK))
    k = l2_norm(k.reshape(B, T, GDN_HEADS, GDN_DK))
    v = v.reshape(B, T, GDN_HEADS, GDN_DV)
    beta = jax.nn.sigmoid(b_raw.astype(jnp.float32))
    g = -jnp.exp(a_log.astype(jnp.float32)) * jax.nn.softplus(
        a_raw.astype(jnp.float32) + dt_bias.astype(jnp.float32))
    o = gated_delta_rule(q, k, v, g, beta)
    o = rms_norm(o, gdn_norm) * jax.nn.silu(
        z.astype(jnp.float32).reshape(B, T, GDN_HEADS, GDN_DV))
    o_a = o.reshape(B, T, GDN_V_W).astype(h.dtype)

    o_b = multiscale_pool(pool_in, pool_w, pool_scale)
    return jnp.concatenate([o_a, o_b], axis=-1) @ w_out


def alibi_slopes(n_heads):
    return jnp.exp2(-8.0 * jnp.arange(1, n_heads + 1, dtype=jnp.float32) / n_heads)


def dilated_band_attention(q, k, v, dil, n_back, slopes):
    B, T, H, dh = q.shape
    L = T // dil
    nb = -(-L // ATT_BLOCK)
    Lp = nb * ATT_BLOCK

    def to_sub(t):
        t = t.reshape(B, L, dil, H, dh).transpose(0, 2, 3, 1, 4)
        t = jnp.pad(t, ((0, 0), (0, 0), (0, 0), (0, Lp - L), (0, 0)))
        return t.reshape(B, dil, H, nb, ATT_BLOCK, dh)

    def band(t):
        prev = jnp.pad(t, ((0, 0), (0, 0), (0, 0), (1, 0), (0, 0), (0, 0)))[:, :, :, :nb]
        return jnp.concatenate([prev, t], axis=4)

    qb = to_sub(q)
    kb = band(to_sub(k))
    vb = band(to_sub(v))

    a = jnp.arange(ATT_BLOCK)[:, None]
    j = jnp.arange(2 * ATT_BLOCK)[None, :]
    rel = ATT_BLOCK + a - j
    blk = jnp.arange(nb)[:, None, None]
    mask = (rel >= 0) & (rel <= n_back) & ((j >= ATT_BLOCK) | (blk > 0))
    bias = -(slopes * dil)[:, None, None, None] * rel.astype(jnp.float32)

    s = jnp.einsum("brhnqd,brhnkd->brhnqk", qb, kb) + bias
    s = jnp.where(mask, s, -jnp.inf)
    m = jnp.max(s, axis=-1, keepdims=True)
    p = jnp.exp(s - m)
    l = jnp.sum(p, axis=-1, keepdims=True)
    o = jnp.einsum("brhnqk,brhnkd->brhnqd", p, vb) / l
    lse = m + jnp.log(l)

    def from_sub(t):
        t = t.reshape(B, dil, H, Lp, -1)[:, :, :, :L]
        return t.transpose(0, 3, 1, 2, 4).reshape(B, T, H, -1)

    return from_sub(o), from_sub(lse)[..., 0]


def odd_mixer(h, w_in, w_out, q_norm, k_norm):
    B, T, _ = h.shape
    slopes = alibi_slopes(ATT_HEADS)
    outs, lses = [], []
    for gi, (window, dil) in enumerate(DIL_PATTERNS):
        cols = w_in[:, gi * 3 * ATT_W:(gi + 1) * 3 * ATT_W]
        proj = (h @ cols).astype(jnp.float32).reshape(B, T, 3, ATT_HEADS, ATT_DH)
        q = rms_norm(proj[:, :, 0], q_norm) * (ATT_DH ** -0.5)
        k = rms_norm(proj[:, :, 1], k_norm)
        v = proj[:, :, 2]
        o, lse = dilated_band_attention(q, k, v, dil, window // dil, slopes)
        outs.append(o)
        lses.append(lse)
    wts = jax.nn.softmax(jnp.stack(lses), axis=0)
    o = jnp.sum(wts[..., None] * jnp.stack(outs), axis=0)
    return o.reshape(B, T, ATT_W).astype(h.dtype) @ w_out


def conv_ffn(h, w_up, conv_w, conv_b, w_down):
    up = h @ w_up
    gate, val = jnp.split(up, 2, axis=-1)
    gate = causal_dwconv(gate, conv_w) + conv_b
    return (jax.nn.silu(gate) * val) @ w_down


def _fwd_setup_inputs(seed: int = 0) -> dict:
    key = jax.random.key(seed)
    ks = jax.random.split(key, 24)
    f32 = jnp.float32
    D = D_MODEL

    def nrm(k, shape, s):
        return jax.random.normal(k, shape, f32) * s

    dt = jnp.exp(jax.random.uniform(ks[10], (N_EVEN, GDN_HEADS), f32,
                                    math.log(1e-3), math.log(1e-1)))
    return {
        "x": nrm(ks[0], (BATCH, SEQ, D), 1.0),
        "c": nrm(ks[1], (BATCH, D), 1.0),
        "ada_w": nrm(ks[2], (DEPTH, D, 6 * D), 0.5 * D ** -0.5),
        "ada_b": nrm(ks[3], (DEPTH, 6 * D), 0.01),
        "norm_mix": 1.0 + nrm(ks[4], (DEPTH, D), 0.02),
        "norm_ffn": 1.0 + nrm(ks[5], (DEPTH, D), 0.02),
        "ev_w_in": nrm(ks[6], (N_EVEN, D, EVEN_IN), D ** -0.5),
        "ev_w_out": nrm(ks[7], (N_EVEN, EVEN_OUT, D), EVEN_OUT ** -0.5),
        "gdn_conv_w": nrm(ks[8], (N_EVEN, GDN_CONV, EVEN_QKV), GDN_CONV ** -0.5),
        "gdn_a_log": jnp.log(jax.random.uniform(ks[9], (N_EVEN, GDN_HEADS), f32, 1.0, 16.0)),
        "gdn_dt_bias": dt + jnp.log(-jnp.expm1(-dt)),
        "gdn_norm": 1.0 + nrm(ks[11], (N_EVEN, GDN_DV), 0.02),
        "pool_w": nrm(ks[12], (N_EVEN, POOL_GROUPS, POOL_GROUP_W, POOL_GROUP_W), POOL_GROUP_W ** -0.5),
        "pool_scale": 1.0 + nrm(ks[13], (N_EVEN, POOL_W), 0.1),
        "od_w_in": nrm(ks[14], (N_ODD, D, ODD_IN), D ** -0.5),
        "od_w_out": nrm(ks[15], (N_ODD, ATT_W, D), ATT_W ** -0.5),
        "att_q_norm": 1.0 + nrm(ks[16], (N_ODD, ATT_DH), 0.02),
        "att_k_norm": 1.0 + nrm(ks[17], (N_ODD, ATT_DH), 0.02),
        "ffn_w_up": nrm(ks[18], (DEPTH, D, 2 * D_FF), D ** -0.5),
        "ffn_conv_w": nrm(ks[19], (DEPTH, FFN_CONV, D_FF), FFN_CONV ** -0.5),
        "ffn_conv_b": nrm(ks[20], (DEPTH, D_FF), 0.01),
        "ffn_w_down": nrm(ks[21], (DEPTH, D_FF, D), D_FF ** -0.5),
    }


def _fwd_reference(x, c, ada_w, ada_b, norm_mix, norm_ffn, ev_w_in, ev_w_out, gdn_conv_w,
              gdn_a_log, gdn_dt_bias, gdn_norm, pool_w, pool_scale, od_w_in, od_w_out,
              att_q_norm, att_k_norm, ffn_w_up, ffn_conv_w, ffn_conv_b, ffn_w_down):
    cs = jax.nn.silu(c)
    for i in range(DEPTH):
        mod = (cs @ ada_w[i] + ada_b[i])[:, None, :]
        sh_m, sc_m, g_m, sh_f, sc_f, g_f = jnp.split(mod, 6, axis=-1)

        hm = rms_norm(x, norm_mix[i]) * (1.0 + sc_m) + sh_m
        if i % 2 == 0:
            e = i // 2
            y = even_mixer(hm, ev_w_in[e], ev_w_out[e], gdn_conv_w[e], gdn_a_log[e],
                           gdn_dt_bias[e], gdn_norm[e], pool_w[e], pool_scale[e])
        else:
            o = i // 2
            y = odd_mixer(hm, od_w_in[o], od_w_out[o], att_q_norm[o], att_k_norm[o])
        x = x + g_m * y

        hf = rms_norm(x, norm_ffn[i]) * (1.0 + sc_f) + sh_f
        x = x + g_f * conv_ffn(hf, ffn_w_up[i], ffn_conv_w[i], ffn_conv_b[i], ffn_w_down[i])
    return x


import jax as _jax
import jax.numpy as _jnp

TWIN_FORMAT = 'train_step'
FWD_PARAMS = ['x', 'c', 'ada_w', 'ada_b', 'norm_mix', 'norm_ffn', 'ev_w_in', 'ev_w_out', 'gdn_conv_w', 'gdn_a_log', 'gdn_dt_bias', 'gdn_norm', 'pool_w', 'pool_scale', 'od_w_in', 'od_w_out', 'att_q_norm', 'att_k_norm', 'ffn_w_up', 'ffn_conv_w', 'ffn_conv_b', 'ffn_w_down']
TWIN_WEIGHTS = ['ada_w', 'ada_b', 'norm_mix', 'norm_ffn', 'ev_w_in', 'ev_w_out', 'gdn_conv_w', 'gdn_a_log', 'gdn_dt_bias', 'gdn_norm', 'pool_w', 'pool_scale', 'od_w_in', 'od_w_out', 'att_q_norm', 'att_k_norm', 'ffn_w_up', 'ffn_conv_w', 'ffn_conv_b', 'ffn_w_down']
TWIN_DIFF_INPUT = 'x'
TWIN_INPUTS = ['x', 'c', 'ada_w', 'ada_b', 'norm_mix', 'norm_ffn', 'ev_w_in', 'ev_w_out', 'gdn_conv_w', 'gdn_a_log', 'gdn_dt_bias', 'gdn_norm', 'pool_w', 'pool_scale', 'od_w_in', 'od_w_out', 'att_q_norm', 'att_k_norm', 'ffn_w_up', 'ffn_conv_w', 'ffn_conv_b', 'ffn_w_down', 'loss_target', 'm_ada_w', 'm_ada_b', 'm_norm_mix', 'm_norm_ffn', 'm_ev_w_in', 'm_ev_w_out', 'm_gdn_conv_w', 'm_gdn_a_log', 'm_gdn_dt_bias', 'm_gdn_norm', 'm_pool_w', 'm_pool_scale', 'm_od_w_in', 'm_od_w_out', 'm_att_q_norm', 'm_att_k_norm', 'm_ffn_w_up', 'm_ffn_conv_w', 'm_ffn_conv_b', 'm_ffn_w_down', 'v_ada_w', 'v_ada_b', 'v_norm_mix', 'v_norm_ffn', 'v_ev_w_in', 'v_ev_w_out', 'v_gdn_conv_w', 'v_gdn_a_log', 'v_gdn_dt_bias', 'v_gdn_norm', 'v_pool_w', 'v_pool_scale', 'v_od_w_in', 'v_od_w_out', 'v_att_q_norm', 'v_att_k_norm', 'v_ffn_w_up', 'v_ffn_conv_w', 'v_ffn_conv_b', 'v_ffn_w_down']
TWIN_OUTPUTS = ['loss', 'grad_x', 'grad_ada_w', 'grad_ada_b', 'grad_norm_mix', 'grad_norm_ffn', 'grad_ev_w_in', 'grad_ev_w_out', 'grad_gdn_conv_w', 'grad_gdn_a_log', 'grad_gdn_dt_bias', 'grad_gdn_norm', 'grad_pool_w', 'grad_pool_scale', 'grad_od_w_in', 'grad_od_w_out', 'grad_att_q_norm', 'grad_att_k_norm', 'grad_ffn_w_up', 'grad_ffn_conv_w', 'grad_ffn_conv_b', 'grad_ffn_w_down', 'delta_ada_w', 'delta_ada_b', 'delta_norm_mix', 'delta_norm_ffn', 'delta_ev_w_in', 'delta_ev_w_out', 'delta_gdn_conv_w', 'delta_gdn_a_log', 'delta_gdn_dt_bias', 'delta_gdn_norm', 'delta_pool_w', 'delta_pool_scale', 'delta_od_w_in', 'delta_od_w_out', 'delta_att_q_norm', 'delta_att_k_norm', 'delta_ffn_w_up', 'delta_ffn_conv_w', 'delta_ffn_conv_b', 'delta_ffn_w_down', 'new_m_ada_w', 'new_m_ada_b', 'new_m_norm_mix', 'new_m_norm_ffn', 'new_m_ev_w_in', 'new_m_ev_w_out', 'new_m_gdn_conv_w', 'new_m_gdn_a_log', 'new_m_gdn_dt_bias', 'new_m_gdn_norm', 'new_m_pool_w', 'new_m_pool_scale', 'new_m_od_w_in', 'new_m_od_w_out', 'new_m_att_q_norm', 'new_m_att_k_norm', 'new_m_ffn_w_up', 'new_m_ffn_conv_w', 'new_m_ffn_conv_b', 'new_m_ffn_w_down', 'new_v_ada_w', 'new_v_ada_b', 'new_v_norm_mix', 'new_v_norm_ffn', 'new_v_ev_w_in', 'new_v_ev_w_out', 'new_v_gdn_conv_w', 'new_v_gdn_a_log', 'new_v_gdn_dt_bias', 'new_v_gdn_norm', 'new_v_pool_w', 'new_v_pool_scale', 'new_v_od_w_in', 'new_v_od_w_out', 'new_v_att_q_norm', 'new_v_att_k_norm', 'new_v_ffn_w_up', 'new_v_ffn_conv_w', 'new_v_ffn_conv_b', 'new_v_ffn_w_down']
TWIN_LEAF_KINDS = {'loss': 'loss', 'grad_x': 'grad_x', 'grad_ada_w': 'grad_w', 'grad_ada_b': 'grad_w', 'grad_norm_mix': 'grad_w', 'grad_norm_ffn': 'grad_w', 'grad_ev_w_in': 'grad_w', 'grad_ev_w_out': 'grad_w', 'grad_gdn_conv_w': 'grad_w', 'grad_gdn_a_log': 'grad_w', 'grad_gdn_dt_bias': 'grad_w', 'grad_gdn_norm': 'grad_w', 'grad_pool_w': 'grad_w', 'grad_pool_scale': 'grad_w', 'grad_od_w_in': 'grad_w', 'grad_od_w_out': 'grad_w', 'grad_att_q_norm': 'grad_w', 'grad_att_k_norm': 'grad_w', 'grad_ffn_w_up': 'grad_w', 'grad_ffn_conv_w': 'grad_w', 'grad_ffn_conv_b': 'grad_w', 'grad_ffn_w_down': 'grad_w', 'delta_ada_w': 'delta_w', 'delta_ada_b': 'delta_w', 'delta_norm_mix': 'delta_w', 'delta_norm_ffn': 'delta_w', 'delta_ev_w_in': 'delta_w', 'delta_ev_w_out': 'delta_w', 'delta_gdn_conv_w': 'delta_w', 'delta_gdn_a_log': 'delta_w', 'delta_gdn_dt_bias': 'delta_w', 'delta_gdn_norm': 'delta_w', 'delta_pool_w': 'delta_w', 'delta_pool_scale': 'delta_w', 'delta_od_w_in': 'delta_w', 'delta_od_w_out': 'delta_w', 'delta_att_q_norm': 'delta_w', 'delta_att_k_norm': 'delta_w', 'delta_ffn_w_up': 'delta_w', 'delta_ffn_conv_w': 'delta_w', 'delta_ffn_conv_b': 'delta_w', 'delta_ffn_w_down': 'delta_w', 'new_m_ada_w': 'new_m', 'new_m_ada_b': 'new_m', 'new_m_norm_mix': 'new_m', 'new_m_norm_ffn': 'new_m', 'new_m_ev_w_in': 'new_m', 'new_m_ev_w_out': 'new_m', 'new_m_gdn_conv_w': 'new_m', 'new_m_gdn_a_log': 'new_m', 'new_m_gdn_dt_bias': 'new_m', 'new_m_gdn_norm': 'new_m', 'new_m_pool_w': 'new_m', 'new_m_pool_scale': 'new_m', 'new_m_od_w_in': 'new_m', 'new_m_od_w_out': 'new_m', 'new_m_att_q_norm': 'new_m', 'new_m_att_k_norm': 'new_m', 'new_m_ffn_w_up': 'new_m', 'new_m_ffn_conv_w': 'new_m', 'new_m_ffn_conv_b': 'new_m', 'new_m_ffn_w_down': 'new_m', 'new_v_ada_w': 'new_v', 'new_v_ada_b': 'new_v', 'new_v_norm_mix': 'new_v', 'new_v_norm_ffn': 'new_v', 'new_v_ev_w_in': 'new_v', 'new_v_ev_w_out': 'new_v', 'new_v_gdn_conv_w': 'new_v', 'new_v_gdn_a_log': 'new_v', 'new_v_gdn_dt_bias': 'new_v', 'new_v_gdn_norm': 'new_v', 'new_v_pool_w': 'new_v', 'new_v_pool_scale': 'new_v', 'new_v_od_w_in': 'new_v', 'new_v_od_w_out': 'new_v', 'new_v_att_q_norm': 'new_v', 'new_v_att_k_norm': 'new_v', 'new_v_ffn_w_up': 'new_v', 'new_v_ffn_conv_w': 'new_v', 'new_v_ffn_conv_b': 'new_v', 'new_v_ffn_w_down': 'new_v'}


def _forward(args):
    return _fwd_reference(*[args[k] for k in FWD_PARAMS])


def _output_shape():
    def fwd():
        inp = _fwd_setup_inputs(0)
        return _fwd_reference(*[inp[k] for k in FWD_PARAMS])
    out = _jax.eval_shape(fwd)
    return out.shape, out.dtype

N_MICROBATCH = 1
ADAM_LR = 0.001
ADAM_B1 = 0.9
ADAM_B2 = 0.999
ADAM_EPS = 1e-08
ADAM_WD = 0.01
ADAM_STEP = 10
PER_EXAMPLE_BATCH_AXIS = {'x': 0, 'c': 0, 'loss_target': 0}
SHARED_INPUTS = []
_WEIGHT_DTYPES = {'ada_w': _jnp.float32, 'ada_b': _jnp.float32, 'norm_mix': _jnp.float32, 'norm_ffn': _jnp.float32, 'ev_w_in': _jnp.float32, 'ev_w_out': _jnp.float32, 'gdn_conv_w': _jnp.float32, 'gdn_a_log': _jnp.float32, 'gdn_dt_bias': _jnp.float32, 'gdn_norm': _jnp.float32, 'pool_w': _jnp.float32, 'pool_scale': _jnp.float32, 'od_w_in': _jnp.float32, 'od_w_out': _jnp.float32, 'att_q_norm': _jnp.float32, 'att_k_norm': _jnp.float32, 'ffn_w_up': _jnp.float32, 'ffn_conv_w': _jnp.float32, 'ffn_conv_b': _jnp.float32, 'ffn_w_down': _jnp.float32}
MOMENT_SCALE = {'ada_w': 1.452039e+00, 'ada_b': 3.831554e+00, 'norm_mix': 2.795918e+00, 'norm_ffn': 6.455718e+00, 'ev_w_in': 2.097647e-01, 'ev_w_out': 2.424394e-01, 'gdn_conv_w': 1.916056e-01, 'gdn_a_log': 6.159199e+00, 'gdn_dt_bias': 5.805972e+00, 'gdn_norm': 1.050471e+01, 'pool_w': 4.409845e-01, 'pool_scale': 5.058022e+00, 'od_w_in': 7.034152e-02, 'od_w_out': 2.090237e-01, 'att_q_norm': 1.578193e+00, 'att_k_norm': 1.575932e+00, 'ffn_w_up': 1.290063e-01, 'ffn_conv_w': 7.664351e-01, 'ffn_conv_b': 8.533708e-01, 'ffn_w_down': 1.413596e-01}


def _to_microbatches(a, axis):
    t = _jnp.moveaxis(a, axis, 0)
    t = t.reshape((N_MICROBATCH, t.shape[0] // N_MICROBATCH) + t.shape[1:])
    return _jnp.moveaxis(t, 1, axis + 1)


def setup_inputs(seed: int = 0) -> dict:
    inp = _fwd_setup_inputs(seed)
    key = _jax.random.fold_in(_jax.random.key(seed), 7919)
    shape, _ = _output_shape()
    out = dict(inp)
    out["loss_target"] = _jax.random.normal(_jax.random.fold_in(key, 0), shape, _jnp.float32)
    for i, name in enumerate(TWIN_WEIGHTS):
        w = inp[name].astype(_jnp.float32)
        if MOMENT_SCALE is None:
            s = _jnp.sqrt(_jnp.mean(_jnp.square(w)) + 1e-30)
        else:
            s = MOMENT_SCALE[name]
        km, kv = _jax.random.split(_jax.random.fold_in(key, i + 1))
        out[name] = w
        out["m_" + name] = s * _jax.random.normal(km, w.shape, _jnp.float32)
        out["v_" + name] = (s * s) * _jax.random.uniform(kv, w.shape, _jnp.float32, 0.5, 1.5)
    if N_MICROBATCH > 1:
        for name, axis in PER_EXAMPLE_BATCH_AXIS.items():
            out[name] = _to_microbatches(out[name], axis)
    return {'x': out['x'], 'c': out['c'], 'ada_w': out['ada_w'], 'ada_b': out['ada_b'], 'norm_mix': out['norm_mix'], 'norm_ffn': out['norm_ffn'], 'ev_w_in': out['ev_w_in'], 'ev_w_out': out['ev_w_out'], 'gdn_conv_w': out['gdn_conv_w'], 'gdn_a_log': out['gdn_a_log'], 'gdn_dt_bias': out['gdn_dt_bias'], 'gdn_norm': out['gdn_norm'], 'pool_w': out['pool_w'], 'pool_scale': out['pool_scale'], 'od_w_in': out['od_w_in'], 'od_w_out': out['od_w_out'], 'att_q_norm': out['att_q_norm'], 'att_k_norm': out['att_k_norm'], 'ffn_w_up': out['ffn_w_up'], 'ffn_conv_w': out['ffn_conv_w'], 'ffn_conv_b': out['ffn_conv_b'], 'ffn_w_down': out['ffn_w_down'], 'loss_target': out['loss_target'], 'm_ada_w': out['m_ada_w'], 'm_ada_b': out['m_ada_b'], 'm_norm_mix': out['m_norm_mix'], 'm_norm_ffn': out['m_norm_ffn'], 'm_ev_w_in': out['m_ev_w_in'], 'm_ev_w_out': out['m_ev_w_out'], 'm_gdn_conv_w': out['m_gdn_conv_w'], 'm_gdn_a_log': out['m_gdn_a_log'], 'm_gdn_dt_bias': out['m_gdn_dt_bias'], 'm_gdn_norm': out['m_gdn_norm'], 'm_pool_w': out['m_pool_w'], 'm_pool_scale': out['m_pool_scale'], 'm_od_w_in': out['m_od_w_in'], 'm_od_w_out': out['m_od_w_out'], 'm_att_q_norm': out['m_att_q_norm'], 'm_att_k_norm': out['m_att_k_norm'], 'm_ffn_w_up': out['m_ffn_w_up'], 'm_ffn_conv_w': out['m_ffn_conv_w'], 'm_ffn_conv_b': out['m_ffn_conv_b'], 'm_ffn_w_down': out['m_ffn_w_down'], 'v_ada_w': out['v_ada_w'], 'v_ada_b': out['v_ada_b'], 'v_norm_mix': out['v_norm_mix'], 'v_norm_ffn': out['v_norm_ffn'], 'v_ev_w_in': out['v_ev_w_in'], 'v_ev_w_out': out['v_ev_w_out'], 'v_gdn_conv_w': out['v_gdn_conv_w'], 'v_gdn_a_log': out['v_gdn_a_log'], 'v_gdn_dt_bias': out['v_gdn_dt_bias'], 'v_gdn_norm': out['v_gdn_norm'], 'v_pool_w': out['v_pool_w'], 'v_pool_scale': out['v_pool_scale'], 'v_od_w_in': out['v_od_w_in'], 'v_od_w_out': out['v_od_w_out'], 'v_att_q_norm': out['v_att_q_norm'], 'v_att_k_norm': out['v_att_k_norm'], 'v_ffn_w_up': out['v_ffn_w_up'], 'v_ffn_conv_w': out['v_ffn_conv_w'], 'v_ffn_conv_b': out['v_ffn_conv_b'], 'v_ffn_w_down': out['v_ffn_w_down']}


def _loss(weights, diff, rest, loss_target):
    with _jax.named_scope("forward"):
        args = {**rest, TWIN_DIFF_INPUT: diff, **{k: w.astype(_WEIGHT_DTYPES[k]) for k, w in weights.items()}}
        y = _forward(args)
    with _jax.named_scope("loss_head"):
        err = _jnp.square(y.astype(_jnp.float32) - loss_target)
        return 0.5 * _jnp.sum(_jnp.mean(err, axis=-1)) if err.ndim else 0.5 * err


def _adamw(w, g, m, v):
    m = ADAM_B1 * m + (1.0 - ADAM_B1) * g
    v = ADAM_B2 * v + (1.0 - ADAM_B2) * _jnp.square(g)
    m_hat = m / (1.0 - ADAM_B1 ** ADAM_STEP)
    v_hat = v / (1.0 - ADAM_B2 ** ADAM_STEP)
    delta = -ADAM_LR * (m_hat / (_jnp.sqrt(v_hat) + ADAM_EPS) + ADAM_WD * w)
    return delta, m, v


def reference(x, c, ada_w, ada_b, norm_mix, norm_ffn, ev_w_in, ev_w_out, gdn_conv_w, gdn_a_log, gdn_dt_bias, gdn_norm, pool_w, pool_scale, od_w_in, od_w_out, att_q_norm, att_k_norm, ffn_w_up, ffn_conv_w, ffn_conv_b, ffn_w_down, loss_target, m_ada_w, m_ada_b, m_norm_mix, m_norm_ffn, m_ev_w_in, m_ev_w_out, m_gdn_conv_w, m_gdn_a_log, m_gdn_dt_bias, m_gdn_norm, m_pool_w, m_pool_scale, m_od_w_in, m_od_w_out, m_att_q_norm, m_att_k_norm, m_ffn_w_up, m_ffn_conv_w, m_ffn_conv_b, m_ffn_w_down, v_ada_w, v_ada_b, v_norm_mix, v_norm_ffn, v_ev_w_in, v_ev_w_out, v_gdn_conv_w, v_gdn_a_log, v_gdn_dt_bias, v_gdn_norm, v_pool_w, v_pool_scale, v_od_w_in, v_od_w_out, v_att_q_norm, v_att_k_norm, v_ffn_w_up, v_ffn_conv_w, v_ffn_conv_b, v_ffn_w_down):
    given = dict(x=x, c=c, ada_w=ada_w, ada_b=ada_b, norm_mix=norm_mix, norm_ffn=norm_ffn, ev_w_in=ev_w_in, ev_w_out=ev_w_out, gdn_conv_w=gdn_conv_w, gdn_a_log=gdn_a_log, gdn_dt_bias=gdn_dt_bias, gdn_norm=gdn_norm, pool_w=pool_w, pool_scale=pool_scale, od_w_in=od_w_in, od_w_out=od_w_out, att_q_norm=att_q_norm, att_k_norm=att_k_norm, ffn_w_up=ffn_w_up, ffn_conv_w=ffn_conv_w, ffn_conv_b=ffn_conv_b, ffn_w_down=ffn_w_down, loss_target=loss_target, m_ada_w=m_ada_w, m_ada_b=m_ada_b, m_norm_mix=m_norm_mix, m_norm_ffn=m_norm_ffn, m_ev_w_in=m_ev_w_in, m_ev_w_out=m_ev_w_out, m_gdn_conv_w=m_gdn_conv_w, m_gdn_a_log=m_gdn_a_log, m_gdn_dt_bias=m_gdn_dt_bias, m_gdn_norm=m_gdn_norm, m_pool_w=m_pool_w, m_pool_scale=m_pool_scale, m_od_w_in=m_od_w_in, m_od_w_out=m_od_w_out, m_att_q_norm=m_att_q_norm, m_att_k_norm=m_att_k_norm, m_ffn_w_up=m_ffn_w_up, m_ffn_conv_w=m_ffn_conv_w, m_ffn_conv_b=m_ffn_conv_b, m_ffn_w_down=m_ffn_w_down, v_ada_w=v_ada_w, v_ada_b=v_ada_b, v_norm_mix=v_norm_mix, v_norm_ffn=v_norm_ffn, v_ev_w_in=v_ev_w_in, v_ev_w_out=v_ev_w_out, v_gdn_conv_w=v_gdn_conv_w, v_gdn_a_log=v_gdn_a_log, v_gdn_dt_bias=v_gdn_dt_bias, v_gdn_norm=v_gdn_norm, v_pool_w=v_pool_w, v_pool_scale=v_pool_scale, v_od_w_in=v_od_w_in, v_od_w_out=v_od_w_out, v_att_q_norm=v_att_q_norm, v_att_k_norm=v_att_k_norm, v_ffn_w_up=v_ffn_w_up, v_ffn_conv_w=v_ffn_conv_w, v_ffn_conv_b=v_ffn_conv_b, v_ffn_w_down=v_ffn_w_down)
    weights = {n: given[n] for n in TWIN_WEIGHTS}
    shared = {n: given[n] for n in SHARED_INPUTS}
    per_example = {n: given[n] for n in ['x', 'c']}
    grad_fn = _jax.value_and_grad(_loss, argnums=(0, 1))

    def one_microbatch(ex, loss_target):
        ex = dict(ex)
        diff = ex.pop(TWIN_DIFF_INPUT)
        return grad_fn(weights, diff, {**shared, **ex}, loss_target)

    if N_MICROBATCH == 1:
        loss, (grad_w, grad_x) = one_microbatch(per_example, given["loss_target"])
    else:
        def body(carry, xs):
            loss_sum, grad_sum = carry
            l_k, (gw_k, gx_k) = one_microbatch(xs[0], xs[1])
            with _jax.named_scope("update"):
                return (loss_sum + l_k, _jax.tree.map(_jnp.add, grad_sum, gw_k)), gx_k

        init = (_jnp.zeros((), _jnp.float32), _jax.tree.map(_jnp.zeros_like, weights))
        (loss, grad_w), grad_x = _jax.lax.scan(body, init, (per_example, given["loss_target"]))
    with _jax.named_scope("update"):
        delta_w, new_m, new_v = {}, {}, {}
        for n in TWIN_WEIGHTS:
            delta_w[n], new_m[n], new_v[n] = _adamw(weights[n], grad_w[n], given["m_" + n], given["v_" + n])
    return (loss, grad_x, *[grad_w[n] for n in TWIN_WEIGHTS], *[delta_w[n] for n in TWIN_WEIGHTS],
            *[new_m[n] for n in TWIN_WEIGHTS], *[new_v[n] for n in TWIN_WEIGHTS])
```

```python
import functools
import math

import jax
import jax.numpy as jnp
from jax import lax
from jax.experimental import pallas as pl
from jax.experimental.pallas import tpu as pltpu

F32 = jnp.float32
BF16 = jnp.bfloat16
HI = lax.Precision.HIGHEST
MESH = pl.DeviceIdType.MESH

N_DEV = 8
RMS_EPS = 1e-6
DEPTH = 4
GDN_HEADS = 4
GDN_D = 128
GDN_CHUNK = 64
GDN_CONV = 4
POOL_WINDOWS = (2, 4, 8, 16)
DIL_PATTERNS = ((128, 1), (512, 4), (2048, 16))
ATT_HEADS = 8
ATT_DH = 128
ATT_BLOCK = 128
FFN_CONV = 3
ADAM_LR, ADAM_B1, ADAM_B2, ADAM_EPS, ADAM_WD, ADAM_STEP = 0.001, 0.9, 0.999, 1e-08, 0.01, 10

HALO = 16
NEG = -1e30
VMEM_LIMIT_BYTES = 56 * 1024 * 1024


def _cp(sem=None, **kw):
    return pltpu.CompilerParams(dimension_semantics=sem, vmem_limit_bytes=VMEM_LIMIT_BYTES, **kw)


def _sds(shape, dtype):
    return jax.ShapeDtypeStruct(tuple(shape), dtype)


def _dot(a, b):
    return jnp.dot(a.astype(BF16), b.astype(BF16), preferred_element_type=F32)


def _dot_nt(a, b):
    return lax.dot_general(a.astype(BF16), b.astype(BF16), (((1,), (1,)), ((), ())), preferred_element_type=F32)


def _dot_tn(a, b):
    return lax.dot_general(a.astype(BF16), b.astype(BF16), (((0,), (0,)), ((), ())), preferred_element_type=F32)


def _dot_hi(a, b):
    return jnp.dot(a, b, preferred_element_type=F32, precision=HI)


def _silu(x):
    return x * jax.nn.sigmoid(x)


def _modnorm(x, gain, sc, sh):
    y = x * lax.rsqrt(jnp.mean(x * x, axis=-1, keepdims=True) + RMS_EPS)
    return y * gain * (1.0 + sc) + sh


def _row_tile(T):
    return 512 if T % 512 == 0 else T


def norm_proj(x, gain, sc, sh, w, tn, name):
    T, D = x.shape
    N = w.shape[1]
    tm = _row_tile(T)

    def body(x_ref, g_ref, sc_ref, sh_ref, w_ref, h_ref, o_ref, h_scr):
        @pl.when(pl.program_id(1) == 0)
        def _():
            h = _modnorm(x_ref[...], g_ref[...], sc_ref[...], sh_ref[...]).astype(BF16)
            h_scr[...] = h
            h_ref[...] = h
        o_ref[...] = jnp.dot(h_scr[...], w_ref[...], preferred_element_type=F32).astype(o_ref.dtype)

    vec = pl.BlockSpec((1, D), lambda i, j: (0, 0))
    return pl.pallas_call(
        body, name=name, grid=(T // tm, N // tn),
        in_specs=[pl.BlockSpec((tm, D), lambda i, j: (i, 0)), vec, vec, vec,
                  pl.BlockSpec((D, tn), lambda i, j: (0, j))],
        out_specs=[pl.BlockSpec((tm, D), lambda i, j: (i, 0)), pl.BlockSpec((tm, tn), lambda i, j: (i, j))],
        out_shape=[_sds((T, D), BF16), _sds((T, N), BF16)],
        scratch_shapes=[pltpu.VMEM((tm, D), BF16)],
        compiler_params=_cp(("parallel", "arbitrary")),
    )(x, gain, sc, sh, w)


def matmul_nn(a, w, out_dtype, name):
    T, K = a.shape
    N = w.shape[1]
    tm = _row_tile(T)

    def body(a_ref, w_ref, o_ref):
        o_ref[...] = _dot(a_ref[...], w_ref[...]).astype(o_ref.dtype)

    return pl.pallas_call(
        body, name=name, grid=(T // tm,),
        in_specs=[pl.BlockSpec((tm, K), lambda i: (i, 0)), pl.BlockSpec((K, N), lambda i: (0, 0))],
        out_specs=pl.BlockSpec((tm, N), lambda i: (i, 0)),
        out_shape=_sds((T, N), out_dtype),
        compiler_params=_cp(("parallel",)),
    )(a, w)


def proj_res(a, w, x, gate, name):
    T, K = a.shape
    D = w.shape[1]
    tm = _row_tile(T)

    def body(a_ref, w_ref, x_ref, g_ref, o_ref, y_ref):
        y = jnp.dot(a_ref[...], w_ref[...], preferred_element_type=F32)
        y_ref[...] = y.astype(BF16)
        o_ref[...] = x_ref[...] + g_ref[...] * y

    return pl.pallas_call(
        body, name=name, grid=(T // tm,),
        in_specs=[pl.BlockSpec((tm, K), lambda i: (i, 0)), pl.BlockSpec((K, D), lambda i: (0, 0)),
                  pl.BlockSpec((tm, D), lambda i: (i, 0)), pl.BlockSpec((1, D), lambda i: (0, 0))],
        out_specs=[pl.BlockSpec((tm, D), lambda i: (i, 0)), pl.BlockSpec((tm, D), lambda i: (i, 0))],
        out_shape=[_sds((T, D), F32), _sds((T, D), BF16)],
        compiler_params=_cp(("parallel",)),
    )(a, w, x, gate)


def bwd_out(dx, gate, y, w, tk, name):
    T, D = dx.shape
    K = w.shape[0]
    tm = _row_tile(T)

    def body(dx_ref, g_ref, y_ref, w_ref, dy_ref, da_ref, dg_ref, dy_scr):
        i, j = pl.program_id(0), pl.program_id(1)

        @pl.when(j == 0)
        def _():
            dxv = dx_ref[...]
            dy = (dxv * g_ref[...]).astype(BF16)
            dy_scr[...] = dy
            dy_ref[...] = dy
            part = jnp.sum(dxv * y_ref[...].astype(F32), axis=0, keepdims=True)

            @pl.when(i == 0)
            def _():
                dg_ref[...] = part

            @pl.when(i > 0)
            def _():
                dg_ref[...] += part

        da_ref[...] = _dot_nt(dy_scr[...], w_ref[...]).astype(BF16)

    return pl.pallas_call(
        body, name=name, grid=(T // tm, K // tk),
        in_specs=[pl.BlockSpec((tm, D), lambda i, j: (i, 0)), pl.BlockSpec((1, D), lambda i, j: (0, 0)),
                  pl.BlockSpec((tm, D), lambda i, j: (i, 0)), pl.BlockSpec((tk, D), lambda i, j: (j, 0))],
        out_specs=[pl.BlockSpec((tm, D), lambda i, j: (i, 0)), pl.BlockSpec((tm, tk), lambda i, j: (i, j)),
                   pl.BlockSpec((1, D), lambda i, j: (0, 0))],
        out_shape=[_sds((T, D), BF16), _sds((T, K), BF16), _sds((1, D), F32)],
        scratch_shapes=[pltpu.VMEM((tm, D), BF16)],
        compiler_params=_cp(("arbitrary", "arbitrary")),
    )(dx, gate, y, w)


def bwd_in(a_list, w, col_blocks, tn, acc, x, gain, sc, sh, dx_res, name):
    T = a_list[0].shape[0]
    D = w.shape[0]
    tm = 256 if T % 256 == 0 else T
    n_a = len(a_list)
    nsteps = a_list[0].shape[1] // tn
    assert all(a.shape[1] == nsteps * tn for a in a_list)
    final = x is not None
    has_acc = acc is not None

    def body(*refs):
        a_refs = refs[:n_a]
        w_refs = refs[n_a:2 * n_a]
        pos = 2 * n_a
        acc_ref = refs[pos] if has_acc else None
        pos += int(has_acc)
        if final:
            x_ref, g_ref, sc_ref, sh_ref, dxr_ref = refs[pos:pos + 5]
            pos += 5
            dx_ref, dg_ref, dsc_ref, dsh_ref, acc_scr = refs[pos:pos + 5]
        else:
            dh_ref, acc_scr = refs[pos:pos + 2]
        i, j = pl.program_id(0), pl.program_id(1)
        part = _dot_nt(a_refs[0][...], w_refs[0][...])
        for k in range(1, n_a):
            part += _dot_nt(a_refs[k][...], w_refs[k][...])

        @pl.when(j == 0)
        def _():
            acc_scr[...] = part + acc_ref[...] if has_acc else part

        @pl.when(j > 0)
        def _():
            acc_scr[...] += part

        @pl.when(j == nsteps - 1)
        def _():
            dh = acc_scr[...]
            if not final:
                dh_ref[...] = dh
                return
            _, vjp = jax.vjp(_modnorm, x_ref[...], g_ref[...], sc_ref[...], sh_ref[...])
            dxn, dg, dsc, dsh = vjp(dh)
            dx_ref[...] = dxr_ref[...] + dxn

            @pl.when(i == 0)
            def _():
                dg_ref[...] = dg
                dsc_ref[...] = dsc
                dsh_ref[...] = dsh

            @pl.when(i > 0)
            def _():
                dg_ref[...] += dg
                dsc_ref[...] += dsc
                dsh_ref[...] += dsh

    row = pl.BlockSpec((tm, D), lambda i, j: (i, 0))
    vec = pl.BlockSpec((1, D), lambda i, j: (0, 0))
    in_specs = [pl.BlockSpec((tm, tn), lambda i, j: (i, j)) for _ in a_list]
    in_specs += [pl.BlockSpec((D, tn), functools.partial(lambda i, j, c0: (0, c0 + j), c0=c0)) for c0 in col_blocks]
    args = list(a_list) + [w] * n_a
    if has_acc:
        in_specs.append(row)
        args.append(acc)
    if final:
        in_specs += [row, vec, vec, vec, row]
        args += [x, gain, sc, sh, dx_res]
        out_specs = [row, vec, vec, vec]
        out_shape = [_sds((T, D), F32)] + [_sds((1, D), F32)] * 3
    else:
        out_specs = row
        out_shape = _sds((T, D), F32)
    return pl.pallas_call(
        body, name=name, grid=(T // tm, nsteps), in_specs=in_specs, out_specs=out_specs, out_shape=out_shape,
        scratch_shapes=[pltpu.VMEM((tm, D), F32)],
        compiler_params=_cp(("arbitrary", "arbitrary")),
    )(*args)


def matmul_tn(a, b, tk, tn, name):
    T, K = a.shape
    N = b.shape[1]
    tt = 1024 if T % 1024 == 0 else T

    def body(a_ref, b_ref, o_ref):
        part = _dot_tn(a_ref[...], b_ref[...])

        @pl.when(pl.program_id(2) == 0)
        def _():
            o_ref[...] = part

        @pl.when(pl.program_id(2) > 0)
        def _():
            o_ref[...] += part

    return pl.pallas_call(
        body, name=name, grid=(K // tk, N // tn, T // tt),
        in_specs=[pl.BlockSpec((tt, tk), lambda k, n, t: (t, k)), pl.BlockSpec((tt, tn), lambda k, n, t: (t, n))],
        out_specs=pl.BlockSpec((tk, tn), lambda k, n, t: (k, n)),
        out_shape=_sds((K, N), F32),
        compiler_params=_cp(("parallel", "parallel", "arbitrary")),
    )(a, b)


def _halo_specs(tm, tc, col_of):
    per = tm // HALO

    def prev(j, i):
        return (jnp.maximum(i * per - 1, 0), col_of(j))

    def nxt(j, i, last):
        return (jnp.minimum((i + 1) * per, last), col_of(j))

    return prev, nxt, per


def _shift_down(ext, s):
    return pltpu.roll(ext, s, 0)


def _shift_up(ext, s):
    return pltpu.roll(ext, ext.shape[0] - s, 0)


def _conv_ext(ext, w):
    K = w.shape[0]
    out = w[K - 1:K] * ext
    for s in range(1, K):
        out += w[K - 1 - s:K - s] * _shift_down(ext, s)
    return out


def _conv_t_ext(dext, w):
    K = w.shape[0]
    out = w[K - 1:K] * dext
    for s in range(1, K):
        out += w[K - 1 - s:K - s] * _shift_up(dext, s)
    return out


def _conv_dw(dc, ext, K, tm):
    rowid = lax.broadcasted_iota(jnp.int32, (8, dc.shape[1]), 0)
    out = jnp.zeros((8, dc.shape[1]), F32)
    for j in range(K):
        s = K - 1 - j
        xs = ext if s == 0 else _shift_down(ext, s)
        out = jnp.where(rowid == j, jnp.sum(dc * xs[HALO:HALO + tm], axis=0, keepdims=True), out)
    return out


def _accum(ref, val, first):
    @pl.when(first)
    def _():
        ref[...] = val

    @pl.when(jnp.logical_not(first))
    def _():
        ref[...] += val


def ffn_mid_fwd(up, conv_w, conv_b, name):
    T, two_f = up.shape
    F = two_f // 2
    tm = _row_tile(T)
    tc = F // 2
    nct = F // tc
    prev, _, per = _halo_specs(tm, tc, lambda j: j)

    def body(g_ref, gp_ref, v_ref, w_ref, b_ref, a_ref):
        i = pl.program_id(1)
        gp = jnp.where(i > 0, gp_ref[...].astype(F32), 0.0)
        ext = jnp.concatenate([gp, g_ref[...].astype(F32)], axis=0)
        c = _conv_ext(ext, w_ref[...])[HALO:] + b_ref[...]
        a_ref[...] = (_silu(c) * v_ref[...].astype(F32)).astype(BF16)

    return pl.pallas_call(
        body, name=name, grid=(nct, T // tm),
        in_specs=[pl.BlockSpec((tm, tc), lambda j, i: (i, j)), pl.BlockSpec((HALO, tc), prev),
                  pl.BlockSpec((tm, tc), lambda j, i: (i, j + nct)),
                  pl.BlockSpec((FFN_CONV, tc), lambda j, i: (0, j)), pl.BlockSpec((1, tc), lambda j, i: (0, j))],
        out_specs=pl.BlockSpec((tm, tc), lambda j, i: (i, j)),
        out_shape=_sds((T, F), BF16),
        compiler_params=_cp(("parallel", "arbitrary")),
    )(up, up, up, conv_w, conv_b)


def ffn_mid_bwd(up, da, conv_w, conv_b, name):
    T, two_f = up.shape
    F = two_f // 2
    tm = _row_tile(T)
    tc = F // 2
    nct = F // tc
    nrow = T // tm
    prev, nxt, per = _halo_specs(tm, tc, lambda j: j)
    last = T // HALO - 1
    nxt_g = functools.partial(nxt, last=last)

    def nxt_v(j, i):
        return (jnp.minimum((i + 1) * per, last), j + nct)

    def body(g_ref, gp_ref, gn_ref, v_ref, vn_ref, da_ref, dan_ref, w_ref, b_ref, dg_ref, dv_ref, dw_ref, db_ref):
        i = pl.program_id(1)
        w = w_ref[...]
        gp = jnp.where(i > 0, gp_ref[...].astype(F32), 0.0)
        inside = i < nrow - 1
        ext = jnp.concatenate([gp, g_ref[...].astype(F32), gn_ref[...].astype(F32)], axis=0)
        zero = jnp.zeros((HALO, tc), F32)
        v_ext = jnp.concatenate([zero, v_ref[...].astype(F32), vn_ref[...].astype(F32)], axis=0)
        da_ext = jnp.concatenate([zero, da_ref[...].astype(F32), jnp.where(inside, dan_ref[...].astype(F32), 0.0)], axis=0)
        c = _conv_ext(ext, w) + b_ref[...]
        sg = jax.nn.sigmoid(c)
        sil = c * sg
        dc = da_ext * v_ext * (sg * (1.0 + c * (1.0 - sg)))
        dv_ref[...] = (da_ext * sil)[HALO:HALO + tm].astype(BF16)
        dg_ref[...] = _conv_t_ext(dc, w)[HALO:HALO + tm].astype(BF16)
        dcm = dc[HALO:HALO + tm]
        _accum(dw_ref, _conv_dw(dcm, ext, FFN_CONV, tm), i == 0)
        _accum(db_ref, jnp.sum(dcm, axis=0, keepdims=True), i == 0)

    main = lambda j, i: (i, j)
    main_v = lambda j, i: (i, j + nct)
    return pl.pallas_call(
        body, name=name, grid=(nct, nrow),
        in_specs=[pl.BlockSpec((tm, tc), main), pl.BlockSpec((HALO, tc), prev), pl.BlockSpec((HALO, tc), nxt_g),
                  pl.BlockSpec((tm, tc), main_v), pl.BlockSpec((HALO, tc), nxt_v),
                  pl.BlockSpec((tm, tc), main), pl.BlockSpec((HALO, tc), nxt_g),
                  pl.BlockSpec((FFN_CONV, tc), lambda j, i: (0, j)), pl.BlockSpec((1, tc), lambda j, i: (0, j))],
        out_specs=[pl.BlockSpec((tm, tc), main), pl.BlockSpec((tm, tc), main),
                   pl.BlockSpec((8, tc), lambda j, i: (0, j)), pl.BlockSpec((1, tc), lambda j, i: (0, j))],
        out_shape=[_sds((T, F), BF16), _sds((T, F), BF16), _sds((8, F), F32), _sds((1, F), F32)],
        compiler_params=_cp(("parallel", "arbitrary")),
    )(up, up, up, up, up, da, da, conv_w, conv_b)


def _l2n(x):
    return x * lax.rsqrt(jnp.sum(x * x, axis=-1, keepdims=True) + RMS_EPS)


def _qkv_tok(c, normed):
    s = _silu(c)
    n = jnp.concatenate([_l2n(s[:, h * GDN_D:(h + 1) * GDN_D]) for h in range(GDN_HEADS)], axis=1)
    return jnp.where(normed, n, s)


def gdn_prep_fwd(proj, conv_w, name):
    T = proj.shape[0]
    W = GDN_HEADS * GDN_D
    tm = _row_tile(T)
    prev, _, per = _halo_specs(tm, W, lambda j: j)

    def body(x_ref, xp_ref, w_ref, o_ref):
        j, i = pl.program_id(0), pl.program_id(1)
        xp = jnp.where(i > 0, xp_ref[...].astype(F32), 0.0)
        ext = jnp.concatenate([xp, x_ref[...].astype(F32)], axis=0)
        c = _conv_ext(ext, w_ref[...])[HALO:]
        o_ref[...] = _qkv_tok(c, j < 2)

    return pl.pallas_call(
        body, name=name, grid=(3, T // tm),
        in_specs=[pl.BlockSpec((tm, W), lambda j, i: (i, j)), pl.BlockSpec((HALO, W), prev),
                  pl.BlockSpec((GDN_CONV, W), lambda j, i: (0, j))],
        out_specs=pl.BlockSpec((tm, W), lambda j, i: (i, j)),
        out_shape=_sds((T, 3 * W), F32),
        compiler_params=_cp(("parallel", "arbitrary")),
    )(proj, proj, conv_w)


def gdn_prep_bwd(proj, dqkv, conv_w, name):
    T = proj.shape[0]
    W = GDN_HEADS * GDN_D
    tm = _row_tile(T)
    nrow = T // tm
    prev, nxt, per = _halo_specs(tm, W, lambda j: j)
    nxt = functools.partial(nxt, last=T // HALO - 1)

    def body(x_ref, xp_ref, xn_ref, d_ref, dn_ref, w_ref, dx_ref, dw_ref):
        j, i = pl.program_id(0), pl.program_id(1)
        w = w_ref[...]
        xp = jnp.where(i > 0, xp_ref[...].astype(F32), 0.0)
        ext = jnp.concatenate([xp, x_ref[...].astype(F32), xn_ref[...].astype(F32)], axis=0)
        d_ext = jnp.concatenate([jnp.zeros((HALO, W), F32), d_ref[...],
                                 jnp.where(i < nrow - 1, dn_ref[...], 0.0)], axis=0)
        c = _conv_ext(ext, w)
        _, vjp = jax.vjp(lambda cc: _qkv_tok(cc, j < 2), c)
        dc, = vjp(d_ext)
        dx_ref[...] = _conv_t_ext(dc, w)[HALO:HALO + tm].astype(BF16)
        _accum(dw_ref, _conv_dw(dc[HALO:HALO + tm], ext, GDN_CONV, tm), i == 0)

    main = lambda j, i: (i, j)
    return pl.pallas_call(
        body, name=name, grid=(3, nrow),
        in_specs=[pl.BlockSpec((tm, W), main), pl.BlockSpec((HALO, W), prev), pl.BlockSpec((HALO, W), nxt),
                  pl.BlockSpec((tm, W), main), pl.BlockSpec((HALO, W), nxt),
                  pl.BlockSpec((GDN_CONV, W), lambda j, i: (0, j))],
        out_specs=[pl.BlockSpec((tm, W), main), pl.BlockSpec((8, W), lambda j, i: (0, j))],
        out_shape=[_sds((T, 3 * W), BF16), _sds((8, 3 * W), F32)],
        compiler_params=_cp(("parallel", "arbitrary")),
    )(proj, proj, proj, dqkv, dqkv, conv_w)


def _gate_tok(ba, a_log, dt_bias):
    z = ba + dt_bias
    softplus = jnp.maximum(z, 0.0) + jnp.log(1.0 + jnp.exp(-jnp.abs(z)))
    lane = lax.broadcasted_iota(jnp.int32, ba.shape, 1)
    return jnp.where(lane < GDN_HEADS, jax.nn.sigmoid(ba), -jnp.exp(a_log) * softplus)


def gate_fwd(ba, a_log, dt_bias, name):
    T, L = ba.shape
    tm = _row_tile(T)

    def body(ba_ref, al_ref, dt_ref, o_ref):
        o_ref[...] = _gate_tok(ba_ref[...], al_ref[...], dt_ref[...])

    vec = pl.BlockSpec((1, L), lambda i: (0, 0))
    return pl.pallas_call(
        body, name=name, grid=(T // tm,),
        in_specs=[pl.BlockSpec((tm, L), lambda i: (i, 0)), vec, vec],
        out_specs=pl.BlockSpec((tm, L), lambda i: (i, 0)), out_shape=_sds((T, L), F32),
        compiler_params=_cp(("parallel",)),
    )(ba, a_log, dt_bias)


def gate_bwd(ba, a_log, dt_bias, dout, name):
    T, L = ba.shape
    tm = _row_tile(T)

    def body(ba_ref, al_ref, dt_ref, d_ref, dba_ref, dal_ref, ddt_ref):
        _, vjp = jax.vjp(_gate_tok, ba_ref[...], al_ref[...], dt_ref[...])
        dba, dal, ddt = vjp(d_ref[...])
        dba_ref[...] = dba.astype(BF16)
        first = pl.program_id(0) == 0
        _accum(dal_ref, dal, first)
        _accum(ddt_ref, ddt, first)

    vec = pl.BlockSpec((1, L), lambda i: (0, 0))
    row = pl.BlockSpec((tm, L), lambda i: (i, 0))
    return pl.pallas_call(
        body, name=name, grid=(T // tm,),
        in_specs=[row, vec, vec, row], out_specs=[row, vec, vec],
        out_shape=[_sds((T, L), BF16), _sds((1, L), F32), _sds((1, L), F32)],
        compiler_params=_cp(("arbitrary",)),
    )(ba, a_log, dt_bias, dout)


def _gdn_chunk(S, q, k, v, gcol, grow, bcol):
    C, dk = q.shape
    r = lax.broadcasted_iota(jnp.int32, (C, C), 0)
    cidx = lax.broadcasted_iota(jnp.int32, (C, C), 1)
    causal = r >= cidx
    strict = r > cidx
    tril = causal.astype(F32)
    triu = (r <= cidx).astype(F32)
    ones = jnp.ones((C, C), F32)
    gc_w = _dot_hi(tril, jnp.broadcast_to(gcol, (C, dk)))
    gc_c = _dot_hi(tril, jnp.broadcast_to(gcol, (C, C)))
    gc_r = _dot_hi(jnp.broadcast_to(grow, (C, C)), triu)
    gt_w = _dot_hi(ones, jnp.broadcast_to(gcol, (C, dk)))
    gt_s = _dot_hi(jnp.ones((dk, C), F32), jnp.broadcast_to(gcol, (C, dk)))
    decay = jnp.where(causal, jnp.exp(jnp.where(causal, gc_c - gc_r, 0.0)), 0.0)
    qs = q * (dk ** -0.5)
    kb = k * bcol
    L = jnp.where(strict, _dot_nt(kb, k) * decay, 0.0)
    eye = (r == cidx).astype(F32)
    tinv = eye - L
    p = L
    n = 2
    while n < C:
        p = _dot_hi(p, p)
        tinv = tinv + _dot_hi(tinv, p)
        n *= 2
    egc = jnp.exp(gc_w)
    u = _dot_hi(tinv, v * bcol)
    w = _dot_hi(tinv, kb * egc)
    intra = jnp.where(causal, _dot_nt(qs, k) * decay, 0.0)
    v_new = u - _dot(w, S)
    o = _dot(qs * egc, S) + _dot(intra, v_new)
    S_new = S * jnp.exp(gt_s) + _dot_tn(k * jnp.exp(gt_w - gc_w), v_new)
    return S_new, o


def gdn_fwd(qkv, gcol, grow, bcol, name):
    T = qkv.shape[0]
    H, D, C = GDN_HEADS, GDN_D, GDN_CHUNK
    W = H * D
    N = T // C

    def body(x_ref, gc_ref, gr_ref, b_ref, o_ref, s_ref, s_scr):
        @pl.when(pl.program_id(0) == 0)
        def _():
            s_scr[...] = jnp.zeros_like(s_scr)
        for h in range(H):
            S = s_scr[h]
            s_ref[h] = S
            q = x_ref[:, h * D:(h + 1) * D]
            k = x_ref[:, W + h * D:W + (h + 1) * D]
            v = x_ref[:, 2 * W + h * D:2 * W + (h + 1) * D]
            S_new, o = _gdn_chunk(S, q, k, v, gc_ref[:, h:h + 1], gr_ref[h:h + 1, :], b_ref[:, h:h + 1])
            s_scr[h] = S_new
            o_ref[:, h * D:(h + 1) * D] = o

    return pl.pallas_call(
        body, name=name, grid=(N,),
        in_specs=[pl.BlockSpec((C, 3 * W), lambda n: (n, 0)), pl.BlockSpec((C, H), lambda n: (n, 0)),
                  pl.BlockSpec((None, H, C), lambda n: (n, 0, 0)), pl.BlockSpec((C, H), lambda n: (n, 0))],
        out_specs=[pl.BlockSpec((C, W), lambda n: (n, 0)), pl.BlockSpec((None, H, D, D), lambda n: (n, 0, 0, 0))],
        out_shape=[_sds((T, W), F32), _sds((N, H, D, D), F32)],
        scratch_shapes=[pltpu.VMEM((H, D, D), F32)],
        compiler_params=_cp(("arbitrary",)),
    )(qkv, gcol, grow, bcol)


def gdn_bwd(qkv, gcol, grow, bcol, states, do, name):
    T = qkv.shape[0]
    H, D, C = GDN_HEADS, GDN_D, GDN_CHUNK
    W = H * D
    N = T // C

    def body(x_ref, gc_ref, gr_ref, b_ref, s_ref, do_ref, dx_ref, dgc_ref, dgr_ref, db_ref, ds_scr):
        @pl.when(pl.program_id(0) == 0)
        def _():
            ds_scr[...] = jnp.zeros_like(ds_scr)
        for h in range(H):
            q = x_ref[:, h * D:(h + 1) * D]
            k = x_ref[:, W + h * D:W + (h + 1) * D]
            v = x_ref[:, 2 * W + h * D:2 * W + (h + 1) * D]
            _, vjp = jax.vjp(_gdn_chunk, s_ref[h], q, k, v, gc_ref[:, h:h + 1], gr_ref[h:h + 1, :], b_ref[:, h:h + 1])
            dS, dq, dk, dv, dgc, dgr, db = vjp((ds_scr[h], do_ref[:, h * D:(h + 1) * D]))
            ds_scr[h] = dS
            dx_ref[:, h * D:(h + 1) * D] = dq
            dx_ref[:, W + h * D:W + (h + 1) * D] = dk
            dx_ref[:, 2 * W + h * D:2 * W + (h + 1) * D] = dv
            dgc_ref[:, h:h + 1] = dgc
            dgr_ref[h:h + 1, :] = dgr
            db_ref[:, h:h + 1] = db

    rev = lambda n: (N - 1 - n, 0)
    rev3 = lambda n: (N - 1 - n, 0, 0)
    return pl.pallas_call(
        body, name=name, grid=(N,),
        in_specs=[pl.BlockSpec((C, 3 * W), rev), pl.BlockSpec((C, H), rev), pl.BlockSpec((None, H, C), rev3),
                  pl.BlockSpec((C, H), rev), pl.BlockSpec((None, H, D, D), lambda n: (N - 1 - n, 0, 0, 0)),
                  pl.BlockSpec((C, W), rev)],
        out_specs=[pl.BlockSpec((C, 3 * W), rev), pl.BlockSpec((C, H), rev), pl.BlockSpec((None, H, C), rev3),
                   pl.BlockSpec((C, H), rev)],
        out_shape=[_sds((T, 3 * W), F32), _sds((T, H), F32), _sds((N, H, C), F32), _sds((T, H), F32)],
        scratch_shapes=[pltpu.VMEM((H, D, D), F32)],
        compiler_params=_cp(("arbitrary",)),
    )(qkv, gcol, grow, bcol, states, do)


def _gated_norm(o, z, gain):
    outs = []
    for h in range(GDN_HEADS):
        oh = o[:, h * GDN_D:(h + 1) * GDN_D]
        y = oh * lax.rsqrt(jnp.mean(oh * oh, axis=-1, keepdims=True) + RMS_EPS) * gain
        outs.append(y * _silu(z[:, h * GDN_D:(h + 1) * GDN_D]))
    return jnp.concatenate(outs, axis=1)


def _window_sums(ext, shift):
    outs = []
    s = ext
    step = 1
    for gi, win in enumerate(POOL_WINDOWS):
        while step < win:
            s = s + shift(s, step)
            step *= 2
        outs.append(s[:, gi * GDN_D:(gi + 1) * GDN_D])
    return jnp.concatenate(outs, axis=1)


def _pool_counts(t0, rows, width):
    t1 = (t0 + 1 + lax.broadcasted_iota(jnp.int32, (rows, width), 0)).astype(F32)
    lane = lax.broadcasted_iota(jnp.int32, (rows, width), 1)
    win = jnp.full((rows, width), float(POOL_WINDOWS[-1]), F32)
    for gi in reversed(range(len(POOL_WINDOWS) - 1)):
        win = jnp.where(lane < (gi + 1) * GDN_D, float(POOL_WINDOWS[gi]), win)
    return jnp.minimum(t1, win)


def even_post_fwd(o, proj, gdn_norm, pool_w, pool_scale, name):
    T = o.shape[0]
    W = GDN_HEADS * GDN_D
    tm = _row_tile(T)
    per = tm // HALO

    def body(o_ref, z_ref, p_ref, pp_ref, gn_ref, pw_ref, ps_ref, out_ref):
        i = pl.program_id(0)
        out_ref[:, :W] = _gated_norm(o_ref[...], z_ref[...].astype(F32), gn_ref[...]).astype(BF16)
        pp = jnp.where(i > 0, pp_ref[...].astype(F32), 0.0)
        ext = jnp.concatenate([pp, p_ref[...].astype(F32)], axis=0)
        pooled = (_window_sums(ext, _shift_down)[HALO:] / _pool_counts(i * tm, tm, W)) - ext[HALO:]
        for gi in range(len(POOL_WINDOWS)):
            sl = slice(gi * GDN_D, (gi + 1) * GDN_D)
            y = _dot(pooled[:, sl], pw_ref[gi]) * ps_ref[:, sl]
            out_ref[:, W + gi * GDN_D:W + (gi + 1) * GDN_D] = y.astype(BF16)

    return pl.pallas_call(
        body, name=name, grid=(T // tm,),
        in_specs=[pl.BlockSpec((tm, W), lambda i: (i, 0)), pl.BlockSpec((tm, W), lambda i: (i, 3)),
                  pl.BlockSpec((tm, W), lambda i: (i, 4)),
                  pl.BlockSpec((HALO, W), lambda i: (jnp.maximum(i * per - 1, 0), 4)),
                  pl.BlockSpec((1, GDN_D), lambda i: (0, 0)),
                  pl.BlockSpec((len(POOL_WINDOWS), GDN_D, GDN_D), lambda i: (0, 0, 0)),
                  pl.BlockSpec((1, W), lambda i: (0, 0))],
        out_specs=pl.BlockSpec((tm, 2 * W), lambda i: (i, 0)),
        out_shape=_sds((T, 2 * W), BF16),
        compiler_params=_cp(("arbitrary",)),
    )(o, proj, proj, proj, gdn_norm, pool_w, pool_scale)


def even_post_bwd(o, proj, gdn_norm, pool_w, pool_scale, dcat, name):
    T = o.shape[0]
    W = GDN_HEADS * GDN_D
    G = len(POOL_WINDOWS)
    tm = _row_tile(T)
    per = tm // HALO
    nrow = T // tm
    last = T // HALO - 1

    def body(o_ref, z_ref, p_ref, pp_ref, gn_ref, pw_ref, ps_ref, d_ref, dn_ref,
             do_ref, dzp_ref, dgn_ref, dpw_ref, dps_ref):
        i = pl.program_id(0)
        first = i == 0
        _, vjp = jax.vjp(_gated_norm, o_ref[...], z_ref[...].astype(F32), gn_ref[...])
        do, dz, dgn = vjp(d_ref[:, :W].astype(F32))
        do_ref[...] = do
        dzp_ref[:, :W] = dz.astype(BF16)
        _accum(dgn_ref, dgn, first)
        pp = jnp.where(first, 0.0, pp_ref[...].astype(F32))
        ext = jnp.concatenate([pp, p_ref[...].astype(F32)], axis=0)
        pooled = (_window_sums(ext, _shift_down)[HALO:] / _pool_counts(i * tm, tm, W)) - ext[HALO:]
        dy_ext = jnp.concatenate([d_ref[:, W:].astype(F32), jnp.where(i < nrow - 1, dn_ref[...].astype(F32), 0.0)], axis=0)
        dys_ext = dy_ext * ps_ref[...]
        dpooled, dscale = [], []
        for gi in range(G):
            sl = slice(gi * GDN_D, (gi + 1) * GDN_D)
            dpooled.append(_dot_nt(dys_ext[:, sl], pw_ref[gi]))
            y = _dot(pooled[:, sl], pw_ref[gi])
            dscale.append(jnp.sum(dy_ext[:tm, sl] * y, axis=0, keepdims=True))
            _accum(dpw_ref.at[gi], _dot_tn(pooled[:, sl], dys_ext[:tm, sl]), first)
        dpooled = jnp.concatenate(dpooled, axis=1)
        _accum(dps_ref, jnp.concatenate(dscale, axis=1), first)
        dmean = dpooled / _pool_counts(i * tm, tm + HALO, W)
        dp = _window_sums(dmean, _shift_up)[:tm] - dpooled[:tm]
        dzp_ref[:, W:] = dp.astype(BF16)

    return pl.pallas_call(
        body, name=name, grid=(nrow,),
        in_specs=[pl.BlockSpec((tm, W), lambda i: (i, 0)), pl.BlockSpec((tm, W), lambda i: (i, 3)),
                  pl.BlockSpec((tm, W), lambda i: (i, 4)),
                  pl.BlockSpec((HALO, W), lambda i: (jnp.maximum(i * per - 1, 0), 4)),
                  pl.BlockSpec((1, GDN_D), lambda i: (0, 0)),
                  pl.BlockSpec((G, GDN_D, GDN_D), lambda i: (0, 0, 0)),
                  pl.BlockSpec((1, W), lambda i: (0, 0)),
                  pl.BlockSpec((tm, 2 * W), lambda i: (i, 0)),
                  pl.BlockSpec((HALO, W), lambda i: (jnp.minimum((i + 1) * per, last), 1))],
        out_specs=[pl.BlockSpec((tm, W), lambda i: (i, 0)), pl.BlockSpec((tm, 2 * W), lambda i: (i, 0)),
                   pl.BlockSpec((1, GDN_D), lambda i: (0, 0)), pl.BlockSpec((G, GDN_D, GDN_D), lambda i: (0, 0, 0)),
                   pl.BlockSpec((1, W), lambda i: (0, 0))],
        out_shape=[_sds((T, W), F32), _sds((T, 2 * W), BF16), _sds((1, GDN_D), F32), _sds((G, GDN_D, GDN_D), F32),
                   _sds((1, W), F32)],
        compiler_params=_cp(("arbitrary",)),
    )(o, proj, proj, proj, gdn_norm, pool_w, pool_scale, dcat, dcat)


def _head_norm(x, gain):
    return x * lax.rsqrt(jnp.mean(x * x, axis=-1, keepdims=True) + RMS_EPS) * gain


def _att_head(q, kp, kc, vp, vc, qn, kn, slope, has_prev):
    B = ATT_BLOCK
    qh = _head_norm(q, qn) * (ATT_DH ** -0.5)
    k = _head_norm(jnp.concatenate([kp, kc], axis=0), kn)
    v = jnp.concatenate([vp, vc], axis=0)
    a = lax.broadcasted_iota(jnp.int32, (B, 2 * B), 0)
    j = lax.broadcasted_iota(jnp.int32, (B, 2 * B), 1)
    rel = B + a - j
    mask = (rel >= 0) & (rel <= B) & ((j >= B) | has_prev)
    s = _dot_nt(qh, k) - slope * rel.astype(F32)
    s = jnp.where(mask, s, NEG)
    m = jnp.max(s, axis=-1, keepdims=True)
    p = jnp.exp(s - m)
    l = jnp.sum(p, axis=-1, keepdims=True)
    o = _dot(p, v) / l
    return o, jnp.broadcast_to(m + jnp.log(l), (B, ATT_DH))


def _att_cols(gi, dil):
    per_tok = 3 * len(DIL_PATTERNS)
    return [functools.partial(lambda r, n, off: (n, r * per_tok + off), off=3 * gi + t) for t in range(3)]


def att_fwd(proj, q_norm, k_norm, gi, name):
    T = proj.shape[0]
    dil = DIL_PATTERNS[gi][1]
    Wd = ATT_HEADS * ATT_DH
    L = T // dil
    nb = L // ATT_BLOCK
    B = ATT_BLOCK
    view = proj.reshape(L, dil * proj.shape[1])
    qc, kc_, vc_ = _att_cols(gi, dil)

    def prev_of(f):
        return lambda r, n: (jnp.maximum(f(r, n)[0] - 1, 0), f(r, n)[1])

    def body(q_ref, kc_ref, kp_ref, vc_ref, vp_ref, qn_ref, kn_ref, o_ref, l_ref):
        has_prev = pl.program_id(1) > 0
        for h in range(ATT_HEADS):
            sl = slice(h * ATT_DH, (h + 1) * ATT_DH)
            slope = dil * (2.0 ** (-8.0 * (h + 1) / ATT_HEADS))
            o, lse = _att_head(q_ref[:, sl].astype(F32), kp_ref[:, sl].astype(F32), kc_ref[:, sl].astype(F32),
                               vp_ref[:, sl], vc_ref[:, sl], qn_ref[...], kn_ref[...], slope, has_prev)
            o_ref[:, sl] = o
            l_ref[:, sl] = lse

    blk = lambda f: pl.BlockSpec((B, Wd), f)
    vec = pl.BlockSpec((1, ATT_DH), lambda r, n: (0, 0))
    out = pl.BlockSpec((B, Wd), lambda r, n: (n, r))
    o, lse = pl.pallas_call(
        body, name=name, grid=(dil, nb),
        in_specs=[blk(qc), blk(kc_), blk(prev_of(kc_)), blk(vc_), blk(prev_of(vc_)), vec, vec],
        out_specs=[out, out], out_shape=[_sds((L, dil * Wd), F32)] * 2,
        compiler_params=_cp(("parallel", "arbitrary")),
    )(view, view, view, view, view, q_norm, k_norm)
    return o.reshape(T, Wd), lse.reshape(T, Wd)


def att_bwd(proj, q_norm, k_norm, do, dlse, gi, name):
    T = proj.shape[0]
    dil = DIL_PATTERNS[gi][1]
    Wd = ATT_HEADS * ATT_DH
    L = T // dil
    nb = L // ATT_BLOCK
    B = ATT_BLOCK
    view = proj.reshape(L, dil * proj.shape[1])
    do_v = do.reshape(L, dil * Wd)
    dl_v = dlse.reshape(L, dil * Wd)
    qc, kc_, vc_ = _att_cols(gi, dil)

    def cur_of(f):
        return lambda r, n: (jnp.minimum(f(r, n)[0], nb - 1), f(r, n)[1])

    def prev_of(f):
        return lambda r, n: (jnp.clip(f(r, n)[0] - 1, 0, nb - 1), f(r, n)[1])

    def body(q_ref, kc_ref, kp_ref, vc_ref, vp_ref, qn_ref, kn_ref, do_ref, dl_ref,
             dq_ref, dk_ref, dv_ref, dqn_ref, dkn_ref, ck_scr, cv_scr):
        r, n = pl.program_id(0), pl.program_id(1)
        has_prev = n > 0
        first = (r == 0) & (n == 0)

        @pl.when(n < nb)
        def _():
            dqn = jnp.zeros((1, ATT_DH), F32)
            dkn = jnp.zeros((1, ATT_DH), F32)
            for h in range(ATT_HEADS):
                sl = slice(h * ATT_DH, (h + 1) * ATT_DH)
                slope = dil * (2.0 ** (-8.0 * (h + 1) / ATT_HEADS))
                f = functools.partial(_att_head, slope=slope, has_prev=has_prev)
                _, vjp = jax.vjp(f, q_ref[:, sl].astype(F32), kp_ref[:, sl].astype(F32), kc_ref[:, sl].astype(F32),
                                 vp_ref[:, sl].astype(F32), vc_ref[:, sl].astype(F32), qn_ref[...], kn_ref[...])
                dq, dkp, dkc, dvp, dvc, dqn_h, dkn_h = vjp((do_ref[:, sl], dl_ref[:, sl]))
                dq_ref[:, sl] = dq.astype(BF16)
                dk_ref[:, sl] = (jnp.where(has_prev, ck_scr[:, sl] + dkp, 0.0)).astype(BF16)
                dv_ref[:, sl] = (jnp.where(has_prev, cv_scr[:, sl] + dvp, 0.0)).astype(BF16)
                ck_scr[:, sl] = dkc
                cv_scr[:, sl] = dvc
                dqn += dqn_h
                dkn += dkn_h
            _accum(dqn_ref, dqn, first)
            _accum(dkn_ref, dkn, first)

        @pl.when(n == nb)
        def _():
            dk_ref[...] = ck_scr[...].astype(BF16)
            dv_ref[...] = cv_scr[...].astype(BF16)

    blk = lambda f: pl.BlockSpec((B, Wd), f)
    vec = pl.BlockSpec((1, ATT_DH), lambda r, n: (0, 0))
    cur = lambda r, n: (jnp.minimum(n, nb - 1), r)
    kv_out = lambda r, n: (jnp.maximum(n - 1, 0), r)
    dq, dk, dv, dqn, dkn = pl.pallas_call(
        body, name=name, grid=(dil, nb + 1),
        in_specs=[blk(cur_of(qc)), blk(cur_of(kc_)), blk(prev_of(kc_)), blk(cur_of(vc_)), blk(prev_of(vc_)), vec, vec,
                  blk(cur), blk(cur)],
        out_specs=[blk(cur), blk(kv_out), blk(kv_out), vec, vec],
        out_shape=[_sds((L, dil * Wd), BF16)] * 3 + [_sds((1, ATT_DH), F32)] * 2,
        scratch_shapes=[pltpu.VMEM((B, Wd), F32), pltpu.VMEM((B, Wd), F32)],
        compiler_params=_cp(("arbitrary", "arbitrary")),
    )(view, view, view, view, view, q_norm, k_norm, do_v, dl_v)
    return dq.reshape(T, Wd), dk.reshape(T, Wd), dv.reshape(T, Wd), dqn, dkn


def _merge(o0, o1, o2, l0, l1, l2):
    m = jnp.maximum(jnp.maximum(l0, l1), l2)
    e0, e1, e2 = jnp.exp(l0 - m), jnp.exp(l1 - m), jnp.exp(l2 - m)
    return (e0 * o0 + e1 * o1 + e2 * o2) / (e0 + e1 + e2)


def merge_fwd(outs, lses, name):
    T, Wd = outs[0].shape
    tm = _row_tile(T)

    def body(o0, o1, o2, l0, l1, l2, out_ref):
        out_ref[...] = _merge(o0[...], o1[...], o2[...], l0[...], l1[...], l2[...]).astype(BF16)

    row = pl.BlockSpec((tm, Wd), lambda i: (i, 0))
    return pl.pallas_call(body, name=name, grid=(T // tm,), in_specs=[row] * 6, out_specs=row,
                          out_shape=_sds((T, Wd), BF16), compiler_params=_cp(("parallel",)))(*outs, *lses)


def merge_bwd(outs, lses, d, name):
    T, Wd = outs[0].shape
    tm = 256 if T % 256 == 0 else T

    def body(o0, o1, o2, l0, l1, l2, d_ref, *douts):
        _, vjp = jax.vjp(_merge, o0[...], o1[...], o2[...], l0[...], l1[...], l2[...])
        for ref, val in zip(douts, vjp(d_ref[...].astype(F32))):
            ref[...] = val

    row = pl.BlockSpec((tm, Wd), lambda i: (i, 0))
    res = pl.pallas_call(body, name=name, grid=(T // tm,), in_specs=[row] * 7, out_specs=[row] * 6,
                         out_shape=[_sds((T, Wd), F32)] * 6, compiler_params=_cp(("parallel",)))(*outs, *lses, d)
    return res[:3], res[3:]


def loss_head(y, target, name):
    T, D = y.shape
    tm = _row_tile(T)

    def body(y_ref, t_ref, l_ref, dy_ref):
        err = y_ref[...] - t_ref[...]
        dy_ref[...] = err * (1.0 / D)
        part = 0.5 * jnp.sum(jnp.sum(err * err, axis=1, keepdims=True) * (1.0 / D), axis=0, keepdims=True)
        _accum(l_ref, jnp.broadcast_to(part, (1, 128)), pl.program_id(0) == 0)

    row = pl.BlockSpec((tm, D), lambda i: (i, 0))
    return pl.pallas_call(body, name=name, grid=(T // tm,), in_specs=[row, row],
                          out_specs=[pl.BlockSpec((1, 128), lambda i: (0, 0)), row],
                          out_shape=[_sds((1, 128), F32), _sds((T, D), F32)],
                          compiler_params=_cp(("arbitrary",)))(y, target)


def ada_fwd(c_all, w, b, name):
    depth, D, n = w.shape

    def body(c_ref, w_ref, b_ref, o_ref):
        o_ref[...] = _dot(_silu(c_ref[...]), w_ref[...]) + b_ref[...]

    return pl.pallas_call(
        body, name=name, grid=(depth,),
        in_specs=[pl.BlockSpec((N_DEV, D), lambda i: (0, 0)), pl.BlockSpec((None, D, n), lambda i: (i, 0, 0)),
                  pl.BlockSpec((None, 1, n), lambda i: (i, 0, 0))],
        out_specs=pl.BlockSpec((None, N_DEV, n), lambda i: (i, 0, 0)),
        out_shape=_sds((depth, N_DEV, n), F32), compiler_params=_cp(("parallel",)),
    )(c_all, w, b)


def ada_bwd(c_all, dmod, name):
    depth, _, n = dmod.shape
    D = c_all.shape[1]

    def body(c_ref, d_ref, o_ref):
        o_ref[...] = _dot_tn(_silu(c_ref[...]), d_ref[...])

    return pl.pallas_call(
        body, name=name, grid=(depth,),
        in_specs=[pl.BlockSpec((N_DEV, D), lambda i: (0, 0)), pl.BlockSpec((None, N_DEV, n), lambda i: (i, 0, 0))],
        out_specs=pl.BlockSpec((None, D, n), lambda i: (i, 0, 0)),
        out_shape=_sds((depth, D, n), F32), compiler_params=_cp(("parallel",)),
    )(c_all, dmod)


def adamw(w, m, v, gparts, name):
    R, C = w.shape
    k = gparts.shape[0]
    tr = R
    for cand in (512, 256, 128, 64, 32, 16, 8):
        if R % cand == 0 and cand * C * 4 <= 2 * 1024 * 1024:
            tr = cand
            break
    bc1 = 1.0 - ADAM_B1 ** ADAM_STEP
    bc2 = 1.0 - ADAM_B2 ** ADAM_STEP

    def body(w_ref, m_ref, v_ref, gp_ref, g_ref, d_ref, nm_ref, nv_ref):
        g = gp_ref[0]
        for q in range(1, k):
            g = g + gp_ref[q]
        nm = ADAM_B1 * m_ref[...] + (1.0 - ADAM_B1) * g
        nv = ADAM_B2 * v_ref[...] + (1.0 - ADAM_B2) * (g * g)
        g_ref[...] = g
        nm_ref[...] = nm
        nv_ref[...] = nv
        d_ref[...] = -ADAM_LR * ((nm / bc1) / (jnp.sqrt(nv / bc2) + ADAM_EPS) + ADAM_WD * w_ref[...])

    row = pl.BlockSpec((tr, C), lambda i: (i, 0))
    return pl.pallas_call(
        body, name=name, grid=(R // tr,),
        in_specs=[row, row, row, pl.BlockSpec((k, tr, C), lambda i: (0, i, 0))],
        out_specs=[row] * 4, out_shape=[_sds((R, C), F32)] * 4, compiler_params=_cp(("parallel",)),
    )(w, m, v, gparts)


def _mesh_pos():
    return lax.axis_index("x"), lax.axis_index("y"), lax.axis_index("c")


def _other_chips(x, y):
    return [(1 - x, y), (x, 1 - y), (1 - x, 1 - y)]


_ANY = pl.BlockSpec(memory_space=pl.ANY)


def all_gather(shards, name):
    n = len(shards)

    def body(*refs):
        ins, outs = refs[:n], refs[n:2 * n]
        send_sems, recv_sems, local_sems = refs[2 * n:]
        x, y, c = _mesh_pos()
        me, sibling = (x, y, c), (x, y, 1 - c)
        chips = _other_chips(x, y)

        def slot(p):
            return 4 * p[0] + 2 * p[1] + p[2]

        def copy(a, k, block, to, src=None):
            dst = outs[a].at[slot(block)]
            return pltpu.make_async_remote_copy(
                src_ref=dst if src is None else src, dst_ref=dst, send_sem=send_sems.at[a, k],
                recv_sem=recv_sems.at[a, k], device_id=to, device_id_type=MESH)

        mine = [pltpu.make_async_copy(ins[a], outs[a].at[slot(me)], local_sems.at[a]) for a in range(n)]
        for cp in mine:
            cp.start()
        first = []
        for a in range(n):
            first.append(copy(a, 0, me, sibling, src=ins[a]))
            first += [copy(a, 1 + j, me, (*chip, c), src=ins[a]) for j, chip in enumerate(chips)]
        for cp in first:
            cp.start()
        passed = []
        for j, chip in enumerate(chips):
            for a in range(n):
                copy(a, 1 + j, (*chip, c), me).wait_recv()
                cp = copy(a, 4 + j, (*chip, c), sibling)
                cp.start()
                passed.append(cp)
        for a in range(n):
            copy(a, 0, sibling, me).wait_recv()
            for j, chip in enumerate(chips):
                copy(a, 4 + j, (*chip, 1 - c), me).wait_recv()
        for cp in first + passed:
            cp.wait_send()
        for cp in mine:
            cp.wait()

    return pl.pallas_call(
        body, name=name, in_specs=[_ANY] * n, out_specs=[_ANY] * n,
        out_shape=[_sds((N_DEV,) + s.shape, s.dtype) for s in shards],
        scratch_shapes=[pltpu.SemaphoreType.DMA((n, 7)), pltpu.SemaphoreType.DMA((n, 7)), pltpu.SemaphoreType.DMA((n,))],
        compiler_params=pltpu.CompilerParams(has_side_effects=True),
    )(*shards)


def exchange_pair(stacked, name):
    n = len(stacked)

    def body(*refs):
        ins, outs = refs[:n], refs[n:2 * n]
        send_sems, recv_sems = refs[2 * n:]
        x, y, c = _mesh_pos()
        copies = [pltpu.make_async_remote_copy(
            src_ref=ins[a].at[2 * q + (1 - c)], dst_ref=outs[a].at[q], send_sem=send_sems.at[a, q],
            recv_sem=recv_sems.at[a, q], device_id=(x, y, 1 - c), device_id_type=MESH)
            for a in range(n) for q in range(4)]
        for cp in copies:
            cp.start()
        for cp in copies:
            cp.wait()

    return pl.pallas_call(
        body, name=name, in_specs=[_ANY] * n, out_specs=[_ANY] * n,
        out_shape=[_sds((4,) + s.shape[1:], s.dtype) for s in stacked],
        scratch_shapes=[pltpu.SemaphoreType.DMA((n, 4)), pltpu.SemaphoreType.DMA((n, 4))],
        compiler_params=pltpu.CompilerParams(has_side_effects=True),
    )(*stacked)


def pair_add(stacked, got, c_idx, name):
    _, R, C = stacked.shape
    tr = R
    for cand in (512, 256, 128, 64, 32, 16, 8):
        if R % cand == 0 and cand * C * 4 <= 2 * 1024 * 1024:
            tr = cand
            break

    def body(c_ref, s_ref, g_ref, o_ref):
        o_ref[...] = s_ref[...] + g_ref[...]

    return pl.pallas_call(
        body, name=name,
        grid_spec=pltpu.PrefetchScalarGridSpec(
            num_scalar_prefetch=1, grid=(4, R // tr),
            in_specs=[pl.BlockSpec((None, tr, C), lambda q, i, c_ref: (2 * q + c_ref[0], i, 0)),
                      pl.BlockSpec((None, tr, C), lambda q, i, c_ref: (q, i, 0))],
            out_specs=pl.BlockSpec((None, tr, C), lambda q, i, c_ref: (q, i, 0))),
        out_shape=_sds((4, R, C), stacked.dtype),
        compiler_params=_cp(("parallel", "parallel")),
    )(c_idx, stacked, got)


def exchange_chips(parts, name):
    n = len(parts)

    def body(*refs):
        ins, outs = refs[:n], refs[n:2 * n]
        send_sems, recv_sems, local_sems = refs[2 * n:]
        x, y, c = _mesh_pos()
        myq = 2 * x + y
        mine = [pltpu.make_async_copy(ins[a].at[myq], outs[a].at[myq], local_sems.at[a]) for a in range(n)]
        for cp in mine:
            cp.start()
        copies = [pltpu.make_async_remote_copy(
            src_ref=ins[a].at[2 * chip[0] + chip[1]], dst_ref=outs[a].at[myq], send_sem=send_sems.at[a, j],
            recv_sem=recv_sems.at[a, j], device_id=(*chip, c), device_id_type=MESH)
            for a in range(n) for j, chip in enumerate(_other_chips(x, y))]
        for cp in copies:
            cp.start()
        for cp in copies:
            cp.wait()
        for cp in mine:
            cp.wait()

    return pl.pallas_call(
        body, name=name, in_specs=[_ANY] * n, out_specs=[_ANY] * n,
        out_shape=[_sds(s.shape, s.dtype) for s in parts],
        scratch_shapes=[pltpu.SemaphoreType.DMA((n, 3)), pltpu.SemaphoreType.DMA((n, 3)), pltpu.SemaphoreType.DMA((n,))],
        compiler_params=pltpu.CompilerParams(has_side_effects=True),
    )(*parts)


def _pad_lanes(v, start, width=128):
    return jnp.pad(v.astype(F32), (start, width - start - v.shape[0]))[None]


def _layer_fwd(i, x, mod, P):
    T = x.shape[0]
    sh_m, sc_m, g_m, sh_f, sc_f, g_f = [mod[k:k + 1] for k in range(6)]
    sv = {"x0": x}
    tag = f"l{i}"
    if i % 2 == 0:
        e = i // 2
        h, proj = norm_proj(x, P["norm_mix"][i:i + 1], sc_m, sh_m, P["ev_main"][e], 1280, tag + "_in")
        ba = matmul_nn(h, P["ev_ba"][e], F32, tag + "_ba")
        al = _pad_lanes(P["gdn_a_log"][e], GDN_HEADS)
        dt = _pad_lanes(P["gdn_dt_bias"][e], GDN_HEADS)
        gate = gate_fwd(ba, al, dt, tag + "_gate")
        beta, g = gate[:, :GDN_HEADS], gate[:, GDN_HEADS:2 * GDN_HEADS]
        grow = g.reshape(T // GDN_CHUNK, GDN_CHUNK, GDN_HEADS).transpose(0, 2, 1)
        qkv = gdn_prep_fwd(proj, P["gdn_conv_w"][e], tag + "_prep")
        o, states = gdn_fwd(qkv, g, grow, beta, tag + "_gdn")
        cat = even_post_fwd(o, proj, P["gdn_norm"][e:e + 1], P["pool_w"][e], P["pool_scale"][e:e + 1], tag + "_post")
        x1, y_m = proj_res(cat, P["ev_w_out"][e], x, g_m, tag + "_out")
        sv.update(h=h, proj=proj, ba=ba, al=al, dt=dt, beta=beta, g=g, grow=grow, qkv=qkv, o=o, states=states, a_m=cat)
    else:
        od = i // 2
        qn, kn = P["att_q_norm"][od:od + 1], P["att_k_norm"][od:od + 1]
        h, proj = norm_proj(x, P["norm_mix"][i:i + 1], sc_m, sh_m, P["od_w_in"][od], 1536, tag + "_in")
        res = [att_fwd(proj, qn, kn, gi, f"{tag}_att{gi}") for gi in range(len(DIL_PATTERNS))]
        outs, lses = [r[0] for r in res], [r[1] for r in res]
        merged = merge_fwd(outs, lses, tag + "_merge")
        x1, y_m = proj_res(merged, P["od_w_out"][od], x, g_m, tag + "_out")
        sv.update(h=h, proj=proj, outs=outs, lses=lses, a_m=merged)
    hf, up = norm_proj(x1, P["norm_ffn"][i:i + 1], sc_f, sh_f, P["ffn_w_up"][i], 1408, tag + "_up")
    a = ffn_mid_fwd(up, P["ffn_conv_w"][i], P["ffn_conv_b"][i:i + 1], tag + "_mid")
    x2, y_f = proj_res(a, P["ffn_w_down"][i], x1, g_f, tag + "_down")
    sv.update(y_m=y_m, x1=x1, hf=hf, up=up, a_f=a, y_f=y_f)
    return x2, sv


def _layer_bwd(i, dx2, sv, mod, P):
    T = dx2.shape[0]
    sh_m, sc_m, g_m, sh_f, sc_f, g_f = [mod[k:k + 1] for k in range(6)]
    tag = f"b{i}"
    G = {}
    F = P["ffn_w_down"].shape[1]
    dy, da, dg_f = bwd_out(dx2, g_f, sv["y_f"], P["ffn_w_down"][i], F // 2, tag + "_down")
    G["ffn_w_down"] = matmul_tn(sv["a_f"], dy, F // 2, dy.shape[1], tag + "_wdown")
    dgate, dval, dcw, dcb = ffn_mid_bwd(sv["up"], da, P["ffn_conv_w"][i], P["ffn_conv_b"][i:i + 1], tag + "_mid")
    G["ffn_conv_w"], G["ffn_conv_b"] = dcw[:FFN_CONV], dcb[0]
    dx1, dnf, dsc_f, dsh_f = bwd_in([dgate, dval], P["ffn_w_up"][i], [0, 2], F // 2, None,
                                    sv["x1"], P["norm_ffn"][i:i + 1], sc_f, sh_f, dx2, tag + "_up")
    G["norm_ffn"] = dnf[0]
    x0, h, proj = sv["x0"], sv["h"], sv["proj"]
    D = x0.shape[1]
    G["ffn_w_up"] = jnp.concatenate([matmul_tn(sv["hf"], d, D, F // 2, f"{tag}_wup{k}")
                                     for k, d in enumerate((dgate, dval))], axis=1)
    gain = P["norm_mix"][i:i + 1]
    if i % 2 == 0:
        e = i // 2
        W = GDN_HEADS * GDN_D
        dy, dcat, dg_m = bwd_out(dx1, g_m, sv["y_m"], P["ev_w_out"][e], 1024, tag + "_out")
        G["ev_w_out"] = matmul_tn(sv["a_m"], dy, 1024, D, tag + "_wout")
        do, dzp, dgn, dpw, dps = even_post_bwd(sv["o"], proj, P["gdn_norm"][e:e + 1], P["pool_w"][e],
                                               P["pool_scale"][e:e + 1], dcat, tag + "_post")
        G["gdn_norm"], G["pool_w"], G["pool_scale"] = dgn[0], dpw, dps[0]
        dqkv, dgc, dgr, dbeta = gdn_bwd(sv["qkv"], sv["g"], sv["grow"], sv["beta"], sv["states"], do, tag + "_gdn")
        dg = dgc + dgr.transpose(0, 2, 1).reshape(T, GDN_HEADS)
        dgate128 = jnp.concatenate([dbeta, dg, jnp.zeros((T, 128 - 2 * GDN_HEADS), F32)], axis=1)
        dba, dal, ddt = gate_bwd(sv["ba"], sv["al"], sv["dt"], dgate128, tag + "_gate")
        G["gdn_a_log"], G["gdn_dt_bias"] = dal[0, GDN_HEADS:2 * GDN_HEADS], ddt[0, GDN_HEADS:2 * GDN_HEADS]
        dqkv_raw, dconv = gdn_prep_bwd(proj, dqkv, P["gdn_conv_w"][e], tag + "_prep")
        G["gdn_conv_w"] = dconv[:GDN_CONV]
        dh = bwd_in([dba], P["ev_ba"][e], [0], 128, None, None, None, None, None, None, tag + "_in0")
        dh = bwd_in([dqkv_raw], P["ev_main"][e], [0], W, dh, None, None, None, None, None, tag + "_in1")
        dx0, dnm, dsc_m, dsh_m = bwd_in([dzp], P["ev_main"][e], [3], W, dh, x0, gain, sc_m, sh_m, dx1, tag + "_in2")
        gw_qkv = matmul_tn(h, dqkv_raw, 1024, W, tag + "_win1")
        gw_zp = matmul_tn(h, dzp, 1024, W, tag + "_win2")
        gw_ba = matmul_tn(h, dba, 1024, 128, tag + "_win0")
        G["ev_w_in"] = jnp.concatenate([gw_qkv, gw_zp[:, :W], gw_ba[:, :2 * GDN_HEADS], gw_zp[:, W:]], axis=1)
    else:
        od = i // 2
        qn, kn = P["att_q_norm"][od:od + 1], P["att_k_norm"][od:od + 1]
        dy, dmerged, dg_m = bwd_out(dx1, g_m, sv["y_m"], P["od_w_out"][od], 1024, tag + "_out")
        G["od_w_out"] = matmul_tn(sv["a_m"], dy, 1024, D, tag + "_wout")
        douts, dlses = merge_bwd(sv["outs"], sv["lses"], dmerged, tag + "_merge")
        dh, gws = None, []
        dqn_sum, dkn_sum = 0.0, 0.0
        ng = len(DIL_PATTERNS)
        for gi in range(ng):
            dq, dk, dv, dqn, dkn = att_bwd(proj, qn, kn, douts[gi], dlses[gi], gi, f"{tag}_att{gi}")
            dqn_sum, dkn_sum = dqn_sum + dqn[0], dkn_sum + dkn[0]
            gws += [matmul_tn(h, d, 1024, D, f"{tag}_win{gi}{k}") for k, d in enumerate((dq, dk, dv))]
            cols = [3 * gi, 3 * gi + 1, 3 * gi + 2]
            if gi < ng - 1:
                dh = bwd_in([dq, dk, dv], P["od_w_in"][od], cols, D, dh, None, None, None, None, None, f"{tag}_in{gi}")
            else:
                dx0, dnm, dsc_m, dsh_m = bwd_in([dq, dk, dv], P["od_w_in"][od], cols, D, dh, x0, gain, sc_m, sh_m, dx1,
                                                f"{tag}_in{gi}")
        G["att_q_norm"], G["att_k_norm"] = dqn_sum, dkn_sum
        G["od_w_in"] = jnp.concatenate(gws, axis=1)
    G["norm_mix"] = dnm[0]
    dmod = jnp.concatenate([dsh_m, dsc_m, dg_m, dsh_f, dsc_f, dg_f], axis=0)
    return dx0, dmod, G


_PER_LAYER = ("norm_mix", "norm_ffn", "ffn_w_up", "ffn_conv_w", "ffn_conv_b", "ffn_w_down")
_PER_EVEN = ("ev_w_in", "ev_w_out", "gdn_conv_w", "gdn_a_log", "gdn_dt_bias", "gdn_norm", "pool_w", "pool_scale")
_PER_ODD = ("od_w_in", "od_w_out", "att_q_norm", "att_k_norm")


def device_step(x, mod, target, P):
    saved = []
    for i in range(DEPTH):
        x, sv = _layer_fwd(i, x, mod[i], P)
        saved.append(sv)
    loss, dx = loss_head(x, target, "loss")
    layer_grads, dmods = [None] * DEPTH, [None] * DEPTH
    for i in reversed(range(DEPTH)):
        dx, dmods[i], layer_grads[i] = _layer_bwd(i, dx, saved[i], mod[i], P)
    G = {k: jnp.stack([layer_grads[i][k] for i in range(DEPTH)]) for k in _PER_LAYER}
    G.update({k: jnp.stack([layer_grads[i][k] for i in range(0, DEPTH, 2)]) for k in _PER_EVEN})
    G.update({k: jnp.stack([layer_grads[i][k] for i in range(1, DEPTH, 2)]) for k in _PER_ODD})
    return loss, dx, jnp.stack(dmods), G


_WEIGHTS = ("ada_w", "ada_b", "norm_mix", "norm_ffn", "ev_w_in", "ev_w_out", "gdn_conv_w", "gdn_a_log", "gdn_dt_bias",
            "gdn_norm", "pool_w", "pool_scale", "od_w_in", "od_w_out", "att_q_norm", "att_k_norm", "ffn_w_up",
            "ffn_conv_w", "ffn_conv_b", "ffn_w_down")
_COL_SHARDED = ("ev_w_in", "od_w_in", "ffn_w_up")
_ROW_SHARDED = ("ev_w_out", "od_w_out", "ffn_w_down")
_SMALL_SHARDED = ("gdn_conv_w", "ffn_conv_w")
_REPLICATED = ("ada_b", "norm_mix", "norm_ffn", "gdn_a_log", "gdn_dt_bias", "gdn_norm", "pool_w", "pool_scale",
               "att_q_norm", "att_k_norm", "ffn_conv_b")
PACK_LANES = 128
PACK_ROWS = 8


def _pack(arrays):
    flat = jnp.concatenate([a.reshape(-1).astype(F32) for a in arrays])
    unit = PACK_LANES * PACK_ROWS
    padded = -(-flat.shape[0] // unit) * unit
    return jnp.pad(flat, (0, padded - flat.shape[0])).reshape(-1, PACK_LANES)


def _unpack(packed, shapes, lead=()):
    flat = packed.reshape(lead + (-1,))
    out, pos = [], 0
    for s in shapes:
        n = math.prod(s)
        out.append(flat[..., pos:pos + n].reshape(lead + tuple(s)))
        pos += n
    return out


def _unshard_cols(g):
    _, L, R, n = g.shape
    return g.transpose(1, 2, 0, 3).reshape(L, R, N_DEV * n)


def _shard_cols(full):
    L, R, N = full.shape
    n = N // N_DEV
    return full.reshape(L * R, N_DEV, n).transpose(1, 0, 2)


def _shard_rows(full):
    L, R, C = full.shape
    r = R // N_DEV
    return full.reshape(L, N_DEV, r, C).transpose(1, 0, 2, 3).reshape(N_DEV, L * r, C)


def kernel(x, c, ada_w, ada_b, norm_mix, norm_ffn, ev_w_in, ev_w_out, gdn_conv_w, gdn_a_log, gdn_dt_bias, gdn_norm, pool_w, pool_scale, od_w_in, od_w_out, att_q_norm, att_k_norm, ffn_w_up, ffn_conv_w, ffn_conv_b, ffn_w_down, loss_target, m_ada_w, m_ada_b, m_norm_mix, m_norm_ffn, m_ev_w_in, m_ev_w_out, m_gdn_conv_w, m_gdn_a_log, m_gdn_dt_bias, m_gdn_norm, m_pool_w, m_pool_scale, m_od_w_in, m_od_w_out, m_att_q_norm, m_att_k_norm, m_ffn_w_up, m_ffn_conv_w, m_ffn_conv_b, m_ffn_w_down, v_ada_w, v_ada_b, v_norm_mix, v_norm_ffn, v_ev_w_in, v_ev_w_out, v_gdn_conv_w, v_gdn_a_log, v_gdn_dt_bias, v_gdn_norm, v_pool_w, v_pool_scale, v_od_w_in, v_od_w_out, v_att_q_norm, v_att_k_norm, v_ffn_w_up, v_ffn_conv_w, v_ffn_conv_b, v_ffn_w_down):
    args = locals()
    Wl = {k: args[k] for k in _WEIGHTS}
    Ml = {k: args["m_" + k] for k in _WEIGHTS}
    Vl = {k: args["v_" + k] for k in _WEIGHTS}
    mx, my, mc = _mesh_pos()
    dev = 4 * mx + 2 * my + mc
    T, D = x.shape[1], x.shape[2]
    x2d, tgt = x.reshape(T, D), loss_target.reshape(T, D)

    small_shapes = [c.shape] + [Wl[k].shape for k in _SMALL_SHARDED]
    big = list(_COL_SHARDED + _ROW_SHARDED)
    gathered = all_gather([_pack([c] + [Wl[k] for k in _SMALL_SHARDED])] + [Wl[k].astype(BF16) for k in big], "gather_w")
    c_all, conv_g, conv_f = _unpack(gathered[0], small_shapes, lead=(N_DEV,))
    c_all = c_all.reshape(N_DEV, D)
    full = dict(zip(big, gathered[1:]))
    P = {k: Wl[k] for k in _REPLICATED}
    P["gdn_conv_w"], P["ffn_conv_w"] = _unshard_cols(conv_g), _unshard_cols(conv_f)
    for k in _COL_SHARDED:
        P[k] = _unshard_cols(full[k])
    for k in _ROW_SHARDED:
        g = full[k]
        P[k] = g.transpose(1, 0, 2, 3).reshape(g.shape[1], N_DEV * g.shape[2], g.shape[3])
    W = GDN_HEADS * GDN_D
    ev = P.pop("ev_w_in")
    P["ev_main"] = jnp.concatenate([ev[:, :, :4 * W], ev[:, :, 4 * W + 2 * GDN_HEADS:]], axis=2)
    P["ev_ba"] = jnp.pad(ev[:, :, 4 * W:4 * W + 2 * GDN_HEADS], ((0, 0), (0, 0), (0, 128 - 2 * GDN_HEADS)))

    n_ada = ada_w.shape[2]
    b_cols = lax.dynamic_slice_in_dim(ada_b, dev * n_ada, n_ada, axis=1)
    mod_cols = ada_fwd(c_all, ada_w, b_cols[:, None, :], "ada_fwd")
    mod_all, = all_gather([mod_cols], "gather_mod")
    mod = lax.dynamic_index_in_dim(mod_all, dev, axis=2, keepdims=False)
    mod = mod.transpose(1, 0, 2).reshape(DEPTH, 6, D)

    loss, dx, dmod, G = device_step(x2d, mod, tgt, P)
    loss = lax.psum(loss[0, 0], ("x", "y", "c"))

    G["ada_b"] = dmod.reshape(DEPTH, 6 * D)
    small = list(_REPLICATED) + list(_SMALL_SHARDED)
    parts_all, = all_gather([_pack([G[k] for k in small])], "gather_small")
    zeros = {k: jnp.zeros_like(G[k]) for k in _SMALL_SHARDED}
    packs = [_pack([src[k] for k in _REPLICATED] + [zeros[k] for k in _SMALL_SHARDED]) for src in (Wl, Ml, Vl)]
    res = adamw(*packs, parts_all, "adamw_small")
    shapes = [G[k].shape for k in small]
    out_g, out_d, out_m, out_v = ({k: a for k, a in zip(small, _unpack(r, shapes))} for r in res)
    dmod_all = _unpack(parts_all, shapes, lead=(N_DEV,))[0].reshape(N_DEV, DEPTH, 6 * D)
    dm_cols = lax.dynamic_slice_in_dim(dmod_all, dev * n_ada, n_ada, axis=2).transpose(1, 0, 2)
    g_ada = ada_bwd(c_all, dm_cols, "ada_bwd")

    def flat2(a):
        return a.reshape(-1, a.shape[-1])

    loc = {k: lax.dynamic_slice_in_dim(out_g[k], dev * Wl[k].shape[-1], Wl[k].shape[-1], axis=out_g[k].ndim - 1)
           for k in _SMALL_SHARDED}
    res = adamw(*[_pack([src[k] for k in _SMALL_SHARDED]) for src in (Wl, Ml, Vl)],
                _pack([loc[k] for k in _SMALL_SHARDED])[None], "adamw_conv")
    for dst, r in zip((out_g, out_d, out_m, out_v), res):
        dst.update(zip(_SMALL_SHARDED, _unpack(r, [Wl[k].shape for k in _SMALL_SHARDED])))
    res = adamw(flat2(ada_w), flat2(m_ada_w), flat2(v_ada_w), flat2(g_ada)[None], "adamw_ada")
    for dst, r in zip((out_g, out_d, out_m, out_v), res):
        dst["ada_w"] = r.reshape(ada_w.shape)

    stacked = [_shard_cols(G[k]) for k in _COL_SHARDED] + [_shard_rows(G[k]) for k in _ROW_SHARDED]
    got = exchange_pair(stacked, "rs_pair")
    c_idx = jnp.reshape(mc, (1,)).astype(jnp.int32)
    chip_parts = [pair_add(s, g, c_idx, f"rs_add_{k}") for s, g, k in zip(stacked, got, big)]
    parts = exchange_chips(chip_parts, "rs_chips")
    for k, p in zip(big, parts):
        res = adamw(flat2(Wl[k]), flat2(Ml[k]), flat2(Vl[k]), p, "adamw_" + k)
        for dst, r in zip((out_g, out_d, out_m, out_v), res):
            dst[k] = r.reshape(Wl[k].shape)

    return (loss, dx.reshape(x.shape), *[out_g[k] for k in _WEIGHTS], *[out_d[k] for k in _WEIGHTS],
            *[out_m[k] for k in _WEIGHTS], *[out_v[k] for k in _WEIGHTS])
```

```python
import functools
import math

import jax
import jax.numpy as jnp
from jax import lax
from jax.experimental import pallas as pl
from jax.experimental.pallas import tpu as pltpu

F32 = jnp.float32
BF16 = jnp.bfloat16
HI = lax.Precision.HIGHEST
MESH = pl.DeviceIdType.MESH

N_DEV = 8
RMS_EPS = 1e-6
DEPTH = 4
GDN_HEADS = 4
GDN_D = 128
GDN_CHUNK = 64
GDN_CONV = 4
POOL_WINDOWS = (2, 4, 8, 16)
DIL_PATTERNS = ((128, 1), (512, 4), (2048, 16))
ATT_HEADS = 8
ATT_DH = 128
ATT_BLOCK = 128
FFN_CONV = 3
ADAM_LR, ADAM_B1, ADAM_B2, ADAM_EPS, ADAM_WD, ADAM_STEP = 0.001, 0.9, 0.999, 1e-08, 0.01, 10

HALO = 16
NEG = -1e30
VMEM_LIMIT_BYTES = 56 * 1024 * 1024


def _cp(sem=None, **kw):
    return pltpu.CompilerParams(dimension_semantics=sem, vmem_limit_bytes=VMEM_LIMIT_BYTES, **kw)


def _sds(shape, dtype):
    return jax.ShapeDtypeStruct(tuple(shape), dtype)


def _dot(a, b):
    return jnp.dot(a.astype(BF16), b.astype(BF16), preferred_element_type=F32)


def _dot_nt(a, b):
    return lax.dot_general(a.astype(BF16), b.astype(BF16), (((1,), (1,)), ((), ())), preferred_element_type=F32)


def _dot_tn(a, b):
    return lax.dot_general(a.astype(BF16), b.astype(BF16), (((0,), (0,)), ((), ())), preferred_element_type=F32)


def _dot_hi(a, b):
    return jnp.dot(a, b, preferred_element_type=F32, precision=HI)


def _silu(x):
    return x * jax.nn.sigmoid(x)


def _modnorm(x, gain, sc, sh):
    y = x * lax.rsqrt(jnp.mean(x * x, axis=-1, keepdims=True) + RMS_EPS)
    return y * gain * (1.0 + sc) + sh


def _row_tile(T):
    return 512 if T % 512 == 0 else T


LANES = 128


def _deinterleave(val, scr, dil):
    tm, width = val.shape
    sub = tm // dil
    ncb = width // LANES
    for cb in range(ncb):
        scr[cb] = val[:, cb * LANES:(cb + 1) * LANES]
    return jnp.concatenate([jnp.concatenate([scr.at[cb][pl.ds(r, sub, stride=dil), :] for cb in range(ncb)], axis=1)
                            for r in range(dil)], axis=0)


def _interleave(val, scr, dil):
    tm, width = val.shape
    sub = tm // dil
    ncb = width // LANES
    for r in range(dil):
        for cb in range(ncb):
            scr.at[cb][pl.ds(r, sub, stride=dil), :] = val[r * sub:(r + 1) * sub, cb * LANES:(cb + 1) * LANES]
    return jnp.concatenate([scr[cb] for cb in range(ncb)], axis=1)


def norm_proj(x, gain, sc, sh, w, tn, name, cols=None, dil=1):
    T, D = x.shape
    c0, ncol = cols if cols is not None else (0, w.shape[1] // tn)
    N = ncol * tn
    tm = 1024 if (ncol > 1 and dil == 1 and T % 1024 == 0) else _row_tile(T)
    sub = tm // dil

    def body(x_ref, g_ref, sc_ref, sh_ref, w_ref, h_ref, o_ref, h_scr, *perm):
        @pl.when(pl.program_id(1) == 0)
        def _():
            h = _modnorm(x_ref[...], g_ref[...], sc_ref[...], sh_ref[...])
            if dil > 1:
                h = _deinterleave(h, perm[0], dil)
            h_scr[...] = h.astype(BF16)
            h_ref[...] = h.reshape(h_ref.shape).astype(BF16)
        res = jnp.dot(h_scr[...], w_ref[...], preferred_element_type=F32)
        o_ref[...] = res.reshape(o_ref.shape).astype(BF16)

    vec = pl.BlockSpec((1, D), lambda i, j: (0, 0))
    if dil == 1:
        out_specs = [pl.BlockSpec((tm, D), lambda i, j: (i, 0)), pl.BlockSpec((tm, tn), lambda i, j: (i, j))]
        out_shape = [_sds((T, D), BF16), _sds((T, N), BF16)]
        scratch = [pltpu.VMEM((tm, D), BF16)]
    else:
        out_specs = [pl.BlockSpec((dil, sub, D), lambda i, j: (0, i, 0)), pl.BlockSpec((dil, sub, tn), lambda i, j: (0, i, j))]
        out_shape = [_sds((dil, T // dil, D), BF16), _sds((dil, T // dil, N), BF16)]
        scratch = [pltpu.VMEM((tm, D), BF16), pltpu.VMEM((D // LANES, tm, LANES), F32)]
    return pl.pallas_call(
        body, name=name, grid=(T // tm, ncol),
        in_specs=[pl.BlockSpec((tm, D), lambda i, j: (i, 0)), vec, vec, vec,
                  pl.BlockSpec((D, tn), lambda i, j: (0, c0 + j))],
        out_specs=out_specs, out_shape=out_shape, scratch_shapes=scratch,
        compiler_params=_cp(("parallel", "arbitrary")),
    )(x, gain, sc, sh, w)


def matmul_nn(a, w, out_dtype, name):
    T, K = a.shape
    N = w.shape[1]
    tm = _row_tile(T)

    def body(a_ref, w_ref, o_ref):
        o_ref[...] = _dot(a_ref[...], w_ref[...]).astype(o_ref.dtype)

    return pl.pallas_call(
        body, name=name, grid=(T // tm,),
        in_specs=[pl.BlockSpec((tm, K), lambda i: (i, 0)), pl.BlockSpec((K, N), lambda i: (0, 0))],
        out_specs=pl.BlockSpec((tm, N), lambda i: (i, 0)),
        out_shape=_sds((T, N), out_dtype),
        compiler_params=_cp(("parallel",)),
    )(a, w)


def proj_res(a, w, x, gate, name):
    T, K = a.shape
    D = w.shape[1]
    tm = _row_tile(T)

    def body(a_ref, w_ref, x_ref, g_ref, o_ref, y_ref):
        y = jnp.dot(a_ref[...], w_ref[...], preferred_element_type=F32)
        y_ref[...] = y.astype(BF16)
        o_ref[...] = x_ref[...] + g_ref[...] * y

    return pl.pallas_call(
        body, name=name, grid=(T // tm,),
        in_specs=[pl.BlockSpec((tm, K), lambda i: (i, 0)), pl.BlockSpec((K, D), lambda i: (0, 0)),
                  pl.BlockSpec((tm, D), lambda i: (i, 0)), pl.BlockSpec((1, D), lambda i: (0, 0))],
        out_specs=[pl.BlockSpec((tm, D), lambda i: (i, 0)), pl.BlockSpec((tm, D), lambda i: (i, 0))],
        out_shape=[_sds((T, D), F32), _sds((T, D), BF16)],
        compiler_params=_cp(("parallel",)),
    )(a, w, x, gate)


def bwd_out(dx, gate, y, w, tk, name):
    T, D = dx.shape
    K = w.shape[0]
    tm = _row_tile(T)

    def body(dx_ref, g_ref, y_ref, w_ref, dy_ref, da_ref, dg_ref, dy_scr):
        i, j = pl.program_id(0), pl.program_id(1)

        @pl.when(j == 0)
        def _():
            dxv = dx_ref[...]
            dy = (dxv * g_ref[...]).astype(BF16)
            dy_scr[...] = dy
            dy_ref[...] = dy
            part = jnp.sum(dxv * y_ref[...].astype(F32), axis=0, keepdims=True)

            @pl.when(i == 0)
            def _():
                dg_ref[...] = part

            @pl.when(i > 0)
            def _():
                dg_ref[...] += part

        da_ref[...] = _dot_nt(dy_scr[...], w_ref[...]).astype(BF16)

    return pl.pallas_call(
        body, name=name, grid=(T // tm, K // tk),
        in_specs=[pl.BlockSpec((tm, D), lambda i, j: (i, 0)), pl.BlockSpec((1, D), lambda i, j: (0, 0)),
                  pl.BlockSpec((tm, D), lambda i, j: (i, 0)), pl.BlockSpec((tk, D), lambda i, j: (j, 0))],
        out_specs=[pl.BlockSpec((tm, D), lambda i, j: (i, 0)), pl.BlockSpec((tm, tk), lambda i, j: (i, j)),
                   pl.BlockSpec((1, D), lambda i, j: (0, 0))],
        out_shape=[_sds((T, D), BF16), _sds((T, K), BF16), _sds((1, D), F32)],
        scratch_shapes=[pltpu.VMEM((tm, D), BF16)],
        compiler_params=_cp(("arbitrary", "arbitrary")),
    )(dx, gate, y, w)


def bwd_in(a_list, w, col_blocks, tn, acc, x, gain, sc, sh, dx_res, name, dil=1):
    T = math.prod(a_list[0].shape[:-1])
    D = w.shape[0]
    tm = 256 if T % 256 == 0 else T
    sub = tm // dil
    n_a = len(a_list)
    nsteps = a_list[0].shape[-1] // tn
    assert all(a.shape[-1] == nsteps * tn for a in a_list)
    final = x is not None
    has_acc = acc is not None

    def body(*refs):
        a_refs = refs[:n_a]
        w_refs = refs[n_a:2 * n_a]
        pos = 2 * n_a
        acc_ref = refs[pos] if has_acc else None
        pos += int(has_acc)
        if final:
            x_ref, g_ref, sc_ref, sh_ref, dxr_ref = refs[pos:pos + 5]
            pos += 5
            dx_ref, dg_ref, dsc_ref, dsh_ref = refs[pos:pos + 4]
            pos += 4
        else:
            dh_ref = refs[pos]
            pos += 1
        acc_scr = refs[pos]
        i, j = pl.program_id(0), pl.program_id(1)
        part = _dot_nt(a_refs[0][...].reshape(tm, tn), w_refs[0][...])
        for k in range(1, n_a):
            part += _dot_nt(a_refs[k][...].reshape(tm, tn), w_refs[k][...])

        @pl.when(j == 0)
        def _():
            acc_scr[...] = part

        @pl.when(j > 0)
        def _():
            acc_scr[...] += part

        @pl.when(j == nsteps - 1)
        def _():
            dh = acc_scr[...]
            if dil > 1:
                dh = _interleave(dh, refs[pos + 1], dil)
            if has_acc:
                dh = dh + acc_ref[...]
            if not final:
                dh_ref[...] = dh
                return
            _, vjp = jax.vjp(_modnorm, x_ref[...], g_ref[...], sc_ref[...], sh_ref[...])
            dxn, dg, dsc, dsh = vjp(dh)
            dx_ref[...] = dxr_ref[...] + dxn

            @pl.when(i == 0)
            def _():
                dg_ref[...] = dg
                dsc_ref[...] = dsc
                dsh_ref[...] = dsh

            @pl.when(i > 0)
            def _():
                dg_ref[...] += dg
                dsc_ref[...] += dsc
                dsh_ref[...] += dsh

    row = pl.BlockSpec((tm, D), lambda i, j: (i, 0))
    vec = pl.BlockSpec((1, D), lambda i, j: (0, 0))
    if dil == 1:
        in_specs = [pl.BlockSpec((tm, tn), lambda i, j: (i, j)) for _ in a_list]
    else:
        in_specs = [pl.BlockSpec((dil, sub, tn), lambda i, j: (0, i, j)) for _ in a_list]
    in_specs += [pl.BlockSpec((D, tn), functools.partial(lambda i, j, c0: (0, c0 + j), c0=c0)) for c0 in col_blocks]
    args = list(a_list) + [w] * n_a
    if has_acc:
        in_specs.append(row)
        args.append(acc)
    if final:
        in_specs += [row, vec, vec, vec, row]
        args += [x, gain, sc, sh, dx_res]
        out_specs = [row, vec, vec, vec]
        out_shape = [_sds((T, D), F32)] + [_sds((1, D), F32)] * 3
    else:
        out_specs = row
        out_shape = _sds((T, D), F32)
    return pl.pallas_call(
        body, name=name, grid=(T // tm, nsteps), in_specs=in_specs, out_specs=out_specs, out_shape=out_shape,
        scratch_shapes=[pltpu.VMEM((tm, D), F32)] + ([pltpu.VMEM((D // LANES, tm, LANES), F32)] if dil > 1 else []),
        compiler_params=_cp(("arbitrary", "arbitrary")),
    )(*args)


def matmul_tn(a, b, tk, tn, name):
    T, K = a.shape
    N = b.shape[1]
    tt = 1024 if T % 1024 == 0 else T

    def body(a_ref, b_ref, o_ref):
        part = _dot_tn(a_ref[...], b_ref[...])

        @pl.when(pl.program_id(2) == 0)
        def _():
            o_ref[...] = part

        @pl.when(pl.program_id(2) > 0)
        def _():
            o_ref[...] += part

    return pl.pallas_call(
        body, name=name, grid=(K // tk, N // tn, T // tt),
        in_specs=[pl.BlockSpec((tt, tk), lambda k, n, t: (t, k)), pl.BlockSpec((tt, tn), lambda k, n, t: (t, n))],
        out_specs=pl.BlockSpec((tk, tn), lambda k, n, t: (k, n)),
        out_shape=_sds((K, N), F32),
        compiler_params=_cp(("parallel", "parallel", "arbitrary")),
    )(a, b)


def _halo_specs(tm, tc, col_of):
    per = tm // HALO

    def prev(j, i):
        return (jnp.maximum(i * per - 1, 0), col_of(j))

    def nxt(j, i, last):
        return (jnp.minimum((i + 1) * per, last), col_of(j))

    return prev, nxt, per


def _shift_down(ext, s):
    return pltpu.roll(ext, s, 0)


def _shift_up(ext, s):
    return pltpu.roll(ext, ext.shape[0] - s, 0)


def _conv_ext(ext, w):
    K = w.shape[0]
    out = w[K - 1:K] * ext
    for s in range(1, K):
        out += w[K - 1 - s:K - s] * _shift_down(ext, s)
    return out


def _conv_t_ext(dext, w):
    K = w.shape[0]
    out = w[K - 1:K] * dext
    for s in range(1, K):
        out += w[K - 1 - s:K - s] * _shift_up(dext, s)
    return out


def _conv_dw(dc, ext, K, tm):
    rowid = lax.broadcasted_iota(jnp.int32, (8, dc.shape[1]), 0)
    out = jnp.zeros((8, dc.shape[1]), F32)
    for j in range(K):
        s = K - 1 - j
        xs = ext if s == 0 else _shift_down(ext, s)
        out = jnp.where(rowid == j, jnp.sum(dc * xs[HALO:HALO + tm], axis=0, keepdims=True), out)
    return out


def _accum(ref, val, first):
    @pl.when(first)
    def _():
        ref[...] = val

    @pl.when(jnp.logical_not(first))
    def _():
        ref[...] += val


def ffn_mid_fwd(up, conv_w, conv_b, name):
    T, two_f = up.shape
    F = two_f // 2
    tm = _row_tile(T)
    tc = F // 2
    nct = F // tc
    prev, _, per = _halo_specs(tm, tc, lambda j: j)

    def body(g_ref, gp_ref, v_ref, w_ref, b_ref, a_ref):
        i = pl.program_id(1)
        gp = jnp.where(i > 0, gp_ref[...].astype(F32), 0.0)
        ext = jnp.concatenate([gp, g_ref[...].astype(F32)], axis=0)
        c = _conv_ext(ext, w_ref[...])[HALO:] + b_ref[...]
        a_ref[...] = (_silu(c) * v_ref[...].astype(F32)).astype(BF16)

    return pl.pallas_call(
        body, name=name, grid=(nct, T // tm),
        in_specs=[pl.BlockSpec((tm, tc), lambda j, i: (i, j)), pl.BlockSpec((HALO, tc), prev),
                  pl.BlockSpec((tm, tc), lambda j, i: (i, j + nct)),
                  pl.BlockSpec((FFN_CONV, tc), lambda j, i: (0, j)), pl.BlockSpec((1, tc), lambda j, i: (0, j))],
        out_specs=pl.BlockSpec((tm, tc), lambda j, i: (i, j)),
        out_shape=_sds((T, F), BF16),
        compiler_params=_cp(("parallel", "arbitrary")),
    )(up, up, up, conv_w, conv_b)


def ffn_mid_bwd(up, da, conv_w, conv_b, name):
    T, two_f = up.shape
    F = two_f // 2
    tm = _row_tile(T)
    tc = F // 2
    nct = F // tc
    nrow = T // tm
    prev, nxt, per = _halo_specs(tm, tc, lambda j: j)
    last = T // HALO - 1
    nxt_g = functools.partial(nxt, last=last)

    def nxt_v(j, i):
        return (jnp.minimum((i + 1) * per, last), j + nct)

    def body(g_ref, gp_ref, gn_ref, v_ref, vn_ref, da_ref, dan_ref, w_ref, b_ref, dg_ref, dv_ref, dw_ref, db_ref):
        i = pl.program_id(1)
        w = w_ref[...]
        gp = jnp.where(i > 0, gp_ref[...].astype(F32), 0.0)
        inside = i < nrow - 1
        ext = jnp.concatenate([gp, g_ref[...].astype(F32), gn_ref[...].astype(F32)], axis=0)
        zero = jnp.zeros((HALO, tc), F32)
        v_ext = jnp.concatenate([zero, v_ref[...].astype(F32), vn_ref[...].astype(F32)], axis=0)
        da_ext = jnp.concatenate([zero, da_ref[...].astype(F32), jnp.where(inside, dan_ref[...].astype(F32), 0.0)], axis=0)
        c = _conv_ext(ext, w) + b_ref[...]
        sg = jax.nn.sigmoid(c)
        sil = c * sg
        dc = da_ext * v_ext * (sg * (1.0 + c * (1.0 - sg)))
        dv_ref[...] = (da_ext * sil)[HALO:HALO + tm].astype(BF16)
        dg_ref[...] = _conv_t_ext(dc, w)[HALO:HALO + tm].astype(BF16)
        dcm = dc[HALO:HALO + tm]
        _accum(dw_ref, _conv_dw(dcm, ext, FFN_CONV, tm), i == 0)
        _accum(db_ref, jnp.sum(dcm, axis=0, keepdims=True), i == 0)

    main = lambda j, i: (i, j)
    main_v = lambda j, i: (i, j + nct)
    return pl.pallas_call(
        body, name=name, grid=(nct, nrow),
        in_specs=[pl.BlockSpec((tm, tc), main), pl.BlockSpec((HALO, tc), prev), pl.BlockSpec((HALO, tc), nxt_g),
                  pl.BlockSpec((tm, tc), main_v), pl.BlockSpec((HALO, tc), nxt_v),
                  pl.BlockSpec((tm, tc), main), pl.BlockSpec((HALO, tc), nxt_g),
                  pl.BlockSpec((FFN_CONV, tc), lambda j, i: (0, j)), pl.BlockSpec((1, tc), lambda j, i: (0, j))],
        out_specs=[pl.BlockSpec((tm, tc), main), pl.BlockSpec((tm, tc), main),
                   pl.BlockSpec((8, tc), lambda j, i: (0, j)), pl.BlockSpec((1, tc), lambda j, i: (0, j))],
        out_shape=[_sds((T, F), BF16), _sds((T, F), BF16), _sds((8, F), F32), _sds((1, F), F32)],
        compiler_params=_cp(("parallel", "arbitrary")),
    )(up, up, up, up, up, da, da, conv_w, conv_b)


def _l2n(x):
    return x * lax.rsqrt(jnp.sum(x * x, axis=-1, keepdims=True) + RMS_EPS)


def _qkv_tok(c, normed):
    s = _silu(c)
    n = jnp.concatenate([_l2n(s[:, h * GDN_D:(h + 1) * GDN_D]) for h in range(GDN_HEADS)], axis=1)
    return jnp.where(normed, n, s)


def gdn_prep_fwd(proj, conv_w, name):
    T = proj.shape[0]
    W = GDN_HEADS * GDN_D
    tm = _row_tile(T)
    prev, _, per = _halo_specs(tm, W, lambda j: j)

    def body(x_ref, xp_ref, w_ref, o_ref):
        j, i = pl.program_id(0), pl.program_id(1)
        xp = jnp.where(i > 0, xp_ref[...].astype(F32), 0.0)
        ext = jnp.concatenate([xp, x_ref[...].astype(F32)], axis=0)
        c = _conv_ext(ext, w_ref[...])[HALO:]
        o_ref[...] = _qkv_tok(c, j < 2)

    return pl.pallas_call(
        body, name=name, grid=(3, T // tm),
        in_specs=[pl.BlockSpec((tm, W), lambda j, i: (i, j)), pl.BlockSpec((HALO, W), prev),
                  pl.BlockSpec((GDN_CONV, W), lambda j, i: (0, j))],
        out_specs=pl.BlockSpec((tm, W), lambda j, i: (i, j)),
        out_shape=_sds((T, 3 * W), F32),
        compiler_params=_cp(("parallel", "arbitrary")),
    )(proj, proj, conv_w)


def gdn_prep_bwd(proj, dqkv, conv_w, name):
    T = proj.shape[0]
    W = GDN_HEADS * GDN_D
    tm = _row_tile(T)
    nrow = T // tm
    prev, nxt, per = _halo_specs(tm, W, lambda j: j)
    nxt = functools.partial(nxt, last=T // HALO - 1)

    def body(x_ref, xp_ref, xn_ref, d_ref, dn_ref, w_ref, dx_ref, dw_ref):
        j, i = pl.program_id(0), pl.program_id(1)
        w = w_ref[...]
        xp = jnp.where(i > 0, xp_ref[...].astype(F32), 0.0)
        ext = jnp.concatenate([xp, x_ref[...].astype(F32), xn_ref[...].astype(F32)], axis=0)
        d_ext = jnp.concatenate([jnp.zeros((HALO, W), F32), d_ref[...],
                                 jnp.where(i < nrow - 1, dn_ref[...], 0.0)], axis=0)
        c = _conv_ext(ext, w)
        _, vjp = jax.vjp(lambda cc: _qkv_tok(cc, j < 2), c)
        dc, = vjp(d_ext)
        dx_ref[...] = _conv_t_ext(dc, w)[HALO:HALO + tm].astype(BF16)
        _accum(dw_ref, _conv_dw(dc[HALO:HALO + tm], ext, GDN_CONV, tm), i == 0)

    main = lambda j, i: (i, j)
    return pl.pallas_call(
        body, name=name, grid=(3, nrow),
        in_specs=[pl.BlockSpec((tm, W), main), pl.BlockSpec((HALO, W), prev), pl.BlockSpec((HALO, W), nxt),
                  pl.BlockSpec((tm, W), main), pl.BlockSpec((HALO, W), nxt),
                  pl.BlockSpec((GDN_CONV, W), lambda j, i: (0, j))],
        out_specs=[pl.BlockSpec((tm, W), main), pl.BlockSpec((8, W), lambda j, i: (0, j))],
        out_shape=[_sds((T, 3 * W), BF16), _sds((8, 3 * W), F32)],
        compiler_params=_cp(("parallel", "arbitrary")),
    )(proj, proj, proj, dqkv, dqkv, conv_w)


def _gate_tok(ba, a_log, dt_bias):
    z = ba + dt_bias
    softplus = jnp.maximum(z, 0.0) + jnp.log(1.0 + jnp.exp(-jnp.abs(z)))
    lane = lax.broadcasted_iota(jnp.int32, ba.shape, 1)
    raw = jnp.where(lane < GDN_HEADS, jax.nn.sigmoid(ba), -jnp.exp(a_log) * softplus)
    n = ba.shape[0]
    r = lax.broadcasted_iota(jnp.int32, (n, n), 0)
    c = lax.broadcasted_iota(jnp.int32, (n, n), 1)
    in_chunk_before = (r >= c) & ((r - c) <= (r & (GDN_CHUNK - 1)))
    return jnp.where(lane < GDN_HEADS, raw, _dot_hi(in_chunk_before.astype(F32), raw))


def gate_fwd(ba, a_log, dt_bias, name):
    T, L = ba.shape
    tm = _row_tile(T)

    def body(ba_ref, al_ref, dt_ref, o_ref):
        o_ref[...] = _gate_tok(ba_ref[...], al_ref[...], dt_ref[...])

    vec = pl.BlockSpec((1, L), lambda i: (0, 0))
    return pl.pallas_call(
        body, name=name, grid=(T // tm,),
        in_specs=[pl.BlockSpec((tm, L), lambda i: (i, 0)), vec, vec],
        out_specs=pl.BlockSpec((tm, L), lambda i: (i, 0)), out_shape=_sds((T, L), F32),
        compiler_params=_cp(("parallel",)),
    )(ba, a_log, dt_bias)


def gate_bwd(ba, a_log, dt_bias, dout, name):
    T, L = ba.shape
    tm = _row_tile(T)

    def body(ba_ref, al_ref, dt_ref, d_ref, dba_ref, dal_ref, ddt_ref):
        _, vjp = jax.vjp(_gate_tok, ba_ref[...], al_ref[...], dt_ref[...])
        dba, dal, ddt = vjp(d_ref[...])
        dba_ref[...] = dba.astype(BF16)
        first = pl.program_id(0) == 0
        _accum(dal_ref, dal, first)
        _accum(ddt_ref, ddt, first)

    vec = pl.BlockSpec((1, L), lambda i: (0, 0))
    row = pl.BlockSpec((tm, L), lambda i: (i, 0))
    return pl.pallas_call(
        body, name=name, grid=(T // tm,),
        in_specs=[row, vec, vec, row], out_specs=[row, vec, vec],
        out_shape=[_sds((T, L), BF16), _sds((1, L), F32), _sds((1, L), F32)],
        compiler_params=_cp(("arbitrary",)),
    )(ba, a_log, dt_bias, dout)


_B_NN = (((2,), (1,)), ((0,), (0,)))
_B_NT = (((2,), (2,)), ((0,), (0,)))
GDN_GROUP = 8


def _split_bf16(a):
    hi = a.astype(BF16)
    return hi, (a - hi.astype(F32)).astype(BF16)


def _bdot3(a, b):
    ah, al = _split_bf16(a)
    bh, bl = _split_bf16(b)
    dg = lambda x, y: lax.dot_general(x, y, _B_NN, preferred_element_type=F32)
    return dg(ah, bh) + (dg(ah, bl) + dg(al, bh))


def _gdn_prep(q, k, v, gcol, grow, bcol):
    B, C, dk = q.shape
    r = lax.broadcasted_iota(jnp.int32, (1, C, C), 1)
    cidx = lax.broadcasted_iota(jnp.int32, (1, C, C), 2)
    causal = r >= cidx
    strict = r > cidx
    decay = jnp.where(causal, jnp.exp(jnp.where(causal, gcol - grow, 0.0)), 0.0)
    qs = q * (dk ** -0.5)
    kb = k * bcol
    kk = lax.dot_general(kb.astype(BF16), k.astype(BF16), _B_NT, preferred_element_type=F32)
    L = jnp.where(strict, kk * decay, 0.0)
    tinv = (r == cidx).astype(F32) - L
    p = L
    n = 2
    while n < C:
        p = _bdot3(p, p)
        tinv = tinv + _bdot3(tinv, p)
        n *= 2
    egc = jnp.exp(gcol)
    u = _bdot3(tinv, v * bcol)
    w = _bdot3(tinv, kb * egc)
    qk = lax.dot_general(qs.astype(BF16), k.astype(BF16), _B_NT, preferred_element_type=F32)
    intra = jnp.where(causal, qk * decay, 0.0)
    last = lax.broadcasted_iota(jnp.int32, (1, C, 1), 1) == C - 1
    gt = jnp.sum(jnp.where(last, gcol, 0.0), axis=1, keepdims=True)
    return w, u, qs * egc, k * jnp.exp(gt - gcol), intra


def _gdn_scan(S, w, u, qg, kdec, intra, gcol):
    C = w.shape[0]
    last = lax.broadcasted_iota(jnp.int32, (C, 1), 0) == C - 1
    gt = jnp.sum(jnp.where(last, gcol, 0.0), axis=0, keepdims=True)
    v_new = u - _dot(w, S)
    o = _dot(qg, S) + _dot(intra, v_new)
    return S * jnp.exp(gt) + _dot_tn(kdec, v_new), o


def _gdn_specs(T, rev=False):
    H, D, C, B = GDN_HEADS, GDN_D, GDN_CHUNK, GDN_GROUP
    nsteps = T // (B * C)
    at = (lambda n: nsteps - 1 - n) if rev else (lambda n: n)
    rows = lambda w: pl.BlockSpec((B * C, w), lambda n: (at(n), 0))
    grow = pl.BlockSpec((B, H, C), lambda n: (at(n), 0, 0))
    sq = pl.BlockSpec((B, H, C, C), lambda n: (at(n), 0, 0, 0))
    st = pl.BlockSpec((B, H, D, D), lambda n: (at(n), 0, 0, 0))
    return rows, grow, sq, st, nsteps


def _head_chunks(ref, h, width):
    return ref[:, h * width:(h + 1) * width].reshape(GDN_GROUP, GDN_CHUNK, width)


def gdn_prep_chunks_fwd(qkv, gcol, grow, bcol, name):
    T = qkv.shape[0]
    H, D, C, B = GDN_HEADS, GDN_D, GDN_CHUNK, GDN_GROUP
    W = H * D
    rows, grow_s, sq, _, nsteps = _gdn_specs(T)

    def body(x_ref, gc_ref, gr_ref, b_ref, w_ref, u_ref, qg_ref, kd_ref, in_ref):
        for h in range(H):
            q, k, v = [x_ref[:, j * W + h * D:j * W + (h + 1) * D].reshape(B, C, D) for j in range(3)]
            outs = _gdn_prep(q, k, v, _head_chunks(gc_ref, h, 1), gr_ref[:, h:h + 1, :], _head_chunks(b_ref, h, 1))
            for ref, val in zip((w_ref, u_ref, qg_ref, kd_ref), outs[:4]):
                ref[:, h * D:(h + 1) * D] = val.reshape(B * C, D).astype(ref.dtype)
            in_ref[:, h] = outs[4].astype(BF16)

    return pl.pallas_call(
        body, name=name, grid=(nsteps,),
        in_specs=[rows(3 * W), rows(H), grow_s, rows(H)],
        out_specs=[rows(W), rows(W), rows(W), rows(W), sq],
        out_shape=[_sds((T, W), BF16), _sds((T, W), F32), _sds((T, W), BF16), _sds((T, W), BF16),
                   _sds((T // C, H, C, C), BF16)],
        compiler_params=_cp(("parallel",)),
    )(qkv, gcol, grow, bcol)


def gdn_prep_chunks_bwd(qkv, gcol, grow, bcol, dw, du, dqg, dkd, dintra, name):
    T = qkv.shape[0]
    H, D, C, B = GDN_HEADS, GDN_D, GDN_CHUNK, GDN_GROUP
    W = H * D
    rows, grow_s, sq, _, nsteps = _gdn_specs(T)

    def body(x_ref, gc_ref, gr_ref, b_ref, dw_ref, du_ref, dqg_ref, dkd_ref, din_ref, dx_ref, dgc_ref, dgr_ref, db_ref):
        for h in range(H):
            q, k, v = [x_ref[:, j * W + h * D:j * W + (h + 1) * D].reshape(B, C, D) for j in range(3)]
            _, vjp = jax.vjp(_gdn_prep, q, k, v, _head_chunks(gc_ref, h, 1), gr_ref[:, h:h + 1, :],
                             _head_chunks(b_ref, h, 1))
            cots = tuple(_head_chunks(r, h, D) for r in (dw_ref, du_ref, dqg_ref, dkd_ref)) + (din_ref[:, h],)
            dq, dk, dv, dgc, dgr, db = vjp(cots)
            for j, val in enumerate((dq, dk, dv)):
                dx_ref[:, j * W + h * D:j * W + (h + 1) * D] = val.reshape(B * C, D)
            dgc_ref[:, h:h + 1] = dgc.reshape(B * C, 1)
            dgr_ref[:, h:h + 1, :] = dgr
            db_ref[:, h:h + 1] = db.reshape(B * C, 1)

    return pl.pallas_call(
        body, name=name, grid=(nsteps,),
        in_specs=[rows(3 * W), rows(H), grow_s, rows(H), rows(W), rows(W), rows(W), rows(W), sq],
        out_specs=[rows(3 * W), rows(H), grow_s, rows(H)],
        out_shape=[_sds((T, 3 * W), F32), _sds((T, H), F32), _sds((T // C, H, C), F32), _sds((T, H), F32)],
        compiler_params=_cp(("parallel",)),
    )(qkv, gcol, grow, bcol, dw, du, dqg, dkd, dintra)


def gdn_fwd(w, u, qg, kdec, intra, gcol, name):
    T = w.shape[0]
    H, D, C, B = GDN_HEADS, GDN_D, GDN_CHUNK, GDN_GROUP
    W = H * D
    rows, _, sq, st, nsteps = _gdn_specs(T)

    def body(w_ref, u_ref, qg_ref, kd_ref, in_ref, gc_ref, o_ref, s_ref, s_scr):
        @pl.when(pl.program_id(0) == 0)
        def _():
            s_scr[...] = jnp.zeros_like(s_scr)

        def chunk(cb, carry):
            rs = pl.ds(pl.multiple_of(cb * C, C), C)
            for h in range(H):
                cs = slice(h * D, (h + 1) * D)
                S = s_scr[h]
                s_ref[cb, h] = S
                S_new, o = _gdn_scan(S, w_ref[rs, cs], u_ref[rs, cs], qg_ref[rs, cs], kd_ref[rs, cs], in_ref[cb, h],
                                     gc_ref[rs, h:h + 1])
                s_scr[h] = S_new
                o_ref[rs, cs] = o
            return carry

        lax.fori_loop(0, B, chunk, 0)

    return pl.pallas_call(
        body, name=name, grid=(nsteps,),
        in_specs=[rows(W), rows(W), rows(W), rows(W), sq, rows(H)],
        out_specs=[rows(W), st],
        out_shape=[_sds((T, W), F32), _sds((T // C, H, D, D), F32)],
        scratch_shapes=[pltpu.VMEM((H, D, D), F32)],
        compiler_params=_cp(("arbitrary",)),
    )(w, u, qg, kdec, intra, gcol)


def gdn_bwd(w, u, qg, kdec, intra, gcol, states, do, name):
    T = w.shape[0]
    H, D, C, B = GDN_HEADS, GDN_D, GDN_CHUNK, GDN_GROUP
    W = H * D
    rows, _, sq, st, nsteps = _gdn_specs(T, rev=True)

    def body(w_ref, u_ref, qg_ref, kd_ref, in_ref, gc_ref, s_ref, do_ref,
             dw_ref, du_ref, dqg_ref, dkd_ref, din_ref, dgc_ref, ds_scr):
        @pl.when(pl.program_id(0) == 0)
        def _():
            ds_scr[...] = jnp.zeros_like(ds_scr)

        def chunk(t, carry):
            cb = B - 1 - t
            rs = pl.ds(pl.multiple_of(cb * C, C), C)
            for h in range(H):
                cs = slice(h * D, (h + 1) * D)
                _, vjp = jax.vjp(_gdn_scan, s_ref[cb, h], w_ref[rs, cs].astype(F32), u_ref[rs, cs],
                                 qg_ref[rs, cs].astype(F32), kd_ref[rs, cs].astype(F32), in_ref[cb, h].astype(F32),
                                 gc_ref[rs, h:h + 1])
                dS, dw, du, dqg, dkd, din, dgc = vjp((ds_scr[h], do_ref[rs, cs]))
                ds_scr[h] = dS
                dw_ref[rs, cs] = dw
                du_ref[rs, cs] = du
                dqg_ref[rs, cs] = dqg
                dkd_ref[rs, cs] = dkd
                din_ref[cb, h] = din
                dgc_ref[rs, h:h + 1] = dgc
            return carry

        lax.fori_loop(0, B, chunk, 0)

    return pl.pallas_call(
        body, name=name, grid=(nsteps,),
        in_specs=[rows(W), rows(W), rows(W), rows(W), sq, rows(H), st, rows(W)],
        out_specs=[rows(W), rows(W), rows(W), rows(W), sq, rows(H)],
        out_shape=[_sds((T, W), F32)] * 4 + [_sds((T // C, H, C, C), F32), _sds((T, H), F32)],
        scratch_shapes=[pltpu.VMEM((H, D, D), F32)],
        compiler_params=_cp(("arbitrary",)),
    )(w, u, qg, kdec, intra, gcol, states, do)


def _gated_norm(o, z, gain):
    outs = []
    for h in range(GDN_HEADS):
        oh = o[:, h * GDN_D:(h + 1) * GDN_D]
        y = oh * lax.rsqrt(jnp.mean(oh * oh, axis=-1, keepdims=True) + RMS_EPS) * gain
        outs.append(y * _silu(z[:, h * GDN_D:(h + 1) * GDN_D]))
    return jnp.concatenate(outs, axis=1)


def _window_sums(ext, shift):
    outs = []
    s = ext
    step = 1
    for gi, win in enumerate(POOL_WINDOWS):
        while step < win:
            s = s + shift(s, step)
            step *= 2
        outs.append(s[:, gi * GDN_D:(gi + 1) * GDN_D])
    return jnp.concatenate(outs, axis=1)


def _pool_counts(t0, rows, width):
    t1 = (t0 + 1 + lax.broadcasted_iota(jnp.int32, (rows, width), 0)).astype(F32)
    lane = lax.broadcasted_iota(jnp.int32, (rows, width), 1)
    win = jnp.full((rows, width), float(POOL_WINDOWS[-1]), F32)
    for gi in reversed(range(len(POOL_WINDOWS) - 1)):
        win = jnp.where(lane < (gi + 1) * GDN_D, float(POOL_WINDOWS[gi]), win)
    return jnp.minimum(t1, win)


def even_post_fwd(o, proj, gdn_norm, pool_w, pool_scale, name):
    T = o.shape[0]
    W = GDN_HEADS * GDN_D
    tm = _row_tile(T)
    per = tm // HALO

    def body(o_ref, z_ref, p_ref, pp_ref, gn_ref, pw_ref, ps_ref, out_ref):
        i = pl.program_id(0)
        out_ref[:, :W] = _gated_norm(o_ref[...], z_ref[...].astype(F32), gn_ref[...]).astype(BF16)
        pp = jnp.where(i > 0, pp_ref[...].astype(F32), 0.0)
        ext = jnp.concatenate([pp, p_ref[...].astype(F32)], axis=0)
        pooled = (_window_sums(ext, _shift_down)[HALO:] / _pool_counts(i * tm, tm, W)) - ext[HALO:]
        for gi in range(len(POOL_WINDOWS)):
            sl = slice(gi * GDN_D, (gi + 1) * GDN_D)
            y = _dot(pooled[:, sl], pw_ref[gi]) * ps_ref[:, sl]
            out_ref[:, W + gi * GDN_D:W + (gi + 1) * GDN_D] = y.astype(BF16)

    return pl.pallas_call(
        body, name=name, grid=(T // tm,),
        in_specs=[pl.BlockSpec((tm, W), lambda i: (i, 0)), pl.BlockSpec((tm, W), lambda i: (i, 3)),
                  pl.BlockSpec((tm, W), lambda i: (i, 4)),
                  pl.BlockSpec((HALO, W), lambda i: (jnp.maximum(i * per - 1, 0), 4)),
                  pl.BlockSpec((1, GDN_D), lambda i: (0, 0)),
                  pl.BlockSpec((len(POOL_WINDOWS), GDN_D, GDN_D), lambda i: (0, 0, 0)),
                  pl.BlockSpec((1, W), lambda i: (0, 0))],
        out_specs=pl.BlockSpec((tm, 2 * W), lambda i: (i, 0)),
        out_shape=_sds((T, 2 * W), BF16),
        compiler_params=_cp(("arbitrary",)),
    )(o, proj, proj, proj, gdn_norm, pool_w, pool_scale)


def even_post_bwd(o, proj, gdn_norm, pool_w, pool_scale, dcat, name):
    T = o.shape[0]
    W = GDN_HEADS * GDN_D
    G = len(POOL_WINDOWS)
    tm = _row_tile(T)
    per = tm // HALO
    nrow = T // tm
    last = T // HALO - 1

    def body(o_ref, z_ref, p_ref, pp_ref, gn_ref, pw_ref, ps_ref, d_ref, dn_ref,
             do_ref, dzp_ref, dgn_ref, dpw_ref, dps_ref):
        i = pl.program_id(0)
        first = i == 0
        _, vjp = jax.vjp(_gated_norm, o_ref[...], z_ref[...].astype(F32), gn_ref[...])
        do, dz, dgn = vjp(d_ref[:, :W].astype(F32))
        do_ref[...] = do
        dzp_ref[:, :W] = dz.astype(BF16)
        _accum(dgn_ref, dgn, first)
        pp = jnp.where(first, 0.0, pp_ref[...].astype(F32))
        ext = jnp.concatenate([pp, p_ref[...].astype(F32)], axis=0)
        pooled = (_window_sums(ext, _shift_down)[HALO:] / _pool_counts(i * tm, tm, W)) - ext[HALO:]
        dy_ext = jnp.concatenate([d_ref[:, W:].astype(F32), jnp.where(i < nrow - 1, dn_ref[...].astype(F32), 0.0)], axis=0)
        dys_ext = dy_ext * ps_ref[...]
        dpooled, dscale = [], []
        for gi in range(G):
            sl = slice(gi * GDN_D, (gi + 1) * GDN_D)
            dpooled.append(_dot_nt(dys_ext[:, sl], pw_ref[gi]))
            y = _dot(pooled[:, sl], pw_ref[gi])
            dscale.append(jnp.sum(dy_ext[:tm, sl] * y, axis=0, keepdims=True))
            _accum(dpw_ref.at[gi], _dot_tn(pooled[:, sl], dys_ext[:tm, sl]), first)
        dpooled = jnp.concatenate(dpooled, axis=1)
        _accum(dps_ref, jnp.concatenate(dscale, axis=1), first)
        dmean = dpooled / _pool_counts(i * tm, tm + HALO, W)
        dp = _window_sums(dmean, _shift_up)[:tm] - dpooled[:tm]
        dzp_ref[:, W:] = dp.astype(BF16)

    return pl.pallas_call(
        body, name=name, grid=(nrow,),
        in_specs=[pl.BlockSpec((tm, W), lambda i: (i, 0)), pl.BlockSpec((tm, W), lambda i: (i, 3)),
                  pl.BlockSpec((tm, W), lambda i: (i, 4)),
                  pl.BlockSpec((HALO, W), lambda i: (jnp.maximum(i * per - 1, 0), 4)),
                  pl.BlockSpec((1, GDN_D), lambda i: (0, 0)),
                  pl.BlockSpec((G, GDN_D, GDN_D), lambda i: (0, 0, 0)),
                  pl.BlockSpec((1, W), lambda i: (0, 0)),
                  pl.BlockSpec((tm, 2 * W), lambda i: (i, 0)),
                  pl.BlockSpec((HALO, W), lambda i: (jnp.minimum((i + 1) * per, last), 1))],
        out_specs=[pl.BlockSpec((tm, W), lambda i: (i, 0)), pl.BlockSpec((tm, 2 * W), lambda i: (i, 0)),
                   pl.BlockSpec((1, GDN_D), lambda i: (0, 0)), pl.BlockSpec((G, GDN_D, GDN_D), lambda i: (0, 0, 0)),
                   pl.BlockSpec((1, W), lambda i: (0, 0))],
        out_shape=[_sds((T, W), F32), _sds((T, 2 * W), BF16), _sds((1, GDN_D), F32), _sds((G, GDN_D, GDN_D), F32),
                   _sds((1, W), F32)],
        compiler_params=_cp(("arbitrary",)),
    )(o, proj, proj, proj, gdn_norm, pool_w, pool_scale, dcat, dcat)


def _head_norm(x, gain):
    return x * lax.rsqrt(jnp.mean(x * x, axis=-1, keepdims=True) + RMS_EPS) * gain


def _att_head(q, kp, kc, vp, vc, qn, kn, slope, has_prev):
    B = ATT_BLOCK
    qh = _head_norm(q, qn) * (ATT_DH ** -0.5)
    k = _head_norm(jnp.concatenate([kp, kc], axis=0), kn)
    v = jnp.concatenate([vp, vc], axis=0)
    a = lax.broadcasted_iota(jnp.int32, (B, 2 * B), 0)
    j = lax.broadcasted_iota(jnp.int32, (B, 2 * B), 1)
    rel = B + a - j
    mask = (rel >= 0) & (rel <= B) & ((j >= B) | has_prev)
    s = _dot_nt(qh, k) - slope * rel.astype(F32)
    s = jnp.where(mask, s, NEG)
    m = jnp.max(s, axis=-1, keepdims=True)
    p = jnp.exp(s - m)
    l = jnp.sum(p, axis=-1, keepdims=True)
    o = _dot(p, v) / l
    return o, jnp.broadcast_to(m + jnp.log(l), (B, ATT_DH))


def att_fwd(qkv, q_norm, k_norm, name):
    dil, L, _ = qkv.shape
    Wd = ATT_HEADS * ATT_DH
    nb = L // ATT_BLOCK
    B = ATT_BLOCK

    def body(q_ref, kc_ref, kp_ref, vc_ref, vp_ref, qn_ref, kn_ref, o_ref, l_ref):
        has_prev = pl.program_id(1) > 0
        for h in range(ATT_HEADS):
            sl = slice(h * ATT_DH, (h + 1) * ATT_DH)
            slope = dil * (2.0 ** (-8.0 * (h + 1) / ATT_HEADS))
            o, lse = _att_head(q_ref[:, sl].astype(F32), kp_ref[:, sl].astype(F32), kc_ref[:, sl].astype(F32),
                               vp_ref[:, sl], vc_ref[:, sl], qn_ref[...], kn_ref[...], slope, has_prev)
            o_ref[:, sl] = o
            l_ref[:, sl] = lse

    cur = lambda t: pl.BlockSpec((None, B, Wd), lambda r, n: (r, n, t))
    prev = lambda t: pl.BlockSpec((None, B, Wd), lambda r, n: (r, jnp.maximum(n - 1, 0), t))
    vec = pl.BlockSpec((1, ATT_DH), lambda r, n: (0, 0))
    out = pl.BlockSpec((None, B, Wd), lambda r, n: (r, n, 0))
    return pl.pallas_call(
        body, name=name, grid=(dil, nb),
        in_specs=[cur(0), cur(1), prev(1), cur(2), prev(2), vec, vec],
        out_specs=[out, out], out_shape=[_sds((dil, L, Wd), F32)] * 2,
        compiler_params=_cp(("parallel", "arbitrary")),
    )(qkv, qkv, qkv, qkv, qkv, q_norm, k_norm)


def att_bwd(qkv, q_norm, k_norm, do, dlse, name):
    dil, L, _ = qkv.shape
    Wd = ATT_HEADS * ATT_DH
    nb = L // ATT_BLOCK
    B = ATT_BLOCK

    def body(q_ref, kc_ref, kp_ref, vc_ref, vp_ref, qn_ref, kn_ref, do_ref, dl_ref,
             dq_ref, dk_ref, dv_ref, dqn_ref, dkn_ref, ck_scr, cv_scr):
        r, n = pl.program_id(0), pl.program_id(1)
        has_prev = n > 0
        first = (r == 0) & (n == 0)

        @pl.when(n < nb)
        def _():
            dqn = jnp.zeros((1, ATT_DH), F32)
            dkn = jnp.zeros((1, ATT_DH), F32)
            for h in range(ATT_HEADS):
                sl = slice(h * ATT_DH, (h + 1) * ATT_DH)
                slope = dil * (2.0 ** (-8.0 * (h + 1) / ATT_HEADS))
                f = functools.partial(_att_head, slope=slope, has_prev=has_prev)
                _, vjp = jax.vjp(f, q_ref[:, sl].astype(F32), kp_ref[:, sl].astype(F32), kc_ref[:, sl].astype(F32),
                                 vp_ref[:, sl].astype(F32), vc_ref[:, sl].astype(F32), qn_ref[...], kn_ref[...])
                dq, dkp, dkc, dvp, dvc, dqn_h, dkn_h = vjp((do_ref[:, sl], dl_ref[:, sl]))
                dq_ref[:, sl] = dq.astype(BF16)
                dk_ref[:, sl] = (jnp.where(has_prev, ck_scr[:, sl] + dkp, 0.0)).astype(BF16)
                dv_ref[:, sl] = (jnp.where(has_prev, cv_scr[:, sl] + dvp, 0.0)).astype(BF16)
                ck_scr[:, sl] = dkc
                cv_scr[:, sl] = dvc
                dqn += dqn_h
                dkn += dkn_h
            _accum(dqn_ref, dqn, first)
            _accum(dkn_ref, dkn, first)

        @pl.when(n == nb)
        def _():
            dk_ref[...] = ck_scr[...].astype(BF16)
            dv_ref[...] = cv_scr[...].astype(BF16)

    cur = lambda t: pl.BlockSpec((None, B, Wd), lambda r, n: (r, jnp.minimum(n, nb - 1), t))
    prev = lambda t: pl.BlockSpec((None, B, Wd), lambda r, n: (r, jnp.clip(n - 1, 0, nb - 1), t))
    vec = pl.BlockSpec((1, ATT_DH), lambda r, n: (0, 0))
    kv_out = pl.BlockSpec((None, B, Wd), lambda r, n: (r, jnp.maximum(n - 1, 0), 0))
    return pl.pallas_call(
        body, name=name, grid=(dil, nb + 1),
        in_specs=[cur(0), cur(1), prev(1), cur(2), prev(2), vec, vec, cur(0), cur(0)],
        out_specs=[cur(0), kv_out, kv_out, vec, vec],
        out_shape=[_sds((dil, L, Wd), BF16)] * 3 + [_sds((1, ATT_DH), F32)] * 2,
        scratch_shapes=[pltpu.VMEM((B, Wd), F32), pltpu.VMEM((B, Wd), F32)],
        compiler_params=_cp(("arbitrary", "arbitrary")),
    )(qkv, qkv, qkv, qkv, qkv, q_norm, k_norm, do, dlse)


def _merge(o0, o1, o2, l0, l1, l2):
    m = jnp.maximum(jnp.maximum(l0, l1), l2)
    e0, e1, e2 = jnp.exp(l0 - m), jnp.exp(l1 - m), jnp.exp(l2 - m)
    return (e0 * o0 + e1 * o1 + e2 * o2) / (e0 + e1 + e2)


def _merge_specs(arrays, tm):
    return [pl.BlockSpec((a.shape[0], tm // a.shape[0], a.shape[2]), lambda i: (0, i, 0)) for a in arrays]


def merge_fwd(outs, lses, name):
    Wd = outs[0].shape[2]
    T = outs[0].shape[0] * outs[0].shape[1]
    tm = _row_tile(T)
    dils = [a.shape[0] for a in outs] * 2

    def body(*refs):
        ins, out_ref, scr = refs[:6], refs[6], refs[7]
        vals = [ref[...].reshape(tm, Wd) for ref in ins]
        vals = [v if dl == 1 else _interleave(v, scr, dl) for v, dl in zip(vals, dils)]
        out_ref[...] = _merge(*vals).astype(BF16)

    return pl.pallas_call(body, name=name, grid=(T // tm,), in_specs=_merge_specs(list(outs) + list(lses), tm),
                          out_specs=pl.BlockSpec((tm, Wd), lambda i: (i, 0)), out_shape=_sds((T, Wd), BF16),
                          scratch_shapes=[pltpu.VMEM((Wd // LANES, tm, LANES), F32)],
                          compiler_params=_cp(("parallel",)))(*outs, *lses)


def merge_bwd(outs, lses, d, name):
    Wd = outs[0].shape[2]
    T = outs[0].shape[0] * outs[0].shape[1]
    tm = 256 if T % 256 == 0 else T
    dils = [a.shape[0] for a in outs] * 2

    def body(*refs):
        ins, d_ref, douts, scr = refs[:6], refs[6], refs[7:13], refs[13]
        vals = [ref[...].reshape(tm, Wd) for ref in ins]
        vals = [v if dl == 1 else _interleave(v, scr, dl) for v, dl in zip(vals, dils)]
        _, vjp = jax.vjp(_merge, *vals)
        for ref, val, dl in zip(douts, vjp(d_ref[...].astype(F32)), dils):
            ref[...] = (val if dl == 1 else _deinterleave(val, scr, dl)).reshape(ref.shape)

    specs = _merge_specs(list(outs) + list(lses), tm)
    res = pl.pallas_call(body, name=name, grid=(T // tm,),
                         in_specs=specs + [pl.BlockSpec((tm, Wd), lambda i: (i, 0))], out_specs=specs,
                         out_shape=[_sds(a.shape, F32) for a in list(outs) + list(lses)],
                         scratch_shapes=[pltpu.VMEM((Wd // LANES, tm, LANES), F32)],
                         compiler_params=_cp(("parallel",)))(*outs, *lses, d)
    return res[:3], res[3:]


def loss_head(y, target, name):
    T, D = y.shape
    tm = _row_tile(T)

    def body(y_ref, t_ref, l_ref, dy_ref):
        err = y_ref[...] - t_ref[...]
        dy_ref[...] = err * (1.0 / D)
        part = 0.5 * jnp.sum(jnp.sum(err * err, axis=1, keepdims=True) * (1.0 / D), axis=0, keepdims=True)
        _accum(l_ref, jnp.broadcast_to(part, (1, 128)), pl.program_id(0) == 0)

    row = pl.BlockSpec((tm, D), lambda i: (i, 0))
    return pl.pallas_call(body, name=name, grid=(T // tm,), in_specs=[row, row],
                          out_specs=[pl.BlockSpec((1, 128), lambda i: (0, 0)), row],
                          out_shape=[_sds((1, 128), F32), _sds((T, D), F32)],
                          compiler_params=_cp(("arbitrary",)))(y, target)


def ada_fwd(c_all, w, b, name):
    depth, D, n = w.shape

    def body(c_ref, w_ref, b_ref, o_ref):
        o_ref[...] = _dot(_silu(c_ref[...]), w_ref[...]) + b_ref[...]

    return pl.pallas_call(
        body, name=name, grid=(depth,),
        in_specs=[pl.BlockSpec((N_DEV, D), lambda i: (0, 0)), pl.BlockSpec((None, D, n), lambda i: (i, 0, 0)),
                  pl.BlockSpec((None, 1, n), lambda i: (i, 0, 0))],
        out_specs=pl.BlockSpec((None, N_DEV, n), lambda i: (i, 0, 0)),
        out_shape=_sds((depth, N_DEV, n), F32), compiler_params=_cp(("parallel",)),
    )(c_all, w, b)


def ada_bwd(c_all, dmod, name):
    depth, _, n = dmod.shape
    D = c_all.shape[1]

    def body(c_ref, d_ref, o_ref):
        o_ref[...] = _dot_tn(_silu(c_ref[...]), d_ref[...])

    return pl.pallas_call(
        body, name=name, grid=(depth,),
        in_specs=[pl.BlockSpec((N_DEV, D), lambda i: (0, 0)), pl.BlockSpec((None, N_DEV, n), lambda i: (i, 0, 0))],
        out_specs=pl.BlockSpec((None, D, n), lambda i: (i, 0, 0)),
        out_shape=_sds((depth, D, n), F32), compiler_params=_cp(("parallel",)),
    )(c_all, dmod)


def adamw(w, m, v, gparts, name):
    R, C = w.shape
    k = gparts.shape[0]
    tr = R
    for cand in (512, 256, 128, 64, 32, 16, 8):
        if R % cand == 0 and cand * C * 4 <= 2 * 1024 * 1024:
            tr = cand
            break
    bc1 = 1.0 - ADAM_B1 ** ADAM_STEP
    bc2 = 1.0 - ADAM_B2 ** ADAM_STEP

    def body(w_ref, m_ref, v_ref, gp_ref, g_ref, d_ref, nm_ref, nv_ref):
        g = gp_ref[0].astype(F32)
        for q in range(1, k):
            g = g + gp_ref[q].astype(F32)
        nm = ADAM_B1 * m_ref[...] + (1.0 - ADAM_B1) * g
        nv = ADAM_B2 * v_ref[...] + (1.0 - ADAM_B2) * (g * g)
        g_ref[...] = g
        nm_ref[...] = nm
        nv_ref[...] = nv
        d_ref[...] = -ADAM_LR * ((nm / bc1) / (jnp.sqrt(nv / bc2) + ADAM_EPS) + ADAM_WD * w_ref[...])

    row = pl.BlockSpec((tr, C), lambda i: (i, 0))
    return pl.pallas_call(
        body, name=name, grid=(R // tr,),
        in_specs=[row, row, row, pl.BlockSpec((k, tr, C), lambda i: (0, i, 0))],
        out_specs=[row] * 4, out_shape=[_sds((R, C), F32)] * 4, compiler_params=_cp(("parallel",)),
    )(w, m, v, gparts)


def _mesh_pos():
    return lax.axis_index("x"), lax.axis_index("y"), lax.axis_index("c")


def _other_chips(x, y):
    return [(1 - x, y), (x, 1 - y), (1 - x, 1 - y)]


_ANY = pl.BlockSpec(memory_space=pl.ANY)


def all_gather(shards, name):
    n = len(shards)

    def body(*refs):
        ins, outs = refs[:n], refs[n:2 * n]
        send_sems, recv_sems, local_sems = refs[2 * n:]
        x, y, c = _mesh_pos()
        me, sibling = (x, y, c), (x, y, 1 - c)
        chips = _other_chips(x, y)

        def slot(p):
            return 4 * p[0] + 2 * p[1] + p[2]

        def copy(a, k, block, to, src=None):
            dst = outs[a].at[slot(block)]
            return pltpu.make_async_remote_copy(
                src_ref=dst if src is None else src, dst_ref=dst, send_sem=send_sems.at[a, k],
                recv_sem=recv_sems.at[a, k], device_id=to, device_id_type=MESH)

        mine = [pltpu.make_async_copy(ins[a], outs[a].at[slot(me)], local_sems.at[a]) for a in range(n)]
        for cp in mine:
            cp.start()
        first = []
        for a in range(n):
            first.append(copy(a, 0, me, sibling, src=ins[a]))
            first += [copy(a, 1 + j, me, (*chip, c), src=ins[a]) for j, chip in enumerate(chips)]
        for cp in first:
            cp.start()
        passed = []
        for j, chip in enumerate(chips):
            for a in range(n):
                copy(a, 1 + j, (*chip, c), me).wait_recv()
                cp = copy(a, 4 + j, (*chip, c), sibling)
                cp.start()
                passed.append(cp)
        for a in range(n):
            copy(a, 0, sibling, me).wait_recv()
            for j, chip in enumerate(chips):
                copy(a, 4 + j, (*chip, 1 - c), me).wait_recv()
        for cp in first + passed:
            cp.wait_send()
        for cp in mine:
            cp.wait()

    return pl.pallas_call(
        body, name=name, in_specs=[_ANY] * n, out_specs=[_ANY] * n,
        out_shape=[_sds((N_DEV,) + s.shape, s.dtype) for s in shards],
        scratch_shapes=[pltpu.SemaphoreType.DMA((n, 7)), pltpu.SemaphoreType.DMA((n, 7)), pltpu.SemaphoreType.DMA((n,))],
        compiler_params=pltpu.CompilerParams(has_side_effects=True),
    )(*shards)


def exchange_pair(stacked, name):
    n = len(stacked)

    def body(*refs):
        ins, outs = refs[:n], refs[n:2 * n]
        send_sems, recv_sems = refs[2 * n:]
        x, y, c = _mesh_pos()
        copies = [pltpu.make_async_remote_copy(
            src_ref=ins[a].at[2 * q + (1 - c)], dst_ref=outs[a].at[q], send_sem=send_sems.at[a, q],
            recv_sem=recv_sems.at[a, q], device_id=(x, y, 1 - c), device_id_type=MESH)
            for a in range(n) for q in range(4)]
        for cp in copies:
            cp.start()
        for cp in copies:
            cp.wait()

    return pl.pallas_call(
        body, name=name, in_specs=[_ANY] * n, out_specs=[_ANY] * n,
        out_shape=[_sds((4,) + s.shape[1:], s.dtype) for s in stacked],
        scratch_shapes=[pltpu.SemaphoreType.DMA((n, 4)), pltpu.SemaphoreType.DMA((n, 4))],
        compiler_params=pltpu.CompilerParams(has_side_effects=True),
    )(*stacked)


def pair_add(stacked, got, c_idx, name):
    _, R, C = stacked.shape
    tr = R
    for cand in (512, 256, 128, 64, 32, 16):
        if R % cand == 0 and cand * C * 4 <= 2 * 1024 * 1024:
            tr = cand
            break

    def body(c_ref, s_ref, g_ref, o_ref):
        o_ref[...] = (s_ref[...] + g_ref[...]).astype(BF16)

    return pl.pallas_call(
        body, name=name,
        grid_spec=pltpu.PrefetchScalarGridSpec(
            num_scalar_prefetch=1, grid=(4, R // tr),
            in_specs=[pl.BlockSpec((None, tr, C), lambda q, i, c_ref: (2 * q + c_ref[0], i, 0)),
                      pl.BlockSpec((None, tr, C), lambda q, i, c_ref: (q, i, 0))],
            out_specs=pl.BlockSpec((None, tr, C), lambda q, i, c_ref: (q, i, 0))),
        out_shape=_sds((4, R, C), BF16),
        compiler_params=_cp(("parallel", "parallel")),
    )(c_idx, stacked, got)


def exchange_chips(parts, name):
    n = len(parts)

    def body(*refs):
        ins, outs = refs[:n], refs[n:2 * n]
        send_sems, recv_sems, local_sems = refs[2 * n:]
        x, y, c = _mesh_pos()
        myq = 2 * x + y
        mine = [pltpu.make_async_copy(ins[a].at[myq], outs[a].at[myq], local_sems.at[a]) for a in range(n)]
        for cp in mine:
            cp.start()
        copies = [pltpu.make_async_remote_copy(
            src_ref=ins[a].at[2 * chip[0] + chip[1]], dst_ref=outs[a].at[myq], send_sem=send_sems.at[a, j],
            recv_sem=recv_sems.at[a, j], device_id=(*chip, c), device_id_type=MESH)
            for a in range(n) for j, chip in enumerate(_other_chips(x, y))]
        for cp in copies:
            cp.start()
        for cp in copies:
            cp.wait()
        for cp in mine:
            cp.wait()

    return pl.pallas_call(
        body, name=name, in_specs=[_ANY] * n, out_specs=[_ANY] * n,
        out_shape=[_sds(s.shape, s.dtype) for s in parts],
        scratch_shapes=[pltpu.SemaphoreType.DMA((n, 3)), pltpu.SemaphoreType.DMA((n, 3)), pltpu.SemaphoreType.DMA((n,))],
        compiler_params=pltpu.CompilerParams(has_side_effects=True),
    )(*parts)


def _pad_lanes(v, start, width=128):
    return jnp.pad(v.astype(F32), (start, width - start - v.shape[0]))[None]


def _layer_fwd(i, x, mod, P):
    T, D = x.shape
    sh_m, sc_m, g_m, sh_f, sc_f, g_f = [mod[k:k + 1] for k in range(6)]
    sv = {"x0": x}
    tag = f"l{i}"
    if i % 2 == 0:
        e = i // 2
        h, proj = norm_proj(x, P["norm_mix"][i:i + 1], sc_m, sh_m, P["ev_main"][e], P["ev_main"].shape[2], tag + "_in")
        ba = matmul_nn(h, P["ev_ba"][e], F32, tag + "_ba")
        al = _pad_lanes(P["gdn_a_log"][e], GDN_HEADS)
        dt = _pad_lanes(P["gdn_dt_bias"][e], GDN_HEADS)
        gate = gate_fwd(ba, al, dt, tag + "_gate")
        beta, g = gate[:, :GDN_HEADS], gate[:, GDN_HEADS:2 * GDN_HEADS]
        grow = g.reshape(T // GDN_CHUNK, GDN_CHUNK, GDN_HEADS).transpose(0, 2, 1)
        qkv = gdn_prep_fwd(proj, P["gdn_conv_w"][e], tag + "_prep")
        chunks = gdn_prep_chunks_fwd(qkv, g, grow, beta, tag + "_chunks")
        o, states = gdn_fwd(*chunks, g, tag + "_gdn")
        cat = even_post_fwd(o, proj, P["gdn_norm"][e:e + 1], P["pool_w"][e], P["pool_scale"][e:e + 1], tag + "_post")
        x1, y_m = proj_res(cat, P["ev_w_out"][e], x, g_m, tag + "_out")
        sv.update(h=h, proj=proj, ba=ba, al=al, dt=dt, beta=beta, g=g, grow=grow, qkv=qkv, chunks=chunks, o=o,
                  states=states, a_m=cat)
    else:
        od = i // 2
        qn, kn = P["att_q_norm"][od:od + 1], P["att_k_norm"][od:od + 1]
        h, proj = [], []
        for gi, (_, dil) in enumerate(DIL_PATTERNS):
            hg, pg = norm_proj(x, P["norm_mix"][i:i + 1], sc_m, sh_m, P["od_w_in"][od], 3 * D, f"{tag}_in{gi}",
                               cols=(gi, 1), dil=dil)
            h.append(hg.reshape(T, -1))
            proj.append(pg.reshape(dil, T // dil, -1))
        res = [att_fwd(proj[gi], qn, kn, f"{tag}_att{gi}") for gi in range(len(DIL_PATTERNS))]
        outs, lses = [r[0] for r in res], [r[1] for r in res]
        merged = merge_fwd(outs, lses, tag + "_merge")
        x1, y_m = proj_res(merged, P["od_w_out"][od], x, g_m, tag + "_out")
        sv.update(h=h, proj=proj, outs=outs, lses=lses, a_m=merged)
    hf, up = norm_proj(x1, P["norm_ffn"][i:i + 1], sc_f, sh_f, P["ffn_w_up"][i], 1408, tag + "_up")
    a = ffn_mid_fwd(up, P["ffn_conv_w"][i], P["ffn_conv_b"][i:i + 1], tag + "_mid")
    x2, y_f = proj_res(a, P["ffn_w_down"][i], x1, g_f, tag + "_down")
    sv.update(y_m=y_m, x1=x1, hf=hf, up=up, a_f=a, y_f=y_f)
    return x2, sv


def _layer_bwd(i, dx2, sv, mod, P):
    T = dx2.shape[0]
    sh_m, sc_m, g_m, sh_f, sc_f, g_f = [mod[k:k + 1] for k in range(6)]
    tag = f"b{i}"
    G = {}
    F = P["ffn_w_down"].shape[1]
    dy, da, dg_f = bwd_out(dx2, g_f, sv["y_f"], P["ffn_w_down"][i], F, tag + "_down")
    G["ffn_w_down"] = matmul_tn(sv["a_f"], dy, F // 2, dy.shape[1], tag + "_wdown")
    dgate, dval, dcw, dcb = ffn_mid_bwd(sv["up"], da, P["ffn_conv_w"][i], P["ffn_conv_b"][i:i + 1], tag + "_mid")
    G["ffn_conv_w"], G["ffn_conv_b"] = dcw[:FFN_CONV], dcb[0]
    dx1, dnf, dsc_f, dsh_f = bwd_in([dgate, dval], P["ffn_w_up"][i], [0, 1], F, None,
                                    sv["x1"], P["norm_ffn"][i:i + 1], sc_f, sh_f, dx2, tag + "_up")
    G["norm_ffn"] = dnf[0]
    x0, h, proj = sv["x0"], sv["h"], sv["proj"]
    D = x0.shape[1]
    G["ffn_w_up"] = jnp.concatenate([matmul_tn(sv["hf"], d, D, F // 2, f"{tag}_wup{k}")
                                     for k, d in enumerate((dgate, dval))], axis=1)
    gain = P["norm_mix"][i:i + 1]
    if i % 2 == 0:
        e = i // 2
        W = GDN_HEADS * GDN_D
        dy, dcat, dg_m = bwd_out(dx1, g_m, sv["y_m"], P["ev_w_out"][e], 1024, tag + "_out")
        G["ev_w_out"] = matmul_tn(sv["a_m"], dy, 1024, D, tag + "_wout")
        do, dzp, dgn, dpw, dps = even_post_bwd(sv["o"], proj, P["gdn_norm"][e:e + 1], P["pool_w"][e],
                                               P["pool_scale"][e:e + 1], dcat, tag + "_post")
        G["gdn_norm"], G["pool_w"], G["pool_scale"] = dgn[0], dpw, dps[0]
        *dchunks, dgc_scan = gdn_bwd(*sv["chunks"], sv["g"], sv["states"], do, tag + "_gdn")
        dqkv, dgc, dgr, dbeta = gdn_prep_chunks_bwd(sv["qkv"], sv["g"], sv["grow"], sv["beta"], *dchunks, tag + "_chunks")
        dg = dgc + dgc_scan + dgr.transpose(0, 2, 1).reshape(T, GDN_HEADS)
        dgate128 = jnp.concatenate([dbeta, dg, jnp.zeros((T, 128 - 2 * GDN_HEADS), F32)], axis=1)
        dba, dal, ddt = gate_bwd(sv["ba"], sv["al"], sv["dt"], dgate128, tag + "_gate")
        G["gdn_a_log"], G["gdn_dt_bias"] = dal[0, GDN_HEADS:2 * GDN_HEADS], ddt[0, GDN_HEADS:2 * GDN_HEADS]
        dqkv_raw, dconv = gdn_prep_bwd(proj, dqkv, P["gdn_conv_w"][e], tag + "_prep")
        G["gdn_conv_w"] = dconv[:GDN_CONV]
        dh = bwd_in([dba], P["ev_ba"][e], [0], 128, None, None, None, None, None, None, tag + "_in0")
        dh = bwd_in([dqkv_raw], P["ev_main"][e], [0], W, dh, None, None, None, None, None, tag + "_in1")
        dx0, dnm, dsc_m, dsh_m = bwd_in([dzp], P["ev_main"][e], [3], W, dh, x0, gain, sc_m, sh_m, dx1, tag + "_in2")
        gw_qkv = matmul_tn(h, dqkv_raw, 1024, W, tag + "_win1")
        gw_zp = matmul_tn(h, dzp, 1024, W, tag + "_win2")
        gw_ba = matmul_tn(h, dba, 1024, 128, tag + "_win0")
        G["ev_w_in"] = jnp.concatenate([gw_qkv, gw_zp[:, :W], gw_ba[:, :2 * GDN_HEADS], gw_zp[:, W:]], axis=1)
    else:
        od = i // 2
        qn, kn = P["att_q_norm"][od:od + 1], P["att_k_norm"][od:od + 1]
        dy, dmerged, dg_m = bwd_out(dx1, g_m, sv["y_m"], P["od_w_out"][od], 1024, tag + "_out")
        G["od_w_out"] = matmul_tn(sv["a_m"], dy, 1024, D, tag + "_wout")
        douts, dlses = merge_bwd(sv["outs"], sv["lses"], dmerged, tag + "_merge")
        dh, gws = None, []
        dqn_sum, dkn_sum = 0.0, 0.0
        ng = len(DIL_PATTERNS)
        for gi, (_, dil) in enumerate(DIL_PATTERNS):
            dq, dk, dv, dqn, dkn = att_bwd(proj[gi], qn, kn, douts[gi], dlses[gi], f"{tag}_att{gi}")
            dqn_sum, dkn_sum = dqn_sum + dqn[0], dkn_sum + dkn[0]
            gws += [matmul_tn(h[gi], d.reshape(T, D), 1024, D, f"{tag}_win{gi}{k}") for k, d in enumerate((dq, dk, dv))]
            cols = [3 * gi, 3 * gi + 1, 3 * gi + 2]
            ops = [dq, dk, dv] if dil > 1 else [d.reshape(T, D) for d in (dq, dk, dv)]
            if gi < ng - 1:
                dh = bwd_in(ops, P["od_w_in"][od], cols, D, dh, None, None, None, None, None, f"{tag}_in{gi}", dil=dil)
            else:
                dx0, dnm, dsc_m, dsh_m = bwd_in(ops, P["od_w_in"][od], cols, D, dh, x0, gain, sc_m, sh_m, dx1,
                                                f"{tag}_in{gi}", dil=dil)
        G["att_q_norm"], G["att_k_norm"] = dqn_sum, dkn_sum
        G["od_w_in"] = jnp.concatenate(gws, axis=1)
    G["norm_mix"] = dnm[0]
    dmod = jnp.concatenate([dsh_m, dsc_m, dg_m, dsh_f, dsc_f, dg_f], axis=0)
    return dx0, dmod, G


_PER_LAYER = ("norm_mix", "norm_ffn", "ffn_w_up", "ffn_conv_w", "ffn_conv_b", "ffn_w_down")
_PER_EVEN = ("ev_w_in", "ev_w_out", "gdn_conv_w", "gdn_a_log", "gdn_dt_bias", "gdn_norm", "pool_w", "pool_scale")
_PER_ODD = ("od_w_in", "od_w_out", "att_q_norm", "att_k_norm")


def device_step(x, mod, target, P):
    saved = []
    for i in range(DEPTH):
        x, sv = _layer_fwd(i, x, mod[i], P)
        saved.append(sv)
    loss, dx = loss_head(x, target, "loss")
    layer_grads, dmods = [None] * DEPTH, [None] * DEPTH
    for i in reversed(range(DEPTH)):
        dx, dmods[i], layer_grads[i] = _layer_bwd(i, dx, saved[i], mod[i], P)
    G = {k: jnp.stack([layer_grads[i][k] for i in range(DEPTH)]) for k in _PER_LAYER}
    G.update({k: jnp.stack([layer_grads[i][k] for i in range(0, DEPTH, 2)]) for k in _PER_EVEN})
    G.update({k: jnp.stack([layer_grads[i][k] for i in range(1, DEPTH, 2)]) for k in _PER_ODD})
    return loss, dx, jnp.stack(dmods), G


_WEIGHTS = ("ada_w", "ada_b", "norm_mix", "norm_ffn", "ev_w_in", "ev_w_out", "gdn_conv_w", "gdn_a_log", "gdn_dt_bias",
            "gdn_norm", "pool_w", "pool_scale", "od_w_in", "od_w_out", "att_q_norm", "att_k_norm", "ffn_w_up",
            "ffn_conv_w", "ffn_conv_b", "ffn_w_down")
_COL_SHARDED = ("ev_w_in", "od_w_in", "ffn_w_up")
_ROW_SHARDED = ("ev_w_out", "od_w_out", "ffn_w_down")
_SMALL_SHARDED = ("gdn_conv_w", "ffn_conv_w")
_REPLICATED = ("ada_b", "norm_mix", "norm_ffn", "gdn_a_log", "gdn_dt_bias", "gdn_norm", "pool_w", "pool_scale",
               "att_q_norm", "att_k_norm", "ffn_conv_b")
PACK_LANES = 128
PACK_ROWS = 8


def _pack(arrays):
    flat = jnp.concatenate([a.reshape(-1).astype(F32) for a in arrays])
    unit = PACK_LANES * PACK_ROWS
    padded = -(-flat.shape[0] // unit) * unit
    return jnp.pad(flat, (0, padded - flat.shape[0])).reshape(-1, PACK_LANES)


def _unpack(packed, shapes, lead=()):
    flat = packed.reshape(lead + (-1,))
    out, pos = [], 0
    for s in shapes:
        n = math.prod(s)
        out.append(flat[..., pos:pos + n].reshape(lead + tuple(s)))
        pos += n
    return out


def _unshard_cols(g):
    _, L, R, n = g.shape
    return g.transpose(1, 2, 0, 3).reshape(L, R, N_DEV * n)


def _shard_cols(full):
    L, R, N = full.shape
    n = N // N_DEV
    return full.reshape(L * R, N_DEV, n).transpose(1, 0, 2)


def _shard_rows(full):
    L, R, C = full.shape
    r = R // N_DEV
    return full.reshape(L, N_DEV, r, C).transpose(1, 0, 2, 3).reshape(N_DEV, L * r, C)


def kernel(x, c, ada_w, ada_b, norm_mix, norm_ffn, ev_w_in, ev_w_out, gdn_conv_w, gdn_a_log, gdn_dt_bias, gdn_norm, pool_w, pool_scale, od_w_in, od_w_out, att_q_norm, att_k_norm, ffn_w_up, ffn_conv_w, ffn_conv_b, ffn_w_down, loss_target, m_ada_w, m_ada_b, m_norm_mix, m_norm_ffn, m_ev_w_in, m_ev_w_out, m_gdn_conv_w, m_gdn_a_log, m_gdn_dt_bias, m_gdn_norm, m_pool_w, m_pool_scale, m_od_w_in, m_od_w_out, m_att_q_norm, m_att_k_norm, m_ffn_w_up, m_ffn_conv_w, m_ffn_conv_b, m_ffn_w_down, v_ada_w, v_ada_b, v_norm_mix, v_norm_ffn, v_ev_w_in, v_ev_w_out, v_gdn_conv_w, v_gdn_a_log, v_gdn_dt_bias, v_gdn_norm, v_pool_w, v_pool_scale, v_od_w_in, v_od_w_out, v_att_q_norm, v_att_k_norm, v_ffn_w_up, v_ffn_conv_w, v_ffn_conv_b, v_ffn_w_down):
    args = locals()
    Wl = {k: args[k] for k in _WEIGHTS}
    Ml = {k: args["m_" + k] for k in _WEIGHTS}
    Vl = {k: args["v_" + k] for k in _WEIGHTS}
    mx, my, mc = _mesh_pos()
    dev = 4 * mx + 2 * my + mc
    T, D = x.shape[1], x.shape[2]
    x2d, tgt = x.reshape(T, D), loss_target.reshape(T, D)

    small_shapes = [c.shape] + [Wl[k].shape for k in _SMALL_SHARDED]
    big = list(_COL_SHARDED + _ROW_SHARDED)
    gathered = all_gather([_pack([c] + [Wl[k] for k in _SMALL_SHARDED])] + [Wl[k].astype(BF16) for k in big], "gather_w")
    c_all, conv_g, conv_f = _unpack(gathered[0], small_shapes, lead=(N_DEV,))
    c_all = c_all.reshape(N_DEV, D)
    full = dict(zip(big, gathered[1:]))
    P = {k: Wl[k] for k in _REPLICATED}
    P["gdn_conv_w"], P["ffn_conv_w"] = _unshard_cols(conv_g), _unshard_cols(conv_f)
    for k in _COL_SHARDED:
        P[k] = _unshard_cols(full[k])
    for k in _ROW_SHARDED:
        g = full[k]
        P[k] = g.transpose(1, 0, 2, 3).reshape(g.shape[1], N_DEV * g.shape[2], g.shape[3])
    W = GDN_HEADS * GDN_D
    ev = P.pop("ev_w_in")
    P["ev_main"] = jnp.concatenate([ev[:, :, :4 * W], ev[:, :, 4 * W + 2 * GDN_HEADS:]], axis=2)
    P["ev_ba"] = jnp.pad(ev[:, :, 4 * W:4 * W + 2 * GDN_HEADS], ((0, 0), (0, 0), (0, 128 - 2 * GDN_HEADS)))

    n_ada = ada_w.shape[2]
    b_cols = lax.dynamic_slice_in_dim(ada_b, dev * n_ada, n_ada, axis=1)
    mod_cols = ada_fwd(c_all, ada_w, b_cols[:, None, :], "ada_fwd")
    mod_all, = all_gather([mod_cols], "gather_mod")
    mod = lax.dynamic_index_in_dim(mod_all, dev, axis=2, keepdims=False)
    mod = mod.transpose(1, 0, 2).reshape(DEPTH, 6, D)

    loss, dx, dmod, G = device_step(x2d, mod, tgt, P)
    loss = lax.psum(loss[0, 0], ("x", "y", "c"))

    G["ada_b"] = dmod.reshape(DEPTH, 6 * D)
    small = list(_REPLICATED) + list(_SMALL_SHARDED)
    parts_all, = all_gather([_pack([G[k] for k in small])], "gather_small")
    zeros = {k: jnp.zeros_like(G[k]) for k in _SMALL_SHARDED}
    packs = [_pack([src[k] for k in _REPLICATED] + [zeros[k] for k in _SMALL_SHARDED]) for src in (Wl, Ml, Vl)]
    res = adamw(*packs, parts_all, "adamw_small")
    shapes = [G[k].shape for k in small]
    out_g, out_d, out_m, out_v = ({k: a for k, a in zip(small, _unpack(r, shapes))} for r in res)
    dmod_all = _unpack(parts_all, shapes, lead=(N_DEV,))[0].reshape(N_DEV, DEPTH, 6 * D)
    dm_cols = lax.dynamic_slice_in_dim(dmod_all, dev * n_ada, n_ada, axis=2).transpose(1, 0, 2)
    g_ada = ada_bwd(c_all, dm_cols, "ada_bwd")

    def flat2(a):
        return a.reshape(-1, a.shape[-1])

    loc = {k: lax.dynamic_slice_in_dim(out_g[k], dev * Wl[k].shape[-1], Wl[k].shape[-1], axis=out_g[k].ndim - 1)
           for k in _SMALL_SHARDED}
    res = adamw(*[_pack([src[k] for k in _SMALL_SHARDED]) for src in (Wl, Ml, Vl)],
                _pack([loc[k] for k in _SMALL_SHARDED])[None], "adamw_conv")
    for dst, r in zip((out_g, out_d, out_m, out_v), res):
        dst.update(zip(_SMALL_SHARDED, _unpack(r, [Wl[k].shape for k in _SMALL_SHARDED])))
    res = adamw(flat2(ada_w), flat2(m_ada_w), flat2(v_ada_w), flat2(g_ada)[None], "adamw_ada")
    for dst, r in zip((out_g, out_d, out_m, out_v), res):
        dst["ada_w"] = r.reshape(ada_w.shape)

    stacked = [_shard_cols(G[k]) for k in _COL_SHARDED] + [_shard_rows(G[k]) for k in _ROW_SHARDED]
    got = exchange_pair(stacked, "rs_pair")
    c_idx = jnp.reshape(mc, (1,)).astype(jnp.int32)
    chip_parts = [pair_add(s, g, c_idx, f"rs_add_{k}") for s, g, k in zip(stacked, got, big)]
    parts = exchange_chips(chip_parts, "rs_chips")
    for k, p in zip(big, parts):
        res = adamw(flat2(Wl[k]), flat2(Ml[k]), flat2(Vl[k]), p, "adamw_" + k)
        for dst, r in zip((out_g, out_d, out_m, out_v), res):
            dst[k] = r.reshape(Wl[k].shape)

    return (loss, dx.reshape(x.shape), *[out_g[k] for k in _WEIGHTS], *[out_d[k] for k in _WEIGHTS],
            *[out_m[k] for k in _WEIGHTS], *[out_v[k] for k in _WEIGHTS])
```

```python
import functools
import math

import jax
import jax.numpy as jnp
from jax import lax
from jax.experimental import pallas as pl
from jax.experimental.pallas import tpu as pltpu

F32 = jnp.float32
BF16 = jnp.bfloat16
HI = lax.Precision.HIGHEST
MESH = pl.DeviceIdType.MESH

N_DEV = 8
RMS_EPS = 1e-6
DEPTH = 4
GDN_HEADS = 4
GDN_D = 128
GDN_CHUNK = 64
GDN_CONV = 4
POOL_WINDOWS = (2, 4, 8, 16)
DIL_PATTERNS = ((128, 1), (512, 4), (2048, 16))
ATT_HEADS = 8
ATT_DH = 128
ATT_BLOCK = 128
FFN_CONV = 3
ADAM_LR, ADAM_B1, ADAM_B2, ADAM_EPS, ADAM_WD, ADAM_STEP = 0.001, 0.9, 0.999, 1e-08, 0.01, 10

HALO = 16
NEG = -1e30
VMEM_LIMIT_BYTES = 56 * 1024 * 1024


def _cp(sem=None, **kw):
    return pltpu.CompilerParams(dimension_semantics=sem, vmem_limit_bytes=VMEM_LIMIT_BYTES, **kw)


def _sds(shape, dtype):
    return jax.ShapeDtypeStruct(tuple(shape), dtype)


def _dot(a, b):
    return jnp.dot(a.astype(BF16), b.astype(BF16), preferred_element_type=F32)


def _dot_nt(a, b):
    return lax.dot_general(a.astype(BF16), b.astype(BF16), (((1,), (1,)), ((), ())), preferred_element_type=F32)


def _dot_tn(a, b):
    return lax.dot_general(a.astype(BF16), b.astype(BF16), (((0,), (0,)), ((), ())), preferred_element_type=F32)


def _dot_hi(a, b):
    return jnp.dot(a, b, preferred_element_type=F32, precision=HI)


def _silu(x):
    return x * jax.nn.sigmoid(x)


def _modnorm(x, gain, sc, sh):
    y = x * lax.rsqrt(jnp.mean(x * x, axis=-1, keepdims=True) + RMS_EPS)
    return y * gain * (1.0 + sc) + sh


def _row_tile(T):
    return 512 if T % 512 == 0 else T


LANES = 128


def _deinterleave(val, scr, dil):
    tm, width = val.shape
    sub = tm // dil
    ncb = width // LANES
    for cb in range(ncb):
        scr[cb] = val[:, cb * LANES:(cb + 1) * LANES]
    return jnp.concatenate([jnp.concatenate([scr.at[cb][pl.ds(r, sub, stride=dil), :] for cb in range(ncb)], axis=1)
                            for r in range(dil)], axis=0)


def _interleave(val, scr, dil):
    tm, width = val.shape
    sub = tm // dil
    ncb = width // LANES
    for r in range(dil):
        for cb in range(ncb):
            scr.at[cb][pl.ds(r, sub, stride=dil), :] = val[r * sub:(r + 1) * sub, cb * LANES:(cb + 1) * LANES]
    return jnp.concatenate([scr[cb] for cb in range(ncb)], axis=1)


def norm_proj(x, gain, sc, sh, w, tn, name, cols=None, dil=1, qk_norms=None):
    T, D = x.shape
    c0, ncol = cols if cols is not None else (0, w.shape[1] // tn)
    N = ncol * tn
    tm = 1024 if (ncol > 1 and dil == 1 and T % 1024 == 0) else _row_tile(T)
    sub = tm // dil
    n_in = 5 + (2 if qk_norms is not None else 0)
    n_out = 2 + (1 if qk_norms is not None else 0)
    Wd = tn // 3

    def body(*refs):
        x_ref, g_ref, sc_ref, sh_ref, w_ref = refs[:5]
        h_ref, o_ref = refs[n_in:n_in + 2]
        h_scr = refs[n_in + n_out]

        @pl.when(pl.program_id(1) == 0)
        def _():
            h = _modnorm(x_ref[...], g_ref[...], sc_ref[...], sh_ref[...])
            if dil > 1:
                h = _deinterleave(h, refs[n_in + n_out + 1], dil)
            h_scr[...] = h.astype(BF16)
            h_ref[...] = h.reshape(h_ref.shape).astype(BF16)
        res = jnp.dot(h_scr[...], w_ref[...], preferred_element_type=F32)
        o_ref[...] = res.reshape(o_ref.shape).astype(BF16)
        if qk_norms is not None:
            qk_ref = refs[n_in + 2]
            parts = []
            for t, scale in enumerate((ATT_DH ** -0.5, 1.0)):
                for hd in range(Wd // ATT_DH):
                    c = t * Wd + hd * ATT_DH
                    parts.append(_head_norm(res[:, c:c + ATT_DH], refs[5 + t][...]) * scale)
            qk_ref[...] = jnp.concatenate(parts, axis=1).reshape(qk_ref.shape).astype(BF16)

    vec = pl.BlockSpec((1, D), lambda i, j: (0, 0))
    in_specs = [pl.BlockSpec((tm, D), lambda i, j: (i, 0)), vec, vec, vec, pl.BlockSpec((D, tn), lambda i, j: (0, c0 + j))]
    args = [x, gain, sc, sh, w]
    out_specs = [pl.BlockSpec((dil, sub, D), lambda i, j: (0, i, 0)), pl.BlockSpec((dil, sub, tn), lambda i, j: (0, i, j))]
    out_shape = [_sds((dil, T // dil, D), BF16), _sds((dil, T // dil, N), BF16)]
    if qk_norms is not None:
        assert ncol == 1
        in_specs += [pl.BlockSpec((1, ATT_DH), lambda i, j: (0, 0))] * 2
        args += list(qk_norms)
        out_specs.append(pl.BlockSpec((dil, sub, 2 * Wd), lambda i, j: (0, i, 0)))
        out_shape.append(_sds((dil, T // dil, 2 * Wd), BF16))
    scratch = [pltpu.VMEM((tm, D), BF16)] + ([pltpu.VMEM((D // LANES, tm, LANES), F32)] if dil > 1 else [])
    res = pl.pallas_call(
        body, name=name, grid=(T // tm, ncol), in_specs=in_specs, out_specs=out_specs, out_shape=out_shape,
        scratch_shapes=scratch, compiler_params=_cp(("parallel", "arbitrary")),
    )(*args)
    return [r.reshape(T, -1) for r in res] if (dil == 1 and qk_norms is None) else res


def matmul_nn(a, w, out_dtype, name):
    T, K = a.shape
    N = w.shape[1]
    tm = _row_tile(T)

    def body(a_ref, w_ref, o_ref):
        o_ref[...] = _dot(a_ref[...], w_ref[...]).astype(o_ref.dtype)

    return pl.pallas_call(
        body, name=name, grid=(T // tm,),
        in_specs=[pl.BlockSpec((tm, K), lambda i: (i, 0)), pl.BlockSpec((K, N), lambda i: (0, 0))],
        out_specs=pl.BlockSpec((tm, N), lambda i: (i, 0)),
        out_shape=_sds((T, N), out_dtype),
        compiler_params=_cp(("parallel",)),
    )(a, w)


def proj_res(a, w, x, gate, name):
    T, K = a.shape
    D = w.shape[1]
    tm = _row_tile(T)

    def body(a_ref, w_ref, x_ref, g_ref, o_ref, y_ref):
        y = jnp.dot(a_ref[...], w_ref[...], preferred_element_type=F32)
        y_ref[...] = y.astype(BF16)
        o_ref[...] = x_ref[...] + g_ref[...] * y

    return pl.pallas_call(
        body, name=name, grid=(T // tm,),
        in_specs=[pl.BlockSpec((tm, K), lambda i: (i, 0)), pl.BlockSpec((K, D), lambda i: (0, 0)),
                  pl.BlockSpec((tm, D), lambda i: (i, 0)), pl.BlockSpec((1, D), lambda i: (0, 0))],
        out_specs=[pl.BlockSpec((tm, D), lambda i: (i, 0)), pl.BlockSpec((tm, D), lambda i: (i, 0))],
        out_shape=[_sds((T, D), F32), _sds((T, D), BF16)],
        compiler_params=_cp(("parallel",)),
    )(a, w, x, gate)


def bwd_out(dx, gate, y, w, tk, name):
    T, D = dx.shape
    K = w.shape[0]
    tm = _row_tile(T)

    def body(dx_ref, g_ref, y_ref, w_ref, dy_ref, da_ref, dg_ref, dy_scr):
        i, j = pl.program_id(0), pl.program_id(1)

        @pl.when(j == 0)
        def _():
            dxv = dx_ref[...]
            dy = (dxv * g_ref[...]).astype(BF16)
            dy_scr[...] = dy
            dy_ref[...] = dy
            part = jnp.sum(dxv * y_ref[...].astype(F32), axis=0, keepdims=True)

            @pl.when(i == 0)
            def _():
                dg_ref[...] = part

            @pl.when(i > 0)
            def _():
                dg_ref[...] += part

        da_ref[...] = _dot_nt(dy_scr[...], w_ref[...]).astype(BF16)

    return pl.pallas_call(
        body, name=name, grid=(T // tm, K // tk),
        in_specs=[pl.BlockSpec((tm, D), lambda i, j: (i, 0)), pl.BlockSpec((1, D), lambda i, j: (0, 0)),
                  pl.BlockSpec((tm, D), lambda i, j: (i, 0)), pl.BlockSpec((tk, D), lambda i, j: (j, 0))],
        out_specs=[pl.BlockSpec((tm, D), lambda i, j: (i, 0)), pl.BlockSpec((tm, tk), lambda i, j: (i, j)),
                   pl.BlockSpec((1, D), lambda i, j: (0, 0))],
        out_shape=[_sds((T, D), BF16), _sds((T, K), BF16), _sds((1, D), F32)],
        scratch_shapes=[pltpu.VMEM((tm, D), BF16)],
        compiler_params=_cp(("arbitrary", "arbitrary")),
    )(dx, gate, y, w)


def bwd_in(a_list, w, col_blocks, tn, acc, x, gain, sc, sh, dx_res, name, dil=1):
    T = math.prod(a_list[0].shape[:-1])
    n_a = len(a_list)
    w_list = list(w) if isinstance(w, (list, tuple)) else [w] * n_a
    D = w_list[0].shape[0]
    tm = 256 if T % 256 == 0 else T
    sub = tm // dil
    tns = [tn if tn is not None else a.shape[-1] for a in a_list]
    nsteps = a_list[0].shape[-1] // tns[0]
    assert all(a.shape[-1] == nsteps * t for a, t in zip(a_list, tns))
    final = x is not None
    has_acc = acc is not None

    def body(*refs):
        a_refs = refs[:n_a]
        w_refs = refs[n_a:2 * n_a]
        pos = 2 * n_a
        acc_ref = refs[pos] if has_acc else None
        pos += int(has_acc)
        if final:
            x_ref, g_ref, sc_ref, sh_ref, dxr_ref = refs[pos:pos + 5]
            pos += 5
            dx_ref, dg_ref, dsc_ref, dsh_ref = refs[pos:pos + 4]
            pos += 4
        else:
            dh_ref = refs[pos]
            pos += 1
        acc_scr = refs[pos]
        i, j = pl.program_id(0), pl.program_id(1)
        part = _dot_nt(a_refs[0][...].reshape(tm, tns[0]), w_refs[0][...])
        for k in range(1, n_a):
            part += _dot_nt(a_refs[k][...].reshape(tm, tns[k]), w_refs[k][...])

        @pl.when(j == 0)
        def _():
            acc_scr[...] = part

        @pl.when(j > 0)
        def _():
            acc_scr[...] += part

        @pl.when(j == nsteps - 1)
        def _():
            dh = acc_scr[...]
            if dil > 1:
                dh = _interleave(dh, refs[pos + 1], dil)
            if has_acc:
                dh = dh + acc_ref[...]
            if not final:
                dh_ref[...] = dh
                return
            _, vjp = jax.vjp(_modnorm, x_ref[...], g_ref[...], sc_ref[...], sh_ref[...])
            dxn, dg, dsc, dsh = vjp(dh)
            dx_ref[...] = dxr_ref[...] + dxn

            @pl.when(i == 0)
            def _():
                dg_ref[...] = dg
                dsc_ref[...] = dsc
                dsh_ref[...] = dsh

            @pl.when(i > 0)
            def _():
                dg_ref[...] += dg
                dsc_ref[...] += dsc
                dsh_ref[...] += dsh

    row = pl.BlockSpec((tm, D), lambda i, j: (i, 0))
    vec = pl.BlockSpec((1, D), lambda i, j: (0, 0))
    if dil == 1:
        in_specs = [pl.BlockSpec((tm, t), lambda i, j: (i, j)) for t in tns]
    else:
        in_specs = [pl.BlockSpec((dil, sub, t), lambda i, j: (0, i, j)) for t in tns]
    in_specs += [pl.BlockSpec((D, t), functools.partial(lambda i, j, c0: (0, c0 + j), c0=c0))
                 for t, c0 in zip(tns, col_blocks)]
    args = list(a_list) + w_list
    if has_acc:
        in_specs.append(row)
        args.append(acc)
    if final:
        in_specs += [row, vec, vec, vec, row]
        args += [x, gain, sc, sh, dx_res]
        out_specs = [row, vec, vec, vec]
        out_shape = [_sds((T, D), F32)] + [_sds((1, D), F32)] * 3
    else:
        out_specs = row
        out_shape = _sds((T, D), F32)
    return pl.pallas_call(
        body, name=name, grid=(T // tm, nsteps), in_specs=in_specs, out_specs=out_specs, out_shape=out_shape,
        scratch_shapes=[pltpu.VMEM((tm, D), F32)] + ([pltpu.VMEM((D // LANES, tm, LANES), F32)] if dil > 1 else []),
        compiler_params=_cp(("arbitrary", "arbitrary")),
    )(*args)


def matmul_tn(a, b, tk, tn, name):
    T, K = a.shape
    N = b.shape[1]
    tt = 1024 if T % 1024 == 0 else T

    def body(a_ref, b_ref, o_ref):
        part = _dot_tn(a_ref[...], b_ref[...])

        @pl.when(pl.program_id(2) == 0)
        def _():
            o_ref[...] = part

        @pl.when(pl.program_id(2) > 0)
        def _():
            o_ref[...] += part

    return pl.pallas_call(
        body, name=name, grid=(K // tk, N // tn, T // tt),
        in_specs=[pl.BlockSpec((tt, tk), lambda k, n, t: (t, k)), pl.BlockSpec((tt, tn), lambda k, n, t: (t, n))],
        out_specs=pl.BlockSpec((tk, tn), lambda k, n, t: (k, n)),
        out_shape=_sds((K, N), F32),
        compiler_params=_cp(("parallel", "parallel", "arbitrary")),
    )(a, b)


def _halo_specs(tm, tc, col_of):
    per = tm // HALO

    def prev(j, i):
        return (jnp.maximum(i * per - 1, 0), col_of(j))

    def nxt(j, i, last):
        return (jnp.minimum((i + 1) * per, last), col_of(j))

    return prev, nxt, per


def _shift_down(ext, s):
    return pltpu.roll(ext, s, 0)


def _shift_up(ext, s):
    return pltpu.roll(ext, ext.shape[0] - s, 0)


def _conv_ext(ext, w):
    K = w.shape[0]
    out = w[K - 1:K] * ext
    for s in range(1, K):
        out += w[K - 1 - s:K - s] * _shift_down(ext, s)
    return out


def _conv_t_ext(dext, w):
    K = w.shape[0]
    out = w[K - 1:K] * dext
    for s in range(1, K):
        out += w[K - 1 - s:K - s] * _shift_up(dext, s)
    return out


def _conv_dw(dc, ext, K, tm):
    rowid = lax.broadcasted_iota(jnp.int32, (8, dc.shape[1]), 0)
    out = jnp.zeros((8, dc.shape[1]), F32)
    for j in range(K):
        s = K - 1 - j
        xs = ext if s == 0 else _shift_down(ext, s)
        out = jnp.where(rowid == j, jnp.sum(dc * xs[HALO:HALO + tm], axis=0, keepdims=True), out)
    return out


def _accum(ref, val, first):
    @pl.when(first)
    def _():
        ref[...] = val

    @pl.when(jnp.logical_not(first))
    def _():
        ref[...] += val


def ffn_mid_fwd(up, conv_w, conv_b, name):
    T, two_f = up.shape
    F = two_f // 2
    tm = _row_tile(T)
    tc = F // 2
    nct = F // tc
    prev, _, per = _halo_specs(tm, tc, lambda j: j)

    def body(g_ref, gp_ref, v_ref, w_ref, b_ref, a_ref):
        i = pl.program_id(1)
        gp = jnp.where(i > 0, gp_ref[...].astype(F32), 0.0)
        ext = jnp.concatenate([gp, g_ref[...].astype(F32)], axis=0)
        c = _conv_ext(ext, w_ref[...])[HALO:] + b_ref[...]
        a_ref[...] = (_silu(c) * v_ref[...].astype(F32)).astype(BF16)

    return pl.pallas_call(
        body, name=name, grid=(nct, T // tm),
        in_specs=[pl.BlockSpec((tm, tc), lambda j, i: (i, j)), pl.BlockSpec((HALO, tc), prev),
                  pl.BlockSpec((tm, tc), lambda j, i: (i, j + nct)),
                  pl.BlockSpec((FFN_CONV, tc), lambda j, i: (0, j)), pl.BlockSpec((1, tc), lambda j, i: (0, j))],
        out_specs=pl.BlockSpec((tm, tc), lambda j, i: (i, j)),
        out_shape=_sds((T, F), BF16),
        compiler_params=_cp(("parallel", "arbitrary")),
    )(up, up, up, conv_w, conv_b)


def ffn_mid_bwd(up, da, conv_w, conv_b, name):
    T, two_f = up.shape
    F = two_f // 2
    tm = _row_tile(T)
    tc = F // 2
    nct = F // tc
    nrow = T // tm
    prev, nxt, per = _halo_specs(tm, tc, lambda j: j)
    last = T // HALO - 1
    nxt_g = functools.partial(nxt, last=last)

    def nxt_v(j, i):
        return (jnp.minimum((i + 1) * per, last), j + nct)

    def body(g_ref, gp_ref, gn_ref, v_ref, vn_ref, da_ref, dan_ref, w_ref, b_ref, dg_ref, dv_ref, dw_ref, db_ref):
        i = pl.program_id(1)
        w = w_ref[...]
        gp = jnp.where(i > 0, gp_ref[...].astype(F32), 0.0)
        inside = i < nrow - 1
        ext = jnp.concatenate([gp, g_ref[...].astype(F32), gn_ref[...].astype(F32)], axis=0)
        zero = jnp.zeros((HALO, tc), F32)
        v_ext = jnp.concatenate([zero, v_ref[...].astype(F32), vn_ref[...].astype(F32)], axis=0)
        da_ext = jnp.concatenate([zero, da_ref[...].astype(F32), jnp.where(inside, dan_ref[...].astype(F32), 0.0)], axis=0)
        c = _conv_ext(ext, w) + b_ref[...]
        sg = jax.nn.sigmoid(c)
        sil = c * sg
        dc = da_ext * v_ext * (sg * (1.0 + c * (1.0 - sg)))
        dv_ref[...] = (da_ext * sil)[HALO:HALO + tm].astype(BF16)
        dg_ref[...] = _conv_t_ext(dc, w)[HALO:HALO + tm].astype(BF16)
        dcm = dc[HALO:HALO + tm]
        _accum(dw_ref, _conv_dw(dcm, ext, FFN_CONV, tm), i == 0)
        _accum(db_ref, jnp.sum(dcm, axis=0, keepdims=True), i == 0)

    main = lambda j, i: (i, j)
    main_v = lambda j, i: (i, j + nct)
    return pl.pallas_call(
        body, name=name, grid=(nct, nrow),
        in_specs=[pl.BlockSpec((tm, tc), main), pl.BlockSpec((HALO, tc), prev), pl.BlockSpec((HALO, tc), nxt_g),
                  pl.BlockSpec((tm, tc), main_v), pl.BlockSpec((HALO, tc), nxt_v),
                  pl.BlockSpec((tm, tc), main), pl.BlockSpec((HALO, tc), nxt_g),
                  pl.BlockSpec((FFN_CONV, tc), lambda j, i: (0, j)), pl.BlockSpec((1, tc), lambda j, i: (0, j))],
        out_specs=[pl.BlockSpec((tm, tc), main), pl.BlockSpec((tm, tc), main),
                   pl.BlockSpec((8, tc), lambda j, i: (0, j)), pl.BlockSpec((1, tc), lambda j, i: (0, j))],
        out_shape=[_sds((T, F), BF16), _sds((T, F), BF16), _sds((8, F), F32), _sds((1, F), F32)],
        compiler_params=_cp(("parallel", "arbitrary")),
    )(up, up, up, up, up, da, da, conv_w, conv_b)


def _l2n(x):
    return x * lax.rsqrt(jnp.sum(x * x, axis=-1, keepdims=True) + RMS_EPS)


def _qkv_tok(c, normed):
    s = _silu(c)
    n = jnp.concatenate([_l2n(s[:, h * GDN_D:(h + 1) * GDN_D]) for h in range(GDN_HEADS)], axis=1)
    return jnp.where(normed, n, s)


def gdn_prep_fwd(proj, conv_w, name):
    T = proj.shape[0]
    W = GDN_HEADS * GDN_D
    tm = _row_tile(T)
    prev, _, per = _halo_specs(tm, W, lambda j: j)

    def body(x_ref, xp_ref, w_ref, o_ref):
        j, i = pl.program_id(0), pl.program_id(1)
        xp = jnp.where(i > 0, xp_ref[...].astype(F32), 0.0)
        ext = jnp.concatenate([xp, x_ref[...].astype(F32)], axis=0)
        c = _conv_ext(ext, w_ref[...])[HALO:]
        o_ref[...] = _qkv_tok(c, j < 2)

    return pl.pallas_call(
        body, name=name, grid=(3, T // tm),
        in_specs=[pl.BlockSpec((tm, W), lambda j, i: (i, j)), pl.BlockSpec((HALO, W), prev),
                  pl.BlockSpec((GDN_CONV, W), lambda j, i: (0, j))],
        out_specs=pl.BlockSpec((tm, W), lambda j, i: (i, j)),
        out_shape=_sds((T, 3 * W), F32),
        compiler_params=_cp(("parallel", "arbitrary")),
    )(proj, proj, conv_w)


def gdn_prep_bwd(proj, dqkv, conv_w, name):
    T = proj.shape[0]
    W = GDN_HEADS * GDN_D
    tm = _row_tile(T)
    nrow = T // tm
    prev, nxt, per = _halo_specs(tm, W, lambda j: j)
    nxt = functools.partial(nxt, last=T // HALO - 1)

    def body(x_ref, xp_ref, xn_ref, d_ref, dn_ref, w_ref, dx_ref, dw_ref):
        j, i = pl.program_id(0), pl.program_id(1)
        w = w_ref[...]
        xp = jnp.where(i > 0, xp_ref[...].astype(F32), 0.0)
        ext = jnp.concatenate([xp, x_ref[...].astype(F32), xn_ref[...].astype(F32)], axis=0)
        d_ext = jnp.concatenate([jnp.zeros((HALO, W), F32), d_ref[...],
                                 jnp.where(i < nrow - 1, dn_ref[...], 0.0)], axis=0)
        c = _conv_ext(ext, w)
        _, vjp = jax.vjp(lambda cc: _qkv_tok(cc, j < 2), c)
        dc, = vjp(d_ext)
        dx_ref[...] = _conv_t_ext(dc, w)[HALO:HALO + tm].astype(BF16)
        _accum(dw_ref, _conv_dw(dc[HALO:HALO + tm], ext, GDN_CONV, tm), i == 0)

    main = lambda j, i: (i, j)
    return pl.pallas_call(
        body, name=name, grid=(3, nrow),
        in_specs=[pl.BlockSpec((tm, W), main), pl.BlockSpec((HALO, W), prev), pl.BlockSpec((HALO, W), nxt),
                  pl.BlockSpec((tm, W), main), pl.BlockSpec((HALO, W), nxt),
                  pl.BlockSpec((GDN_CONV, W), lambda j, i: (0, j))],
        out_specs=[pl.BlockSpec((tm, W), main), pl.BlockSpec((8, W), lambda j, i: (0, j))],
        out_shape=[_sds((T, 3 * W), BF16), _sds((8, 3 * W), F32)],
        compiler_params=_cp(("parallel", "arbitrary")),
    )(proj, proj, proj, dqkv, dqkv, conv_w)


def _gate_tok(ba, a_log, dt_bias):
    z = ba + dt_bias
    softplus = jnp.maximum(z, 0.0) + jnp.log(1.0 + jnp.exp(-jnp.abs(z)))
    lane = lax.broadcasted_iota(jnp.int32, ba.shape, 1)
    raw = jnp.where(lane < GDN_HEADS, jax.nn.sigmoid(ba), -jnp.exp(a_log) * softplus)
    n = ba.shape[0]
    r = lax.broadcasted_iota(jnp.int32, (n, n), 0)
    c = lax.broadcasted_iota(jnp.int32, (n, n), 1)
    in_chunk_before = (r >= c) & ((r - c) <= (r & (GDN_CHUNK - 1)))
    return jnp.where(lane < GDN_HEADS, raw, _dot_hi(in_chunk_before.astype(F32), raw))


def gate_fwd(ba, a_log, dt_bias, name):
    T, L = ba.shape
    tm = _row_tile(T)

    def body(ba_ref, al_ref, dt_ref, o_ref):
        o_ref[...] = _gate_tok(ba_ref[...], al_ref[...], dt_ref[...])

    vec = pl.BlockSpec((1, L), lambda i: (0, 0))
    return pl.pallas_call(
        body, name=name, grid=(T // tm,),
        in_specs=[pl.BlockSpec((tm, L), lambda i: (i, 0)), vec, vec],
        out_specs=pl.BlockSpec((tm, L), lambda i: (i, 0)), out_shape=_sds((T, L), F32),
        compiler_params=_cp(("parallel",)),
    )(ba, a_log, dt_bias)


def gate_bwd(ba, a_log, dt_bias, dout, name):
    T, L = ba.shape
    tm = _row_tile(T)

    def body(ba_ref, al_ref, dt_ref, d_ref, dba_ref, dal_ref, ddt_ref):
        _, vjp = jax.vjp(_gate_tok, ba_ref[...], al_ref[...], dt_ref[...])
        dba, dal, ddt = vjp(d_ref[...])
        dba_ref[...] = dba.astype(BF16)
        first = pl.program_id(0) == 0
        _accum(dal_ref, dal, first)
        _accum(ddt_ref, ddt, first)

    vec = pl.BlockSpec((1, L), lambda i: (0, 0))
    row = pl.BlockSpec((tm, L), lambda i: (i, 0))
    return pl.pallas_call(
        body, name=name, grid=(T // tm,),
        in_specs=[row, vec, vec, row], out_specs=[row, vec, vec],
        out_shape=[_sds((T, L), BF16), _sds((1, L), F32), _sds((1, L), F32)],
        compiler_params=_cp(("arbitrary",)),
    )(ba, a_log, dt_bias, dout)


_B_NN = (((2,), (1,)), ((0,), (0,)))
_B_NT = (((2,), (2,)), ((0,), (0,)))
GDN_GROUP = 8


def _split_bf16(a):
    hi = a.astype(BF16)
    return hi, (a - hi.astype(F32)).astype(BF16)


def _bdot3(a, b):
    ah, al = _split_bf16(a)
    bh, bl = _split_bf16(b)
    dg = lambda x, y: lax.dot_general(x, y, _B_NN, preferred_element_type=F32)
    return dg(ah, bh) + (dg(ah, bl) + dg(al, bh))


def _gdn_prep(q, k, v, gcol, grow, bcol):
    B, C, dk = q.shape
    r = lax.broadcasted_iota(jnp.int32, (1, C, C), 1)
    cidx = lax.broadcasted_iota(jnp.int32, (1, C, C), 2)
    causal = r >= cidx
    strict = r > cidx
    decay = jnp.where(causal, jnp.exp(jnp.where(causal, gcol - grow, 0.0)), 0.0)
    qs = q * (dk ** -0.5)
    kb = k * bcol
    kk = lax.dot_general(kb.astype(BF16), k.astype(BF16), _B_NT, preferred_element_type=F32)
    L = jnp.where(strict, kk * decay, 0.0)
    tinv = (r == cidx).astype(F32) - L
    p = L
    n = 2
    while n < C:
        p = _bdot3(p, p)
        tinv = tinv + _bdot3(tinv, p)
        n *= 2
    egc = jnp.exp(gcol)
    u = _bdot3(tinv, v * bcol)
    w = _bdot3(tinv, kb * egc)
    qk = lax.dot_general(qs.astype(BF16), k.astype(BF16), _B_NT, preferred_element_type=F32)
    intra = jnp.where(causal, qk * decay, 0.0)
    last = lax.broadcasted_iota(jnp.int32, (1, C, 1), 1) == C - 1
    gt = jnp.sum(jnp.where(last, gcol, 0.0), axis=1, keepdims=True)
    return w, u, qs * egc, k * jnp.exp(gt - gcol), intra


def _gdn_scan(S, w, u, qg, kdec, intra, gcol):
    C = w.shape[0]
    last = lax.broadcasted_iota(jnp.int32, (C, 1), 0) == C - 1
    gt = jnp.sum(jnp.where(last, gcol, 0.0), axis=0, keepdims=True)
    v_new = u - _dot(w, S)
    o = _dot(qg, S) + _dot(intra, v_new)
    return S * jnp.exp(gt) + _dot_tn(kdec, v_new), o


def _gdn_specs(T, rev=False):
    H, D, C, B = GDN_HEADS, GDN_D, GDN_CHUNK, GDN_GROUP
    nsteps = T // (B * C)
    at = (lambda n: nsteps - 1 - n) if rev else (lambda n: n)
    rows = lambda w: pl.BlockSpec((B * C, w), lambda n: (at(n), 0))
    grow = pl.BlockSpec((B, H, C), lambda n: (at(n), 0, 0))
    sq = pl.BlockSpec((B, H, C, C), lambda n: (at(n), 0, 0, 0))
    st = pl.BlockSpec((B, H, D, D), lambda n: (at(n), 0, 0, 0))
    return rows, grow, sq, st, nsteps


def _head_chunks(ref, h, width):
    return ref[:, h * width:(h + 1) * width].reshape(GDN_GROUP, GDN_CHUNK, width)


def gdn_prep_chunks_fwd(qkv, gcol, grow, bcol, name):
    T = qkv.shape[0]
    H, D, C, B = GDN_HEADS, GDN_D, GDN_CHUNK, GDN_GROUP
    W = H * D
    rows, grow_s, sq, _, nsteps = _gdn_specs(T)

    def body(x_ref, gc_ref, gr_ref, b_ref, w_ref, u_ref, qg_ref, kd_ref, in_ref):
        for h in range(H):
            q, k, v = [x_ref[:, j * W + h * D:j * W + (h + 1) * D].reshape(B, C, D) for j in range(3)]
            outs = _gdn_prep(q, k, v, _head_chunks(gc_ref, h, 1), gr_ref[:, h:h + 1, :], _head_chunks(b_ref, h, 1))
            for ref, val in zip((w_ref, u_ref, qg_ref, kd_ref), outs[:4]):
                ref[:, h * D:(h + 1) * D] = val.reshape(B * C, D).astype(ref.dtype)
            in_ref[:, h] = outs[4].astype(BF16)

    return pl.pallas_call(
        body, name=name, grid=(nsteps,),
        in_specs=[rows(3 * W), rows(H), grow_s, rows(H)],
        out_specs=[rows(W), rows(W), rows(W), rows(W), sq],
        out_shape=[_sds((T, W), BF16), _sds((T, W), F32), _sds((T, W), BF16), _sds((T, W), BF16),
                   _sds((T // C, H, C, C), BF16)],
        compiler_params=_cp(("parallel",)),
    )(qkv, gcol, grow, bcol)


def gdn_prep_chunks_bwd(qkv, gcol, grow, bcol, dw, du, dqg, dkd, dintra, name):
    T = qkv.shape[0]
    H, D, C, B = GDN_HEADS, GDN_D, GDN_CHUNK, GDN_GROUP
    W = H * D
    rows, grow_s, sq, _, nsteps = _gdn_specs(T)

    def body(x_ref, gc_ref, gr_ref, b_ref, dw_ref, du_ref, dqg_ref, dkd_ref, din_ref, dx_ref, dgc_ref, dgr_ref, db_ref):
        for h in range(H):
            q, k, v = [x_ref[:, j * W + h * D:j * W + (h + 1) * D].reshape(B, C, D) for j in range(3)]
            _, vjp = jax.vjp(_gdn_prep, q, k, v, _head_chunks(gc_ref, h, 1), gr_ref[:, h:h + 1, :],
                             _head_chunks(b_ref, h, 1))
            cots = tuple(_head_chunks(r, h, D) for r in (dw_ref, du_ref, dqg_ref, dkd_ref)) + (din_ref[:, h],)
            dq, dk, dv, dgc, dgr, db = vjp(cots)
            for j, val in enumerate((dq, dk, dv)):
                dx_ref[:, j * W + h * D:j * W + (h + 1) * D] = val.reshape(B * C, D)
            dgc_ref[:, h:h + 1] = dgc.reshape(B * C, 1)
            dgr_ref[:, h:h + 1, :] = dgr
            db_ref[:, h:h + 1] = db.reshape(B * C, 1)

    return pl.pallas_call(
        body, name=name, grid=(nsteps,),
        in_specs=[rows(3 * W), rows(H), grow_s, rows(H), rows(W), rows(W), rows(W), rows(W), sq],
        out_specs=[rows(3 * W), rows(H), grow_s, rows(H)],
        out_shape=[_sds((T, 3 * W), F32), _sds((T, H), F32), _sds((T // C, H, C), F32), _sds((T, H), F32)],
        compiler_params=_cp(("parallel",)),
    )(qkv, gcol, grow, bcol, dw, du, dqg, dkd, dintra)


def gdn_fwd(w, u, qg, kdec, intra, gcol, name):
    T = w.shape[0]
    H, D, C, B = GDN_HEADS, GDN_D, GDN_CHUNK, GDN_GROUP
    W = H * D
    rows, _, sq, st, nsteps = _gdn_specs(T)

    def body(w_ref, u_ref, qg_ref, kd_ref, in_ref, gc_ref, o_ref, s_ref, s_scr):
        @pl.when(pl.program_id(0) == 0)
        def _():
            s_scr[...] = jnp.zeros_like(s_scr)

        def chunk(cb, carry):
            rs = pl.ds(pl.multiple_of(cb * C, C), C)
            for h in range(H):
                cs = slice(h * D, (h + 1) * D)
                S = s_scr[h]
                s_ref[cb, h] = S
                S_new, o = _gdn_scan(S, w_ref[rs, cs], u_ref[rs, cs], qg_ref[rs, cs], kd_ref[rs, cs], in_ref[cb, h],
                                     gc_ref[rs, h:h + 1])
                s_scr[h] = S_new
                o_ref[rs, cs] = o
            return carry

        lax.fori_loop(0, B, chunk, 0)

    return pl.pallas_call(
        body, name=name, grid=(nsteps,),
        in_specs=[rows(W), rows(W), rows(W), rows(W), sq, rows(H)],
        out_specs=[rows(W), st],
        out_shape=[_sds((T, W), F32), _sds((T // C, H, D, D), F32)],
        scratch_shapes=[pltpu.VMEM((H, D, D), F32)],
        compiler_params=_cp(("arbitrary",)),
    )(w, u, qg, kdec, intra, gcol)


def gdn_bwd(w, u, qg, kdec, intra, gcol, states, do, name):
    T = w.shape[0]
    H, D, C, B = GDN_HEADS, GDN_D, GDN_CHUNK, GDN_GROUP
    W = H * D
    rows, _, sq, st, nsteps = _gdn_specs(T, rev=True)

    def body(w_ref, u_ref, qg_ref, kd_ref, in_ref, gc_ref, s_ref, do_ref,
             dw_ref, du_ref, dqg_ref, dkd_ref, din_ref, dgc_ref, ds_scr):
        @pl.when(pl.program_id(0) == 0)
        def _():
            ds_scr[...] = jnp.zeros_like(ds_scr)

        def chunk(t, carry):
            cb = B - 1 - t
            rs = pl.ds(pl.multiple_of(cb * C, C), C)
            for h in range(H):
                cs = slice(h * D, (h + 1) * D)
                _, vjp = jax.vjp(_gdn_scan, s_ref[cb, h], w_ref[rs, cs].astype(F32), u_ref[rs, cs],
                                 qg_ref[rs, cs].astype(F32), kd_ref[rs, cs].astype(F32), in_ref[cb, h].astype(F32),
                                 gc_ref[rs, h:h + 1])
                dS, dw, du, dqg, dkd, din, dgc = vjp((ds_scr[h], do_ref[rs, cs]))
                ds_scr[h] = dS
                dw_ref[rs, cs] = dw
                du_ref[rs, cs] = du
                dqg_ref[rs, cs] = dqg
                dkd_ref[rs, cs] = dkd
                din_ref[cb, h] = din
                dgc_ref[rs, h:h + 1] = dgc
            return carry

        lax.fori_loop(0, B, chunk, 0)

    return pl.pallas_call(
        body, name=name, grid=(nsteps,),
        in_specs=[rows(W), rows(W), rows(W), rows(W), sq, rows(H), st, rows(W)],
        out_specs=[rows(W), rows(W), rows(W), rows(W), sq, rows(H)],
        out_shape=[_sds((T, W), F32)] * 4 + [_sds((T // C, H, C, C), F32), _sds((T, H), F32)],
        scratch_shapes=[pltpu.VMEM((H, D, D), F32)],
        compiler_params=_cp(("arbitrary",)),
    )(w, u, qg, kdec, intra, gcol, states, do)


def _gated_norm(o, z, gain):
    outs = []
    for h in range(GDN_HEADS):
        oh = o[:, h * GDN_D:(h + 1) * GDN_D]
        y = oh * lax.rsqrt(jnp.mean(oh * oh, axis=-1, keepdims=True) + RMS_EPS) * gain
        outs.append(y * _silu(z[:, h * GDN_D:(h + 1) * GDN_D]))
    return jnp.concatenate(outs, axis=1)


def _window_sums(ext, shift):
    outs = []
    s = ext
    step = 1
    for gi, win in enumerate(POOL_WINDOWS):
        while step < win:
            s = s + shift(s, step)
            step *= 2
        outs.append(s[:, gi * GDN_D:(gi + 1) * GDN_D])
    return jnp.concatenate(outs, axis=1)


def _pool_counts(t0, rows, width):
    t1 = (t0 + 1 + lax.broadcasted_iota(jnp.int32, (rows, width), 0)).astype(F32)
    lane = lax.broadcasted_iota(jnp.int32, (rows, width), 1)
    win = jnp.full((rows, width), float(POOL_WINDOWS[-1]), F32)
    for gi in reversed(range(len(POOL_WINDOWS) - 1)):
        win = jnp.where(lane < (gi + 1) * GDN_D, float(POOL_WINDOWS[gi]), win)
    return jnp.minimum(t1, win)


def even_post_fwd(o, proj, gdn_norm, pool_w, pool_scale, name):
    T = o.shape[0]
    W = GDN_HEADS * GDN_D
    tm = _row_tile(T)
    per = tm // HALO

    def body(o_ref, z_ref, p_ref, pp_ref, gn_ref, pw_ref, ps_ref, out_ref):
        i = pl.program_id(0)
        out_ref[:, :W] = _gated_norm(o_ref[...], z_ref[...].astype(F32), gn_ref[...]).astype(BF16)
        pp = jnp.where(i > 0, pp_ref[...].astype(F32), 0.0)
        ext = jnp.concatenate([pp, p_ref[...].astype(F32)], axis=0)
        pooled = (_window_sums(ext, _shift_down)[HALO:] / _pool_counts(i * tm, tm, W)) - ext[HALO:]
        for gi in range(len(POOL_WINDOWS)):
            sl = slice(gi * GDN_D, (gi + 1) * GDN_D)
            y = _dot(pooled[:, sl], pw_ref[gi]) * ps_ref[:, sl]
            out_ref[:, W + gi * GDN_D:W + (gi + 1) * GDN_D] = y.astype(BF16)

    return pl.pallas_call(
        body, name=name, grid=(T // tm,),
        in_specs=[pl.BlockSpec((tm, W), lambda i: (i, 0)), pl.BlockSpec((tm, W), lambda i: (i, 3)),
                  pl.BlockSpec((tm, W), lambda i: (i, 4)),
                  pl.BlockSpec((HALO, W), lambda i: (jnp.maximum(i * per - 1, 0), 4)),
                  pl.BlockSpec((1, GDN_D), lambda i: (0, 0)),
                  pl.BlockSpec((len(POOL_WINDOWS), GDN_D, GDN_D), lambda i: (0, 0, 0)),
                  pl.BlockSpec((1, W), lambda i: (0, 0))],
        out_specs=pl.BlockSpec((tm, 2 * W), lambda i: (i, 0)),
        out_shape=_sds((T, 2 * W), BF16),
        compiler_params=_cp(("arbitrary",)),
    )(o, proj, proj, proj, gdn_norm, pool_w, pool_scale)


def even_post_bwd(o, proj, gdn_norm, pool_w, pool_scale, dcat, name):
    T = o.shape[0]
    W = GDN_HEADS * GDN_D
    G = len(POOL_WINDOWS)
    tm = _row_tile(T)
    per = tm // HALO
    nrow = T // tm
    last = T // HALO - 1

    def body(o_ref, z_ref, p_ref, pp_ref, gn_ref, pw_ref, ps_ref, d_ref, dn_ref,
             do_ref, dzp_ref, dgn_ref, dpw_ref, dps_ref):
        i = pl.program_id(0)
        first = i == 0
        _, vjp = jax.vjp(_gated_norm, o_ref[...], z_ref[...].astype(F32), gn_ref[...])
        do, dz, dgn = vjp(d_ref[:, :W].astype(F32))
        do_ref[...] = do
        dzp_ref[:, :W] = dz.astype(BF16)
        _accum(dgn_ref, dgn, first)
        pp = jnp.where(first, 0.0, pp_ref[...].astype(F32))
        ext = jnp.concatenate([pp, p_ref[...].astype(F32)], axis=0)
        pooled = (_window_sums(ext, _shift_down)[HALO:] / _pool_counts(i * tm, tm, W)) - ext[HALO:]
        dy_ext = jnp.concatenate([d_ref[:, W:].astype(F32), jnp.where(i < nrow - 1, dn_ref[...].astype(F32), 0.0)], axis=0)
        dys_ext = dy_ext * ps_ref[...]
        dpooled, dscale = [], []
        for gi in range(G):
            sl = slice(gi * GDN_D, (gi + 1) * GDN_D)
            dpooled.append(_dot_nt(dys_ext[:, sl], pw_ref[gi]))
            y = _dot(pooled[:, sl], pw_ref[gi])
            dscale.append(jnp.sum(dy_ext[:tm, sl] * y, axis=0, keepdims=True))
            _accum(dpw_ref.at[gi], _dot_tn(pooled[:, sl], dys_ext[:tm, sl]), first)
        dpooled = jnp.concatenate(dpooled, axis=1)
        _accum(dps_ref, jnp.concatenate(dscale, axis=1), first)
        dmean = dpooled / _pool_counts(i * tm, tm + HALO, W)
        dp = _window_sums(dmean, _shift_up)[:tm] - dpooled[:tm]
        dzp_ref[:, W:] = dp.astype(BF16)

    return pl.pallas_call(
        body, name=name, grid=(nrow,),
        in_specs=[pl.BlockSpec((tm, W), lambda i: (i, 0)), pl.BlockSpec((tm, W), lambda i: (i, 3)),
                  pl.BlockSpec((tm, W), lambda i: (i, 4)),
                  pl.BlockSpec((HALO, W), lambda i: (jnp.maximum(i * per - 1, 0), 4)),
                  pl.BlockSpec((1, GDN_D), lambda i: (0, 0)),
                  pl.BlockSpec((G, GDN_D, GDN_D), lambda i: (0, 0, 0)),
                  pl.BlockSpec((1, W), lambda i: (0, 0)),
                  pl.BlockSpec((tm, 2 * W), lambda i: (i, 0)),
                  pl.BlockSpec((HALO, W), lambda i: (jnp.minimum((i + 1) * per, last), 1))],
        out_specs=[pl.BlockSpec((tm, W), lambda i: (i, 0)), pl.BlockSpec((tm, 2 * W), lambda i: (i, 0)),
                   pl.BlockSpec((1, GDN_D), lambda i: (0, 0)), pl.BlockSpec((G, GDN_D, GDN_D), lambda i: (0, 0, 0)),
                   pl.BlockSpec((1, W), lambda i: (0, 0))],
        out_shape=[_sds((T, W), F32), _sds((T, 2 * W), BF16), _sds((1, GDN_D), F32), _sds((G, GDN_D, GDN_D), F32),
                   _sds((1, W), F32)],
        compiler_params=_cp(("arbitrary",)),
    )(o, proj, proj, proj, gdn_norm, pool_w, pool_scale, dcat, dcat)


def _head_norm(x, gain):
    return x * lax.rsqrt(jnp.mean(x * x, axis=-1, keepdims=True) + RMS_EPS) * gain


def _att_scores(q, k, slope, has_prev):
    B = ATT_BLOCK
    a = lax.broadcasted_iota(jnp.int32, (B, 2 * B), 0)
    j = lax.broadcasted_iota(jnp.int32, (B, 2 * B), 1)
    rel = B + a - j
    mask = (rel >= 0) & (rel <= B) & ((j >= B) | has_prev)
    s = _dot_nt(q, k) - slope * rel.astype(F32)
    return jnp.where(mask, s, NEG), mask


def _alibi_slope(h, dil):
    return dil * (2.0 ** (-8.0 * (h + 1) / ATT_HEADS))


def att_fwd(qk, qkv, name):
    dil, L, _ = qk.shape
    Wd = ATT_HEADS * ATT_DH
    nb = L // ATT_BLOCK
    B = ATT_BLOCK

    def body(q_ref, kc_ref, kp_ref, vc_ref, vp_ref, o_ref, l_ref):
        has_prev = pl.program_id(1) > 0
        for h in range(ATT_HEADS):
            sl = slice(h * ATT_DH, (h + 1) * ATT_DH)
            k = jnp.concatenate([kp_ref[:, sl], kc_ref[:, sl]], axis=0)
            v = jnp.concatenate([vp_ref[:, sl], vc_ref[:, sl]], axis=0)
            s, _ = _att_scores(q_ref[:, sl], k, _alibi_slope(h, dil), has_prev)
            m = jnp.max(s, axis=-1, keepdims=True)
            p = jnp.exp(s - m)
            l = jnp.sum(p, axis=-1, keepdims=True)
            o_ref[:, sl] = (_dot(p, v) / l).astype(BF16)
            l_ref[:, sl] = jnp.broadcast_to(m + jnp.log(l), (B, ATT_DH))

    cur = lambda t: pl.BlockSpec((None, B, Wd), lambda r, n: (r, n, t))
    prev = lambda t: pl.BlockSpec((None, B, Wd), lambda r, n: (r, jnp.maximum(n - 1, 0), t))
    out = pl.BlockSpec((None, B, Wd), lambda r, n: (r, n, 0))
    return pl.pallas_call(
        body, name=name, grid=(dil, nb),
        in_specs=[cur(0), cur(1), prev(1), cur(2), prev(2)],
        out_specs=[out, out], out_shape=[_sds((dil, L, Wd), BF16), _sds((dil, L, Wd), F32)],
        compiler_params=_cp(("parallel", "arbitrary")),
    )(qk, qk, qk, qkv, qkv)


def att_bwd(qk, qkv, lse, do, dd, name):
    dil, L, _ = qk.shape
    Wd = ATT_HEADS * ATT_DH
    nb = L // ATT_BLOCK
    B = ATT_BLOCK

    def body(q_ref, kc_ref, kp_ref, vc_ref, vp_ref, l_ref, do_ref, dd_ref, dq_ref, dk_ref, dv_ref, ck_scr, cv_scr):
        n = pl.program_id(1)
        has_prev = n > 0

        @pl.when(n < nb)
        def _():
            for h in range(ATT_HEADS):
                sl = slice(h * ATT_DH, (h + 1) * ATT_DH)
                q = q_ref[:, sl]
                k = jnp.concatenate([kp_ref[:, sl], kc_ref[:, sl]], axis=0)
                v = jnp.concatenate([vp_ref[:, sl], vc_ref[:, sl]], axis=0)
                do = do_ref[:, sl]
                s, mask = _att_scores(q, k, _alibi_slope(h, dil), has_prev)
                p = jnp.where(mask, jnp.exp(s - l_ref[:, h * ATT_DH:h * ATT_DH + 1]), 0.0)
                delta = jnp.sum(dd_ref[:, sl].astype(F32), axis=-1, keepdims=True)
                ds = p * (_dot_nt(do, v) - delta)
                dq_ref[:, sl] = _dot(ds, k).astype(BF16)
                dk = _dot_tn(ds, q)
                dv = _dot_tn(p, do)
                dk_ref[:, sl] = (jnp.where(has_prev, ck_scr[:, sl] + dk[:B], 0.0)).astype(BF16)
                dv_ref[:, sl] = (jnp.where(has_prev, cv_scr[:, sl] + dv[:B], 0.0)).astype(BF16)
                ck_scr[:, sl] = dk[B:]
                cv_scr[:, sl] = dv[B:]

        @pl.when(n == nb)
        def _():
            dk_ref[...] = ck_scr[...].astype(BF16)
            dv_ref[...] = cv_scr[...].astype(BF16)

    cur = lambda t: pl.BlockSpec((None, B, Wd), lambda r, n: (r, jnp.minimum(n, nb - 1), t))
    prev = lambda t: pl.BlockSpec((None, B, Wd), lambda r, n: (r, jnp.clip(n - 1, 0, nb - 1), t))
    kv_out = pl.BlockSpec((None, B, Wd), lambda r, n: (r, jnp.maximum(n - 1, 0), 0))
    return pl.pallas_call(
        body, name=name, grid=(dil, nb + 1),
        in_specs=[cur(0), cur(1), prev(1), cur(2), prev(2), cur(0), cur(0), cur(0)],
        out_specs=[cur(0), kv_out, kv_out],
        out_shape=[_sds((dil, L, Wd), BF16)] * 3,
        scratch_shapes=[pltpu.VMEM((B, Wd), F32), pltpu.VMEM((B, Wd), F32)],
        compiler_params=_cp(("parallel", "arbitrary")),
    )(qk, qk, qk, qkv, qkv, lse, do, dd)


def qk_norm_bwd(qkv, dq, dk, q_norm, k_norm, name):
    T = qkv.shape[0]
    Wd = dq.shape[1]
    tm = _row_tile(T)

    def body(q_ref, k_ref, dq_ref, dk_ref, qn_ref, kn_ref, oq_ref, ok_ref, dqn_ref, dkn_ref):
        first = pl.program_id(0) == 0
        for x_ref, d_ref, g_ref, o_ref, dg_ref, scale in ((q_ref, dq_ref, qn_ref, oq_ref, dqn_ref, ATT_DH ** -0.5),
                                                         (k_ref, dk_ref, kn_ref, ok_ref, dkn_ref, 1.0)):
            dg = jnp.zeros((1, ATT_DH), F32)
            for h in range(Wd // ATT_DH):
                sl = slice(h * ATT_DH, (h + 1) * ATT_DH)
                _, vjp = jax.vjp(lambda x, g: _head_norm(x, g) * scale, x_ref[:, sl].astype(F32), g_ref[...])
                dx, dg_h = vjp(d_ref[:, sl].astype(F32))
                o_ref[:, sl] = dx.astype(BF16)
                dg += dg_h
            _accum(dg_ref, dg, first)

    row = lambda t: pl.BlockSpec((tm, Wd), lambda i: (i, t))
    vec = pl.BlockSpec((1, ATT_DH), lambda i: (0, 0))
    return pl.pallas_call(
        body, name=name, grid=(T // tm,),
        in_specs=[row(0), row(1), row(0), row(0), vec, vec], out_specs=[row(0), row(0), vec, vec],
        out_shape=[_sds((T, Wd), BF16)] * 2 + [_sds((1, ATT_DH), F32)] * 2,
        compiler_params=_cp(("arbitrary",)),
    )(qkv, qkv, dq, dk, q_norm, k_norm)


def _merge_weights(l0, l1, l2):
    m = jnp.maximum(jnp.maximum(l0, l1), l2)
    e = [jnp.exp(l - m) for l in (l0, l1, l2)]
    tot = e[0] + e[1] + e[2]
    return [x / tot for x in e]


def _merge_specs(arrays, tm):
    return [pl.BlockSpec((a.shape[0], tm // a.shape[0], a.shape[2]), lambda i: (0, i, 0)) for a in arrays]


def _merge_load(refs, dils, tm, scr):
    vals = [ref[...].astype(F32).reshape(tm, ref.shape[-1]) for ref in refs]
    return [v if dl == 1 else _interleave(v, scr, dl) for v, dl in zip(vals, dils)]


def merge_fwd(outs, lses, name):
    Wd = outs[0].shape[2]
    T = outs[0].shape[0] * outs[0].shape[1]
    tm = _row_tile(T)
    dils = [a.shape[0] for a in outs] * 2

    def body(*refs):
        vals = _merge_load(refs[:6], dils, tm, refs[7])
        w = _merge_weights(*vals[3:])
        refs[6][...] = (w[0] * vals[0] + w[1] * vals[1] + w[2] * vals[2]).astype(BF16)

    return pl.pallas_call(body, name=name, grid=(T // tm,), in_specs=_merge_specs(list(outs) + list(lses), tm),
                          out_specs=pl.BlockSpec((tm, Wd), lambda i: (i, 0)), out_shape=_sds((T, Wd), BF16),
                          scratch_shapes=[pltpu.VMEM((Wd // LANES, tm, LANES), F32)],
                          compiler_params=_cp(("parallel",)))(*outs, *lses)


def merge_bwd(outs, lses, d, name):
    Wd = outs[0].shape[2]
    T = outs[0].shape[0] * outs[0].shape[1]
    tm = 256 if T % 256 == 0 else T
    dils = [a.shape[0] for a in outs] * 2

    def body(*refs):
        scr = refs[13]
        vals = _merge_load(refs[:6], dils, tm, scr)
        w = _merge_weights(*vals[3:])
        dv = refs[6][...].astype(F32)
        dvm = dv * (w[0] * vals[0] + w[1] * vals[1] + w[2] * vals[2])
        for g in range(3):
            for ref, val in ((refs[7 + g], w[g] * dv), (refs[10 + g], w[g] * dvm)):
                ref[...] = (val if dils[g] == 1 else _deinterleave(val, scr, dils[g])).reshape(ref.shape).astype(BF16)

    specs = _merge_specs(list(outs) + list(lses), tm)
    res = pl.pallas_call(body, name=name, grid=(T // tm,),
                         in_specs=specs + [pl.BlockSpec((tm, Wd), lambda i: (i, 0))], out_specs=specs,
                         out_shape=[_sds(a.shape, BF16) for a in list(outs) + list(outs)],
                         scratch_shapes=[pltpu.VMEM((Wd // LANES, tm, LANES), F32)],
                         compiler_params=_cp(("parallel",)))(*outs, *lses, d)
    return res[:3], res[3:]


def loss_head(y, target, name):
    T, D = y.shape
    tm = _row_tile(T)

    def body(y_ref, t_ref, l_ref, dy_ref):
        err = y_ref[...] - t_ref[...]
        dy_ref[...] = err * (1.0 / D)
        part = 0.5 * jnp.sum(jnp.sum(err * err, axis=1, keepdims=True) * (1.0 / D), axis=0, keepdims=True)
        _accum(l_ref, jnp.broadcast_to(part, (1, 128)), pl.program_id(0) == 0)

    row = pl.BlockSpec((tm, D), lambda i: (i, 0))
    return pl.pallas_call(body, name=name, grid=(T // tm,), in_specs=[row, row],
                          out_specs=[pl.BlockSpec((1, 128), lambda i: (0, 0)), row],
                          out_shape=[_sds((1, 128), F32), _sds((T, D), F32)],
                          compiler_params=_cp(("arbitrary",)))(y, target)


def ada_fwd(c_all, w, b, name):
    depth, D, n = w.shape

    def body(c_ref, w_ref, b_ref, o_ref):
        o_ref[...] = _dot(_silu(c_ref[...]), w_ref[...]) + b_ref[...]

    return pl.pallas_call(
        body, name=name, grid=(depth,),
        in_specs=[pl.BlockSpec((N_DEV, D), lambda i: (0, 0)), pl.BlockSpec((None, D, n), lambda i: (i, 0, 0)),
                  pl.BlockSpec((None, 1, n), lambda i: (i, 0, 0))],
        out_specs=pl.BlockSpec((None, N_DEV, n), lambda i: (i, 0, 0)),
        out_shape=_sds((depth, N_DEV, n), F32), compiler_params=_cp(("parallel",)),
    )(c_all, w, b)


def ada_bwd(c_all, dmod, name):
    depth, _, n = dmod.shape
    D = c_all.shape[1]

    def body(c_ref, d_ref, o_ref):
        o_ref[...] = _dot_tn(_silu(c_ref[...]), d_ref[...])

    return pl.pallas_call(
        body, name=name, grid=(depth,),
        in_specs=[pl.BlockSpec((N_DEV, D), lambda i: (0, 0)), pl.BlockSpec((None, N_DEV, n), lambda i: (i, 0, 0))],
        out_specs=pl.BlockSpec((None, D, n), lambda i: (i, 0, 0)),
        out_shape=_sds((depth, D, n), F32), compiler_params=_cp(("parallel",)),
    )(c_all, dmod)


def adamw(w, m, v, gparts, name):
    R, C = w.shape
    k = gparts.shape[0]
    tr = R
    for cand in (512, 256, 128, 64, 32, 16, 8):
        if R % cand == 0 and cand * C * 4 <= 2 * 1024 * 1024:
            tr = cand
            break
    bc1 = 1.0 - ADAM_B1 ** ADAM_STEP
    bc2 = 1.0 - ADAM_B2 ** ADAM_STEP

    def body(w_ref, m_ref, v_ref, gp_ref, g_ref, d_ref, nm_ref, nv_ref):
        g = gp_ref[0].astype(F32)
        for q in range(1, k):
            g = g + gp_ref[q].astype(F32)
        nm = ADAM_B1 * m_ref[...] + (1.0 - ADAM_B1) * g
        nv = ADAM_B2 * v_ref[...] + (1.0 - ADAM_B2) * (g * g)
        g_ref[...] = g
        nm_ref[...] = nm
        nv_ref[...] = nv
        d_ref[...] = -ADAM_LR * ((nm / bc1) / (jnp.sqrt(nv / bc2) + ADAM_EPS) + ADAM_WD * w_ref[...])

    row = pl.BlockSpec((tr, C), lambda i: (i, 0))
    return pl.pallas_call(
        body, name=name, grid=(R // tr,),
        in_specs=[row, row, row, pl.BlockSpec((k, tr, C), lambda i: (0, i, 0))],
        out_specs=[row] * 4, out_shape=[_sds((R, C), F32)] * 4, compiler_params=_cp(("parallel",)),
    )(w, m, v, gparts)


def _mesh_pos():
    return lax.axis_index("x"), lax.axis_index("y"), lax.axis_index("c")


def _other_chips(x, y):
    return [(1 - x, y), (x, 1 - y), (1 - x, 1 - y)]


_ANY = pl.BlockSpec(memory_space=pl.ANY)


def all_gather(shards, name):
    n = len(shards)

    def body(*refs):
        ins, outs = refs[:n], refs[n:2 * n]
        send_sems, recv_sems, local_sems = refs[2 * n:]
        x, y, c = _mesh_pos()
        me, sibling = (x, y, c), (x, y, 1 - c)
        chips = _other_chips(x, y)

        def slot(p):
            return 4 * p[0] + 2 * p[1] + p[2]

        def copy(a, k, block, to, src=None):
            dst = outs[a].at[slot(block)]
            return pltpu.make_async_remote_copy(
                src_ref=dst if src is None else src, dst_ref=dst, send_sem=send_sems.at[a, k],
                recv_sem=recv_sems.at[a, k], device_id=to, device_id_type=MESH)

        mine = [pltpu.make_async_copy(ins[a], outs[a].at[slot(me)], local_sems.at[a]) for a in range(n)]
        for cp in mine:
            cp.start()
        first = []
        for a in range(n):
            first.append(copy(a, 0, me, sibling, src=ins[a]))
            first += [copy(a, 1 + j, me, (*chip, c), src=ins[a]) for j, chip in enumerate(chips)]
        for cp in first:
            cp.start()
        passed = []
        for j, chip in enumerate(chips):
            for a in range(n):
                copy(a, 1 + j, (*chip, c), me).wait_recv()
                cp = copy(a, 4 + j, (*chip, c), sibling)
                cp.start()
                passed.append(cp)
        for a in range(n):
            copy(a, 0, sibling, me).wait_recv()
            for j, chip in enumerate(chips):
                copy(a, 4 + j, (*chip, 1 - c), me).wait_recv()
        for cp in first + passed:
            cp.wait_send()
        for cp in mine:
            cp.wait()

    return pl.pallas_call(
        body, name=name, in_specs=[_ANY] * n, out_specs=[_ANY] * n,
        out_shape=[_sds((N_DEV,) + s.shape, s.dtype) for s in shards],
        scratch_shapes=[pltpu.SemaphoreType.DMA((n, 7)), pltpu.SemaphoreType.DMA((n, 7)), pltpu.SemaphoreType.DMA((n,))],
        compiler_params=pltpu.CompilerParams(has_side_effects=True),
    )(*shards)


def exchange_pair(stacked, name):
    n = len(stacked)

    def body(*refs):
        ins, outs = refs[:n], refs[n:2 * n]
        send_sems, recv_sems = refs[2 * n:]
        x, y, c = _mesh_pos()
        copies = [pltpu.make_async_remote_copy(
            src_ref=ins[a].at[2 * q + (1 - c)], dst_ref=outs[a].at[q], send_sem=send_sems.at[a, q],
            recv_sem=recv_sems.at[a, q], device_id=(x, y, 1 - c), device_id_type=MESH)
            for a in range(n) for q in range(4)]
        for cp in copies:
            cp.start()
        for cp in copies:
            cp.wait()

    return pl.pallas_call(
        body, name=name, in_specs=[_ANY] * n, out_specs=[_ANY] * n,
        out_shape=[_sds((4,) + s.shape[1:], s.dtype) for s in stacked],
        scratch_shapes=[pltpu.SemaphoreType.DMA((n, 4)), pltpu.SemaphoreType.DMA((n, 4))],
        compiler_params=pltpu.CompilerParams(has_side_effects=True),
    )(*stacked)


def pair_add(stacked, got, c_idx, name):
    _, R, C = stacked.shape
    tr = R
    for cand in (512, 256, 128, 64, 32, 16):
        if R % cand == 0 and cand * C * 4 <= 2 * 1024 * 1024:
            tr = cand
            break

    def body(c_ref, s_ref, g_ref, o_ref):
        o_ref[...] = (s_ref[...] + g_ref[...]).astype(BF16)

    return pl.pallas_call(
        body, name=name,
        grid_spec=pltpu.PrefetchScalarGridSpec(
            num_scalar_prefetch=1, grid=(4, R // tr),
            in_specs=[pl.BlockSpec((None, tr, C), lambda q, i, c_ref: (2 * q + c_ref[0], i, 0)),
                      pl.BlockSpec((None, tr, C), lambda q, i, c_ref: (q, i, 0))],
            out_specs=pl.BlockSpec((None, tr, C), lambda q, i, c_ref: (q, i, 0))),
        out_shape=_sds((4, R, C), BF16),
        compiler_params=_cp(("parallel", "parallel")),
    )(c_idx, stacked, got)


def exchange_chips(parts, name):
    n = len(parts)

    def body(*refs):
        ins, outs = refs[:n], refs[n:2 * n]
        send_sems, recv_sems, local_sems = refs[2 * n:]
        x, y, c = _mesh_pos()
        myq = 2 * x + y
        mine = [pltpu.make_async_copy(ins[a].at[myq], outs[a].at[myq], local_sems.at[a]) for a in range(n)]
        for cp in mine:
            cp.start()
        copies = [pltpu.make_async_remote_copy(
            src_ref=ins[a].at[2 * chip[0] + chip[1]], dst_ref=outs[a].at[myq], send_sem=send_sems.at[a, j],
            recv_sem=recv_sems.at[a, j], device_id=(*chip, c), device_id_type=MESH)
            for a in range(n) for j, chip in enumerate(_other_chips(x, y))]
        for cp in copies:
            cp.start()
        for cp in copies:
            cp.wait()
        for cp in mine:
            cp.wait()

    return pl.pallas_call(
        body, name=name, in_specs=[_ANY] * n, out_specs=[_ANY] * n,
        out_shape=[_sds(s.shape, s.dtype) for s in parts],
        scratch_shapes=[pltpu.SemaphoreType.DMA((n, 3)), pltpu.SemaphoreType.DMA((n, 3)), pltpu.SemaphoreType.DMA((n,))],
        compiler_params=pltpu.CompilerParams(has_side_effects=True),
    )(*parts)


def _pad_lanes(v, start, width=128):
    return jnp.pad(v.astype(F32), (start, width - start - v.shape[0]))[None]


def _layer_fwd(i, x, mod, P):
    T, D = x.shape
    sh_m, sc_m, g_m, sh_f, sc_f, g_f = [mod[k:k + 1] for k in range(6)]
    sv = {"x0": x}
    tag = f"l{i}"
    if i % 2 == 0:
        e = i // 2
        h, proj = norm_proj(x, P["norm_mix"][i:i + 1], sc_m, sh_m, P["ev_main"][e], P["ev_main"].shape[2], tag + "_in")
        ba = matmul_nn(h, P["ev_ba"][e], F32, tag + "_ba")
        al = _pad_lanes(P["gdn_a_log"][e], GDN_HEADS)
        dt = _pad_lanes(P["gdn_dt_bias"][e], GDN_HEADS)
        gate = gate_fwd(ba, al, dt, tag + "_gate")
        beta, g = gate[:, :GDN_HEADS], gate[:, GDN_HEADS:2 * GDN_HEADS]
        grow = g.reshape(T // GDN_CHUNK, GDN_CHUNK, GDN_HEADS).transpose(0, 2, 1)
        qkv = gdn_prep_fwd(proj, P["gdn_conv_w"][e], tag + "_prep")
        chunks = gdn_prep_chunks_fwd(qkv, g, grow, beta, tag + "_chunks")
        o, states = gdn_fwd(*chunks, g, tag + "_gdn")
        cat = even_post_fwd(o, proj, P["gdn_norm"][e:e + 1], P["pool_w"][e], P["pool_scale"][e:e + 1], tag + "_post")
        x1, y_m = proj_res(cat, P["ev_w_out"][e], x, g_m, tag + "_out")
        sv.update(h=h, proj=proj, ba=ba, al=al, dt=dt, beta=beta, g=g, grow=grow, qkv=qkv, chunks=chunks, o=o,
                  states=states, a_m=cat)
    else:
        od = i // 2
        qn, kn = P["att_q_norm"][od:od + 1], P["att_k_norm"][od:od + 1]
        h, proj, qk = [], [], []
        for gi, (_, dil) in enumerate(DIL_PATTERNS):
            hg, pg, qkg = norm_proj(x, P["norm_mix"][i:i + 1], sc_m, sh_m, P["od_w_in"][od], 3 * D, f"{tag}_in{gi}",
                                    cols=(gi, 1), dil=dil, qk_norms=(qn, kn))
            h.append(hg.reshape(T, D))
            proj.append(pg)
            qk.append(qkg)
        res = [att_fwd(qk[gi], proj[gi], f"{tag}_att{gi}") for gi in range(len(DIL_PATTERNS))]
        outs, lses = [r[0] for r in res], [r[1] for r in res]
        merged = merge_fwd(outs, lses, tag + "_merge")
        x1, y_m = proj_res(merged, P["od_w_out"][od], x, g_m, tag + "_out")
        sv.update(h=h, proj=proj, qk=qk, outs=outs, lses=lses, a_m=merged)
    hf, up = norm_proj(x1, P["norm_ffn"][i:i + 1], sc_f, sh_f, P["ffn_w_up"][i], 1408, tag + "_up")
    a = ffn_mid_fwd(up, P["ffn_conv_w"][i], P["ffn_conv_b"][i:i + 1], tag + "_mid")
    x2, y_f = proj_res(a, P["ffn_w_down"][i], x1, g_f, tag + "_down")
    sv.update(y_m=y_m, x1=x1, hf=hf, up=up, a_f=a, y_f=y_f)
    return x2, sv


def _layer_bwd(i, dx2, sv, mod, P):
    T = dx2.shape[0]
    sh_m, sc_m, g_m, sh_f, sc_f, g_f = [mod[k:k + 1] for k in range(6)]
    tag = f"b{i}"
    G = {}
    F = P["ffn_w_down"].shape[1]
    dy, da, dg_f = bwd_out(dx2, g_f, sv["y_f"], P["ffn_w_down"][i], F, tag + "_down")
    G["ffn_w_down"] = matmul_tn(sv["a_f"], dy, F // 2, dy.shape[1], tag + "_wdown")
    dgate, dval, dcw, dcb = ffn_mid_bwd(sv["up"], da, P["ffn_conv_w"][i], P["ffn_conv_b"][i:i + 1], tag + "_mid")
    G["ffn_conv_w"], G["ffn_conv_b"] = dcw[:FFN_CONV], dcb[0]
    dx1, dnf, dsc_f, dsh_f = bwd_in([dgate, dval], P["ffn_w_up"][i], [0, 1], F, None,
                                    sv["x1"], P["norm_ffn"][i:i + 1], sc_f, sh_f, dx2, tag + "_up")
    G["norm_ffn"] = dnf[0]
    x0, h, proj = sv["x0"], sv["h"], sv["proj"]
    D = x0.shape[1]
    G["ffn_w_up"] = jnp.concatenate([matmul_tn(sv["hf"], d, D, F // 2, f"{tag}_wup{k}")
                                     for k, d in enumerate((dgate, dval))], axis=1)
    gain = P["norm_mix"][i:i + 1]
    if i % 2 == 0:
        e = i // 2
        W = GDN_HEADS * GDN_D
        dy, dcat, dg_m = bwd_out(dx1, g_m, sv["y_m"], P["ev_w_out"][e], 1024, tag + "_out")
        G["ev_w_out"] = matmul_tn(sv["a_m"], dy, 1024, D, tag + "_wout")
        do, dzp, dgn, dpw, dps = even_post_bwd(sv["o"], proj, P["gdn_norm"][e:e + 1], P["pool_w"][e],
                                               P["pool_scale"][e:e + 1], dcat, tag + "_post")
        G["gdn_norm"], G["pool_w"], G["pool_scale"] = dgn[0], dpw, dps[0]
        *dchunks, dgc_scan = gdn_bwd(*sv["chunks"], sv["g"], sv["states"], do, tag + "_gdn")
        dqkv, dgc, dgr, dbeta = gdn_prep_chunks_bwd(sv["qkv"], sv["g"], sv["grow"], sv["beta"], *dchunks, tag + "_chunks")
        dg = dgc + dgc_scan + dgr.transpose(0, 2, 1).reshape(T, GDN_HEADS)
        dgate128 = jnp.concatenate([dbeta, dg, jnp.zeros((T, 128 - 2 * GDN_HEADS), F32)], axis=1)
        dba, dal, ddt = gate_bwd(sv["ba"], sv["al"], sv["dt"], dgate128, tag + "_gate")
        G["gdn_a_log"], G["gdn_dt_bias"] = dal[0, GDN_HEADS:2 * GDN_HEADS], ddt[0, GDN_HEADS:2 * GDN_HEADS]
        dqkv_raw, dconv = gdn_prep_bwd(proj, dqkv, P["gdn_conv_w"][e], tag + "_prep")
        G["gdn_conv_w"] = dconv[:GDN_CONV]
        w_main = P["ev_main"][e]
        dx0, dnm, dsc_m, dsh_m = bwd_in([dba, dqkv_raw, dzp], [P["ev_ba"][e], w_main[:, :3 * W], w_main[:, 3 * W:]],
                                        [0, 0, 0], None, None, x0, gain, sc_m, sh_m, dx1, tag + "_in")
        gw_qkv = matmul_tn(h, dqkv_raw, 1024, W, tag + "_win1")
        gw_zp = matmul_tn(h, dzp, 1024, W, tag + "_win2")
        gw_ba = matmul_tn(h, dba, 1024, 128, tag + "_win0")
        G["ev_w_in"] = jnp.concatenate([gw_qkv, gw_zp[:, :W], gw_ba[:, :2 * GDN_HEADS], gw_zp[:, W:]], axis=1)
    else:
        od = i // 2
        qn, kn = P["att_q_norm"][od:od + 1], P["att_k_norm"][od:od + 1]
        dy, dmerged, dg_m = bwd_out(dx1, g_m, sv["y_m"], P["od_w_out"][od], 1024, tag + "_out")
        G["od_w_out"] = matmul_tn(sv["a_m"], dy, 1024, D, tag + "_wout")
        douts, dds = merge_bwd(sv["outs"], sv["lses"], dmerged, tag + "_merge")
        dh, gws = None, []
        dqn_sum, dkn_sum = 0.0, 0.0
        ng = len(DIL_PATTERNS)
        for gi, (_, dil) in enumerate(DIL_PATTERNS):
            dqs, dks, dv = att_bwd(sv["qk"][gi], proj[gi], sv["lses"][gi], douts[gi], dds[gi], f"{tag}_att{gi}")
            dq, dk, dqn, dkn = qk_norm_bwd(proj[gi].reshape(T, 3 * D), dqs.reshape(T, D), dks.reshape(T, D), qn, kn,
                                           f"{tag}_qkn{gi}")
            dqn_sum, dkn_sum = dqn_sum + dqn[0], dkn_sum + dkn[0]
            gws += [matmul_tn(h[gi], d.reshape(T, D), 1024, D, f"{tag}_win{gi}{k}") for k, d in enumerate((dq, dk, dv))]
            cols = [3 * gi, 3 * gi + 1, 3 * gi + 2]
            ops = [d.reshape(dil, T // dil, D) if dil > 1 else d.reshape(T, D) for d in (dq, dk, dv)]
            if gi < ng - 1:
                dh = bwd_in(ops, P["od_w_in"][od], cols, D, dh, None, None, None, None, None, f"{tag}_in{gi}", dil=dil)
            else:
                dx0, dnm, dsc_m, dsh_m = bwd_in(ops, P["od_w_in"][od], cols, D, dh, x0, gain, sc_m, sh_m, dx1,
                                                f"{tag}_in{gi}", dil=dil)
        G["att_q_norm"], G["att_k_norm"] = dqn_sum, dkn_sum
        G["od_w_in"] = jnp.concatenate(gws, axis=1)
    G["norm_mix"] = dnm[0]
    dmod = jnp.concatenate([dsh_m, dsc_m, dg_m, dsh_f, dsc_f, dg_f], axis=0)
    return dx0, dmod, G


_PER_LAYER = ("norm_mix", "norm_ffn", "ffn_w_up", "ffn_conv_w", "ffn_conv_b", "ffn_w_down")
_PER_EVEN = ("ev_w_in", "ev_w_out", "gdn_conv_w", "gdn_a_log", "gdn_dt_bias", "gdn_norm", "pool_w", "pool_scale")
_PER_ODD = ("od_w_in", "od_w_out", "att_q_norm", "att_k_norm")


def device_step(x, mod, target, P):
    saved = []
    for i in range(DEPTH):
        x, sv = _layer_fwd(i, x, mod[i], P)
        saved.append(sv)
    loss, dx = loss_head(x, target, "loss")
    layer_grads, dmods = [None] * DEPTH, [None] * DEPTH
    for i in reversed(range(DEPTH)):
        dx, dmods[i], layer_grads[i] = _layer_bwd(i, dx, saved[i], mod[i], P)
    G = {k: jnp.stack([layer_grads[i][k] for i in range(DEPTH)]) for k in _PER_LAYER}
    G.update({k: jnp.stack([layer_grads[i][k] for i in range(0, DEPTH, 2)]) for k in _PER_EVEN})
    G.update({k: jnp.stack([layer_grads[i][k] for i in range(1, DEPTH, 2)]) for k in _PER_ODD})
    return loss, dx, jnp.stack(dmods), G


_WEIGHTS = ("ada_w", "ada_b", "norm_mix", "norm_ffn", "ev_w_in", "ev_w_out", "gdn_conv_w", "gdn_a_log", "gdn_dt_bias",
            "gdn_norm", "pool_w", "pool_scale", "od_w_in", "od_w_out", "att_q_norm", "att_k_norm", "ffn_w_up",
            "ffn_conv_w", "ffn_conv_b", "ffn_w_down")
_COL_SHARDED = ("ev_w_in", "od_w_in", "ffn_w_up")
_ROW_SHARDED = ("ev_w_out", "od_w_out", "ffn_w_down")
_SMALL_SHARDED = ("gdn_conv_w", "ffn_conv_w")
_REPLICATED = ("ada_b", "norm_mix", "norm_ffn", "gdn_a_log", "gdn_dt_bias", "gdn_norm", "pool_w", "pool_scale",
               "att_q_norm", "att_k_norm", "ffn_conv_b")
PACK_LANES = 128
PACK_ROWS = 8


def _pack(arrays):
    flat = jnp.concatenate([a.reshape(-1).astype(F32) for a in arrays])
    unit = PACK_LANES * PACK_ROWS
    padded = -(-flat.shape[0] // unit) * unit
    return jnp.pad(flat, (0, padded - flat.shape[0])).reshape(-1, PACK_LANES)


def _unpack(packed, shapes, lead=()):
    flat = packed.reshape(lead + (-1,))
    out, pos = [], 0
    for s in shapes:
        n = math.prod(s)
        out.append(flat[..., pos:pos + n].reshape(lead + tuple(s)))
        pos += n
    return out


def _unshard_cols(g):
    _, L, R, n = g.shape
    return g.transpose(1, 2, 0, 3).reshape(L, R, N_DEV * n)


def _shard_cols(full):
    L, R, N = full.shape
    n = N // N_DEV
    return full.reshape(L * R, N_DEV, n).transpose(1, 0, 2)


def _shard_rows(full):
    L, R, C = full.shape
    r = R // N_DEV
    return full.reshape(L, N_DEV, r, C).transpose(1, 0, 2, 3).reshape(N_DEV, L * r, C)


def kernel(x, c, ada_w, ada_b, norm_mix, norm_ffn, ev_w_in, ev_w_out, gdn_conv_w, gdn_a_log, gdn_dt_bias, gdn_norm, pool_w, pool_scale, od_w_in, od_w_out, att_q_norm, att_k_norm, ffn_w_up, ffn_conv_w, ffn_conv_b, ffn_w_down, loss_target, m_ada_w, m_ada_b, m_norm_mix, m_norm_ffn, m_ev_w_in, m_ev_w_out, m_gdn_conv_w, m_gdn_a_log, m_gdn_dt_bias, m_gdn_norm, m_pool_w, m_pool_scale, m_od_w_in, m_od_w_out, m_att_q_norm, m_att_k_norm, m_ffn_w_up, m_ffn_conv_w, m_ffn_conv_b, m_ffn_w_down, v_ada_w, v_ada_b, v_norm_mix, v_norm_ffn, v_ev_w_in, v_ev_w_out, v_gdn_conv_w, v_gdn_a_log, v_gdn_dt_bias, v_gdn_norm, v_pool_w, v_pool_scale, v_od_w_in, v_od_w_out, v_att_q_norm, v_att_k_norm, v_ffn_w_up, v_ffn_conv_w, v_ffn_conv_b, v_ffn_w_down):
    args = locals()
    Wl = {k: args[k] for k in _WEIGHTS}
    Ml = {k: args["m_" + k] for k in _WEIGHTS}
    Vl = {k: args["v_" + k] for k in _WEIGHTS}
    mx, my, mc = _mesh_pos()
    dev = 4 * mx + 2 * my + mc
    T, D = x.shape[1], x.shape[2]
    x2d, tgt = x.reshape(T, D), loss_target.reshape(T, D)

    small_shapes = [c.shape] + [Wl[k].shape for k in _SMALL_SHARDED]
    big = list(_COL_SHARDED + _ROW_SHARDED)
    gathered = all_gather([_pack([c] + [Wl[k] for k in _SMALL_SHARDED])] + [Wl[k].astype(BF16) for k in big], "gather_w")
    c_all, conv_g, conv_f = _unpack(gathered[0], small_shapes, lead=(N_DEV,))
    c_all = c_all.reshape(N_DEV, D)
    full = dict(zip(big, gathered[1:]))
    P = {k: Wl[k] for k in _REPLICATED}
    P["gdn_conv_w"], P["ffn_conv_w"] = _unshard_cols(conv_g), _unshard_cols(conv_f)
    for k in _COL_SHARDED:
        P[k] = _unshard_cols(full[k])
    for k in _ROW_SHARDED:
        g = full[k]
        P[k] = g.transpose(1, 0, 2, 3).reshape(g.shape[1], N_DEV * g.shape[2], g.shape[3])
    W = GDN_HEADS * GDN_D
    ev = P.pop("ev_w_in")
    P["ev_main"] = jnp.concatenate([ev[:, :, :4 * W], ev[:, :, 4 * W + 2 * GDN_HEADS:]], axis=2)
    P["ev_ba"] = jnp.pad(ev[:, :, 4 * W:4 * W + 2 * GDN_HEADS], ((0, 0), (0, 0), (0, 128 - 2 * GDN_HEADS)))

    n_ada = ada_w.shape[2]
    b_cols = lax.dynamic_slice_in_dim(ada_b, dev * n_ada, n_ada, axis=1)
    mod_cols = ada_fwd(c_all, ada_w, b_cols[:, None, :], "ada_fwd")
    mod_all, = all_gather([mod_cols], "gather_mod")
    mod = lax.dynamic_index_in_dim(mod_all, dev, axis=2, keepdims=False)
    mod = mod.transpose(1, 0, 2).reshape(DEPTH, 6, D)

    loss, dx, dmod, G = device_step(x2d, mod, tgt, P)
    loss = lax.psum(loss[0, 0], ("x", "y", "c"))

    G["ada_b"] = dmod.reshape(DEPTH, 6 * D)
    small = list(_REPLICATED) + list(_SMALL_SHARDED)
    parts_all, = all_gather([_pack([G[k] for k in small])], "gather_small")
    zeros = {k: jnp.zeros_like(G[k]) for k in _SMALL_SHARDED}
    packs = [_pack([src[k] for k in _REPLICATED] + [zeros[k] for k in _SMALL_SHARDED]) for src in (Wl, Ml, Vl)]
    res = adamw(*packs, parts_all, "adamw_small")
    shapes = [G[k].shape for k in small]
    out_g, out_d, out_m, out_v = ({k: a for k, a in zip(small, _unpack(r, shapes))} for r in res)
    dmod_all = _unpack(parts_all, shapes, lead=(N_DEV,))[0].reshape(N_DEV, DEPTH, 6 * D)
    dm_cols = lax.dynamic_slice_in_dim(dmod_all, dev * n_ada, n_ada, axis=2).transpose(1, 0, 2)
    g_ada = ada_bwd(c_all, dm_cols, "ada_bwd")

    def flat2(a):
        return a.reshape(-1, a.shape[-1])

    loc = {k: lax.dynamic_slice_in_dim(out_g[k], dev * Wl[k].shape[-1], Wl[k].shape[-1], axis=out_g[k].ndim - 1)
           for k in _SMALL_SHARDED}
    res = adamw(*[_pack([src[k] for k in _SMALL_SHARDED]) for src in (Wl, Ml, Vl)],
                _pack([loc[k] for k in _SMALL_SHARDED])[None], "adamw_conv")
    for dst, r in zip((out_g, out_d, out_m, out_v), res):
        dst.update(zip(_SMALL_SHARDED, _unpack(r, [Wl[k].shape for k in _SMALL_SHARDED])))
    res = adamw(flat2(ada_w), flat2(m_ada_w), flat2(v_ada_w), flat2(g_ada)[None], "adamw_ada")
    for dst, r in zip((out_g, out_d, out_m, out_v), res):
        dst["ada_w"] = r.reshape(ada_w.shape)

    stacked = [_shard_cols(G[k]) for k in _COL_SHARDED] + [_shard_rows(G[k]) for k in _ROW_SHARDED]
    got = exchange_pair(stacked, "rs_pair")
    c_idx = jnp.reshape(mc, (1,)).astype(jnp.int32)
    chip_parts = [pair_add(s, g, c_idx, f"rs_add_{k}") for s, g, k in zip(stacked, got, big)]
    parts = exchange_chips(chip_parts, "rs_chips")
    for k, p in zip(big, parts):
        res = adamw(flat2(Wl[k]), flat2(Ml[k]), flat2(Vl[k]), p, "adamw_" + k)
        for dst, r in zip((out_g, out_d, out_m, out_v), res):
            dst[k] = r.reshape(Wl[k].shape)

    return (loss, dx.reshape(x.shape), *[out_g[k] for k in _WEIGHTS], *[out_d[k] for k in _WEIGHTS],
            *[out_m[k] for k in _WEIGHTS], *[out_v[k] for k in _WEIGHTS])
```

```python
import functools
import math

import jax
import jax.numpy as jnp
from jax import lax
from jax.experimental import pallas as pl
from jax.experimental.pallas import tpu as pltpu

F32 = jnp.float32
BF16 = jnp.bfloat16
HI = lax.Precision.HIGHEST
MESH = pl.DeviceIdType.MESH

N_DEV = 8
RMS_EPS = 1e-6
DEPTH = 4
GDN_HEADS = 4
GDN_D = 128
GDN_CHUNK = 64
GDN_CONV = 4
POOL_WINDOWS = (2, 4, 8, 16)
DIL_PATTERNS = ((128, 1), (512, 4), (2048, 16))
ATT_HEADS = 8
ATT_DH = 128
ATT_BLOCK = 128
FFN_CONV = 3
ADAM_LR, ADAM_B1, ADAM_B2, ADAM_EPS, ADAM_WD, ADAM_STEP = 0.001, 0.9, 0.999, 1e-08, 0.01, 10

HALO = 16
NEG = -1e30
VMEM_LIMIT_BYTES = 56 * 1024 * 1024


def _cp(sem=None, **kw):
    return pltpu.CompilerParams(dimension_semantics=sem, vmem_limit_bytes=VMEM_LIMIT_BYTES, **kw)


def _sds(shape, dtype):
    return jax.ShapeDtypeStruct(tuple(shape), dtype)


def _dot(a, b):
    return jnp.dot(a.astype(BF16), b.astype(BF16), preferred_element_type=F32)


def _dot_nt(a, b):
    return lax.dot_general(a.astype(BF16), b.astype(BF16), (((1,), (1,)), ((), ())), preferred_element_type=F32)


def _dot_tn(a, b):
    return lax.dot_general(a.astype(BF16), b.astype(BF16), (((0,), (0,)), ((), ())), preferred_element_type=F32)


def _dot_hi(a, b):
    return jnp.dot(a, b, preferred_element_type=F32, precision=HI)


def _silu(x):
    return x * jax.nn.sigmoid(x)


def _modnorm(x, gain, sc, sh):
    y = x * lax.rsqrt(jnp.mean(x * x, axis=-1, keepdims=True) + RMS_EPS)
    return y * gain * (1.0 + sc) + sh


def _row_tile(T):
    return 512 if T % 512 == 0 else T


LANES = 128


def _deinterleave(val, scr, dil):
    tm, width = val.shape
    sub = tm // dil
    ncb = width // LANES
    for cb in range(ncb):
        scr[cb] = val[:, cb * LANES:(cb + 1) * LANES]
    return jnp.concatenate([jnp.concatenate([scr.at[cb][pl.ds(r, sub, stride=dil), :] for cb in range(ncb)], axis=1)
                            for r in range(dil)], axis=0)


def _interleave(val, scr, dil):
    tm, width = val.shape
    sub = tm // dil
    ncb = width // LANES
    for r in range(dil):
        for cb in range(ncb):
            scr.at[cb][pl.ds(r, sub, stride=dil), :] = val[r * sub:(r + 1) * sub, cb * LANES:(cb + 1) * LANES]
    return jnp.concatenate([scr[cb] for cb in range(ncb)], axis=1)


def norm_proj(x, gain, sc, sh, w, tn, name, cols=None, dil=1, qk_norms=None):
    T, D = x.shape
    c0, ncol = cols if cols is not None else (0, w.shape[1] // tn)
    N = ncol * tn
    tm = 1024 if (ncol > 1 and dil == 1 and T % 1024 == 0) else _row_tile(T)
    sub = tm // dil
    n_in = 5 + (2 if qk_norms is not None else 0)
    n_out = 2 + (1 if qk_norms is not None else 0)
    Wd = tn // 3

    def body(*refs):
        x_ref, g_ref, sc_ref, sh_ref, w_ref = refs[:5]
        h_ref, o_ref = refs[n_in:n_in + 2]
        h_scr = refs[n_in + n_out]

        @pl.when(pl.program_id(1) == 0)
        def _():
            h = _modnorm(x_ref[...], g_ref[...], sc_ref[...], sh_ref[...])
            if dil > 1:
                h = _deinterleave(h, refs[n_in + n_out + 1], dil)
            h_scr[...] = h.astype(BF16)
            h_ref[...] = h.reshape(h_ref.shape).astype(BF16)
        res = jnp.dot(h_scr[...], w_ref[...], preferred_element_type=F32)
        o_ref[...] = res.reshape(o_ref.shape).astype(BF16)
        if qk_norms is not None:
            qk_ref = refs[n_in + 2]
            parts = []
            for t, scale in enumerate((ATT_DH ** -0.5, 1.0)):
                for hd in range(Wd // ATT_DH):
                    c = t * Wd + hd * ATT_DH
                    parts.append(_head_norm(res[:, c:c + ATT_DH], refs[5 + t][...]) * scale)
            qk_ref[...] = jnp.concatenate(parts, axis=1).reshape(qk_ref.shape).astype(BF16)

    vec = pl.BlockSpec((1, D), lambda i, j: (0, 0))
    in_specs = [pl.BlockSpec((tm, D), lambda i, j: (i, 0)), vec, vec, vec, pl.BlockSpec((D, tn), lambda i, j: (0, c0 + j))]
    args = [x, gain, sc, sh, w]
    out_specs = [pl.BlockSpec((dil, sub, D), lambda i, j: (0, i, 0)), pl.BlockSpec((dil, sub, tn), lambda i, j: (0, i, j))]
    out_shape = [_sds((dil, T // dil, D), BF16), _sds((dil, T // dil, N), BF16)]
    if qk_norms is not None:
        assert ncol == 1
        in_specs += [pl.BlockSpec((1, ATT_DH), lambda i, j: (0, 0))] * 2
        args += list(qk_norms)
        out_specs.append(pl.BlockSpec((dil, sub, 2 * Wd), lambda i, j: (0, i, 0)))
        out_shape.append(_sds((dil, T // dil, 2 * Wd), BF16))
    scratch = [pltpu.VMEM((tm, D), BF16)] + ([pltpu.VMEM((D // LANES, tm, LANES), F32)] if dil > 1 else [])
    res = pl.pallas_call(
        body, name=name, grid=(T // tm, ncol), in_specs=in_specs, out_specs=out_specs, out_shape=out_shape,
        scratch_shapes=scratch, compiler_params=_cp(("parallel", "arbitrary")),
    )(*args)
    return [r.reshape(T, -1) for r in res] if (dil == 1 and qk_norms is None) else res


def matmul_nn(a, w, out_dtype, name):
    T, K = a.shape
    N = w.shape[1]
    tm = _row_tile(T)

    def body(a_ref, w_ref, o_ref):
        o_ref[...] = _dot(a_ref[...], w_ref[...]).astype(o_ref.dtype)

    return pl.pallas_call(
        body, name=name, grid=(T // tm,),
        in_specs=[pl.BlockSpec((tm, K), lambda i: (i, 0)), pl.BlockSpec((K, N), lambda i: (0, 0))],
        out_specs=pl.BlockSpec((tm, N), lambda i: (i, 0)),
        out_shape=_sds((T, N), out_dtype),
        compiler_params=_cp(("parallel",)),
    )(a, w)


def proj_res(a, w, x, gate, name):
    T, K = a.shape
    D = w.shape[1]
    tm = _row_tile(T)

    def body(a_ref, w_ref, x_ref, g_ref, o_ref, y_ref):
        y = jnp.dot(a_ref[...], w_ref[...], preferred_element_type=F32)
        y_ref[...] = y.astype(BF16)
        o_ref[...] = x_ref[...] + g_ref[...] * y

    return pl.pallas_call(
        body, name=name, grid=(T // tm,),
        in_specs=[pl.BlockSpec((tm, K), lambda i: (i, 0)), pl.BlockSpec((K, D), lambda i: (0, 0)),
                  pl.BlockSpec((tm, D), lambda i: (i, 0)), pl.BlockSpec((1, D), lambda i: (0, 0))],
        out_specs=[pl.BlockSpec((tm, D), lambda i: (i, 0)), pl.BlockSpec((tm, D), lambda i: (i, 0))],
        out_shape=[_sds((T, D), F32), _sds((T, D), BF16)],
        compiler_params=_cp(("parallel",)),
    )(a, w, x, gate)


def bwd_out(dx, gate, y, w, tk, name):
    T, D = dx.shape
    K = w.shape[0]
    tm = _row_tile(T)

    def body(dx_ref, g_ref, y_ref, w_ref, dy_ref, da_ref, dg_ref, dy_scr):
        i, j = pl.program_id(0), pl.program_id(1)

        @pl.when(j == 0)
        def _():
            dxv = dx_ref[...]
            dy = (dxv * g_ref[...]).astype(BF16)
            dy_scr[...] = dy
            dy_ref[...] = dy
            part = jnp.sum(dxv * y_ref[...].astype(F32), axis=0, keepdims=True)

            @pl.when(i == 0)
            def _():
                dg_ref[...] = part

            @pl.when(i > 0)
            def _():
                dg_ref[...] += part

        da_ref[...] = _dot_nt(dy_scr[...], w_ref[...]).astype(BF16)

    return pl.pallas_call(
        body, name=name, grid=(T // tm, K // tk),
        in_specs=[pl.BlockSpec((tm, D), lambda i, j: (i, 0)), pl.BlockSpec((1, D), lambda i, j: (0, 0)),
                  pl.BlockSpec((tm, D), lambda i, j: (i, 0)), pl.BlockSpec((tk, D), lambda i, j: (j, 0))],
        out_specs=[pl.BlockSpec((tm, D), lambda i, j: (i, 0)), pl.BlockSpec((tm, tk), lambda i, j: (i, j)),
                   pl.BlockSpec((1, D), lambda i, j: (0, 0))],
        out_shape=[_sds((T, D), BF16), _sds((T, K), BF16), _sds((1, D), F32)],
        scratch_shapes=[pltpu.VMEM((tm, D), BF16)],
        compiler_params=_cp(("arbitrary", "arbitrary")),
    )(dx, gate, y, w)


def bwd_in(a_list, w, col_blocks, tn, acc, x, gain, sc, sh, dx_res, name, dil=1):
    T = math.prod(a_list[0].shape[:-1])
    n_a = len(a_list)
    w_list = list(w) if isinstance(w, (list, tuple)) else [w] * n_a
    D = w_list[0].shape[0]
    tm = 256 if T % 256 == 0 else T
    sub = tm // dil
    tns = [tn if tn is not None else a.shape[-1] for a in a_list]
    nsteps = a_list[0].shape[-1] // tns[0]
    assert all(a.shape[-1] == nsteps * t for a, t in zip(a_list, tns))
    final = x is not None
    has_acc = acc is not None

    def body(*refs):
        a_refs = refs[:n_a]
        w_refs = refs[n_a:2 * n_a]
        pos = 2 * n_a
        acc_ref = refs[pos] if has_acc else None
        pos += int(has_acc)
        if final:
            x_ref, g_ref, sc_ref, sh_ref, dxr_ref = refs[pos:pos + 5]
            pos += 5
            dx_ref, dg_ref, dsc_ref, dsh_ref = refs[pos:pos + 4]
            pos += 4
        else:
            dh_ref = refs[pos]
            pos += 1
        acc_scr = refs[pos]
        i, j = pl.program_id(0), pl.program_id(1)
        part = _dot_nt(a_refs[0][...].reshape(tm, tns[0]), w_refs[0][...])
        for k in range(1, n_a):
            part += _dot_nt(a_refs[k][...].reshape(tm, tns[k]), w_refs[k][...])

        @pl.when(j == 0)
        def _():
            acc_scr[...] = part

        @pl.when(j > 0)
        def _():
            acc_scr[...] += part

        @pl.when(j == nsteps - 1)
        def _():
            dh = acc_scr[...]
            if dil > 1:
                dh = _interleave(dh, refs[pos + 1], dil)
            if has_acc:
                dh = dh + acc_ref[...]
            if not final:
                dh_ref[...] = dh
                return
            _, vjp = jax.vjp(_modnorm, x_ref[...], g_ref[...], sc_ref[...], sh_ref[...])
            dxn, dg, dsc, dsh = vjp(dh)
            dx_ref[...] = dxr_ref[...] + dxn

            @pl.when(i == 0)
            def _():
                dg_ref[...] = dg
                dsc_ref[...] = dsc
                dsh_ref[...] = dsh

            @pl.when(i > 0)
            def _():
                dg_ref[...] += dg
                dsc_ref[...] += dsc
                dsh_ref[...] += dsh

    row = pl.BlockSpec((tm, D), lambda i, j: (i, 0))
    vec = pl.BlockSpec((1, D), lambda i, j: (0, 0))
    if dil == 1:
        in_specs = [pl.BlockSpec((tm, t), lambda i, j: (i, j)) for t in tns]
    else:
        in_specs = [pl.BlockSpec((dil, sub, t), lambda i, j: (0, i, j)) for t in tns]
    in_specs += [pl.BlockSpec((D, t), functools.partial(lambda i, j, c0: (0, c0 + j), c0=c0))
                 for t, c0 in zip(tns, col_blocks)]
    args = list(a_list) + w_list
    if has_acc:
        in_specs.append(row)
        args.append(acc)
    if final:
        in_specs += [row, vec, vec, vec, row]
        args += [x, gain, sc, sh, dx_res]
        out_specs = [row, vec, vec, vec]
        out_shape = [_sds((T, D), F32)] + [_sds((1, D), F32)] * 3
    else:
        out_specs = row
        out_shape = _sds((T, D), F32)
    return pl.pallas_call(
        body, name=name, grid=(T // tm, nsteps), in_specs=in_specs, out_specs=out_specs, out_shape=out_shape,
        scratch_shapes=[pltpu.VMEM((tm, D), F32)] + ([pltpu.VMEM((D // LANES, tm, LANES), F32)] if dil > 1 else []),
        compiler_params=_cp(("arbitrary", "arbitrary")),
    )(*args)


def matmul_tn(a, b, tk, tn, name, into=None, layer=0, col0=0):
    T, K = a.shape
    N = b.shape[1]
    tt = 1024 if T % 1024 == 0 else T

    def body(a_ref, b_ref, *rest):
        o_ref = rest[-1]
        part = _dot_tn(a_ref[...], b_ref[...])

        @pl.when(pl.program_id(2) == 0)
        def _():
            o_ref[...] = part

        @pl.when(pl.program_id(2) > 0)
        def _():
            o_ref[...] += part

    in_specs = [pl.BlockSpec((tt, tk), lambda k, n, t: (t, k)), pl.BlockSpec((tt, tn), lambda k, n, t: (t, n))]
    if into is None:
        return pl.pallas_call(
            body, name=name, grid=(K // tk, N // tn, T // tt), in_specs=in_specs,
            out_specs=pl.BlockSpec((tk, tn), lambda k, n, t: (k, n)), out_shape=_sds((K, N), F32),
            compiler_params=_cp(("parallel", "parallel", "arbitrary")),
        )(a, b)
    return pl.pallas_call(
        body, name=name, grid=(K // tk, N // tn, T // tt), in_specs=in_specs + [_ANY],
        out_specs=pl.BlockSpec((None, tk, tn), lambda k, n, t: (layer, k, col0 + n)),
        out_shape=_sds(into.shape, F32), input_output_aliases={2: 0},
        compiler_params=_cp(("parallel", "parallel", "arbitrary")),
    )(a, b, into)


def _halo_specs(tm, tc, col_of):
    per = tm // HALO

    def prev(j, i):
        return (jnp.maximum(i * per - 1, 0), col_of(j))

    def nxt(j, i, last):
        return (jnp.minimum((i + 1) * per, last), col_of(j))

    return prev, nxt, per


def _shift_down(ext, s):
    return pltpu.roll(ext, s, 0)


def _shift_up(ext, s):
    return pltpu.roll(ext, ext.shape[0] - s, 0)


def _conv_ext(ext, w):
    K = w.shape[0]
    out = w[K - 1:K] * ext
    for s in range(1, K):
        out += w[K - 1 - s:K - s] * _shift_down(ext, s)
    return out


def _conv_t_ext(dext, w):
    K = w.shape[0]
    out = w[K - 1:K] * dext
    for s in range(1, K):
        out += w[K - 1 - s:K - s] * _shift_up(dext, s)
    return out


def _conv_dw(dc, ext, K, tm):
    rowid = lax.broadcasted_iota(jnp.int32, (8, dc.shape[1]), 0)
    out = jnp.zeros((8, dc.shape[1]), F32)
    for j in range(K):
        s = K - 1 - j
        xs = ext if s == 0 else _shift_down(ext, s)
        out = jnp.where(rowid == j, jnp.sum(dc * xs[HALO:HALO + tm], axis=0, keepdims=True), out)
    return out


def _accum(ref, val, first):
    @pl.when(first)
    def _():
        ref[...] = val

    @pl.when(jnp.logical_not(first))
    def _():
        ref[...] += val


def ffn_mid_fwd(up, conv_w, conv_b, name):
    T, two_f = up.shape
    F = two_f // 2
    tm = _row_tile(T)
    tc = F // 2
    nct = F // tc
    prev, _, per = _halo_specs(tm, tc, lambda j: j)

    def body(g_ref, gp_ref, v_ref, w_ref, b_ref, a_ref):
        i = pl.program_id(1)
        gp = jnp.where(i > 0, gp_ref[...].astype(F32), 0.0)
        ext = jnp.concatenate([gp, g_ref[...].astype(F32)], axis=0)
        c = _conv_ext(ext, w_ref[...])[HALO:] + b_ref[...]
        a_ref[...] = (_silu(c) * v_ref[...].astype(F32)).astype(BF16)

    return pl.pallas_call(
        body, name=name, grid=(nct, T // tm),
        in_specs=[pl.BlockSpec((tm, tc), lambda j, i: (i, j)), pl.BlockSpec((HALO, tc), prev),
                  pl.BlockSpec((tm, tc), lambda j, i: (i, j + nct)),
                  pl.BlockSpec((FFN_CONV, tc), lambda j, i: (0, j)), pl.BlockSpec((1, tc), lambda j, i: (0, j))],
        out_specs=pl.BlockSpec((tm, tc), lambda j, i: (i, j)),
        out_shape=_sds((T, F), BF16),
        compiler_params=_cp(("parallel", "arbitrary")),
    )(up, up, up, conv_w, conv_b)


def ffn_mid_bwd(up, da, conv_w, conv_b, name):
    T, two_f = up.shape
    F = two_f // 2
    tm = _row_tile(T)
    tc = F // 2
    nct = F // tc
    nrow = T // tm
    prev, nxt, per = _halo_specs(tm, tc, lambda j: j)
    last = T // HALO - 1
    nxt_g = functools.partial(nxt, last=last)

    def nxt_v(j, i):
        return (jnp.minimum((i + 1) * per, last), j + nct)

    def body(g_ref, gp_ref, gn_ref, v_ref, vn_ref, da_ref, dan_ref, w_ref, b_ref, dg_ref, dv_ref, dw_ref, db_ref):
        i = pl.program_id(1)
        w = w_ref[...]
        gp = jnp.where(i > 0, gp_ref[...].astype(F32), 0.0)
        inside = i < nrow - 1
        ext = jnp.concatenate([gp, g_ref[...].astype(F32), gn_ref[...].astype(F32)], axis=0)
        zero = jnp.zeros((HALO, tc), F32)
        v_ext = jnp.concatenate([zero, v_ref[...].astype(F32), vn_ref[...].astype(F32)], axis=0)
        da_ext = jnp.concatenate([zero, da_ref[...].astype(F32), jnp.where(inside, dan_ref[...].astype(F32), 0.0)], axis=0)
        c = _conv_ext(ext, w) + b_ref[...]
        sg = jax.nn.sigmoid(c)
        sil = c * sg
        dc = da_ext * v_ext * (sg * (1.0 + c * (1.0 - sg)))
        dv_ref[...] = (da_ext * sil)[HALO:HALO + tm].astype(BF16)
        dg_ref[...] = _conv_t_ext(dc, w)[HALO:HALO + tm].astype(BF16)
        dcm = dc[HALO:HALO + tm]
        _accum(dw_ref, _conv_dw(dcm, ext, FFN_CONV, tm), i == 0)
        _accum(db_ref, jnp.sum(dcm, axis=0, keepdims=True), i == 0)

    main = lambda j, i: (i, j)
    main_v = lambda j, i: (i, j + nct)
    return pl.pallas_call(
        body, name=name, grid=(nct, nrow),
        in_specs=[pl.BlockSpec((tm, tc), main), pl.BlockSpec((HALO, tc), prev), pl.BlockSpec((HALO, tc), nxt_g),
                  pl.BlockSpec((tm, tc), main_v), pl.BlockSpec((HALO, tc), nxt_v),
                  pl.BlockSpec((tm, tc), main), pl.BlockSpec((HALO, tc), nxt_g),
                  pl.BlockSpec((FFN_CONV, tc), lambda j, i: (0, j)), pl.BlockSpec((1, tc), lambda j, i: (0, j))],
        out_specs=[pl.BlockSpec((tm, tc), main), pl.BlockSpec((tm, tc), main),
                   pl.BlockSpec((8, tc), lambda j, i: (0, j)), pl.BlockSpec((1, tc), lambda j, i: (0, j))],
        out_shape=[_sds((T, F), BF16), _sds((T, F), BF16), _sds((8, F), F32), _sds((1, F), F32)],
        compiler_params=_cp(("parallel", "arbitrary")),
    )(up, up, up, up, up, da, da, conv_w, conv_b)


def _l2n(x):
    return x * lax.rsqrt(jnp.sum(x * x, axis=-1, keepdims=True) + RMS_EPS)


def _qkv_tok(c, normed):
    s = _silu(c)
    n = jnp.concatenate([_l2n(s[:, h * GDN_D:(h + 1) * GDN_D]) for h in range(GDN_HEADS)], axis=1)
    return jnp.where(normed, n, s)


def gdn_prep_fwd(proj, conv_w, name):
    T = proj.shape[0]
    W = GDN_HEADS * GDN_D
    tm = _row_tile(T)
    prev, _, per = _halo_specs(tm, W, lambda j: j)

    def body(x_ref, xp_ref, w_ref, o_ref):
        j, i = pl.program_id(0), pl.program_id(1)
        xp = jnp.where(i > 0, xp_ref[...].astype(F32), 0.0)
        ext = jnp.concatenate([xp, x_ref[...].astype(F32)], axis=0)
        c = _conv_ext(ext, w_ref[...])[HALO:]
        o_ref[...] = _qkv_tok(c, j < 2)

    return pl.pallas_call(
        body, name=name, grid=(3, T // tm),
        in_specs=[pl.BlockSpec((tm, W), lambda j, i: (i, j)), pl.BlockSpec((HALO, W), prev),
                  pl.BlockSpec((GDN_CONV, W), lambda j, i: (0, j))],
        out_specs=pl.BlockSpec((tm, W), lambda j, i: (i, j)),
        out_shape=_sds((T, 3 * W), F32),
        compiler_params=_cp(("parallel", "arbitrary")),
    )(proj, proj, conv_w)


def gdn_prep_bwd(proj, dqkv, conv_w, name):
    T = proj.shape[0]
    W = GDN_HEADS * GDN_D
    tm = _row_tile(T)
    nrow = T // tm
    prev, nxt, per = _halo_specs(tm, W, lambda j: j)
    nxt = functools.partial(nxt, last=T // HALO - 1)

    def body(x_ref, xp_ref, xn_ref, d_ref, dn_ref, w_ref, dx_ref, dw_ref):
        j, i = pl.program_id(0), pl.program_id(1)
        w = w_ref[...]
        xp = jnp.where(i > 0, xp_ref[...].astype(F32), 0.0)
        ext = jnp.concatenate([xp, x_ref[...].astype(F32), xn_ref[...].astype(F32)], axis=0)
        d_ext = jnp.concatenate([jnp.zeros((HALO, W), F32), d_ref[...],
                                 jnp.where(i < nrow - 1, dn_ref[...], 0.0)], axis=0)
        c = _conv_ext(ext, w)
        _, vjp = jax.vjp(lambda cc: _qkv_tok(cc, j < 2), c)
        dc, = vjp(d_ext)
        dx_ref[...] = _conv_t_ext(dc, w)[HALO:HALO + tm].astype(BF16)
        _accum(dw_ref, _conv_dw(dc[HALO:HALO + tm], ext, GDN_CONV, tm), i == 0)

    main = lambda j, i: (i, j)
    return pl.pallas_call(
        body, name=name, grid=(3, nrow),
        in_specs=[pl.BlockSpec((tm, W), main), pl.BlockSpec((HALO, W), prev), pl.BlockSpec((HALO, W), nxt),
                  pl.BlockSpec((tm, W), main), pl.BlockSpec((HALO, W), nxt),
                  pl.BlockSpec((GDN_CONV, W), lambda j, i: (0, j))],
        out_specs=[pl.BlockSpec((tm, W), main), pl.BlockSpec((8, W), lambda j, i: (0, j))],
        out_shape=[_sds((T, 3 * W), BF16), _sds((8, 3 * W), F32)],
        compiler_params=_cp(("parallel", "arbitrary")),
    )(proj, proj, proj, dqkv, dqkv, conv_w)


def _gate_tok(ba, a_log, dt_bias):
    z = ba + dt_bias
    softplus = jnp.maximum(z, 0.0) + jnp.log(1.0 + jnp.exp(-jnp.abs(z)))
    lane = lax.broadcasted_iota(jnp.int32, ba.shape, 1)
    raw = jnp.where(lane < GDN_HEADS, jax.nn.sigmoid(ba), -jnp.exp(a_log) * softplus)
    n = ba.shape[0]
    r = lax.broadcasted_iota(jnp.int32, (n, n), 0)
    c = lax.broadcasted_iota(jnp.int32, (n, n), 1)
    in_chunk_before = (r >= c) & ((r - c) <= (r & (GDN_CHUNK - 1)))
    return jnp.where(lane < GDN_HEADS, raw, _dot_hi(in_chunk_before.astype(F32), raw))


def gate_fwd(ba, a_log, dt_bias, name):
    T, L = ba.shape
    tm = _row_tile(T)

    def body(ba_ref, al_ref, dt_ref, o_ref):
        o_ref[...] = _gate_tok(ba_ref[...], al_ref[...], dt_ref[...])

    vec = pl.BlockSpec((1, L), lambda i: (0, 0))
    return pl.pallas_call(
        body, name=name, grid=(T // tm,),
        in_specs=[pl.BlockSpec((tm, L), lambda i: (i, 0)), vec, vec],
        out_specs=pl.BlockSpec((tm, L), lambda i: (i, 0)), out_shape=_sds((T, L), F32),
        compiler_params=_cp(("parallel",)),
    )(ba, a_log, dt_bias)


def gate_bwd(ba, a_log, dt_bias, dout_a, dout_b, name):
    T, L = ba.shape
    tm = _row_tile(T)

    def body(ba_ref, al_ref, dt_ref, d_ref, d2_ref, dba_ref, dal_ref, ddt_ref):
        _, vjp = jax.vjp(_gate_tok, ba_ref[...], al_ref[...], dt_ref[...])
        dba, dal, ddt = vjp(d_ref[...] + d2_ref[...])
        dba_ref[...] = dba.astype(BF16)
        first = pl.program_id(0) == 0
        _accum(dal_ref, dal, first)
        _accum(ddt_ref, ddt, first)

    vec = pl.BlockSpec((1, L), lambda i: (0, 0))
    row = pl.BlockSpec((tm, L), lambda i: (i, 0))
    return pl.pallas_call(
        body, name=name, grid=(T // tm,),
        in_specs=[row, vec, vec, row, row], out_specs=[row, vec, vec],
        out_shape=[_sds((T, L), BF16), _sds((1, L), F32), _sds((1, L), F32)],
        compiler_params=_cp(("arbitrary",)),
    )(ba, a_log, dt_bias, dout_a, dout_b)


_B_NN = (((2,), (1,)), ((0,), (0,)))
_B_NT = (((2,), (2,)), ((0,), (0,)))
GDN_GROUP = 8


def _split_bf16(a):
    hi = a.astype(BF16)
    return hi, (a - hi.astype(F32)).astype(BF16)


def _bdot3(a, b):
    ah, al = _split_bf16(a)
    bh, bl = _split_bf16(b)
    dg = lambda x, y: lax.dot_general(x, y, _B_NN, preferred_element_type=F32)
    return dg(ah, bh) + (dg(ah, bl) + dg(al, bh))


def _exact_ones_dot(x, dims, ones_shape, ones_first):
    ones = jnp.ones(ones_shape, BF16)
    out = None
    for _ in range(3):
        piece = x.astype(BF16)
        ops = (ones, piece) if ones_first else (piece, ones)
        term = lax.dot_general(*ops, dims, preferred_element_type=F32)
        out = term if out is None else out + term
        x = x - piece.astype(F32)
    return out


@jax.custom_vjp
def _rows_from_cols(col):
    B, C, _ = col.shape
    eye = (lax.broadcasted_iota(jnp.int32, (1, C, C), 1) == lax.broadcasted_iota(jnp.int32, (1, C, C), 2)).astype(F32)
    return _exact_ones_dot(col * eye, _B_NN, (B, C, C), True)


def _rows_from_cols_fwd(col):
    return _rows_from_cols(col), None


def _rows_from_cols_bwd(_, ct):
    B, C, _ = ct.shape
    sums = _exact_ones_dot(ct, (((1,), (1,)), ((0,), (0,))), (B, C, LANES), False)
    return (sums[:, :, :1],)


_rows_from_cols.defvjp(_rows_from_cols_fwd, _rows_from_cols_bwd)


def _gdn_prep(q, k, v, gcol, bcol):
    B, C, dk = q.shape
    r = lax.broadcasted_iota(jnp.int32, (1, C, C), 1)
    cidx = lax.broadcasted_iota(jnp.int32, (1, C, C), 2)
    causal = r >= cidx
    strict = r > cidx
    decay = jnp.where(causal, jnp.exp(jnp.where(causal, gcol - _rows_from_cols(gcol), 0.0)), 0.0)
    qs = q * (dk ** -0.5)
    kb = k * bcol
    kk = lax.dot_general(kb.astype(BF16), k.astype(BF16), _B_NT, preferred_element_type=F32)
    L = jnp.where(strict, kk * decay, 0.0)
    tinv = (r == cidx).astype(F32) - L
    p = L
    n = 2
    while n < C:
        p = _bdot3(p, p)
        tinv = tinv + _bdot3(tinv, p)
        n *= 2
    egc = jnp.exp(gcol)
    u = _bdot3(tinv, v * bcol)
    w = _bdot3(tinv, kb * egc)
    qk = lax.dot_general(qs.astype(BF16), k.astype(BF16), _B_NT, preferred_element_type=F32)
    intra = jnp.where(causal, qk * decay, 0.0)
    last = lax.broadcasted_iota(jnp.int32, (1, C, 1), 1) == C - 1
    gt = jnp.sum(jnp.where(last, gcol, 0.0), axis=1, keepdims=True)
    return w, u, qs * egc, k * jnp.exp(gt - gcol), intra


def _gdn_scan(S, w, u, qg, kdec, intra, gcol):
    C = w.shape[0]
    last = lax.broadcasted_iota(jnp.int32, (C, 1), 0) == C - 1
    gt = jnp.sum(jnp.where(last, gcol, 0.0), axis=0, keepdims=True)
    v_new = u - _dot(w, S)
    o = _dot(qg, S) + _dot(intra, v_new)
    return S * jnp.exp(gt) + _dot_tn(kdec, v_new), o


def _gdn_specs(T, rev=False):
    H, D, C, B = GDN_HEADS, GDN_D, GDN_CHUNK, GDN_GROUP
    nsteps = T // (B * C)
    at = (lambda n: nsteps - 1 - n) if rev else (lambda n: n)
    rows = lambda w: pl.BlockSpec((B * C, w), lambda n: (at(n), 0))
    sq = pl.BlockSpec((B, H, C, C), lambda n: (at(n), 0, 0, 0))
    st = pl.BlockSpec((B, H, D, D), lambda n: (at(n), 0, 0, 0))
    return rows, sq, st, nsteps


def _gate_cols(gate_ref, h, rows=slice(None)):
    return gate_ref[rows, h:h + 1], gate_ref[rows, GDN_HEADS + h:GDN_HEADS + h + 1]


def gdn_prep_chunks_fwd(qkv, gate, name):
    T = qkv.shape[0]
    H, D, C, B = GDN_HEADS, GDN_D, GDN_CHUNK, GDN_GROUP
    W = H * D
    rows, sq, _, nsteps = _gdn_specs(T)

    def body(x_ref, g_ref, w_ref, u_ref, qg_ref, kd_ref, in_ref):
        for h in range(H):
            q, k, v = [x_ref[:, j * W + h * D:j * W + (h + 1) * D].reshape(B, C, D) for j in range(3)]
            bcol, gcol = [c.reshape(B, C, 1) for c in _gate_cols(g_ref, h)]
            outs = _gdn_prep(q, k, v, gcol, bcol)
            for ref, val in zip((w_ref, u_ref, qg_ref, kd_ref), outs[:4]):
                ref[:, h * D:(h + 1) * D] = val.reshape(B * C, D).astype(ref.dtype)
            in_ref[:, h] = outs[4].astype(BF16)

    return pl.pallas_call(
        body, name=name, grid=(nsteps,),
        in_specs=[rows(3 * W), rows(LANES)],
        out_specs=[rows(W), rows(W), rows(W), rows(W), sq],
        out_shape=[_sds((T, W), BF16), _sds((T, W), F32), _sds((T, W), BF16), _sds((T, W), BF16),
                   _sds((T // C, H, C, C), BF16)],
        compiler_params=_cp(("parallel",)),
    )(qkv, gate)


def gdn_prep_chunks_bwd(qkv, gate, dw, du, dqg, dkd, dintra, name):
    T = qkv.shape[0]
    H, D, C, B = GDN_HEADS, GDN_D, GDN_CHUNK, GDN_GROUP
    W = H * D
    rows, sq, _, nsteps = _gdn_specs(T)

    def body(x_ref, g_ref, dw_ref, du_ref, dqg_ref, dkd_ref, din_ref, dx_ref, dg_ref):
        dg_ref[...] = jnp.zeros_like(dg_ref)
        for h in range(H):
            q, k, v = [x_ref[:, j * W + h * D:j * W + (h + 1) * D].reshape(B, C, D) for j in range(3)]
            bcol, gcol = [c.reshape(B, C, 1) for c in _gate_cols(g_ref, h)]
            _, vjp = jax.vjp(_gdn_prep, q, k, v, gcol, bcol)
            cots = tuple(r[:, h * D:(h + 1) * D].reshape(B, C, D) for r in (dw_ref, du_ref, dqg_ref, dkd_ref))
            dq, dk, dv, dgc, db = vjp(cots + (din_ref[:, h],))
            for j, val in enumerate((dq, dk, dv)):
                dx_ref[:, j * W + h * D:j * W + (h + 1) * D] = val.reshape(B * C, D)
            dg_ref[:, h:h + 1] = db.reshape(B * C, 1)
            dg_ref[:, H + h:H + h + 1] = dgc.reshape(B * C, 1)

    return pl.pallas_call(
        body, name=name, grid=(nsteps,),
        in_specs=[rows(3 * W), rows(LANES), rows(W), rows(W), rows(W), rows(W), sq],
        out_specs=[rows(3 * W), rows(LANES)],
        out_shape=[_sds((T, 3 * W), F32), _sds((T, LANES), F32)],
        compiler_params=_cp(("parallel",)),
    )(qkv, gate, dw, du, dqg, dkd, dintra)


def gdn_fwd(w, u, qg, kdec, intra, gate, name):
    T = w.shape[0]
    H, D, C, B = GDN_HEADS, GDN_D, GDN_CHUNK, GDN_GROUP
    W = H * D
    rows, sq, st, nsteps = _gdn_specs(T)

    def body(w_ref, u_ref, qg_ref, kd_ref, in_ref, g_ref, o_ref, s_ref, s_scr):
        @pl.when(pl.program_id(0) == 0)
        def _():
            s_scr[...] = jnp.zeros_like(s_scr)

        def chunk(cb, carry):
            rs = pl.ds(pl.multiple_of(cb * C, C), C)
            for h in range(H):
                cs = slice(h * D, (h + 1) * D)
                S = s_scr[h]
                s_ref[cb, h] = S
                S_new, o = _gdn_scan(S, w_ref[rs, cs], u_ref[rs, cs], qg_ref[rs, cs], kd_ref[rs, cs], in_ref[cb, h],
                                     _gate_cols(g_ref, h, rs)[1])
                s_scr[h] = S_new
                o_ref[rs, cs] = o
            return carry

        lax.fori_loop(0, B, chunk, 0)

    return pl.pallas_call(
        body, name=name, grid=(nsteps,),
        in_specs=[rows(W), rows(W), rows(W), rows(W), sq, rows(LANES)],
        out_specs=[rows(W), st],
        out_shape=[_sds((T, W), F32), _sds((T // C, H, D, D), F32)],
        scratch_shapes=[pltpu.VMEM((H, D, D), F32)],
        compiler_params=_cp(("arbitrary",)),
    )(w, u, qg, kdec, intra, gate)


def gdn_bwd(w, u, qg, kdec, intra, gate, states, do, name):
    T = w.shape[0]
    H, D, C, B = GDN_HEADS, GDN_D, GDN_CHUNK, GDN_GROUP
    W = H * D
    rows, sq, st, nsteps = _gdn_specs(T, rev=True)

    def body(w_ref, u_ref, qg_ref, kd_ref, in_ref, g_ref, s_ref, do_ref,
             dw_ref, du_ref, dqg_ref, dkd_ref, din_ref, dg_ref, ds_scr):
        @pl.when(pl.program_id(0) == 0)
        def _():
            ds_scr[...] = jnp.zeros_like(ds_scr)
        dg_ref[...] = jnp.zeros_like(dg_ref)

        def chunk(t, carry):
            cb = B - 1 - t
            rs = pl.ds(pl.multiple_of(cb * C, C), C)
            for h in range(H):
                cs = slice(h * D, (h + 1) * D)
                _, vjp = jax.vjp(_gdn_scan, s_ref[cb, h], w_ref[rs, cs].astype(F32), u_ref[rs, cs],
                                 qg_ref[rs, cs].astype(F32), kd_ref[rs, cs].astype(F32), in_ref[cb, h].astype(F32),
                                 _gate_cols(g_ref, h, rs)[1])
                dS, dw, du, dqg, dkd, din, dgc = vjp((ds_scr[h], do_ref[rs, cs]))
                ds_scr[h] = dS
                dw_ref[rs, cs] = dw
                du_ref[rs, cs] = du
                dqg_ref[rs, cs] = dqg
                dkd_ref[rs, cs] = dkd
                din_ref[cb, h] = din
                dg_ref[rs, H + h:H + h + 1] = dgc
            return carry

        lax.fori_loop(0, B, chunk, 0)

    return pl.pallas_call(
        body, name=name, grid=(nsteps,),
        in_specs=[rows(W), rows(W), rows(W), rows(W), sq, rows(LANES), st, rows(W)],
        out_specs=[rows(W), rows(W), rows(W), rows(W), sq, rows(LANES)],
        out_shape=[_sds((T, W), F32)] * 4 + [_sds((T // C, H, C, C), F32), _sds((T, LANES), F32)],
        scratch_shapes=[pltpu.VMEM((H, D, D), F32)],
        compiler_params=_cp(("arbitrary",)),
    )(w, u, qg, kdec, intra, gate, states, do)


def _gated_norm(o, z, gain):
    outs = []
    for h in range(GDN_HEADS):
        oh = o[:, h * GDN_D:(h + 1) * GDN_D]
        y = oh * lax.rsqrt(jnp.mean(oh * oh, axis=-1, keepdims=True) + RMS_EPS) * gain
        outs.append(y * _silu(z[:, h * GDN_D:(h + 1) * GDN_D]))
    return jnp.concatenate(outs, axis=1)


def _window_sums(ext, shift):
    outs = []
    s = ext
    step = 1
    for gi, win in enumerate(POOL_WINDOWS):
        while step < win:
            s = s + shift(s, step)
            step *= 2
        outs.append(s[:, gi * GDN_D:(gi + 1) * GDN_D])
    return jnp.concatenate(outs, axis=1)


def _pool_counts(t0, rows, width):
    t1 = (t0 + 1 + lax.broadcasted_iota(jnp.int32, (rows, width), 0)).astype(F32)
    lane = lax.broadcasted_iota(jnp.int32, (rows, width), 1)
    win = jnp.full((rows, width), float(POOL_WINDOWS[-1]), F32)
    for gi in reversed(range(len(POOL_WINDOWS) - 1)):
        win = jnp.where(lane < (gi + 1) * GDN_D, float(POOL_WINDOWS[gi]), win)
    return jnp.minimum(t1, win)


def even_post_fwd(o, proj, gdn_norm, pool_w, pool_scale, name):
    T = o.shape[0]
    W = GDN_HEADS * GDN_D
    tm = _row_tile(T)
    per = tm // HALO

    def body(o_ref, z_ref, p_ref, pp_ref, gn_ref, pw_ref, ps_ref, out_ref):
        i = pl.program_id(0)
        out_ref[:, :W] = _gated_norm(o_ref[...], z_ref[...].astype(F32), gn_ref[...]).astype(BF16)
        pp = jnp.where(i > 0, pp_ref[...].astype(F32), 0.0)
        ext = jnp.concatenate([pp, p_ref[...].astype(F32)], axis=0)
        pooled = (_window_sums(ext, _shift_down)[HALO:] / _pool_counts(i * tm, tm, W)) - ext[HALO:]
        for gi in range(len(POOL_WINDOWS)):
            sl = slice(gi * GDN_D, (gi + 1) * GDN_D)
            y = _dot(pooled[:, sl], pw_ref[gi]) * ps_ref[:, sl]
            out_ref[:, W + gi * GDN_D:W + (gi + 1) * GDN_D] = y.astype(BF16)

    return pl.pallas_call(
        body, name=name, grid=(T // tm,),
        in_specs=[pl.BlockSpec((tm, W), lambda i: (i, 0)), pl.BlockSpec((tm, W), lambda i: (i, 3)),
                  pl.BlockSpec((tm, W), lambda i: (i, 4)),
                  pl.BlockSpec((HALO, W), lambda i: (jnp.maximum(i * per - 1, 0), 4)),
                  pl.BlockSpec((1, GDN_D), lambda i: (0, 0)),
                  pl.BlockSpec((len(POOL_WINDOWS), GDN_D, GDN_D), lambda i: (0, 0, 0)),
                  pl.BlockSpec((1, W), lambda i: (0, 0))],
        out_specs=pl.BlockSpec((tm, 2 * W), lambda i: (i, 0)),
        out_shape=_sds((T, 2 * W), BF16),
        compiler_params=_cp(("arbitrary",)),
    )(o, proj, proj, proj, gdn_norm, pool_w, pool_scale)


def even_post_bwd(o, proj, gdn_norm, pool_w, pool_scale, dcat, name):
    T = o.shape[0]
    W = GDN_HEADS * GDN_D
    G = len(POOL_WINDOWS)
    tm = _row_tile(T)
    per = tm // HALO
    nrow = T // tm
    last = T // HALO - 1

    def body(o_ref, z_ref, p_ref, pp_ref, gn_ref, pw_ref, ps_ref, d_ref, dn_ref,
             do_ref, dzp_ref, dgn_ref, dpw_ref, dps_ref):
        i = pl.program_id(0)
        first = i == 0
        _, vjp = jax.vjp(_gated_norm, o_ref[...], z_ref[...].astype(F32), gn_ref[...])
        do, dz, dgn = vjp(d_ref[:, :W].astype(F32))
        do_ref[...] = do
        dzp_ref[:, :W] = dz.astype(BF16)
        _accum(dgn_ref, dgn, first)
        pp = jnp.where(first, 0.0, pp_ref[...].astype(F32))
        ext = jnp.concatenate([pp, p_ref[...].astype(F32)], axis=0)
        pooled = (_window_sums(ext, _shift_down)[HALO:] / _pool_counts(i * tm, tm, W)) - ext[HALO:]
        dy_ext = jnp.concatenate([d_ref[:, W:].astype(F32), jnp.where(i < nrow - 1, dn_ref[...].astype(F32), 0.0)], axis=0)
        dys_ext = dy_ext * ps_ref[...]
        dpooled, dscale = [], []
        for gi in range(G):
            sl = slice(gi * GDN_D, (gi + 1) * GDN_D)
            dpooled.append(_dot_nt(dys_ext[:, sl], pw_ref[gi]))
            y = _dot(pooled[:, sl], pw_ref[gi])
            dscale.append(jnp.sum(dy_ext[:tm, sl] * y, axis=0, keepdims=True))
            _accum(dpw_ref.at[gi], _dot_tn(pooled[:, sl], dys_ext[:tm, sl]), first)
        dpooled = jnp.concatenate(dpooled, axis=1)
        _accum(dps_ref, jnp.concatenate(dscale, axis=1), first)
        dmean = dpooled / _pool_counts(i * tm, tm + HALO, W)
        dp = _window_sums(dmean, _shift_up)[:tm] - dpooled[:tm]
        dzp_ref[:, W:] = dp.astype(BF16)

    return pl.pallas_call(
        body, name=name, grid=(nrow,),
        in_specs=[pl.BlockSpec((tm, W), lambda i: (i, 0)), pl.BlockSpec((tm, W), lambda i: (i, 3)),
                  pl.BlockSpec((tm, W), lambda i: (i, 4)),
                  pl.BlockSpec((HALO, W), lambda i: (jnp.maximum(i * per - 1, 0), 4)),
                  pl.BlockSpec((1, GDN_D), lambda i: (0, 0)),
                  pl.BlockSpec((G, GDN_D, GDN_D), lambda i: (0, 0, 0)),
                  pl.BlockSpec((1, W), lambda i: (0, 0)),
                  pl.BlockSpec((tm, 2 * W), lambda i: (i, 0)),
                  pl.BlockSpec((HALO, W), lambda i: (jnp.minimum((i + 1) * per, last), 1))],
        out_specs=[pl.BlockSpec((tm, W), lambda i: (i, 0)), pl.BlockSpec((tm, 2 * W), lambda i: (i, 0)),
                   pl.BlockSpec((1, GDN_D), lambda i: (0, 0)), pl.BlockSpec((G, GDN_D, GDN_D), lambda i: (0, 0, 0)),
                   pl.BlockSpec((1, W), lambda i: (0, 0))],
        out_shape=[_sds((T, W), F32), _sds((T, 2 * W), BF16), _sds((1, GDN_D), F32), _sds((G, GDN_D, GDN_D), F32),
                   _sds((1, W), F32)],
        compiler_params=_cp(("arbitrary",)),
    )(o, proj, proj, proj, gdn_norm, pool_w, pool_scale, dcat, dcat)


def _head_norm(x, gain):
    return x * lax.rsqrt(jnp.mean(x * x, axis=-1, keepdims=True) + RMS_EPS) * gain


def _att_scores(q, k, slope, has_prev):
    B = ATT_BLOCK
    a = lax.broadcasted_iota(jnp.int32, (B, 2 * B), 0)
    j = lax.broadcasted_iota(jnp.int32, (B, 2 * B), 1)
    rel = B + a - j
    mask = (rel >= 0) & (rel <= B) & ((j >= B) | has_prev)
    s = _dot_nt(q, k) - slope * rel.astype(F32)
    return jnp.where(mask, s, NEG), mask


def _alibi_slope(h, dil):
    return dil * (2.0 ** (-8.0 * (h + 1) / ATT_HEADS))


def att_fwd(qk, qkv, name):
    dil, L, _ = qk.shape
    Wd = ATT_HEADS * ATT_DH
    nb = L // ATT_BLOCK
    B = ATT_BLOCK

    def body(q_ref, kc_ref, kp_ref, vc_ref, vp_ref, o_ref, l_ref):
        has_prev = pl.program_id(1) > 0
        for h in range(ATT_HEADS):
            sl = slice(h * ATT_DH, (h + 1) * ATT_DH)
            k = jnp.concatenate([kp_ref[:, sl], kc_ref[:, sl]], axis=0)
            v = jnp.concatenate([vp_ref[:, sl], vc_ref[:, sl]], axis=0)
            s, _ = _att_scores(q_ref[:, sl], k, _alibi_slope(h, dil), has_prev)
            m = jnp.max(s, axis=-1, keepdims=True)
            p = jnp.exp(s - m)
            l = jnp.sum(p, axis=-1, keepdims=True)
            o_ref[:, sl] = (_dot(p, v) / l).astype(BF16)
            l_ref[:, sl] = jnp.broadcast_to(m + jnp.log(l), (B, ATT_DH))

    cur = lambda t: pl.BlockSpec((None, B, Wd), lambda r, n: (r, n, t))
    prev = lambda t: pl.BlockSpec((None, B, Wd), lambda r, n: (r, jnp.maximum(n - 1, 0), t))
    out = pl.BlockSpec((None, B, Wd), lambda r, n: (r, n, 0))
    return pl.pallas_call(
        body, name=name, grid=(dil, nb),
        in_specs=[cur(0), cur(1), prev(1), cur(2), prev(2)],
        out_specs=[out, out], out_shape=[_sds((dil, L, Wd), BF16), _sds((dil, L, Wd), F32)],
        compiler_params=_cp(("parallel", "arbitrary")),
    )(qk, qk, qk, qkv, qkv)


def att_bwd(qk, qkv, lse, do, dd, name):
    dil, L, _ = qk.shape
    Wd = ATT_HEADS * ATT_DH
    nb = L // ATT_BLOCK
    B = ATT_BLOCK

    def body(q_ref, kc_ref, kp_ref, vc_ref, vp_ref, l_ref, do_ref, dd_ref, dq_ref, dk_ref, dv_ref, ck_scr, cv_scr):
        n = pl.program_id(1)
        has_prev = n > 0

        @pl.when(n < nb)
        def _():
            for h in range(ATT_HEADS):
                sl = slice(h * ATT_DH, (h + 1) * ATT_DH)
                q = q_ref[:, sl]
                k = jnp.concatenate([kp_ref[:, sl], kc_ref[:, sl]], axis=0)
                v = jnp.concatenate([vp_ref[:, sl], vc_ref[:, sl]], axis=0)
                do = do_ref[:, sl]
                s, mask = _att_scores(q, k, _alibi_slope(h, dil), has_prev)
                p = jnp.where(mask, jnp.exp(s - l_ref[:, h * ATT_DH:h * ATT_DH + 1]), 0.0)
                delta = jnp.sum(dd_ref[:, sl].astype(F32), axis=-1, keepdims=True)
                ds = p * (_dot_nt(do, v) - delta)
                dq_ref[:, sl] = _dot(ds, k).astype(BF16)
                dk = _dot_tn(ds, q)
                dv = _dot_tn(p, do)
                dk_ref[:, sl] = (jnp.where(has_prev, ck_scr[:, sl] + dk[:B], 0.0)).astype(BF16)
                dv_ref[:, sl] = (jnp.where(has_prev, cv_scr[:, sl] + dv[:B], 0.0)).astype(BF16)
                ck_scr[:, sl] = dk[B:]
                cv_scr[:, sl] = dv[B:]

        @pl.when(n == nb)
        def _():
            dk_ref[...] = ck_scr[...].astype(BF16)
            dv_ref[...] = cv_scr[...].astype(BF16)

    cur = lambda t: pl.BlockSpec((None, B, Wd), lambda r, n: (r, jnp.minimum(n, nb - 1), t))
    prev = lambda t: pl.BlockSpec((None, B, Wd), lambda r, n: (r, jnp.clip(n - 1, 0, nb - 1), t))
    kv_out = pl.BlockSpec((None, B, Wd), lambda r, n: (r, jnp.maximum(n - 1, 0), 0))
    return pl.pallas_call(
        body, name=name, grid=(dil, nb + 1),
        in_specs=[cur(0), cur(1), prev(1), cur(2), prev(2), cur(0), cur(0), cur(0)],
        out_specs=[cur(0), kv_out, kv_out],
        out_shape=[_sds((dil, L, Wd), BF16)] * 3,
        scratch_shapes=[pltpu.VMEM((B, Wd), F32), pltpu.VMEM((B, Wd), F32)],
        compiler_params=_cp(("parallel", "arbitrary")),
    )(qk, qk, qk, qkv, qkv, lse, do, dd)


def qk_norm_bwd(qkv, dq, dk, q_norm, k_norm, name):
    T = qkv.shape[0]
    Wd = dq.shape[1]
    tm = _row_tile(T)

    def body(q_ref, k_ref, dq_ref, dk_ref, qn_ref, kn_ref, oq_ref, ok_ref, dqn_ref, dkn_ref):
        first = pl.program_id(0) == 0
        for x_ref, d_ref, g_ref, o_ref, dg_ref, scale in ((q_ref, dq_ref, qn_ref, oq_ref, dqn_ref, ATT_DH ** -0.5),
                                                         (k_ref, dk_ref, kn_ref, ok_ref, dkn_ref, 1.0)):
            dg = jnp.zeros((1, ATT_DH), F32)
            for h in range(Wd // ATT_DH):
                sl = slice(h * ATT_DH, (h + 1) * ATT_DH)
                _, vjp = jax.vjp(lambda x, g: _head_norm(x, g) * scale, x_ref[:, sl].astype(F32), g_ref[...])
                dx, dg_h = vjp(d_ref[:, sl].astype(F32))
                o_ref[:, sl] = dx.astype(BF16)
                dg += dg_h
            _accum(dg_ref, dg, first)

    row = lambda t: pl.BlockSpec((tm, Wd), lambda i: (i, t))
    vec = pl.BlockSpec((1, ATT_DH), lambda i: (0, 0))
    return pl.pallas_call(
        body, name=name, grid=(T // tm,),
        in_specs=[row(0), row(1), row(0), row(0), vec, vec], out_specs=[row(0), row(0), vec, vec],
        out_shape=[_sds((T, Wd), BF16)] * 2 + [_sds((1, ATT_DH), F32)] * 2,
        compiler_params=_cp(("arbitrary",)),
    )(qkv, qkv, dq, dk, q_norm, k_norm)


def _merge_weights(l0, l1, l2):
    m = jnp.maximum(jnp.maximum(l0, l1), l2)
    e = [jnp.exp(l - m) for l in (l0, l1, l2)]
    tot = e[0] + e[1] + e[2]
    return [x / tot for x in e]


def _merge_specs(arrays, tm):
    return [pl.BlockSpec((a.shape[0], tm // a.shape[0], a.shape[2]), lambda i: (0, i, 0)) for a in arrays]


def _merge_load(refs, dils, tm, scr):
    vals = [ref[...].astype(F32).reshape(tm, ref.shape[-1]) for ref in refs]
    return [v if dl == 1 else _interleave(v, scr, dl) for v, dl in zip(vals, dils)]


def merge_fwd(outs, lses, name):
    Wd = outs[0].shape[2]
    T = outs[0].shape[0] * outs[0].shape[1]
    tm = _row_tile(T)
    dils = [a.shape[0] for a in outs] * 2

    def body(*refs):
        vals = _merge_load(refs[:6], dils, tm, refs[7])
        w = _merge_weights(*vals[3:])
        refs[6][...] = (w[0] * vals[0] + w[1] * vals[1] + w[2] * vals[2]).astype(BF16)

    return pl.pallas_call(body, name=name, grid=(T // tm,), in_specs=_merge_specs(list(outs) + list(lses), tm),
                          out_specs=pl.BlockSpec((tm, Wd), lambda i: (i, 0)), out_shape=_sds((T, Wd), BF16),
                          scratch_shapes=[pltpu.VMEM((Wd // LANES, tm, LANES), F32)],
                          compiler_params=_cp(("parallel",)))(*outs, *lses)


def merge_bwd(outs, lses, d, name):
    Wd = outs[0].shape[2]
    T = outs[0].shape[0] * outs[0].shape[1]
    tm = 256 if T % 256 == 0 else T
    dils = [a.shape[0] for a in outs] * 2

    def body(*refs):
        scr = refs[13]
        vals = _merge_load(refs[:6], dils, tm, scr)
        w = _merge_weights(*vals[3:])
        dv = refs[6][...].astype(F32)
        dvm = dv * (w[0] * vals[0] + w[1] * vals[1] + w[2] * vals[2])
        for g in range(3):
            for ref, val in ((refs[7 + g], w[g] * dv), (refs[10 + g], w[g] * dvm)):
                ref[...] = (val if dils[g] == 1 else _deinterleave(val, scr, dils[g])).reshape(ref.shape).astype(BF16)

    specs = _merge_specs(list(outs) + list(lses), tm)
    res = pl.pallas_call(body, name=name, grid=(T // tm,),
                         in_specs=specs + [pl.BlockSpec((tm, Wd), lambda i: (i, 0))], out_specs=specs,
                         out_shape=[_sds(a.shape, BF16) for a in list(outs) + list(outs)],
                         scratch_shapes=[pltpu.VMEM((Wd // LANES, tm, LANES), F32)],
                         compiler_params=_cp(("parallel",)))(*outs, *lses, d)
    return res[:3], res[3:]


def loss_head(y, target, name):
    T, D = y.shape
    tm = _row_tile(T)

    def body(y_ref, t_ref, l_ref, dy_ref):
        err = y_ref[...] - t_ref[...]
        dy_ref[...] = err * (1.0 / D)
        part = 0.5 * jnp.sum(jnp.sum(err * err, axis=1, keepdims=True) * (1.0 / D), axis=0, keepdims=True)
        _accum(l_ref, jnp.broadcast_to(part, (1, 128)), pl.program_id(0) == 0)

    row = pl.BlockSpec((tm, D), lambda i: (i, 0))
    return pl.pallas_call(body, name=name, grid=(T // tm,), in_specs=[row, row],
                          out_specs=[pl.BlockSpec((1, 128), lambda i: (0, 0)), row],
                          out_shape=[_sds((1, 128), F32), _sds((T, D), F32)],
                          compiler_params=_cp(("arbitrary",)))(y, target)


def ada_fwd(c_all, w, b, name):
    depth, D, n = w.shape

    def body(c_ref, w_ref, b_ref, o_ref):
        o_ref[...] = _dot(_silu(c_ref[...]), w_ref[...]) + b_ref[...]

    return pl.pallas_call(
        body, name=name, grid=(depth,),
        in_specs=[pl.BlockSpec((N_DEV, D), lambda i: (0, 0)), pl.BlockSpec((None, D, n), lambda i: (i, 0, 0)),
                  pl.BlockSpec((None, 1, n), lambda i: (i, 0, 0))],
        out_specs=pl.BlockSpec((None, N_DEV, n), lambda i: (i, 0, 0)),
        out_shape=_sds((depth, N_DEV, n), F32), compiler_params=_cp(("parallel",)),
    )(c_all, w, b)


def ada_bwd(c_all, dmod, name):
    depth, _, n = dmod.shape
    D = c_all.shape[1]

    def body(c_ref, d_ref, o_ref):
        o_ref[...] = _dot_tn(_silu(c_ref[...]), d_ref[...])

    return pl.pallas_call(
        body, name=name, grid=(depth,),
        in_specs=[pl.BlockSpec((N_DEV, D), lambda i: (0, 0)), pl.BlockSpec((None, N_DEV, n), lambda i: (i, 0, 0))],
        out_specs=pl.BlockSpec((None, D, n), lambda i: (i, 0, 0)),
        out_shape=_sds((depth, D, n), F32), compiler_params=_cp(("parallel",)),
    )(c_all, dmod)


def adamw(w, m, v, gparts, name):
    R, C = w.shape
    k = gparts.shape[0]
    tr = R
    for cand in (512, 256, 128, 64, 32, 16, 8):
        if R % cand == 0 and cand * C * 4 <= 2 * 1024 * 1024:
            tr = cand
            break
    bc1 = 1.0 - ADAM_B1 ** ADAM_STEP
    bc2 = 1.0 - ADAM_B2 ** ADAM_STEP

    def body(w_ref, m_ref, v_ref, gp_ref, g_ref, d_ref, nm_ref, nv_ref):
        g = gp_ref[0].astype(F32)
        for q in range(1, k):
            g = g + gp_ref[q].astype(F32)
        nm = ADAM_B1 * m_ref[...] + (1.0 - ADAM_B1) * g
        nv = ADAM_B2 * v_ref[...] + (1.0 - ADAM_B2) * (g * g)
        g_ref[...] = g
        nm_ref[...] = nm
        nv_ref[...] = nv
        d_ref[...] = -ADAM_LR * ((nm / bc1) / (jnp.sqrt(nv / bc2) + ADAM_EPS) + ADAM_WD * w_ref[...])

    row = pl.BlockSpec((tr, C), lambda i: (i, 0))
    return pl.pallas_call(
        body, name=name, grid=(R // tr,),
        in_specs=[row, row, row, pl.BlockSpec((k, tr, C), lambda i: (0, i, 0))],
        out_specs=[row] * 4, out_shape=[_sds((R, C), F32)] * 4, compiler_params=_cp(("parallel",)),
    )(w, m, v, gparts)


def _mesh_pos():
    return lax.axis_index("x"), lax.axis_index("y"), lax.axis_index("c")


def _other_chips(x, y):
    return [(1 - x, y), (x, 1 - y), (1 - x, 1 - y)]


_ANY = pl.BlockSpec(memory_space=pl.ANY)


def all_gather(shards, name):
    n = len(shards)

    def body(*refs):
        ins, outs = refs[:n], refs[n:2 * n]
        send_sems, recv_sems, local_sems = refs[2 * n:]
        x, y, c = _mesh_pos()
        me, sibling = (x, y, c), (x, y, 1 - c)
        chips = _other_chips(x, y)

        def slot(p):
            return 4 * p[0] + 2 * p[1] + p[2]

        def copy(a, k, block, to, src=None):
            dst = outs[a].at[slot(block)]
            return pltpu.make_async_remote_copy(
                src_ref=dst if src is None else src, dst_ref=dst, send_sem=send_sems.at[a, k],
                recv_sem=recv_sems.at[a, k], device_id=to, device_id_type=MESH)

        mine = [pltpu.make_async_copy(ins[a], outs[a].at[slot(me)], local_sems.at[a]) for a in range(n)]
        for cp in mine:
            cp.start()
        first = []
        for a in range(n):
            first.append(copy(a, 0, me, sibling, src=ins[a]))
            first += [copy(a, 1 + j, me, (*chip, c), src=ins[a]) for j, chip in enumerate(chips)]
        for cp in first:
            cp.start()
        passed = []
        for j, chip in enumerate(chips):
            for a in range(n):
                copy(a, 1 + j, (*chip, c), me).wait_recv()
                cp = copy(a, 4 + j, (*chip, c), sibling)
                cp.start()
                passed.append(cp)
        for a in range(n):
            copy(a, 0, sibling, me).wait_recv()
            for j, chip in enumerate(chips):
                copy(a, 4 + j, (*chip, 1 - c), me).wait_recv()
        for cp in first + passed:
            cp.wait_send()
        for cp in mine:
            cp.wait()

    return pl.pallas_call(
        body, name=name, in_specs=[_ANY] * n, out_specs=[_ANY] * n,
        out_shape=[_sds((N_DEV,) + s.shape, s.dtype) for s in shards],
        scratch_shapes=[pltpu.SemaphoreType.DMA((n, 7)), pltpu.SemaphoreType.DMA((n, 7)), pltpu.SemaphoreType.DMA((n,))],
        compiler_params=pltpu.CompilerParams(has_side_effects=True),
    )(*shards)


def exchange_pair(stacked, name):
    n = len(stacked)

    def body(*refs):
        ins, outs = refs[:n], refs[n:2 * n]
        send_sems, recv_sems = refs[2 * n:]
        x, y, c = _mesh_pos()
        copies = [pltpu.make_async_remote_copy(
            src_ref=ins[a].at[2 * q + (1 - c)], dst_ref=outs[a].at[q], send_sem=send_sems.at[a, q],
            recv_sem=recv_sems.at[a, q], device_id=(x, y, 1 - c), device_id_type=MESH)
            for a in range(n) for q in range(4)]
        for cp in copies:
            cp.start()
        for cp in copies:
            cp.wait()

    return pl.pallas_call(
        body, name=name, in_specs=[_ANY] * n, out_specs=[_ANY] * n,
        out_shape=[_sds((4,) + s.shape[1:], s.dtype) for s in stacked],
        scratch_shapes=[pltpu.SemaphoreType.DMA((n, 4)), pltpu.SemaphoreType.DMA((n, 4))],
        compiler_params=pltpu.CompilerParams(has_side_effects=True),
    )(*stacked)


def pair_add(stacked, got, c_idx, name):
    _, R, C = stacked.shape
    tr = R
    for cand in (512, 256, 128, 64, 32, 16):
        if R % cand == 0 and cand * C * 4 <= 2 * 1024 * 1024:
            tr = cand
            break

    def body(c_ref, s_ref, g_ref, o_ref):
        o_ref[...] = (s_ref[...] + g_ref[...]).astype(BF16)

    return pl.pallas_call(
        body, name=name,
        grid_spec=pltpu.PrefetchScalarGridSpec(
            num_scalar_prefetch=1, grid=(4, R // tr),
            in_specs=[pl.BlockSpec((None, tr, C), lambda q, i, c_ref: (2 * q + c_ref[0], i, 0)),
                      pl.BlockSpec((None, tr, C), lambda q, i, c_ref: (q, i, 0))],
            out_specs=pl.BlockSpec((None, tr, C), lambda q, i, c_ref: (q, i, 0))),
        out_shape=_sds((4, R, C), BF16),
        compiler_params=_cp(("parallel", "parallel")),
    )(c_idx, stacked, got)


def exchange_chips(parts, name):
    n = len(parts)

    def body(*refs):
        ins, outs = refs[:n], refs[n:2 * n]
        send_sems, recv_sems, local_sems = refs[2 * n:]
        x, y, c = _mesh_pos()
        myq = 2 * x + y
        mine = [pltpu.make_async_copy(ins[a].at[myq], outs[a].at[myq], local_sems.at[a]) for a in range(n)]
        for cp in mine:
            cp.start()
        copies = [pltpu.make_async_remote_copy(
            src_ref=ins[a].at[2 * chip[0] + chip[1]], dst_ref=outs[a].at[myq], send_sem=send_sems.at[a, j],
            recv_sem=recv_sems.at[a, j], device_id=(*chip, c), device_id_type=MESH)
            for a in range(n) for j, chip in enumerate(_other_chips(x, y))]
        for cp in copies:
            cp.start()
        for cp in copies:
            cp.wait()
        for cp in mine:
            cp.wait()

    return pl.pallas_call(
        body, name=name, in_specs=[_ANY] * n, out_specs=[_ANY] * n,
        out_shape=[_sds(s.shape, s.dtype) for s in parts],
        scratch_shapes=[pltpu.SemaphoreType.DMA((n, 3)), pltpu.SemaphoreType.DMA((n, 3)), pltpu.SemaphoreType.DMA((n,))],
        compiler_params=pltpu.CompilerParams(has_side_effects=True),
    )(*parts)


def _pad_lanes(v, start, width=128):
    return jnp.pad(v.astype(F32), (start, width - start - v.shape[0]))[None]


def _layer_fwd(i, x, mod, P):
    T, D = x.shape
    sh_m, sc_m, g_m, sh_f, sc_f, g_f = [mod[k:k + 1] for k in range(6)]
    sv = {"x0": x}
    tag = f"l{i}"
    if i % 2 == 0:
        e = i // 2
        h, proj = norm_proj(x, P["norm_mix"][i:i + 1], sc_m, sh_m, P["ev_main"][e], P["ev_main"].shape[2], tag + "_in")
        ba = matmul_nn(h, P["ev_ba"][e], F32, tag + "_ba")
        al = _pad_lanes(P["gdn_a_log"][e], GDN_HEADS)
        dt = _pad_lanes(P["gdn_dt_bias"][e], GDN_HEADS)
        gate = gate_fwd(ba, al, dt, tag + "_gate")
        qkv = gdn_prep_fwd(proj, P["gdn_conv_w"][e], tag + "_prep")
        chunks = gdn_prep_chunks_fwd(qkv, gate, tag + "_chunks")
        o, states = gdn_fwd(*chunks, gate, tag + "_gdn")
        cat = even_post_fwd(o, proj, P["gdn_norm"][e:e + 1], P["pool_w"][e], P["pool_scale"][e:e + 1], tag + "_post")
        x1, y_m = proj_res(cat, P["ev_w_out"][e], x, g_m, tag + "_out")
        sv.update(h=h, proj=proj, ba=ba, al=al, dt=dt, gate=gate, qkv=qkv, chunks=chunks, o=o, states=states, a_m=cat)
    else:
        od = i // 2
        qn, kn = P["att_q_norm"][od:od + 1], P["att_k_norm"][od:od + 1]
        h, proj, qk = [], [], []
        for gi, (_, dil) in enumerate(DIL_PATTERNS):
            hg, pg, qkg = norm_proj(x, P["norm_mix"][i:i + 1], sc_m, sh_m, P["od_w_in"][od], 3 * D, f"{tag}_in{gi}",
                                    cols=(gi, 1), dil=dil, qk_norms=(qn, kn))
            h.append(hg.reshape(T, D))
            proj.append(pg)
            qk.append(qkg)
        res = [att_fwd(qk[gi], proj[gi], f"{tag}_att{gi}") for gi in range(len(DIL_PATTERNS))]
        outs, lses = [r[0] for r in res], [r[1] for r in res]
        merged = merge_fwd(outs, lses, tag + "_merge")
        x1, y_m = proj_res(merged, P["od_w_out"][od], x, g_m, tag + "_out")
        sv.update(h=h, proj=proj, qk=qk, outs=outs, lses=lses, a_m=merged)
    hf, up = norm_proj(x1, P["norm_ffn"][i:i + 1], sc_f, sh_f, P["ffn_w_up"][i], 1408, tag + "_up")
    a = ffn_mid_fwd(up, P["ffn_conv_w"][i], P["ffn_conv_b"][i:i + 1], tag + "_mid")
    x2, y_f = proj_res(a, P["ffn_w_down"][i], x1, g_f, tag + "_down")
    sv.update(y_m=y_m, x1=x1, hf=hf, up=up, a_f=a, y_f=y_f)
    return x2, sv


def _layer_bwd(i, dx2, sv, mod, P, bufs):
    T = dx2.shape[0]
    sh_m, sc_m, g_m, sh_f, sc_f, g_f = [mod[k:k + 1] for k in range(6)]
    tag = f"b{i}"
    G = {}
    F = P["ffn_w_down"].shape[1]
    dy, da, dg_f = bwd_out(dx2, g_f, sv["y_f"], P["ffn_w_down"][i], F, tag + "_down")
    bufs["ffn_w_down"] = matmul_tn(sv["a_f"], dy, F // 2, dy.shape[1], tag + "_wdown", into=bufs["ffn_w_down"], layer=i)
    dgate, dval, dcw, dcb = ffn_mid_bwd(sv["up"], da, P["ffn_conv_w"][i], P["ffn_conv_b"][i:i + 1], tag + "_mid")
    G["ffn_conv_w"], G["ffn_conv_b"] = dcw[:FFN_CONV], dcb[0]
    dx1, dnf, dsc_f, dsh_f = bwd_in([dgate, dval], P["ffn_w_up"][i], [0, 1], F, None,
                                    sv["x1"], P["norm_ffn"][i:i + 1], sc_f, sh_f, dx2, tag + "_up")
    G["norm_ffn"] = dnf[0]
    x0, h, proj = sv["x0"], sv["h"], sv["proj"]
    D = x0.shape[1]
    for k, d in enumerate((dgate, dval)):
        bufs["ffn_w_up"] = matmul_tn(sv["hf"], d, D, F // 2, f"{tag}_wup{k}", into=bufs["ffn_w_up"], layer=i, col0=2 * k)
    gain = P["norm_mix"][i:i + 1]
    if i % 2 == 0:
        e = i // 2
        W = GDN_HEADS * GDN_D
        dy, dcat, dg_m = bwd_out(dx1, g_m, sv["y_m"], P["ev_w_out"][e], 1024, tag + "_out")
        bufs["ev_w_out"] = matmul_tn(sv["a_m"], dy, 1024, D, tag + "_wout", into=bufs["ev_w_out"], layer=e)
        do, dzp, dgn, dpw, dps = even_post_bwd(sv["o"], proj, P["gdn_norm"][e:e + 1], P["pool_w"][e],
                                               P["pool_scale"][e:e + 1], dcat, tag + "_post")
        G["gdn_norm"], G["pool_w"], G["pool_scale"] = dgn[0], dpw, dps[0]
        *dchunks, dgate_scan = gdn_bwd(*sv["chunks"], sv["gate"], sv["states"], do, tag + "_gdn")
        dqkv, dgate_prep = gdn_prep_chunks_bwd(sv["qkv"], sv["gate"], *dchunks, tag + "_chunks")
        dba, dal, ddt = gate_bwd(sv["ba"], sv["al"], sv["dt"], dgate_prep, dgate_scan, tag + "_gate")
        G["gdn_a_log"], G["gdn_dt_bias"] = dal[0, GDN_HEADS:2 * GDN_HEADS], ddt[0, GDN_HEADS:2 * GDN_HEADS]
        dqkv_raw, dconv = gdn_prep_bwd(proj, dqkv, P["gdn_conv_w"][e], tag + "_prep")
        G["gdn_conv_w"] = dconv[:GDN_CONV]
        w_main = P["ev_main"][e]
        dx0, dnm, dsc_m, dsh_m = bwd_in([dba, dqkv_raw, dzp], [P["ev_ba"][e], w_main[:, :3 * W], w_main[:, 3 * W:]],
                                        [0, 0, 0], None, None, x0, gain, sc_m, sh_m, dx1, tag + "_in")
        gw_qkv = matmul_tn(h, dqkv_raw, 1024, W, tag + "_win1")
        gw_zp = matmul_tn(h, dzp, 1024, W, tag + "_win2")
        gw_ba = matmul_tn(h, dba, 1024, 128, tag + "_win0")
        G["ev_w_in"] = jnp.concatenate([gw_qkv, gw_zp[:, :W], gw_ba[:, :2 * GDN_HEADS], gw_zp[:, W:]], axis=1)
    else:
        od = i // 2
        qn, kn = P["att_q_norm"][od:od + 1], P["att_k_norm"][od:od + 1]
        dy, dmerged, dg_m = bwd_out(dx1, g_m, sv["y_m"], P["od_w_out"][od], 1024, tag + "_out")
        bufs["od_w_out"] = matmul_tn(sv["a_m"], dy, 1024, D, tag + "_wout", into=bufs["od_w_out"], layer=od)
        douts, dds = merge_bwd(sv["outs"], sv["lses"], dmerged, tag + "_merge")
        dh = None
        dqn_sum, dkn_sum = 0.0, 0.0
        ng = len(DIL_PATTERNS)
        for gi, (_, dil) in enumerate(DIL_PATTERNS):
            dqs, dks, dv = att_bwd(sv["qk"][gi], proj[gi], sv["lses"][gi], douts[gi], dds[gi], f"{tag}_att{gi}")
            dq, dk, dqn, dkn = qk_norm_bwd(proj[gi].reshape(T, 3 * D), dqs.reshape(T, D), dks.reshape(T, D), qn, kn,
                                           f"{tag}_qkn{gi}")
            dqn_sum, dkn_sum = dqn_sum + dqn[0], dkn_sum + dkn[0]
            for k, d in enumerate((dq, dk, dv)):
                bufs["od_w_in"] = matmul_tn(h[gi], d.reshape(T, D), 1024, D, f"{tag}_win{gi}{k}", into=bufs["od_w_in"],
                                            layer=od, col0=3 * gi + k)
            cols = [3 * gi, 3 * gi + 1, 3 * gi + 2]
            ops = [d.reshape(dil, T // dil, D) if dil > 1 else d.reshape(T, D) for d in (dq, dk, dv)]
            if gi < ng - 1:
                dh = bwd_in(ops, P["od_w_in"][od], cols, D, dh, None, None, None, None, None, f"{tag}_in{gi}", dil=dil)
            else:
                dx0, dnm, dsc_m, dsh_m = bwd_in(ops, P["od_w_in"][od], cols, D, dh, x0, gain, sc_m, sh_m, dx1,
                                                f"{tag}_in{gi}", dil=dil)
        G["att_q_norm"], G["att_k_norm"] = dqn_sum, dkn_sum
    G["norm_mix"] = dnm[0]
    dmod = jnp.concatenate([dsh_m, dsc_m, dg_m, dsh_f, dsc_f, dg_f], axis=0)
    return dx0, dmod, G


_PER_LAYER = ("norm_mix", "norm_ffn", "ffn_conv_w", "ffn_conv_b")
_PER_EVEN = ("ev_w_in", "gdn_conv_w", "gdn_a_log", "gdn_dt_bias", "gdn_norm", "pool_w", "pool_scale")
_PER_ODD = ("att_q_norm", "att_k_norm")
_IN_PLACE = ("ffn_w_up", "ffn_w_down", "ev_w_out", "od_w_in", "od_w_out")


def device_step(x, mod, target, P):
    saved = []
    for i in range(DEPTH):
        x, sv = _layer_fwd(i, x, mod[i], P)
        saved.append(sv)
    loss, dx = loss_head(x, target, "loss")
    layer_grads, dmods = [None] * DEPTH, [None] * DEPTH
    bufs = {k: lax.empty(P[k].shape, F32) for k in _IN_PLACE}
    for i in reversed(range(DEPTH)):
        dx, dmods[i], layer_grads[i] = _layer_bwd(i, dx, saved[i], mod[i], P, bufs)
    G = dict(bufs)
    G.update({k: jnp.stack([layer_grads[i][k] for i in range(DEPTH)]) for k in _PER_LAYER})
    G.update({k: jnp.stack([layer_grads[i][k] for i in range(0, DEPTH, 2)]) for k in _PER_EVEN})
    G.update({k: jnp.stack([layer_grads[i][k] for i in range(1, DEPTH, 2)]) for k in _PER_ODD})
    return loss, dx, jnp.stack(dmods), G


_WEIGHTS = ("ada_w", "ada_b", "norm_mix", "norm_ffn", "ev_w_in", "ev_w_out", "gdn_conv_w", "gdn_a_log", "gdn_dt_bias",
            "gdn_norm", "pool_w", "pool_scale", "od_w_in", "od_w_out", "att_q_norm", "att_k_norm", "ffn_w_up",
            "ffn_conv_w", "ffn_conv_b", "ffn_w_down")
_COL_SHARDED = ("ev_w_in", "od_w_in", "ffn_w_up")
_ROW_SHARDED = ("ev_w_out", "od_w_out", "ffn_w_down")
_SMALL_SHARDED = ("gdn_conv_w", "ffn_conv_w")
_REPLICATED = ("ada_b", "norm_mix", "norm_ffn", "gdn_a_log", "gdn_dt_bias", "gdn_norm", "pool_w", "pool_scale",
               "att_q_norm", "att_k_norm", "ffn_conv_b")
PACK_LANES = 128
PACK_ROWS = 8


def _pack(arrays):
    flat = jnp.concatenate([a.reshape(-1).astype(F32) for a in arrays])
    unit = PACK_LANES * PACK_ROWS
    padded = -(-flat.shape[0] // unit) * unit
    return jnp.pad(flat, (0, padded - flat.shape[0])).reshape(-1, PACK_LANES)


def _unpack(packed, shapes, lead=()):
    flat = packed.reshape(lead + (-1,))
    out, pos = [], 0
    for s in shapes:
        n = math.prod(s)
        out.append(flat[..., pos:pos + n].reshape(lead + tuple(s)))
        pos += n
    return out


def _unshard_cols(g):
    _, L, R, n = g.shape
    return g.transpose(1, 2, 0, 3).reshape(L, R, N_DEV * n)


def _shard_cols(full):
    L, R, N = full.shape
    n = N // N_DEV
    return full.reshape(L * R, N_DEV, n).transpose(1, 0, 2)


def _shard_rows(full):
    L, R, C = full.shape
    r = R // N_DEV
    return full.reshape(L, N_DEV, r, C).transpose(1, 0, 2, 3).reshape(N_DEV, L * r, C)


def kernel(x, c, ada_w, ada_b, norm_mix, norm_ffn, ev_w_in, ev_w_out, gdn_conv_w, gdn_a_log, gdn_dt_bias, gdn_norm, pool_w, pool_scale, od_w_in, od_w_out, att_q_norm, att_k_norm, ffn_w_up, ffn_conv_w, ffn_conv_b, ffn_w_down, loss_target, m_ada_w, m_ada_b, m_norm_mix, m_norm_ffn, m_ev_w_in, m_ev_w_out, m_gdn_conv_w, m_gdn_a_log, m_gdn_dt_bias, m_gdn_norm, m_pool_w, m_pool_scale, m_od_w_in, m_od_w_out, m_att_q_norm, m_att_k_norm, m_ffn_w_up, m_ffn_conv_w, m_ffn_conv_b, m_ffn_w_down, v_ada_w, v_ada_b, v_norm_mix, v_norm_ffn, v_ev_w_in, v_ev_w_out, v_gdn_conv_w, v_gdn_a_log, v_gdn_dt_bias, v_gdn_norm, v_pool_w, v_pool_scale, v_od_w_in, v_od_w_out, v_att_q_norm, v_att_k_norm, v_ffn_w_up, v_ffn_conv_w, v_ffn_conv_b, v_ffn_w_down):
    args = locals()
    Wl = {k: args[k] for k in _WEIGHTS}
    Ml = {k: args["m_" + k] for k in _WEIGHTS}
    Vl = {k: args["v_" + k] for k in _WEIGHTS}
    mx, my, mc = _mesh_pos()
    dev = 4 * mx + 2 * my + mc
    T, D = x.shape[1], x.shape[2]
    x2d, tgt = x.reshape(T, D), loss_target.reshape(T, D)

    small_shapes = [c.shape] + [Wl[k].shape for k in _SMALL_SHARDED]
    big = list(_COL_SHARDED + _ROW_SHARDED)
    gathered = all_gather([_pack([c] + [Wl[k] for k in _SMALL_SHARDED])] + [Wl[k].astype(BF16) for k in big], "gather_w")
    c_all, conv_g, conv_f = _unpack(gathered[0], small_shapes, lead=(N_DEV,))
    c_all = c_all.reshape(N_DEV, D)
    full = dict(zip(big, gathered[1:]))
    P = {k: Wl[k] for k in _REPLICATED}
    P["gdn_conv_w"], P["ffn_conv_w"] = _unshard_cols(conv_g), _unshard_cols(conv_f)
    for k in _COL_SHARDED:
        P[k] = _unshard_cols(full[k])
    for k in _ROW_SHARDED:
        g = full[k]
        P[k] = g.transpose(1, 0, 2, 3).reshape(g.shape[1], N_DEV * g.shape[2], g.shape[3])
    W = GDN_HEADS * GDN_D
    ev = P.pop("ev_w_in")
    P["ev_main"] = jnp.concatenate([ev[:, :, :4 * W], ev[:, :, 4 * W + 2 * GDN_HEADS:]], axis=2)
    P["ev_ba"] = jnp.pad(ev[:, :, 4 * W:4 * W + 2 * GDN_HEADS], ((0, 0), (0, 0), (0, 128 - 2 * GDN_HEADS)))

    n_ada = ada_w.shape[2]
    b_cols = lax.dynamic_slice_in_dim(ada_b, dev * n_ada, n_ada, axis=1)
    mod_cols = ada_fwd(c_all, ada_w, b_cols[:, None, :], "ada_fwd")
    mod_all, = all_gather([mod_cols], "gather_mod")
    mod = lax.dynamic_index_in_dim(mod_all, dev, axis=2, keepdims=False)
    mod = mod.transpose(1, 0, 2).reshape(DEPTH, 6, D)

    loss, dx, dmod, G = device_step(x2d, mod, tgt, P)
    loss = lax.psum(loss[0, 0], ("x", "y", "c"))

    G["ada_b"] = dmod.reshape(DEPTH, 6 * D)
    small = list(_REPLICATED) + list(_SMALL_SHARDED)
    parts_all, = all_gather([_pack([G[k] for k in small])], "gather_small")
    zeros = {k: jnp.zeros_like(G[k]) for k in _SMALL_SHARDED}
    packs = [_pack([src[k] for k in _REPLICATED] + [zeros[k] for k in _SMALL_SHARDED]) for src in (Wl, Ml, Vl)]
    res = adamw(*packs, parts_all, "adamw_small")
    shapes = [G[k].shape for k in small]
    out_g, out_d, out_m, out_v = ({k: a for k, a in zip(small, _unpack(r, shapes))} for r in res)
    dmod_all = _unpack(parts_all, shapes, lead=(N_DEV,))[0].reshape(N_DEV, DEPTH, 6 * D)
    dm_cols = lax.dynamic_slice_in_dim(dmod_all, dev * n_ada, n_ada, axis=2).transpose(1, 0, 2)
    g_ada = ada_bwd(c_all, dm_cols, "ada_bwd")

    def flat2(a):
        return a.reshape(-1, a.shape[-1])

    loc = {k: lax.dynamic_slice_in_dim(out_g[k], dev * Wl[k].shape[-1], Wl[k].shape[-1], axis=out_g[k].ndim - 1)
           for k in _SMALL_SHARDED}
    res = adamw(*[_pack([src[k] for k in _SMALL_SHARDED]) for src in (Wl, Ml, Vl)],
                _pack([loc[k] for k in _SMALL_SHARDED])[None], "adamw_conv")
    for dst, r in zip((out_g, out_d, out_m, out_v), res):
        dst.update(zip(_SMALL_SHARDED, _unpack(r, [Wl[k].shape for k in _SMALL_SHARDED])))
    res = adamw(flat2(ada_w), flat2(m_ada_w), flat2(v_ada_w), flat2(g_ada)[None], "adamw_ada")
    for dst, r in zip((out_g, out_d, out_m, out_v), res):
        dst["ada_w"] = r.reshape(ada_w.shape)

    stacked = [_shard_cols(G[k]) for k in _COL_SHARDED] + [_shard_rows(G[k]) for k in _ROW_SHARDED]
    got = exchange_pair(stacked, "rs_pair")
    c_idx = jnp.reshape(mc, (1,)).astype(jnp.int32)
    chip_parts = [pair_add(s, g, c_idx, f"rs_add_{k}") for s, g, k in zip(stacked, got, big)]
    parts = exchange_chips(chip_parts, "rs_chips")
    for k, p in zip(big, parts):
        res = adamw(flat2(Wl[k]), flat2(Ml[k]), flat2(Vl[k]), p, "adamw_" + k)
        for dst, r in zip((out_g, out_d, out_m, out_v), res):
            dst[k] = r.reshape(Wl[k].shape)

    return (loss, dx.reshape(x.shape), *[out_g[k] for k in _WEIGHTS], *[out_d[k] for k in _WEIGHTS],
            *[out_m[k] for k in _WEIGHTS], *[out_v[k] for k in _WEIGHTS])
```

```python
import functools
import math

import jax
import jax.numpy as jnp
from jax import lax
from jax.experimental import pallas as pl
from jax.experimental.pallas import tpu as pltpu

F32 = jnp.float32
BF16 = jnp.bfloat16
HI = lax.Precision.HIGHEST
MESH = pl.DeviceIdType.MESH

N_DEV = 8
RMS_EPS = 1e-6
DEPTH = 4
GDN_HEADS = 4
GDN_D = 128
GDN_CHUNK = 64
GDN_CONV = 4
POOL_WINDOWS = (2, 4, 8, 16)
DIL_PATTERNS = ((128, 1), (512, 4), (2048, 16))
ATT_HEADS = 8
ATT_DH = 128
ATT_BLOCK = 128
FFN_CONV = 3
ADAM_LR, ADAM_B1, ADAM_B2, ADAM_EPS, ADAM_WD, ADAM_STEP = 0.001, 0.9, 0.999, 1e-08, 0.01, 10

HALO = 16
NEG = -1e30
VMEM_LIMIT_BYTES = 56 * 1024 * 1024


def _cp(sem=None, **kw):
    return pltpu.CompilerParams(dimension_semantics=sem, vmem_limit_bytes=VMEM_LIMIT_BYTES, **kw)


def _sds(shape, dtype):
    return jax.ShapeDtypeStruct(tuple(shape), dtype)


def _dot(a, b):
    return jnp.dot(a.astype(BF16), b.astype(BF16), preferred_element_type=F32)


def _dot_nt(a, b):
    return lax.dot_general(a.astype(BF16), b.astype(BF16), (((1,), (1,)), ((), ())), preferred_element_type=F32)


def _dot_tn(a, b):
    return lax.dot_general(a.astype(BF16), b.astype(BF16), (((0,), (0,)), ((), ())), preferred_element_type=F32)


def _dot_hi(a, b):
    return jnp.dot(a, b, preferred_element_type=F32, precision=HI)


def _silu(x):
    return x * jax.nn.sigmoid(x)


def _modnorm(x, gain, sc, sh):
    y = x * lax.rsqrt(jnp.mean(x * x, axis=-1, keepdims=True) + RMS_EPS)
    return y * gain * (1.0 + sc) + sh


def _row_tile(T):
    return 512 if T % 512 == 0 else T


LANES = 128


def _deinterleave(val, scr, dil):
    tm, width = val.shape
    sub = tm // dil
    ncb = width // LANES
    for cb in range(ncb):
        scr[cb] = val[:, cb * LANES:(cb + 1) * LANES]
    return jnp.concatenate([jnp.concatenate([scr.at[cb][pl.ds(r, sub, stride=dil), :] for cb in range(ncb)], axis=1)
                            for r in range(dil)], axis=0)


def _interleave(val, scr, dil):
    tm, width = val.shape
    sub = tm // dil
    ncb = width // LANES
    for r in range(dil):
        for cb in range(ncb):
            scr.at[cb][pl.ds(r, sub, stride=dil), :] = val[r * sub:(r + 1) * sub, cb * LANES:(cb + 1) * LANES]
    return jnp.concatenate([scr[cb] for cb in range(ncb)], axis=1)


def norm_proj(x, gain, sc, sh, w, tn, name, cols=None, dil=1, qk_norms=None, gather=None):
    T, D = x.shape
    c0, ncol = cols if cols is not None else (0, w.shape[1] // tn)
    N = ncol * tn
    tm = 1024 if (ncol > 1 and dil == 1 and T % 1024 == 0) else _row_tile(T)
    sub = tm // dil
    n_g = len(gather) if gather else 0
    n_own_in = 5 + (2 if qk_norms is not None else 0)
    n_own_out = 2 + (1 if qk_norms is not None else 0)
    n_in, n_out = n_own_in + n_g, n_own_out + n_g
    n_scr = 1 + (1 if dil > 1 else 0)
    nsteps = (T // tm) * ncol
    Wd = tn // 3

    def body(*refs):
        x_ref, g_ref, sc_ref, sh_ref, w_ref = refs[:5]
        h_ref, o_ref = refs[n_in:n_in + 2]
        h_scr = refs[n_in + n_out]
        step = pl.program_id(0) * ncol + pl.program_id(1)
        if n_g:
            exch = _Gather(refs[n_own_in:n_in], refs[n_in + n_own_out:n_in + n_out], *refs[n_in + n_out + n_scr:])
            pl.when(step == 0)(exch.start)
            pl.when(step == nsteps // 2)(exch.forward)

        @pl.when(pl.program_id(1) == 0)
        def _():
            h = _modnorm(x_ref[...], g_ref[...], sc_ref[...], sh_ref[...])
            if dil > 1:
                h = _deinterleave(h, refs[n_in + n_out + 1], dil)
            h_scr[...] = h.astype(BF16)
            h_ref[...] = h.reshape(h_ref.shape).astype(BF16)
        res = jnp.dot(h_scr[...], w_ref[...], preferred_element_type=F32)
        o_ref[...] = res.reshape(o_ref.shape).astype(BF16)
        if qk_norms is not None:
            qk_ref = refs[n_in + 2]
            parts = []
            for t, scale in enumerate((ATT_DH ** -0.5, 1.0)):
                for hd in range(Wd // ATT_DH):
                    c = t * Wd + hd * ATT_DH
                    parts.append(_head_norm(res[:, c:c + ATT_DH], refs[5 + t][...]) * scale)
            qk_ref[...] = jnp.concatenate(parts, axis=1).reshape(qk_ref.shape).astype(BF16)
        if n_g:
            pl.when(step == nsteps - 1)(exch.finish)

    vec = pl.BlockSpec((1, D), lambda i, j: (0, 0))
    in_specs = [pl.BlockSpec((tm, D), lambda i, j: (i, 0)), vec, vec, vec, pl.BlockSpec((D, tn), lambda i, j: (0, c0 + j))]
    args = [x, gain, sc, sh, w]
    out_specs = [pl.BlockSpec((dil, sub, D), lambda i, j: (0, i, 0)), pl.BlockSpec((dil, sub, tn), lambda i, j: (0, i, j))]
    out_shape = [_sds((dil, T // dil, D), BF16), _sds((dil, T // dil, N), BF16)]
    if qk_norms is not None:
        assert ncol == 1
        in_specs += [pl.BlockSpec((1, ATT_DH), lambda i, j: (0, 0))] * 2
        args += list(qk_norms)
        out_specs.append(pl.BlockSpec((dil, sub, 2 * Wd), lambda i, j: (0, i, 0)))
        out_shape.append(_sds((dil, T // dil, 2 * Wd), BF16))
    scratch = [pltpu.VMEM((tm, D), BF16)] + ([pltpu.VMEM((D // LANES, tm, LANES), F32)] if dil > 1 else [])
    if n_g:
        in_specs += [_ANY] * n_g
        args += list(gather)
        out_specs += [_ANY] * n_g
        out_shape += [_sds((N_DEV,) + g.shape, g.dtype) for g in gather]
        scratch += _gather_scratch(n_g)
    res = pl.pallas_call(
        body, name=name, grid=(T // tm, ncol), in_specs=in_specs, out_specs=out_specs, out_shape=out_shape,
        scratch_shapes=scratch, compiler_params=_cp(("arbitrary", "arbitrary") if n_g else ("parallel", "arbitrary")),
    )(*args)
    if dil == 1 and qk_norms is None:
        return [r.reshape(T, -1) for r in res[:2]] + list(res[2:])
    return res


def matmul_nn(a, w, out_dtype, name):
    T, K = a.shape
    N = w.shape[1]
    tm = _row_tile(T)

    def body(a_ref, w_ref, o_ref):
        o_ref[...] = _dot(a_ref[...], w_ref[...]).astype(o_ref.dtype)

    return pl.pallas_call(
        body, name=name, grid=(T // tm,),
        in_specs=[pl.BlockSpec((tm, K), lambda i: (i, 0)), pl.BlockSpec((K, N), lambda i: (0, 0))],
        out_specs=pl.BlockSpec((tm, N), lambda i: (i, 0)),
        out_shape=_sds((T, N), out_dtype),
        compiler_params=_cp(("parallel",)),
    )(a, w)


def proj_res(a, w, x, gate, name):
    T, K = a.shape
    D = w.shape[1]
    tm = _row_tile(T)

    def body(a_ref, w_ref, x_ref, g_ref, o_ref, y_ref):
        y = jnp.dot(a_ref[...], w_ref[...], preferred_element_type=F32)
        y_ref[...] = y.astype(BF16)
        o_ref[...] = x_ref[...] + g_ref[...] * y

    return pl.pallas_call(
        body, name=name, grid=(T // tm,),
        in_specs=[pl.BlockSpec((tm, K), lambda i: (i, 0)), pl.BlockSpec((K, D), lambda i: (0, 0)),
                  pl.BlockSpec((tm, D), lambda i: (i, 0)), pl.BlockSpec((1, D), lambda i: (0, 0))],
        out_specs=[pl.BlockSpec((tm, D), lambda i: (i, 0)), pl.BlockSpec((tm, D), lambda i: (i, 0))],
        out_shape=[_sds((T, D), F32), _sds((T, D), BF16)],
        compiler_params=_cp(("parallel",)),
    )(a, w, x, gate)


def bwd_out(dx, gate, y, w, tk, name):
    T, D = dx.shape
    K = w.shape[0]
    tm = _row_tile(T)

    def body(dx_ref, g_ref, y_ref, w_ref, dy_ref, da_ref, dg_ref, dy_scr):
        i, j = pl.program_id(0), pl.program_id(1)

        @pl.when(j == 0)
        def _():
            dxv = dx_ref[...]
            dy = (dxv * g_ref[...]).astype(BF16)
            dy_scr[...] = dy
            dy_ref[...] = dy
            part = jnp.sum(dxv * y_ref[...].astype(F32), axis=0, keepdims=True)

            @pl.when(i == 0)
            def _():
                dg_ref[...] = part

            @pl.when(i > 0)
            def _():
                dg_ref[...] += part

        da_ref[...] = _dot_nt(dy_scr[...], w_ref[...]).astype(BF16)

    return pl.pallas_call(
        body, name=name, grid=(T // tm, K // tk),
        in_specs=[pl.BlockSpec((tm, D), lambda i, j: (i, 0)), pl.BlockSpec((1, D), lambda i, j: (0, 0)),
                  pl.BlockSpec((tm, D), lambda i, j: (i, 0)), pl.BlockSpec((tk, D), lambda i, j: (j, 0))],
        out_specs=[pl.BlockSpec((tm, D), lambda i, j: (i, 0)), pl.BlockSpec((tm, tk), lambda i, j: (i, j)),
                   pl.BlockSpec((1, D), lambda i, j: (0, 0))],
        out_shape=[_sds((T, D), BF16), _sds((T, K), BF16), _sds((1, D), F32)],
        scratch_shapes=[pltpu.VMEM((tm, D), BF16)],
        compiler_params=_cp(("arbitrary", "arbitrary")),
    )(dx, gate, y, w)


def bwd_in(a_list, w, col_blocks, tn, acc, x, gain, sc, sh, dx_res, name, dil=1):
    T = math.prod(a_list[0].shape[:-1])
    n_a = len(a_list)
    w_list = list(w) if isinstance(w, (list, tuple)) else [w] * n_a
    D = w_list[0].shape[0]
    tm = 256 if T % 256 == 0 else T
    sub = tm // dil
    tns = [tn if tn is not None else a.shape[-1] for a in a_list]
    nsteps = a_list[0].shape[-1] // tns[0]
    assert all(a.shape[-1] == nsteps * t for a, t in zip(a_list, tns))
    final = x is not None
    has_acc = acc is not None

    def body(*refs):
        a_refs = refs[:n_a]
        w_refs = refs[n_a:2 * n_a]
        pos = 2 * n_a
        acc_ref = refs[pos] if has_acc else None
        pos += int(has_acc)
        if final:
            x_ref, g_ref, sc_ref, sh_ref, dxr_ref = refs[pos:pos + 5]
            pos += 5
            dx_ref, dg_ref, dsc_ref, dsh_ref = refs[pos:pos + 4]
            pos += 4
        else:
            dh_ref = refs[pos]
            pos += 1
        acc_scr = refs[pos]
        i, j = pl.program_id(0), pl.program_id(1)
        part = _dot_nt(a_refs[0][...].reshape(tm, tns[0]), w_refs[0][...])
        for k in range(1, n_a):
            part += _dot_nt(a_refs[k][...].reshape(tm, tns[k]), w_refs[k][...])

        @pl.when(j == 0)
        def _():
            acc_scr[...] = part

        @pl.when(j > 0)
        def _():
            acc_scr[...] += part

        @pl.when(j == nsteps - 1)
        def _():
            dh = acc_scr[...]
            if dil > 1:
                dh = _interleave(dh, refs[pos + 1], dil)
            if has_acc:
                dh = dh + acc_ref[...]
            if not final:
                dh_ref[...] = dh
                return
            _, vjp = jax.vjp(_modnorm, x_ref[...], g_ref[...], sc_ref[...], sh_ref[...])
            dxn, dg, dsc, dsh = vjp(dh)
            dx_ref[...] = dxr_ref[...] + dxn

            @pl.when(i == 0)
            def _():
                dg_ref[...] = dg
                dsc_ref[...] = dsc
                dsh_ref[...] = dsh

            @pl.when(i > 0)
            def _():
                dg_ref[...] += dg
                dsc_ref[...] += dsc
                dsh_ref[...] += dsh

    row = pl.BlockSpec((tm, D), lambda i, j: (i, 0))
    vec = pl.BlockSpec((1, D), lambda i, j: (0, 0))
    if dil == 1:
        in_specs = [pl.BlockSpec((tm, t), lambda i, j: (i, j)) for t in tns]
    else:
        in_specs = [pl.BlockSpec((dil, sub, t), lambda i, j: (0, i, j)) for t in tns]
    in_specs += [pl.BlockSpec((D, t), functools.partial(lambda i, j, c0: (0, c0 + j), c0=c0))
                 for t, c0 in zip(tns, col_blocks)]
    args = list(a_list) + w_list
    if has_acc:
        in_specs.append(row)
        args.append(acc)
    if final:
        in_specs += [row, vec, vec, vec, row]
        args += [x, gain, sc, sh, dx_res]
        out_specs = [row, vec, vec, vec]
        out_shape = [_sds((T, D), F32)] + [_sds((1, D), F32)] * 3
    else:
        out_specs = row
        out_shape = _sds((T, D), F32)
    return pl.pallas_call(
        body, name=name, grid=(T // tm, nsteps), in_specs=in_specs, out_specs=out_specs, out_shape=out_shape,
        scratch_shapes=[pltpu.VMEM((tm, D), F32)] + ([pltpu.VMEM((D // LANES, tm, LANES), F32)] if dil > 1 else []),
        compiler_params=_cp(("arbitrary", "arbitrary")),
    )(*args)


def matmul_tn(a, b, tk, tn, name, into=None, layer=0, col0=0):
    T, K = a.shape
    N = b.shape[1]
    tt = 1024 if T % 1024 == 0 else T
    nt = T // tt

    def body(a_ref, b_ref, *rest):
        o_ref, acc = rest[-2:]
        part = _dot_tn(a_ref[...], b_ref[...])

        @pl.when(pl.program_id(2) == 0)
        def _():
            acc[...] = part

        @pl.when(pl.program_id(2) > 0)
        def _():
            acc[...] += part

        @pl.when(pl.program_id(2) == nt - 1)
        def _():
            o_ref[...] = acc[...].astype(BF16)

    in_specs = [pl.BlockSpec((tt, tk), lambda k, n, t: (t, k)), pl.BlockSpec((tt, tn), lambda k, n, t: (t, n))]
    common = dict(name=name, grid=(K // tk, N // tn, nt), scratch_shapes=[pltpu.VMEM((tk, tn), F32)],
                  compiler_params=_cp(("parallel", "parallel", "arbitrary")))
    if into is None:
        return pl.pallas_call(body, in_specs=in_specs, out_specs=pl.BlockSpec((tk, tn), lambda k, n, t: (k, n)),
                              out_shape=_sds((K, N), BF16), **common)(a, b)
    return pl.pallas_call(
        body, in_specs=in_specs + [_ANY], out_specs=pl.BlockSpec((None, tk, tn), lambda k, n, t: (layer, k, col0 + n)),
        out_shape=_sds(into.shape, BF16), input_output_aliases={2: 0}, **common)(a, b, into)


def _halo_specs(tm, tc, col_of):
    per = tm // HALO

    def prev(j, i):
        return (jnp.maximum(i * per - 1, 0), col_of(j))

    def nxt(j, i, last):
        return (jnp.minimum((i + 1) * per, last), col_of(j))

    return prev, nxt, per


def _shift_down(ext, s):
    return pltpu.roll(ext, s, 0)


def _shift_up(ext, s):
    return pltpu.roll(ext, ext.shape[0] - s, 0)


def _conv_ext(ext, w):
    K = w.shape[0]
    out = w[K - 1:K] * ext
    for s in range(1, K):
        out += w[K - 1 - s:K - s] * _shift_down(ext, s)
    return out


def _conv_t_ext(dext, w):
    K = w.shape[0]
    out = w[K - 1:K] * dext
    for s in range(1, K):
        out += w[K - 1 - s:K - s] * _shift_up(dext, s)
    return out


def _conv_dw(dc, ext, K, tm):
    rowid = lax.broadcasted_iota(jnp.int32, (8, dc.shape[1]), 0)
    out = jnp.zeros((8, dc.shape[1]), F32)
    for j in range(K):
        s = K - 1 - j
        xs = ext if s == 0 else _shift_down(ext, s)
        out = jnp.where(rowid == j, jnp.sum(dc * xs[HALO:HALO + tm], axis=0, keepdims=True), out)
    return out


def _accum(ref, val, first):
    @pl.when(first)
    def _():
        ref[...] = val

    @pl.when(jnp.logical_not(first))
    def _():
        ref[...] += val


def ffn_mid_fwd(up, conv_w, conv_b, name):
    T, two_f = up.shape
    F = two_f // 2
    tm = _row_tile(T)
    tc = F // 2
    nct = F // tc
    prev, _, per = _halo_specs(tm, tc, lambda j: j)

    def body(g_ref, gp_ref, v_ref, w_ref, b_ref, a_ref):
        i = pl.program_id(1)
        gp = jnp.where(i > 0, gp_ref[...].astype(F32), 0.0)
        ext = jnp.concatenate([gp, g_ref[...].astype(F32)], axis=0)
        c = _conv_ext(ext, w_ref[...])[HALO:] + b_ref[...]
        a_ref[...] = (_silu(c) * v_ref[...].astype(F32)).astype(BF16)

    return pl.pallas_call(
        body, name=name, grid=(nct, T // tm),
        in_specs=[pl.BlockSpec((tm, tc), lambda j, i: (i, j)), pl.BlockSpec((HALO, tc), prev),
                  pl.BlockSpec((tm, tc), lambda j, i: (i, j + nct)),
                  pl.BlockSpec((FFN_CONV, tc), lambda j, i: (0, j)), pl.BlockSpec((1, tc), lambda j, i: (0, j))],
        out_specs=pl.BlockSpec((tm, tc), lambda j, i: (i, j)),
        out_shape=_sds((T, F), BF16),
        compiler_params=_cp(("parallel", "arbitrary")),
    )(up, up, up, conv_w, conv_b)


def ffn_mid_bwd(up, da, conv_w, conv_b, name):
    T, two_f = up.shape
    F = two_f // 2
    tm = _row_tile(T)
    tc = F // 2
    nct = F // tc
    nrow = T // tm
    prev, nxt, per = _halo_specs(tm, tc, lambda j: j)
    last = T // HALO - 1
    nxt_g = functools.partial(nxt, last=last)

    def nxt_v(j, i):
        return (jnp.minimum((i + 1) * per, last), j + nct)

    def body(g_ref, gp_ref, gn_ref, v_ref, vn_ref, da_ref, dan_ref, w_ref, b_ref, dg_ref, dv_ref, dw_ref, db_ref):
        i = pl.program_id(1)
        w = w_ref[...]
        gp = jnp.where(i > 0, gp_ref[...].astype(F32), 0.0)
        inside = i < nrow - 1
        ext = jnp.concatenate([gp, g_ref[...].astype(F32), gn_ref[...].astype(F32)], axis=0)
        zero = jnp.zeros((HALO, tc), F32)
        v_ext = jnp.concatenate([zero, v_ref[...].astype(F32), vn_ref[...].astype(F32)], axis=0)
        da_ext = jnp.concatenate([zero, da_ref[...].astype(F32), jnp.where(inside, dan_ref[...].astype(F32), 0.0)], axis=0)
        c = _conv_ext(ext, w) + b_ref[...]
        sg = jax.nn.sigmoid(c)
        sil = c * sg
        dc = da_ext * v_ext * (sg * (1.0 + c * (1.0 - sg)))
        dv_ref[...] = (da_ext * sil)[HALO:HALO + tm].astype(BF16)
        dg_ref[...] = _conv_t_ext(dc, w)[HALO:HALO + tm].astype(BF16)
        dcm = dc[HALO:HALO + tm]
        _accum(dw_ref, _conv_dw(dcm, ext, FFN_CONV, tm), i == 0)
        _accum(db_ref, jnp.sum(dcm, axis=0, keepdims=True), i == 0)

    main = lambda j, i: (i, j)
    main_v = lambda j, i: (i, j + nct)
    return pl.pallas_call(
        body, name=name, grid=(nct, nrow),
        in_specs=[pl.BlockSpec((tm, tc), main), pl.BlockSpec((HALO, tc), prev), pl.BlockSpec((HALO, tc), nxt_g),
                  pl.BlockSpec((tm, tc), main_v), pl.BlockSpec((HALO, tc), nxt_v),
                  pl.BlockSpec((tm, tc), main), pl.BlockSpec((HALO, tc), nxt_g),
                  pl.BlockSpec((FFN_CONV, tc), lambda j, i: (0, j)), pl.BlockSpec((1, tc), lambda j, i: (0, j))],
        out_specs=[pl.BlockSpec((tm, tc), main), pl.BlockSpec((tm, tc), main),
                   pl.BlockSpec((8, tc), lambda j, i: (0, j)), pl.BlockSpec((1, tc), lambda j, i: (0, j))],
        out_shape=[_sds((T, F), BF16), _sds((T, F), BF16), _sds((8, F), F32), _sds((1, F), F32)],
        compiler_params=_cp(("parallel", "arbitrary")),
    )(up, up, up, up, up, da, da, conv_w, conv_b)


def _l2n(x):
    return x * lax.rsqrt(jnp.sum(x * x, axis=-1, keepdims=True) + RMS_EPS)


def _qkv_tok(c, normed):
    s = _silu(c)
    n = jnp.concatenate([_l2n(s[:, h * GDN_D:(h + 1) * GDN_D]) for h in range(GDN_HEADS)], axis=1)
    return jnp.where(normed, n, s)


def gdn_prep_fwd(proj, conv_w, name):
    T = proj.shape[0]
    W = GDN_HEADS * GDN_D
    tm = _row_tile(T)
    prev, _, per = _halo_specs(tm, W, lambda j: j)

    def body(x_ref, xp_ref, w_ref, o_ref):
        j, i = pl.program_id(0), pl.program_id(1)
        xp = jnp.where(i > 0, xp_ref[...].astype(F32), 0.0)
        ext = jnp.concatenate([xp, x_ref[...].astype(F32)], axis=0)
        c = _conv_ext(ext, w_ref[...])[HALO:]
        o_ref[...] = _qkv_tok(c, j < 2)

    return pl.pallas_call(
        body, name=name, grid=(3, T // tm),
        in_specs=[pl.BlockSpec((tm, W), lambda j, i: (i, j)), pl.BlockSpec((HALO, W), prev),
                  pl.BlockSpec((GDN_CONV, W), lambda j, i: (0, j))],
        out_specs=pl.BlockSpec((tm, W), lambda j, i: (i, j)),
        out_shape=_sds((T, 3 * W), F32),
        compiler_params=_cp(("parallel", "arbitrary")),
    )(proj, proj, conv_w)


def gdn_prep_bwd(proj, dqkv, conv_w, name):
    T = proj.shape[0]
    W = GDN_HEADS * GDN_D
    tm = _row_tile(T)
    nrow = T // tm
    prev, nxt, per = _halo_specs(tm, W, lambda j: j)
    nxt = functools.partial(nxt, last=T // HALO - 1)

    def body(x_ref, xp_ref, xn_ref, d_ref, dn_ref, w_ref, dx_ref, dw_ref):
        j, i = pl.program_id(0), pl.program_id(1)
        w = w_ref[...]
        xp = jnp.where(i > 0, xp_ref[...].astype(F32), 0.0)
        ext = jnp.concatenate([xp, x_ref[...].astype(F32), xn_ref[...].astype(F32)], axis=0)
        d_ext = jnp.concatenate([jnp.zeros((HALO, W), F32), d_ref[...],
                                 jnp.where(i < nrow - 1, dn_ref[...], 0.0)], axis=0)
        c = _conv_ext(ext, w)
        _, vjp = jax.vjp(lambda cc: _qkv_tok(cc, j < 2), c)
        dc, = vjp(d_ext)
        dx_ref[...] = _conv_t_ext(dc, w)[HALO:HALO + tm].astype(BF16)
        _accum(dw_ref, _conv_dw(dc[HALO:HALO + tm], ext, GDN_CONV, tm), i == 0)

    main = lambda j, i: (i, j)
    return pl.pallas_call(
        body, name=name, grid=(3, nrow),
        in_specs=[pl.BlockSpec((tm, W), main), pl.BlockSpec((HALO, W), prev), pl.BlockSpec((HALO, W), nxt),
                  pl.BlockSpec((tm, W), main), pl.BlockSpec((HALO, W), nxt),
                  pl.BlockSpec((GDN_CONV, W), lambda j, i: (0, j))],
        out_specs=[pl.BlockSpec((tm, W), main), pl.BlockSpec((8, W), lambda j, i: (0, j))],
        out_shape=[_sds((T, 3 * W), BF16), _sds((8, 3 * W), F32)],
        compiler_params=_cp(("parallel", "arbitrary")),
    )(proj, proj, proj, dqkv, dqkv, conv_w)


def _gate_tok(ba, a_log, dt_bias):
    z = ba + dt_bias
    softplus = jnp.maximum(z, 0.0) + jnp.log(1.0 + jnp.exp(-jnp.abs(z)))
    lane = lax.broadcasted_iota(jnp.int32, ba.shape, 1)
    raw = jnp.where(lane < GDN_HEADS, jax.nn.sigmoid(ba), -jnp.exp(a_log) * softplus)
    n = ba.shape[0]
    r = lax.broadcasted_iota(jnp.int32, (n, n), 0)
    c = lax.broadcasted_iota(jnp.int32, (n, n), 1)
    in_chunk_before = (r >= c) & ((r - c) <= (r & (GDN_CHUNK - 1)))
    return jnp.where(lane < GDN_HEADS, raw, _dot_hi(in_chunk_before.astype(F32), raw))


def gate_fwd(ba, a_log, dt_bias, name):
    T, L = ba.shape
    tm = _row_tile(T)

    def body(ba_ref, al_ref, dt_ref, o_ref):
        o_ref[...] = _gate_tok(ba_ref[...], al_ref[...], dt_ref[...])

    vec = pl.BlockSpec((1, L), lambda i: (0, 0))
    return pl.pallas_call(
        body, name=name, grid=(T // tm,),
        in_specs=[pl.BlockSpec((tm, L), lambda i: (i, 0)), vec, vec],
        out_specs=pl.BlockSpec((tm, L), lambda i: (i, 0)), out_shape=_sds((T, L), F32),
        compiler_params=_cp(("parallel",)),
    )(ba, a_log, dt_bias)


def gate_bwd(ba, a_log, dt_bias, dout_a, dout_b, name):
    T, L = ba.shape
    tm = _row_tile(T)

    def body(ba_ref, al_ref, dt_ref, d_ref, d2_ref, dba_ref, dal_ref, ddt_ref):
        _, vjp = jax.vjp(_gate_tok, ba_ref[...], al_ref[...], dt_ref[...])
        dba, dal, ddt = vjp(d_ref[...] + d2_ref[...])
        dba_ref[...] = dba.astype(BF16)
        first = pl.program_id(0) == 0
        _accum(dal_ref, dal, first)
        _accum(ddt_ref, ddt, first)

    vec = pl.BlockSpec((1, L), lambda i: (0, 0))
    row = pl.BlockSpec((tm, L), lambda i: (i, 0))
    return pl.pallas_call(
        body, name=name, grid=(T // tm,),
        in_specs=[row, vec, vec, row, row], out_specs=[row, vec, vec],
        out_shape=[_sds((T, L), BF16), _sds((1, L), F32), _sds((1, L), F32)],
        compiler_params=_cp(("arbitrary",)),
    )(ba, a_log, dt_bias, dout_a, dout_b)


_B_NN = (((2,), (1,)), ((0,), (0,)))
_B_NT = (((2,), (2,)), ((0,), (0,)))
GDN_GROUP = 8


def _split_bf16(a):
    hi = a.astype(BF16)
    return hi, (a - hi.astype(F32)).astype(BF16)


def _bdot3(a, b):
    ah, al = _split_bf16(a)
    bh, bl = _split_bf16(b)
    dg = lambda x, y: lax.dot_general(x, y, _B_NN, preferred_element_type=F32)
    return dg(ah, bh) + (dg(ah, bl) + dg(al, bh))


def _exact_ones_dot(x, dims, ones_shape, ones_first):
    ones = jnp.ones(ones_shape, BF16)
    out = None
    for _ in range(3):
        piece = x.astype(BF16)
        ops = (ones, piece) if ones_first else (piece, ones)
        term = lax.dot_general(*ops, dims, preferred_element_type=F32)
        out = term if out is None else out + term
        x = x - piece.astype(F32)
    return out


@jax.custom_vjp
def _rows_from_cols(col):
    B, C, _ = col.shape
    eye = (lax.broadcasted_iota(jnp.int32, (1, C, C), 1) == lax.broadcasted_iota(jnp.int32, (1, C, C), 2)).astype(F32)
    return _exact_ones_dot(col * eye, _B_NN, (B, C, C), True)


def _rows_from_cols_fwd(col):
    return _rows_from_cols(col), None


def _rows_from_cols_bwd(_, ct):
    B, C, _ = ct.shape
    sums = _exact_ones_dot(ct, (((1,), (1,)), ((0,), (0,))), (B, C, LANES), False)
    return (sums[:, :, :1],)


_rows_from_cols.defvjp(_rows_from_cols_fwd, _rows_from_cols_bwd)


def _gdn_prep(q, k, v, gcol, bcol):
    B, C, dk = q.shape
    r = lax.broadcasted_iota(jnp.int32, (1, C, C), 1)
    cidx = lax.broadcasted_iota(jnp.int32, (1, C, C), 2)
    causal = r >= cidx
    strict = r > cidx
    decay = jnp.where(causal, jnp.exp(jnp.where(causal, gcol - _rows_from_cols(gcol), 0.0)), 0.0)
    qs = q * (dk ** -0.5)
    kb = k * bcol
    kk = lax.dot_general(kb.astype(BF16), k.astype(BF16), _B_NT, preferred_element_type=F32)
    L = jnp.where(strict, kk * decay, 0.0)
    tinv = (r == cidx).astype(F32) - L
    p = L
    n = 2
    while n < C:
        p = _bdot3(p, p)
        tinv = tinv + _bdot3(tinv, p)
        n *= 2
    egc = jnp.exp(gcol)
    u = _bdot3(tinv, v * bcol)
    w = _bdot3(tinv, kb * egc)
    qk = lax.dot_general(qs.astype(BF16), k.astype(BF16), _B_NT, preferred_element_type=F32)
    intra = jnp.where(causal, qk * decay, 0.0)
    last = lax.broadcasted_iota(jnp.int32, (1, C, 1), 1) == C - 1
    gt = jnp.sum(jnp.where(last, gcol, 0.0), axis=1, keepdims=True)
    return w, u, qs * egc, k * jnp.exp(gt - gcol), intra


def _gdn_scan(S, w, u, qg, kdec, intra, gcol):
    C = w.shape[0]
    last = lax.broadcasted_iota(jnp.int32, (C, 1), 0) == C - 1
    gt = jnp.sum(jnp.where(last, gcol, 0.0), axis=0, keepdims=True)
    v_new = u - _dot(w, S)
    o = _dot(qg, S) + _dot(intra, v_new)
    return S * jnp.exp(gt) + _dot_tn(kdec, v_new), o


def _gdn_specs(T, rev=False):
    H, D, C, B = GDN_HEADS, GDN_D, GDN_CHUNK, GDN_GROUP
    nsteps = T // (B * C)
    at = (lambda n: nsteps - 1 - n) if rev else (lambda n: n)
    rows = lambda w: pl.BlockSpec((B * C, w), lambda n: (at(n), 0))
    sq = pl.BlockSpec((B, H, C, C), lambda n: (at(n), 0, 0, 0))
    st = pl.BlockSpec((B, H, D, D), lambda n: (at(n), 0, 0, 0))
    return rows, sq, st, nsteps


def _gate_cols(gate_ref, h, rows=slice(None)):
    return gate_ref[rows, h:h + 1], gate_ref[rows, GDN_HEADS + h:GDN_HEADS + h + 1]


def gdn_prep_chunks_fwd(qkv, gate, name):
    T = qkv.shape[0]
    H, D, C, B = GDN_HEADS, GDN_D, GDN_CHUNK, GDN_GROUP
    W = H * D
    rows, sq, _, nsteps = _gdn_specs(T)

    def body(x_ref, g_ref, w_ref, u_ref, qg_ref, kd_ref, in_ref):
        for h in range(H):
            q, k, v = [x_ref[:, j * W + h * D:j * W + (h + 1) * D].reshape(B, C, D) for j in range(3)]
            bcol, gcol = [c.reshape(B, C, 1) for c in _gate_cols(g_ref, h)]
            outs = _gdn_prep(q, k, v, gcol, bcol)
            for ref, val in zip((w_ref, u_ref, qg_ref, kd_ref), outs[:4]):
                ref[:, h * D:(h + 1) * D] = val.reshape(B * C, D).astype(ref.dtype)
            in_ref[:, h] = outs[4].astype(BF16)

    return pl.pallas_call(
        body, name=name, grid=(nsteps,),
        in_specs=[rows(3 * W), rows(LANES)],
        out_specs=[rows(W), rows(W), rows(W), rows(W), sq],
        out_shape=[_sds((T, W), BF16), _sds((T, W), F32), _sds((T, W), BF16), _sds((T, W), BF16),
                   _sds((T // C, H, C, C), BF16)],
        compiler_params=_cp(("parallel",)),
    )(qkv, gate)


def gdn_prep_chunks_bwd(qkv, gate, dw, du, dqg, dkd, dintra, name):
    T = qkv.shape[0]
    H, D, C, B = GDN_HEADS, GDN_D, GDN_CHUNK, GDN_GROUP
    W = H * D
    rows, sq, _, nsteps = _gdn_specs(T)

    def body(x_ref, g_ref, dw_ref, du_ref, dqg_ref, dkd_ref, din_ref, dx_ref, dg_ref):
        dg_ref[...] = jnp.zeros_like(dg_ref)
        for h in range(H):
            q, k, v = [x_ref[:, j * W + h * D:j * W + (h + 1) * D].reshape(B, C, D) for j in range(3)]
            bcol, gcol = [c.reshape(B, C, 1) for c in _gate_cols(g_ref, h)]
            _, vjp = jax.vjp(_gdn_prep, q, k, v, gcol, bcol)
            cots = tuple(r[:, h * D:(h + 1) * D].reshape(B, C, D) for r in (dw_ref, du_ref, dqg_ref, dkd_ref))
            dq, dk, dv, dgc, db = vjp(cots + (din_ref[:, h],))
            for j, val in enumerate((dq, dk, dv)):
                dx_ref[:, j * W + h * D:j * W + (h + 1) * D] = val.reshape(B * C, D)
            dg_ref[:, h:h + 1] = db.reshape(B * C, 1)
            dg_ref[:, H + h:H + h + 1] = dgc.reshape(B * C, 1)

    return pl.pallas_call(
        body, name=name, grid=(nsteps,),
        in_specs=[rows(3 * W), rows(LANES), rows(W), rows(W), rows(W), rows(W), sq],
        out_specs=[rows(3 * W), rows(LANES)],
        out_shape=[_sds((T, 3 * W), F32), _sds((T, LANES), F32)],
        compiler_params=_cp(("parallel",)),
    )(qkv, gate, dw, du, dqg, dkd, dintra)


def gdn_fwd(w, u, qg, kdec, intra, gate, name):
    T = w.shape[0]
    H, D, C, B = GDN_HEADS, GDN_D, GDN_CHUNK, GDN_GROUP
    W = H * D
    rows, sq, st, nsteps = _gdn_specs(T)

    def body(w_ref, u_ref, qg_ref, kd_ref, in_ref, g_ref, o_ref, s_ref, s_scr):
        @pl.when(pl.program_id(0) == 0)
        def _():
            s_scr[...] = jnp.zeros_like(s_scr)

        def chunk(cb, carry):
            rs = pl.ds(pl.multiple_of(cb * C, C), C)
            for h in range(H):
                cs = slice(h * D, (h + 1) * D)
                S = s_scr[h]
                s_ref[cb, h] = S
                S_new, o = _gdn_scan(S, w_ref[rs, cs], u_ref[rs, cs], qg_ref[rs, cs], kd_ref[rs, cs], in_ref[cb, h],
                                     _gate_cols(g_ref, h, rs)[1])
                s_scr[h] = S_new
                o_ref[rs, cs] = o
            return carry

        lax.fori_loop(0, B, chunk, 0)

    return pl.pallas_call(
        body, name=name, grid=(nsteps,),
        in_specs=[rows(W), rows(W), rows(W), rows(W), sq, rows(LANES)],
        out_specs=[rows(W), st],
        out_shape=[_sds((T, W), F32), _sds((T // C, H, D, D), F32)],
        scratch_shapes=[pltpu.VMEM((H, D, D), F32)],
        compiler_params=_cp(("arbitrary",)),
    )(w, u, qg, kdec, intra, gate)


def gdn_bwd(w, u, qg, kdec, intra, gate, states, do, name):
    T = w.shape[0]
    H, D, C, B = GDN_HEADS, GDN_D, GDN_CHUNK, GDN_GROUP
    W = H * D
    rows, sq, st, nsteps = _gdn_specs(T, rev=True)

    def body(w_ref, u_ref, qg_ref, kd_ref, in_ref, g_ref, s_ref, do_ref,
             dw_ref, du_ref, dqg_ref, dkd_ref, din_ref, dg_ref, ds_scr):
        @pl.when(pl.program_id(0) == 0)
        def _():
            ds_scr[...] = jnp.zeros_like(ds_scr)
        dg_ref[...] = jnp.zeros_like(dg_ref)

        def chunk(t, carry):
            cb = B - 1 - t
            rs = pl.ds(pl.multiple_of(cb * C, C), C)
            for h in range(H):
                cs = slice(h * D, (h + 1) * D)
                _, vjp = jax.vjp(_gdn_scan, s_ref[cb, h], w_ref[rs, cs].astype(F32), u_ref[rs, cs],
                                 qg_ref[rs, cs].astype(F32), kd_ref[rs, cs].astype(F32), in_ref[cb, h].astype(F32),
                                 _gate_cols(g_ref, h, rs)[1])
                dS, dw, du, dqg, dkd, din, dgc = vjp((ds_scr[h], do_ref[rs, cs]))
                ds_scr[h] = dS
                dw_ref[rs, cs] = dw
                du_ref[rs, cs] = du
                dqg_ref[rs, cs] = dqg
                dkd_ref[rs, cs] = dkd
                din_ref[cb, h] = din
                dg_ref[rs, H + h:H + h + 1] = dgc
            return carry

        lax.fori_loop(0, B, chunk, 0)

    return pl.pallas_call(
        body, name=name, grid=(nsteps,),
        in_specs=[rows(W), rows(W), rows(W), rows(W), sq, rows(LANES), st, rows(W)],
        out_specs=[rows(W), rows(W), rows(W), rows(W), sq, rows(LANES)],
        out_shape=[_sds((T, W), F32)] * 4 + [_sds((T // C, H, C, C), F32), _sds((T, LANES), F32)],
        scratch_shapes=[pltpu.VMEM((H, D, D), F32)],
        compiler_params=_cp(("arbitrary",)),
    )(w, u, qg, kdec, intra, gate, states, do)


def _gated_norm(o, z, gain):
    outs = []
    for h in range(GDN_HEADS):
        oh = o[:, h * GDN_D:(h + 1) * GDN_D]
        y = oh * lax.rsqrt(jnp.mean(oh * oh, axis=-1, keepdims=True) + RMS_EPS) * gain
        outs.append(y * _silu(z[:, h * GDN_D:(h + 1) * GDN_D]))
    return jnp.concatenate(outs, axis=1)


def _window_sums(ext, shift):
    outs = []
    s = ext
    step = 1
    for gi, win in enumerate(POOL_WINDOWS):
        while step < win:
            s = s + shift(s, step)
            step *= 2
        outs.append(s[:, gi * GDN_D:(gi + 1) * GDN_D])
    return jnp.concatenate(outs, axis=1)


def _pool_counts(t0, rows, width):
    t1 = (t0 + 1 + lax.broadcasted_iota(jnp.int32, (rows, width), 0)).astype(F32)
    lane = lax.broadcasted_iota(jnp.int32, (rows, width), 1)
    win = jnp.full((rows, width), float(POOL_WINDOWS[-1]), F32)
    for gi in reversed(range(len(POOL_WINDOWS) - 1)):
        win = jnp.where(lane < (gi + 1) * GDN_D, float(POOL_WINDOWS[gi]), win)
    return jnp.minimum(t1, win)


def even_post_fwd(o, proj, gdn_norm, pool_w, pool_scale, name):
    T = o.shape[0]
    W = GDN_HEADS * GDN_D
    tm = _row_tile(T)
    per = tm // HALO

    def body(o_ref, z_ref, p_ref, pp_ref, gn_ref, pw_ref, ps_ref, out_ref):
        i = pl.program_id(0)
        out_ref[:, :W] = _gated_norm(o_ref[...], z_ref[...].astype(F32), gn_ref[...]).astype(BF16)
        pp = jnp.where(i > 0, pp_ref[...].astype(F32), 0.0)
        ext = jnp.concatenate([pp, p_ref[...].astype(F32)], axis=0)
        pooled = (_window_sums(ext, _shift_down)[HALO:] / _pool_counts(i * tm, tm, W)) - ext[HALO:]
        for gi in range(len(POOL_WINDOWS)):
            sl = slice(gi * GDN_D, (gi + 1) * GDN_D)
            y = _dot(pooled[:, sl], pw_ref[gi]) * ps_ref[:, sl]
            out_ref[:, W + gi * GDN_D:W + (gi + 1) * GDN_D] = y.astype(BF16)

    return pl.pallas_call(
        body, name=name, grid=(T // tm,),
        in_specs=[pl.BlockSpec((tm, W), lambda i: (i, 0)), pl.BlockSpec((tm, W), lambda i: (i, 3)),
                  pl.BlockSpec((tm, W), lambda i: (i, 4)),
                  pl.BlockSpec((HALO, W), lambda i: (jnp.maximum(i * per - 1, 0), 4)),
                  pl.BlockSpec((1, GDN_D), lambda i: (0, 0)),
                  pl.BlockSpec((len(POOL_WINDOWS), GDN_D, GDN_D), lambda i: (0, 0, 0)),
                  pl.BlockSpec((1, W), lambda i: (0, 0))],
        out_specs=pl.BlockSpec((tm, 2 * W), lambda i: (i, 0)),
        out_shape=_sds((T, 2 * W), BF16),
        compiler_params=_cp(("arbitrary",)),
    )(o, proj, proj, proj, gdn_norm, pool_w, pool_scale)


def even_post_bwd(o, proj, gdn_norm, pool_w, pool_scale, dcat, name):
    T = o.shape[0]
    W = GDN_HEADS * GDN_D
    G = len(POOL_WINDOWS)
    tm = _row_tile(T)
    per = tm // HALO
    nrow = T // tm
    last = T // HALO - 1

    def body(o_ref, z_ref, p_ref, pp_ref, gn_ref, pw_ref, ps_ref, d_ref, dn_ref,
             do_ref, dzp_ref, dgn_ref, dpw_ref, dps_ref):
        i = pl.program_id(0)
        first = i == 0
        _, vjp = jax.vjp(_gated_norm, o_ref[...], z_ref[...].astype(F32), gn_ref[...])
        do, dz, dgn = vjp(d_ref[:, :W].astype(F32))
        do_ref[...] = do
        dzp_ref[:, :W] = dz.astype(BF16)
        _accum(dgn_ref, dgn, first)
        pp = jnp.where(first, 0.0, pp_ref[...].astype(F32))
        ext = jnp.concatenate([pp, p_ref[...].astype(F32)], axis=0)
        pooled = (_window_sums(ext, _shift_down)[HALO:] / _pool_counts(i * tm, tm, W)) - ext[HALO:]
        dy_ext = jnp.concatenate([d_ref[:, W:].astype(F32), jnp.where(i < nrow - 1, dn_ref[...].astype(F32), 0.0)], axis=0)
        dys_ext = dy_ext * ps_ref[...]
        dpooled, dscale = [], []
        for gi in range(G):
            sl = slice(gi * GDN_D, (gi + 1) * GDN_D)
            dpooled.append(_dot_nt(dys_ext[:, sl], pw_ref[gi]))
            y = _dot(pooled[:, sl], pw_ref[gi])
            dscale.append(jnp.sum(dy_ext[:tm, sl] * y, axis=0, keepdims=True))
            _accum(dpw_ref.at[gi], _dot_tn(pooled[:, sl], dys_ext[:tm, sl]), first)
        dpooled = jnp.concatenate(dpooled, axis=1)
        _accum(dps_ref, jnp.concatenate(dscale, axis=1), first)
        dmean = dpooled / _pool_counts(i * tm, tm + HALO, W)
        dp = _window_sums(dmean, _shift_up)[:tm] - dpooled[:tm]
        dzp_ref[:, W:] = dp.astype(BF16)

    return pl.pallas_call(
        body, name=name, grid=(nrow,),
        in_specs=[pl.BlockSpec((tm, W), lambda i: (i, 0)), pl.BlockSpec((tm, W), lambda i: (i, 3)),
                  pl.BlockSpec((tm, W), lambda i: (i, 4)),
                  pl.BlockSpec((HALO, W), lambda i: (jnp.maximum(i * per - 1, 0), 4)),
                  pl.BlockSpec((1, GDN_D), lambda i: (0, 0)),
                  pl.BlockSpec((G, GDN_D, GDN_D), lambda i: (0, 0, 0)),
                  pl.BlockSpec((1, W), lambda i: (0, 0)),
                  pl.BlockSpec((tm, 2 * W), lambda i: (i, 0)),
                  pl.BlockSpec((HALO, W), lambda i: (jnp.minimum((i + 1) * per, last), 1))],
        out_specs=[pl.BlockSpec((tm, W), lambda i: (i, 0)), pl.BlockSpec((tm, 2 * W), lambda i: (i, 0)),
                   pl.BlockSpec((1, GDN_D), lambda i: (0, 0)), pl.BlockSpec((G, GDN_D, GDN_D), lambda i: (0, 0, 0)),
                   pl.BlockSpec((1, W), lambda i: (0, 0))],
        out_shape=[_sds((T, W), F32), _sds((T, 2 * W), BF16), _sds((1, GDN_D), F32), _sds((G, GDN_D, GDN_D), F32),
                   _sds((1, W), F32)],
        compiler_params=_cp(("arbitrary",)),
    )(o, proj, proj, proj, gdn_norm, pool_w, pool_scale, dcat, dcat)


def _head_norm(x, gain):
    return x * lax.rsqrt(jnp.mean(x * x, axis=-1, keepdims=True) + RMS_EPS) * gain


def _att_scores(q, k, slope, has_prev):
    B = ATT_BLOCK
    a = lax.broadcasted_iota(jnp.int32, (B, 2 * B), 0)
    j = lax.broadcasted_iota(jnp.int32, (B, 2 * B), 1)
    rel = B + a - j
    mask = (rel >= 0) & (rel <= B) & ((j >= B) | has_prev)
    s = _dot_nt(q, k) - slope * rel.astype(F32)
    return jnp.where(mask, s, NEG), mask


def _alibi_slope(h, dil):
    return dil * (2.0 ** (-8.0 * (h + 1) / ATT_HEADS))


def att_fwd(qk, qkv, name):
    dil, L, _ = qk.shape
    Wd = ATT_HEADS * ATT_DH
    nb = L // ATT_BLOCK
    B = ATT_BLOCK

    def body(q_ref, kc_ref, kp_ref, vc_ref, vp_ref, o_ref, l_ref):
        has_prev = pl.program_id(1) > 0
        for h in range(ATT_HEADS):
            sl = slice(h * ATT_DH, (h + 1) * ATT_DH)
            k = jnp.concatenate([kp_ref[:, sl], kc_ref[:, sl]], axis=0)
            v = jnp.concatenate([vp_ref[:, sl], vc_ref[:, sl]], axis=0)
            s, _ = _att_scores(q_ref[:, sl], k, _alibi_slope(h, dil), has_prev)
            m = jnp.max(s, axis=-1, keepdims=True)
            p = jnp.exp(s - m)
            l = jnp.sum(p, axis=-1, keepdims=True)
            o_ref[:, sl] = (_dot(p, v) / l).astype(BF16)
            l_ref[:, sl] = jnp.broadcast_to(m + jnp.log(l), (B, ATT_DH))

    cur = lambda t: pl.BlockSpec((None, B, Wd), lambda r, n: (r, n, t))
    prev = lambda t: pl.BlockSpec((None, B, Wd), lambda r, n: (r, jnp.maximum(n - 1, 0), t))
    out = pl.BlockSpec((None, B, Wd), lambda r, n: (r, n, 0))
    return pl.pallas_call(
        body, name=name, grid=(dil, nb),
        in_specs=[cur(0), cur(1), prev(1), cur(2), prev(2)],
        out_specs=[out, out], out_shape=[_sds((dil, L, Wd), BF16), _sds((dil, L, Wd), F32)],
        compiler_params=_cp(("parallel", "arbitrary")),
    )(qk, qk, qk, qkv, qkv)


def att_bwd(qk, qkv, lse, do, dd, name):
    dil, L, _ = qk.shape
    Wd = ATT_HEADS * ATT_DH
    nb = L // ATT_BLOCK
    B = ATT_BLOCK

    def body(q_ref, kc_ref, kp_ref, vc_ref, vp_ref, l_ref, do_ref, dd_ref, dq_ref, dk_ref, dv_ref, ck_scr, cv_scr):
        n = pl.program_id(1)
        has_prev = n > 0

        @pl.when(n < nb)
        def _():
            for h in range(ATT_HEADS):
                sl = slice(h * ATT_DH, (h + 1) * ATT_DH)
                q = q_ref[:, sl]
                k = jnp.concatenate([kp_ref[:, sl], kc_ref[:, sl]], axis=0)
                v = jnp.concatenate([vp_ref[:, sl], vc_ref[:, sl]], axis=0)
                do = do_ref[:, sl]
                s, mask = _att_scores(q, k, _alibi_slope(h, dil), has_prev)
                p = jnp.where(mask, jnp.exp(s - l_ref[:, h * ATT_DH:h * ATT_DH + 1]), 0.0)
                delta = jnp.sum(dd_ref[:, sl].astype(F32), axis=-1, keepdims=True)
                ds = p * (_dot_nt(do, v) - delta)
                dq_ref[:, sl] = _dot(ds, k).astype(BF16)
                dk = _dot_tn(ds, q)
                dv = _dot_tn(p, do)
                dk_ref[:, sl] = (jnp.where(has_prev, ck_scr[:, sl] + dk[:B], 0.0)).astype(BF16)
                dv_ref[:, sl] = (jnp.where(has_prev, cv_scr[:, sl] + dv[:B], 0.0)).astype(BF16)
                ck_scr[:, sl] = dk[B:]
                cv_scr[:, sl] = dv[B:]

        @pl.when(n == nb)
        def _():
            dk_ref[...] = ck_scr[...].astype(BF16)
            dv_ref[...] = cv_scr[...].astype(BF16)

    cur = lambda t: pl.BlockSpec((None, B, Wd), lambda r, n: (r, jnp.minimum(n, nb - 1), t))
    prev = lambda t: pl.BlockSpec((None, B, Wd), lambda r, n: (r, jnp.clip(n - 1, 0, nb - 1), t))
    kv_out = pl.BlockSpec((None, B, Wd), lambda r, n: (r, jnp.maximum(n - 1, 0), 0))
    return pl.pallas_call(
        body, name=name, grid=(dil, nb + 1),
        in_specs=[cur(0), cur(1), prev(1), cur(2), prev(2), cur(0), cur(0), cur(0)],
        out_specs=[cur(0), kv_out, kv_out],
        out_shape=[_sds((dil, L, Wd), BF16)] * 3,
        scratch_shapes=[pltpu.VMEM((B, Wd), F32), pltpu.VMEM((B, Wd), F32)],
        compiler_params=_cp(("parallel", "arbitrary")),
    )(qk, qk, qk, qkv, qkv, lse, do, dd)


def qk_norm_bwd(qkv, dq, dk, q_norm, k_norm, name):
    T = qkv.shape[0]
    Wd = dq.shape[1]
    tm = _row_tile(T)

    def body(q_ref, k_ref, dq_ref, dk_ref, qn_ref, kn_ref, oq_ref, ok_ref, dqn_ref, dkn_ref):
        first = pl.program_id(0) == 0
        for x_ref, d_ref, g_ref, o_ref, dg_ref, scale in ((q_ref, dq_ref, qn_ref, oq_ref, dqn_ref, ATT_DH ** -0.5),
                                                         (k_ref, dk_ref, kn_ref, ok_ref, dkn_ref, 1.0)):
            dg = jnp.zeros((1, ATT_DH), F32)
            for h in range(Wd // ATT_DH):
                sl = slice(h * ATT_DH, (h + 1) * ATT_DH)
                _, vjp = jax.vjp(lambda x, g: _head_norm(x, g) * scale, x_ref[:, sl].astype(F32), g_ref[...])
                dx, dg_h = vjp(d_ref[:, sl].astype(F32))
                o_ref[:, sl] = dx.astype(BF16)
                dg += dg_h
            _accum(dg_ref, dg, first)

    row = lambda t: pl.BlockSpec((tm, Wd), lambda i: (i, t))
    vec = pl.BlockSpec((1, ATT_DH), lambda i: (0, 0))
    return pl.pallas_call(
        body, name=name, grid=(T // tm,),
        in_specs=[row(0), row(1), row(0), row(0), vec, vec], out_specs=[row(0), row(0), vec, vec],
        out_shape=[_sds((T, Wd), BF16)] * 2 + [_sds((1, ATT_DH), F32)] * 2,
        compiler_params=_cp(("arbitrary",)),
    )(qkv, qkv, dq, dk, q_norm, k_norm)


def _merge_weights(l0, l1, l2):
    m = jnp.maximum(jnp.maximum(l0, l1), l2)
    e = [jnp.exp(l - m) for l in (l0, l1, l2)]
    tot = e[0] + e[1] + e[2]
    return [x / tot for x in e]


def _merge_specs(arrays, tm):
    return [pl.BlockSpec((a.shape[0], tm // a.shape[0], a.shape[2]), lambda i: (0, i, 0)) for a in arrays]


def _merge_load(refs, dils, tm, scr):
    vals = [ref[...].astype(F32).reshape(tm, ref.shape[-1]) for ref in refs]
    return [v if dl == 1 else _interleave(v, scr, dl) for v, dl in zip(vals, dils)]


def merge_fwd(outs, lses, name):
    Wd = outs[0].shape[2]
    T = outs[0].shape[0] * outs[0].shape[1]
    tm = _row_tile(T)
    dils = [a.shape[0] for a in outs] * 2

    def body(*refs):
        vals = _merge_load(refs[:6], dils, tm, refs[7])
        w = _merge_weights(*vals[3:])
        refs[6][...] = (w[0] * vals[0] + w[1] * vals[1] + w[2] * vals[2]).astype(BF16)

    return pl.pallas_call(body, name=name, grid=(T // tm,), in_specs=_merge_specs(list(outs) + list(lses), tm),
                          out_specs=pl.BlockSpec((tm, Wd), lambda i: (i, 0)), out_shape=_sds((T, Wd), BF16),
                          scratch_shapes=[pltpu.VMEM((Wd // LANES, tm, LANES), F32)],
                          compiler_params=_cp(("parallel",)))(*outs, *lses)


def merge_bwd(outs, lses, d, name):
    Wd = outs[0].shape[2]
    T = outs[0].shape[0] * outs[0].shape[1]
    tm = 256 if T % 256 == 0 else T
    dils = [a.shape[0] for a in outs] * 2

    def body(*refs):
        scr = refs[13]
        vals = _merge_load(refs[:6], dils, tm, scr)
        w = _merge_weights(*vals[3:])
        dv = refs[6][...].astype(F32)
        dvm = dv * (w[0] * vals[0] + w[1] * vals[1] + w[2] * vals[2])
        for g in range(3):
            for ref, val in ((refs[7 + g], w[g] * dv), (refs[10 + g], w[g] * dvm)):
                ref[...] = (val if dils[g] == 1 else _deinterleave(val, scr, dils[g])).reshape(ref.shape).astype(BF16)

    specs = _merge_specs(list(outs) + list(lses), tm)
    res = pl.pallas_call(body, name=name, grid=(T // tm,),
                         in_specs=specs + [pl.BlockSpec((tm, Wd), lambda i: (i, 0))], out_specs=specs,
                         out_shape=[_sds(a.shape, BF16) for a in list(outs) + list(outs)],
                         scratch_shapes=[pltpu.VMEM((Wd // LANES, tm, LANES), F32)],
                         compiler_params=_cp(("parallel",)))(*outs, *lses, d)
    return res[:3], res[3:]


def loss_head(y, target, name):
    T, D = y.shape
    tm = _row_tile(T)

    def body(y_ref, t_ref, l_ref, dy_ref):
        err = y_ref[...] - t_ref[...]
        dy_ref[...] = err * (1.0 / D)
        part = 0.5 * jnp.sum(jnp.sum(err * err, axis=1, keepdims=True) * (1.0 / D), axis=0, keepdims=True)
        _accum(l_ref, jnp.broadcast_to(part, (1, 128)), pl.program_id(0) == 0)

    row = pl.BlockSpec((tm, D), lambda i: (i, 0))
    return pl.pallas_call(body, name=name, grid=(T // tm,), in_specs=[row, row],
                          out_specs=[pl.BlockSpec((1, 128), lambda i: (0, 0)), row],
                          out_shape=[_sds((1, 128), F32), _sds((T, D), F32)],
                          compiler_params=_cp(("arbitrary",)))(y, target)


def ada_fwd(c_all, w, b, name):
    depth, D, n = w.shape

    def body(c_ref, w_ref, b_ref, o_ref):
        o_ref[...] = _dot(_silu(c_ref[...]), w_ref[...]) + b_ref[...]

    return pl.pallas_call(
        body, name=name, grid=(depth,),
        in_specs=[pl.BlockSpec((N_DEV, D), lambda i: (0, 0)), pl.BlockSpec((None, D, n), lambda i: (i, 0, 0)),
                  pl.BlockSpec((None, 1, n), lambda i: (i, 0, 0))],
        out_specs=pl.BlockSpec((None, N_DEV, n), lambda i: (i, 0, 0)),
        out_shape=_sds((depth, N_DEV, n), F32), compiler_params=_cp(("parallel",)),
    )(c_all, w, b)


def ada_bwd(c_all, dmod, name):
    depth, _, n = dmod.shape
    D = c_all.shape[1]

    def body(c_ref, d_ref, o_ref):
        o_ref[...] = _dot_tn(_silu(c_ref[...]), d_ref[...])

    return pl.pallas_call(
        body, name=name, grid=(depth,),
        in_specs=[pl.BlockSpec((N_DEV, D), lambda i: (0, 0)), pl.BlockSpec((None, N_DEV, n), lambda i: (i, 0, 0))],
        out_specs=pl.BlockSpec((None, D, n), lambda i: (i, 0, 0)),
        out_shape=_sds((depth, D, n), F32), compiler_params=_cp(("parallel",)),
    )(c_all, dmod)


def adamw(w, m, v, gparts, name):
    R, C = w.shape
    k = gparts.shape[0]
    tr = R
    for cand in (512, 256, 128, 64, 32, 16, 8):
        if R % cand == 0 and cand * C * 4 <= 2 * 1024 * 1024:
            tr = cand
            break
    bc1 = 1.0 - ADAM_B1 ** ADAM_STEP
    bc2 = 1.0 - ADAM_B2 ** ADAM_STEP

    def body(w_ref, m_ref, v_ref, gp_ref, g_ref, d_ref, nm_ref, nv_ref):
        g = gp_ref[0].astype(F32)
        for q in range(1, k):
            g = g + gp_ref[q].astype(F32)
        nm = ADAM_B1 * m_ref[...] + (1.0 - ADAM_B1) * g
        nv = ADAM_B2 * v_ref[...] + (1.0 - ADAM_B2) * (g * g)
        g_ref[...] = g
        nm_ref[...] = nm
        nv_ref[...] = nv
        d_ref[...] = -ADAM_LR * ((nm / bc1) / (jnp.sqrt(nv / bc2) + ADAM_EPS) + ADAM_WD * w_ref[...])

    row = pl.BlockSpec((tr, C), lambda i: (i, 0))
    return pl.pallas_call(
        body, name=name, grid=(R // tr,),
        in_specs=[row, row, row, pl.BlockSpec((k, tr, C), lambda i: (0, i, 0))],
        out_specs=[row] * 4, out_shape=[_sds((R, C), F32)] * 4, compiler_params=_cp(("parallel",)),
    )(w, m, v, gparts)


def _mesh_pos():
    return lax.axis_index("x"), lax.axis_index("y"), lax.axis_index("c")


def _other_chips(x, y):
    return [(1 - x, y), (x, 1 - y), (1 - x, 1 - y)]


_ANY = pl.BlockSpec(memory_space=pl.ANY)


class _Gather:
    def __init__(self, ins, outs, send_sems, recv_sems, local_sems):
        self.ins, self.outs, self.n = ins, outs, len(ins)
        self.send_sems, self.recv_sems, self.local_sems = send_sems, recv_sems, local_sems
        self.x, self.y, self.c = _mesh_pos()
        self.me, self.sibling = (self.x, self.y, self.c), (self.x, self.y, 1 - self.c)
        self.chips = _other_chips(self.x, self.y)

    def _copy(self, a, k, block, to, src=None):
        dst = self.outs[a].at[4 * block[0] + 2 * block[1] + block[2]]
        return pltpu.make_async_remote_copy(
            src_ref=dst if src is None else src, dst_ref=dst, send_sem=self.send_sems.at[a, k],
            recv_sem=self.recv_sems.at[a, k], device_id=to, device_id_type=MESH)

    def _mine(self):
        me_slot = 4 * self.x + 2 * self.y + self.c
        return [pltpu.make_async_copy(self.ins[a], self.outs[a].at[me_slot], self.local_sems.at[a]) for a in range(self.n)]

    def _first(self):
        out = []
        for a in range(self.n):
            out.append(self._copy(a, 0, self.me, self.sibling, src=self.ins[a]))
            out += [self._copy(a, 1 + j, self.me, (*chip, self.c), src=self.ins[a]) for j, chip in enumerate(self.chips)]
        return out

    def _passed(self):
        return [self._copy(a, 4 + j, (*chip, self.c), self.sibling) for j, chip in enumerate(self.chips) for a in range(self.n)]

    def start(self):
        for cp in self._mine() + self._first():
            cp.start()

    def forward(self):
        for j, chip in enumerate(self.chips):
            for a in range(self.n):
                self._copy(a, 1 + j, (*chip, self.c), self.me).wait_recv()
                self._copy(a, 4 + j, (*chip, self.c), self.sibling).start()

    def finish(self):
        for a in range(self.n):
            self._copy(a, 0, self.sibling, self.me).wait_recv()
            for j, chip in enumerate(self.chips):
                self._copy(a, 4 + j, (*chip, 1 - self.c), self.me).wait_recv()
        for cp in self._first() + self._passed():
            cp.wait_send()
        for cp in self._mine():
            cp.wait()


def _gather_scratch(n):
    return [pltpu.SemaphoreType.DMA((n, 7)), pltpu.SemaphoreType.DMA((n, 7)), pltpu.SemaphoreType.DMA((n,))]


def all_gather(shards, name):
    n = len(shards)

    def body(*refs):
        g = _Gather(refs[:n], refs[n:2 * n], *refs[2 * n:])
        g.start()
        g.forward()
        g.finish()

    return pl.pallas_call(
        body, name=name, in_specs=[_ANY] * n, out_specs=[_ANY] * n,
        out_shape=[_sds((N_DEV,) + s.shape, s.dtype) for s in shards],
        scratch_shapes=_gather_scratch(n),
        compiler_params=pltpu.CompilerParams(has_side_effects=True),
    )(*shards)


def exchange_pair(stacked, name):
    n = len(stacked)

    def body(*refs):
        ins, outs = refs[:n], refs[n:2 * n]
        send_sems, recv_sems = refs[2 * n:]
        x, y, c = _mesh_pos()
        copies = [pltpu.make_async_remote_copy(
            src_ref=ins[a].at[2 * q + (1 - c)], dst_ref=outs[a].at[q], send_sem=send_sems.at[a, q],
            recv_sem=recv_sems.at[a, q], device_id=(x, y, 1 - c), device_id_type=MESH)
            for a in range(n) for q in range(4)]
        for cp in copies:
            cp.start()
        for cp in copies:
            cp.wait()

    return pl.pallas_call(
        body, name=name, in_specs=[_ANY] * n, out_specs=[_ANY] * n,
        out_shape=[_sds((4,) + s.shape[1:], s.dtype) for s in stacked],
        scratch_shapes=[pltpu.SemaphoreType.DMA((n, 4)), pltpu.SemaphoreType.DMA((n, 4))],
        compiler_params=pltpu.CompilerParams(has_side_effects=True),
    )(*stacked)


def pair_add(stacked, got, c_idx, name):
    _, R, C = stacked.shape
    tr = R
    for cand in (512, 256, 128, 64, 32, 16):
        if R % cand == 0 and cand * C * 4 <= 2 * 1024 * 1024:
            tr = cand
            break

    def body(c_ref, s_ref, g_ref, o_ref):
        o_ref[...] = (s_ref[...].astype(F32) + g_ref[...].astype(F32)).astype(BF16)

    return pl.pallas_call(
        body, name=name,
        grid_spec=pltpu.PrefetchScalarGridSpec(
            num_scalar_prefetch=1, grid=(4, R // tr),
            in_specs=[pl.BlockSpec((None, tr, C), lambda q, i, c_ref: (2 * q + c_ref[0], i, 0)),
                      pl.BlockSpec((None, tr, C), lambda q, i, c_ref: (q, i, 0))],
            out_specs=pl.BlockSpec((None, tr, C), lambda q, i, c_ref: (q, i, 0))),
        out_shape=_sds((4, R, C), BF16),
        compiler_params=_cp(("parallel", "parallel")),
    )(c_idx, stacked, got)


def exchange_chips(parts, name):
    n = len(parts)

    def body(*refs):
        ins, outs = refs[:n], refs[n:2 * n]
        send_sems, recv_sems, local_sems = refs[2 * n:]
        x, y, c = _mesh_pos()
        myq = 2 * x + y
        mine = [pltpu.make_async_copy(ins[a].at[myq], outs[a].at[myq], local_sems.at[a]) for a in range(n)]
        for cp in mine:
            cp.start()
        copies = [pltpu.make_async_remote_copy(
            src_ref=ins[a].at[2 * chip[0] + chip[1]], dst_ref=outs[a].at[myq], send_sem=send_sems.at[a, j],
            recv_sem=recv_sems.at[a, j], device_id=(*chip, c), device_id_type=MESH)
            for a in range(n) for j, chip in enumerate(_other_chips(x, y))]
        for cp in copies:
            cp.start()
        for cp in copies:
            cp.wait()
        for cp in mine:
            cp.wait()

    return pl.pallas_call(
        body, name=name, in_specs=[_ANY] * n, out_specs=[_ANY] * n,
        out_shape=[_sds(s.shape, s.dtype) for s in parts],
        scratch_shapes=[pltpu.SemaphoreType.DMA((n, 3)), pltpu.SemaphoreType.DMA((n, 3)), pltpu.SemaphoreType.DMA((n,))],
        compiler_params=pltpu.CompilerParams(has_side_effects=True),
    )(*parts)


def _pad_lanes(v, start, width=128):
    return jnp.pad(v.astype(F32), (start, width - start - v.shape[0]))[None]


def _layer_fwd(i, x, mod, P, next_shards=None):
    T, D = x.shape
    sh_m, sc_m, g_m, sh_f, sc_f, g_f = [mod[k:k + 1] for k in range(6)]
    sv = {"x0": x}
    tag = f"l{i}"
    if i % 2 == 0:
        e = i // 2
        h, proj = norm_proj(x, P["norm_mix"][i:i + 1], sc_m, sh_m, P["ev_main"][e], P["ev_main"][e].shape[1], tag + "_in")
        ba = matmul_nn(h, P["ev_ba"][e], F32, tag + "_ba")
        al = _pad_lanes(P["gdn_a_log"][e], GDN_HEADS)
        dt = _pad_lanes(P["gdn_dt_bias"][e], GDN_HEADS)
        gate = gate_fwd(ba, al, dt, tag + "_gate")
        qkv = gdn_prep_fwd(proj, P["gdn_conv_w"][e], tag + "_prep")
        chunks = gdn_prep_chunks_fwd(qkv, gate, tag + "_chunks")
        o, states = gdn_fwd(*chunks, gate, tag + "_gdn")
        cat = even_post_fwd(o, proj, P["gdn_norm"][e:e + 1], P["pool_w"][e], P["pool_scale"][e:e + 1], tag + "_post")
        x1, y_m = proj_res(cat, P["ev_w_out"][e], x, g_m, tag + "_out")
        sv.update(h=h, proj=proj, ba=ba, al=al, dt=dt, gate=gate, qkv=qkv, chunks=chunks, o=o, states=states, a_m=cat)
    else:
        od = i // 2
        qn, kn = P["att_q_norm"][od:od + 1], P["att_k_norm"][od:od + 1]
        h, proj, qk = [], [], []
        for gi, (_, dil) in enumerate(DIL_PATTERNS):
            hg, pg, qkg = norm_proj(x, P["norm_mix"][i:i + 1], sc_m, sh_m, P["od_w_in"][od], 3 * D, f"{tag}_in{gi}",
                                    cols=(gi, 1), dil=dil, qk_norms=(qn, kn))
            h.append(hg.reshape(T, D))
            proj.append(pg)
            qk.append(qkg)
        res = [att_fwd(qk[gi], proj[gi], f"{tag}_att{gi}") for gi in range(len(DIL_PATTERNS))]
        outs, lses = [r[0] for r in res], [r[1] for r in res]
        merged = merge_fwd(outs, lses, tag + "_merge")
        x1, y_m = proj_res(merged, P["od_w_out"][od], x, g_m, tag + "_out")
        sv.update(h=h, proj=proj, qk=qk, outs=outs, lses=lses, a_m=merged)
    hf, up, *got = norm_proj(x1, P["norm_ffn"][i:i + 1], sc_f, sh_f, P["ffn_w_up"][i], 1408, tag + "_up",
                             gather=list(next_shards.values()) if next_shards else None)
    a = ffn_mid_fwd(up, P["ffn_conv_w"][i], P["ffn_conv_b"][i:i + 1], tag + "_mid")
    x2, y_f = proj_res(a, P["ffn_w_down"][i], x1, g_f, tag + "_down")
    sv.update(y_m=y_m, x1=x1, hf=hf, up=up, a_f=a, y_f=y_f)
    return x2, sv, (dict(zip(next_shards.keys(), got)) if next_shards else None)


def add_layer_weights(P, gathered):
    W = GDN_HEADS * GDN_D
    for k, g in gathered.items():
        if k in _COL_SHARDED:
            full = g.transpose(1, 0, 2).reshape(g.shape[1], N_DEV * g.shape[2])
        else:
            full = g.reshape(N_DEV * g.shape[1], g.shape[2])
        if k == "ev_w_in":
            P.setdefault("ev_main", []).append(jnp.concatenate([full[:, :4 * W], full[:, 4 * W + 2 * GDN_HEADS:]], axis=1))
            P.setdefault("ev_ba", []).append(jnp.pad(full[:, 4 * W:4 * W + 2 * GDN_HEADS], ((0, 0), (0, 128 - 2 * GDN_HEADS))))
        else:
            P.setdefault(k, []).append(full)


def _layer_bwd(i, dx2, sv, mod, P, bufs):
    T = dx2.shape[0]
    sh_m, sc_m, g_m, sh_f, sc_f, g_f = [mod[k:k + 1] for k in range(6)]
    tag = f"b{i}"
    G = {}
    F = P["ffn_w_down"][i].shape[0]
    dy, da, dg_f = bwd_out(dx2, g_f, sv["y_f"], P["ffn_w_down"][i], F, tag + "_down")
    bufs["ffn_w_down"] = matmul_tn(sv["a_f"], dy, F // 2, dy.shape[1], tag + "_wdown", into=bufs["ffn_w_down"], layer=i)
    dgate, dval, dcw, dcb = ffn_mid_bwd(sv["up"], da, P["ffn_conv_w"][i], P["ffn_conv_b"][i:i + 1], tag + "_mid")
    G["ffn_conv_w"], G["ffn_conv_b"] = dcw[:FFN_CONV], dcb[0]
    dx1, dnf, dsc_f, dsh_f = bwd_in([dgate, dval], P["ffn_w_up"][i], [0, 1], F, None,
                                    sv["x1"], P["norm_ffn"][i:i + 1], sc_f, sh_f, dx2, tag + "_up")
    G["norm_ffn"] = dnf[0]
    x0, h, proj = sv["x0"], sv["h"], sv["proj"]
    D = x0.shape[1]
    for k, d in enumerate((dgate, dval)):
        bufs["ffn_w_up"] = matmul_tn(sv["hf"], d, D, F // 2, f"{tag}_wup{k}", into=bufs["ffn_w_up"], layer=i, col0=2 * k)
    gain = P["norm_mix"][i:i + 1]
    if i % 2 == 0:
        e = i // 2
        W = GDN_HEADS * GDN_D
        dy, dcat, dg_m = bwd_out(dx1, g_m, sv["y_m"], P["ev_w_out"][e], 1024, tag + "_out")
        bufs["ev_w_out"] = matmul_tn(sv["a_m"], dy, 1024, D, tag + "_wout", into=bufs["ev_w_out"], layer=e)
        do, dzp, dgn, dpw, dps = even_post_bwd(sv["o"], proj, P["gdn_norm"][e:e + 1], P["pool_w"][e],
                                               P["pool_scale"][e:e + 1], dcat, tag + "_post")
        G["gdn_norm"], G["pool_w"], G["pool_scale"] = dgn[0], dpw, dps[0]
        *dchunks, dgate_scan = gdn_bwd(*sv["chunks"], sv["gate"], sv["states"], do, tag + "_gdn")
        dqkv, dgate_prep = gdn_prep_chunks_bwd(sv["qkv"], sv["gate"], *dchunks, tag + "_chunks")
        dba, dal, ddt = gate_bwd(sv["ba"], sv["al"], sv["dt"], dgate_prep, dgate_scan, tag + "_gate")
        G["gdn_a_log"], G["gdn_dt_bias"] = dal[0, GDN_HEADS:2 * GDN_HEADS], ddt[0, GDN_HEADS:2 * GDN_HEADS]
        dqkv_raw, dconv = gdn_prep_bwd(proj, dqkv, P["gdn_conv_w"][e], tag + "_prep")
        G["gdn_conv_w"] = dconv[:GDN_CONV]
        w_main = P["ev_main"][e]
        dx0, dnm, dsc_m, dsh_m = bwd_in([dba, dqkv_raw, dzp], [P["ev_ba"][e], w_main[:, :3 * W], w_main[:, 3 * W:]],
                                        [0, 0, 0], None, None, x0, gain, sc_m, sh_m, dx1, tag + "_in")
        gw_qkv = matmul_tn(h, dqkv_raw, 1024, W, tag + "_win1")
        gw_zp = matmul_tn(h, dzp, 1024, W, tag + "_win2")
        gw_ba = matmul_tn(h, dba, 1024, 128, tag + "_win0")
        G["ev_w_in"] = jnp.concatenate([gw_qkv, gw_zp[:, :W], gw_ba[:, :2 * GDN_HEADS], gw_zp[:, W:]], axis=1)
    else:
        od = i // 2
        qn, kn = P["att_q_norm"][od:od + 1], P["att_k_norm"][od:od + 1]
        dy, dmerged, dg_m = bwd_out(dx1, g_m, sv["y_m"], P["od_w_out"][od], 1024, tag + "_out")
        bufs["od_w_out"] = matmul_tn(sv["a_m"], dy, 1024, D, tag + "_wout", into=bufs["od_w_out"], layer=od)
        douts, dds = merge_bwd(sv["outs"], sv["lses"], dmerged, tag + "_merge")
        dh = None
        dqn_sum, dkn_sum = 0.0, 0.0
        ng = len(DIL_PATTERNS)
        for gi, (_, dil) in enumerate(DIL_PATTERNS):
            dqs, dks, dv = att_bwd(sv["qk"][gi], proj[gi], sv["lses"][gi], douts[gi], dds[gi], f"{tag}_att{gi}")
            dq, dk, dqn, dkn = qk_norm_bwd(proj[gi].reshape(T, 3 * D), dqs.reshape(T, D), dks.reshape(T, D), qn, kn,
                                           f"{tag}_qkn{gi}")
            dqn_sum, dkn_sum = dqn_sum + dqn[0], dkn_sum + dkn[0]
            for k, d in enumerate((dq, dk, dv)):
                bufs["od_w_in"] = matmul_tn(h[gi], d.reshape(T, D), 1024, D, f"{tag}_win{gi}{k}", into=bufs["od_w_in"],
                                            layer=od, col0=3 * gi + k)
            cols = [3 * gi, 3 * gi + 1, 3 * gi + 2]
            ops = [d.reshape(dil, T // dil, D) if dil > 1 else d.reshape(T, D) for d in (dq, dk, dv)]
            if gi < ng - 1:
                dh = bwd_in(ops, P["od_w_in"][od], cols, D, dh, None, None, None, None, None, f"{tag}_in{gi}", dil=dil)
            else:
                dx0, dnm, dsc_m, dsh_m = bwd_in(ops, P["od_w_in"][od], cols, D, dh, x0, gain, sc_m, sh_m, dx1,
                                                f"{tag}_in{gi}", dil=dil)
        G["att_q_norm"], G["att_k_norm"] = dqn_sum, dkn_sum
    G["norm_mix"] = dnm[0]
    dmod = jnp.concatenate([dsh_m, dsc_m, dg_m, dsh_f, dsc_f, dg_f], axis=0)
    return dx0, dmod, G


_PER_LAYER = ("norm_mix", "norm_ffn", "ffn_conv_w", "ffn_conv_b")
_PER_EVEN = ("ev_w_in", "gdn_conv_w", "gdn_a_log", "gdn_dt_bias", "gdn_norm", "pool_w", "pool_scale")
_PER_ODD = ("att_q_norm", "att_k_norm")
_IN_PLACE = ("ffn_w_up", "ffn_w_down", "ev_w_out", "od_w_in", "od_w_out")


def device_step(x, mod, target, P, layer_shards=None):
    saved = []
    for i in range(DEPTH):
        nxt = layer_shards[i + 1] if (layer_shards is not None and i + 1 < DEPTH) else None
        x, sv, got = _layer_fwd(i, x, mod[i], P, nxt)
        if got is not None:
            add_layer_weights(P, got)
        saved.append(sv)
    loss, dx = loss_head(x, target, "loss")
    layer_grads, dmods = [None] * DEPTH, [None] * DEPTH
    bufs = {k: lax.empty((len(P[k]),) + tuple(P[k][0].shape), BF16) for k in _IN_PLACE}
    for i in reversed(range(DEPTH)):
        dx, dmods[i], layer_grads[i] = _layer_bwd(i, dx, saved[i], mod[i], P, bufs)
    G = dict(bufs)
    G.update({k: jnp.stack([layer_grads[i][k] for i in range(DEPTH)]) for k in _PER_LAYER})
    G.update({k: jnp.stack([layer_grads[i][k] for i in range(0, DEPTH, 2)]) for k in _PER_EVEN})
    G.update({k: jnp.stack([layer_grads[i][k] for i in range(1, DEPTH, 2)]) for k in _PER_ODD})
    return loss, dx, jnp.stack(dmods), G


_WEIGHTS = ("ada_w", "ada_b", "norm_mix", "norm_ffn", "ev_w_in", "ev_w_out", "gdn_conv_w", "gdn_a_log", "gdn_dt_bias",
            "gdn_norm", "pool_w", "pool_scale", "od_w_in", "od_w_out", "att_q_norm", "att_k_norm", "ffn_w_up",
            "ffn_conv_w", "ffn_conv_b", "ffn_w_down")
_COL_SHARDED = ("ev_w_in", "od_w_in", "ffn_w_up")
_ROW_SHARDED = ("ev_w_out", "od_w_out", "ffn_w_down")
_SMALL_SHARDED = ("gdn_conv_w", "ffn_conv_w")
_REPLICATED = ("ada_b", "norm_mix", "norm_ffn", "gdn_a_log", "gdn_dt_bias", "gdn_norm", "pool_w", "pool_scale",
               "att_q_norm", "att_k_norm", "ffn_conv_b")
PACK_LANES = 128
PACK_ROWS = 8


def _pack_rows(n):
    unit = PACK_LANES * PACK_ROWS
    return -(-n // unit) * PACK_ROWS


def _pack(arrays):
    parts = []
    for a in arrays:
        flat = a.reshape(-1).astype(F32)
        rows = _pack_rows(flat.shape[0])
        parts.append(jnp.pad(flat, (0, rows * PACK_LANES - flat.shape[0])).reshape(rows, PACK_LANES))
    return jnp.concatenate(parts, axis=0)


def _unpack(packed, shapes, lead=()):
    out, row = [], 0
    for s in shapes:
        n = math.prod(s)
        rows = _pack_rows(n)
        blk = packed[..., row:row + rows, :].reshape(lead + (rows * PACK_LANES,))
        out.append(blk[..., :n].reshape(lead + tuple(s)))
        row += rows
    return out


def _unshard_cols(g):
    _, L, R, n = g.shape
    return g.transpose(1, 2, 0, 3).reshape(L, R, N_DEV * n)


def _shard_cols(full):
    L, R, N = full.shape
    n = N // N_DEV
    return full.reshape(L * R, N_DEV, n).transpose(1, 0, 2)


def _shard_rows(full):
    L, R, C = full.shape
    r = R // N_DEV
    return full.reshape(L, N_DEV, r, C).transpose(1, 0, 2, 3).reshape(N_DEV, L * r, C)


def kernel(x, c, ada_w, ada_b, norm_mix, norm_ffn, ev_w_in, ev_w_out, gdn_conv_w, gdn_a_log, gdn_dt_bias, gdn_norm, pool_w, pool_scale, od_w_in, od_w_out, att_q_norm, att_k_norm, ffn_w_up, ffn_conv_w, ffn_conv_b, ffn_w_down, loss_target, m_ada_w, m_ada_b, m_norm_mix, m_norm_ffn, m_ev_w_in, m_ev_w_out, m_gdn_conv_w, m_gdn_a_log, m_gdn_dt_bias, m_gdn_norm, m_pool_w, m_pool_scale, m_od_w_in, m_od_w_out, m_att_q_norm, m_att_k_norm, m_ffn_w_up, m_ffn_conv_w, m_ffn_conv_b, m_ffn_w_down, v_ada_w, v_ada_b, v_norm_mix, v_norm_ffn, v_ev_w_in, v_ev_w_out, v_gdn_conv_w, v_gdn_a_log, v_gdn_dt_bias, v_gdn_norm, v_pool_w, v_pool_scale, v_od_w_in, v_od_w_out, v_att_q_norm, v_att_k_norm, v_ffn_w_up, v_ffn_conv_w, v_ffn_conv_b, v_ffn_w_down):
    args = locals()
    Wl = {k: args[k] for k in _WEIGHTS}
    Ml = {k: args["m_" + k] for k in _WEIGHTS}
    Vl = {k: args["v_" + k] for k in _WEIGHTS}
    mx, my, mc = _mesh_pos()
    dev = 4 * mx + 2 * my + mc
    T, D = x.shape[1], x.shape[2]
    x2d, tgt = x.reshape(T, D), loss_target.reshape(T, D)

    small_shapes = [c.shape] + [Wl[k].shape for k in _SMALL_SHARDED]
    big = list(_COL_SHARDED + _ROW_SHARDED)
    layer_shards = []
    for i in range(DEPTH):
        mixer = ("ev_w_in", "ev_w_out") if i % 2 == 0 else ("od_w_in", "od_w_out")
        shards = {k: Wl[k][i // 2].astype(BF16) for k in mixer}
        shards.update({k: Wl[k][i].astype(BF16) for k in ("ffn_w_up", "ffn_w_down")})
        layer_shards.append(shards)
    gathered = all_gather([_pack([c] + [Wl[k] for k in _SMALL_SHARDED])] + list(layer_shards[0].values()), "gather_w0")
    c_all, conv_g, conv_f = _unpack(gathered[0], small_shapes, lead=(N_DEV,))
    c_all = c_all.reshape(N_DEV, D)
    P = {k: Wl[k] for k in _REPLICATED}
    P["gdn_conv_w"], P["ffn_conv_w"] = _unshard_cols(conv_g), _unshard_cols(conv_f)
    add_layer_weights(P, dict(zip(layer_shards[0].keys(), gathered[1:])))

    n_ada = ada_w.shape[2]
    b_cols = lax.dynamic_slice_in_dim(ada_b, dev * n_ada, n_ada, axis=1)
    mod_cols = ada_fwd(c_all, ada_w, b_cols[:, None, :], "ada_fwd")
    mod_all, = all_gather([mod_cols], "gather_mod")
    mod = lax.dynamic_index_in_dim(mod_all, dev, axis=2, keepdims=False)
    mod = mod.transpose(1, 0, 2).reshape(DEPTH, 6, D)

    loss, dx, dmod, G = device_step(x2d, mod, tgt, P, layer_shards)
    loss = lax.psum(loss[0, 0], ("x", "y", "c"))

    G["ada_b"] = dmod.reshape(DEPTH, 6 * D)
    small = list(_REPLICATED) + list(_SMALL_SHARDED)
    parts_all, = all_gather([_pack([G[k] for k in small])], "gather_small")
    zeros = {k: jnp.zeros_like(G[k]) for k in _SMALL_SHARDED}
    packs = [_pack([src[k] for k in _REPLICATED] + [zeros[k] for k in _SMALL_SHARDED]) for src in (Wl, Ml, Vl)]
    res = adamw(*packs, parts_all, "adamw_small")
    shapes = [G[k].shape for k in small]
    out_g, out_d, out_m, out_v = ({k: a for k, a in zip(small, _unpack(r, shapes))} for r in res)
    dmod_all = _unpack(parts_all, shapes, lead=(N_DEV,))[0].reshape(N_DEV, DEPTH, 6 * D)
    dm_cols = lax.dynamic_slice_in_dim(dmod_all, dev * n_ada, n_ada, axis=2).transpose(1, 0, 2)
    g_ada = ada_bwd(c_all, dm_cols, "ada_bwd")

    def flat2(a):
        return a.reshape(-1, a.shape[-1])

    loc = {k: lax.dynamic_slice_in_dim(out_g[k], dev * Wl[k].shape[-1], Wl[k].shape[-1], axis=out_g[k].ndim - 1)
           for k in _SMALL_SHARDED}
    res = adamw(*[_pack([src[k] for k in _SMALL_SHARDED]) for src in (Wl, Ml, Vl)],
                _pack([loc[k] for k in _SMALL_SHARDED])[None], "adamw_conv")
    for dst, r in zip((out_g, out_d, out_m, out_v), res):
        dst.update(zip(_SMALL_SHARDED, _unpack(r, [Wl[k].shape for k in _SMALL_SHARDED])))
    res = adamw(flat2(ada_w), flat2(m_ada_w), flat2(v_ada_w), flat2(g_ada)[None], "adamw_ada")
    for dst, r in zip((out_g, out_d, out_m, out_v), res):
        dst["ada_w"] = r.reshape(ada_w.shape)

    stacked = [_shard_cols(G[k]) for k in _COL_SHARDED] + [_shard_rows(G[k]) for k in _ROW_SHARDED]
    got = exchange_pair(stacked, "rs_pair")
    c_idx = jnp.reshape(mc, (1,)).astype(jnp.int32)
    chip_parts = [pair_add(s, g, c_idx, f"rs_add_{k}") for s, g, k in zip(stacked, got, big)]
    parts = exchange_chips(chip_parts, "rs_chips")
    for k, p in zip(big, parts):
        res = adamw(flat2(Wl[k]), flat2(Ml[k]), flat2(Vl[k]), p, "adamw_" + k)
        for dst, r in zip((out_g, out_d, out_m, out_v), res):
            dst[k] = r.reshape(Wl[k].shape)

    return (loss, dx.reshape(x.shape), *[out_g[k] for k in _WEIGHTS], *[out_d[k] for k in _WEIGHTS],
            *[out_m[k] for k in _WEIGHTS], *[out_v[k] for k in _WEIGHTS])
```

```python
import functools
import math

import jax
import jax.numpy as jnp
from jax import lax
from jax.experimental import pallas as pl
from jax.experimental.pallas import tpu as pltpu

F32 = jnp.float32
BF16 = jnp.bfloat16
HI = lax.Precision.HIGHEST
MESH = pl.DeviceIdType.MESH

N_DEV = 8
RMS_EPS = 1e-6
DEPTH = 4
GDN_HEADS = 4
GDN_D = 128
GDN_CHUNK = 64
GDN_CONV = 4
POOL_WINDOWS = (2, 4, 8, 16)
DIL_PATTERNS = ((128, 1), (512, 4), (2048, 16))
ATT_HEADS = 8
ATT_DH = 128
ATT_BLOCK = 128
FFN_CONV = 3
ADAM_LR, ADAM_B1, ADAM_B2, ADAM_EPS, ADAM_WD, ADAM_STEP = 0.001, 0.9, 0.999, 1e-08, 0.01, 10

HALO = 16
NEG = -1e30
VMEM_LIMIT_BYTES = 56 * 1024 * 1024


def _cp(sem=None, **kw):
    return pltpu.CompilerParams(dimension_semantics=sem, vmem_limit_bytes=VMEM_LIMIT_BYTES, **kw)


def _sds(shape, dtype):
    return jax.ShapeDtypeStruct(tuple(shape), dtype)


def _dot(a, b):
    return jnp.dot(a.astype(BF16), b.astype(BF16), preferred_element_type=F32)


def _dot_nt(a, b):
    return lax.dot_general(a.astype(BF16), b.astype(BF16), (((1,), (1,)), ((), ())), preferred_element_type=F32)


def _dot_tn(a, b):
    return lax.dot_general(a.astype(BF16), b.astype(BF16), (((0,), (0,)), ((), ())), preferred_element_type=F32)


def _dot_hi(a, b):
    return jnp.dot(a, b, preferred_element_type=F32, precision=HI)


def _silu(x):
    return x * jax.nn.sigmoid(x)


def _modnorm(x, gain, sc, sh):
    y = x * lax.rsqrt(jnp.mean(x * x, axis=-1, keepdims=True) + RMS_EPS)
    return y * gain * (1.0 + sc) + sh


def _row_tile(T):
    return 512 if T % 512 == 0 else T


LANES = 128


def _deinterleave(val, scr, dil):
    tm, width = val.shape
    sub = tm // dil
    ncb = width // LANES
    for cb in range(ncb):
        scr[cb] = val[:, cb * LANES:(cb + 1) * LANES]
    return jnp.concatenate([jnp.concatenate([scr.at[cb][pl.ds(r, sub, stride=dil), :] for cb in range(ncb)], axis=1)
                            for r in range(dil)], axis=0)


def _interleave(val, scr, dil):
    tm, width = val.shape
    sub = tm // dil
    ncb = width // LANES
    for r in range(dil):
        for cb in range(ncb):
            scr.at[cb][pl.ds(r, sub, stride=dil), :] = val[r * sub:(r + 1) * sub, cb * LANES:(cb + 1) * LANES]
    return jnp.concatenate([scr[cb] for cb in range(ncb)], axis=1)


def norm_proj(x, gain, sc, sh, w, tn, name, cols=None, dil=1, qk_norms=None, gather=None):
    T, D = x.shape
    c0, ncol = cols if cols is not None else (0, w.shape[1] // tn)
    N = ncol * tn
    tm = 1024 if (ncol > 1 and dil == 1 and T % 1024 == 0) else _row_tile(T)
    sub = tm // dil
    n_g = len(gather) if gather else 0
    n_own_in = 5 + (2 if qk_norms is not None else 0)
    n_own_out = 2 + (1 if qk_norms is not None else 0)
    n_in, n_out = n_own_in + n_g, n_own_out + n_g
    n_scr = 1 + (1 if dil > 1 else 0)
    nsteps = (T // tm) * ncol
    Wd = tn // 3

    def body(*refs):
        x_ref, g_ref, sc_ref, sh_ref, w_ref = refs[:5]
        h_ref, o_ref = refs[n_in:n_in + 2]
        h_scr = refs[n_in + n_out]
        step = pl.program_id(0) * ncol + pl.program_id(1)
        if n_g:
            exch = _Gather(refs[n_own_in:n_in], refs[n_in + n_own_out:n_in + n_out], *refs[n_in + n_out + n_scr:])
            pl.when(step == 0)(exch.start)
            pl.when(step == nsteps // 2)(exch.forward)

        @pl.when(pl.program_id(1) == 0)
        def _():
            h = _modnorm(x_ref[...], g_ref[...], sc_ref[...], sh_ref[...])
            if dil > 1:
                h = _deinterleave(h, refs[n_in + n_out + 1], dil)
            h_scr[...] = h.astype(BF16)
            h_ref[...] = h.reshape(h_ref.shape).astype(BF16)
        res = jnp.dot(h_scr[...], w_ref[...], preferred_element_type=F32)
        o_ref[...] = res.reshape(o_ref.shape).astype(BF16)
        if qk_norms is not None:
            qk_ref = refs[n_in + 2]
            parts = []
            for t, scale in enumerate((ATT_DH ** -0.5, 1.0)):
                for hd in range(Wd // ATT_DH):
                    c = t * Wd + hd * ATT_DH
                    parts.append(_head_norm(res[:, c:c + ATT_DH], refs[5 + t][...]) * scale)
            qk_ref[...] = jnp.concatenate(parts, axis=1).reshape(qk_ref.shape).astype(BF16)
        if n_g:
            pl.when(step == nsteps - 1)(exch.finish)

    vec = pl.BlockSpec((1, D), lambda i, j: (0, 0))
    in_specs = [pl.BlockSpec((tm, D), lambda i, j: (i, 0)), vec, vec, vec, pl.BlockSpec((D, tn), lambda i, j: (0, c0 + j))]
    args = [x, gain, sc, sh, w]
    out_specs = [pl.BlockSpec((dil, sub, D), lambda i, j: (0, i, 0)), pl.BlockSpec((dil, sub, tn), lambda i, j: (0, i, j))]
    out_shape = [_sds((dil, T // dil, D), BF16), _sds((dil, T // dil, N), BF16)]
    if qk_norms is not None:
        assert ncol == 1
        in_specs += [pl.BlockSpec((1, ATT_DH), lambda i, j: (0, 0))] * 2
        args += list(qk_norms)
        out_specs.append(pl.BlockSpec((dil, sub, 2 * Wd), lambda i, j: (0, i, 0)))
        out_shape.append(_sds((dil, T // dil, 2 * Wd), BF16))
    scratch = [pltpu.VMEM((tm, D), BF16)] + ([pltpu.VMEM((D // LANES, tm, LANES), F32)] if dil > 1 else [])
    if n_g:
        in_specs += [_ANY] * n_g
        args += list(gather)
        out_specs += [_ANY] * n_g
        out_shape += [_sds((N_DEV,) + g.shape, g.dtype) for g in gather]
        scratch += _gather_scratch(n_g)
    res = pl.pallas_call(
        body, name=name, grid=(T // tm, ncol), in_specs=in_specs, out_specs=out_specs, out_shape=out_shape,
        scratch_shapes=scratch, compiler_params=_cp(("arbitrary", "arbitrary") if n_g else ("parallel", "arbitrary")),
    )(*args)
    if dil == 1 and qk_norms is None:
        return [r.reshape(T, -1) for r in res[:2]] + list(res[2:])
    return res


def matmul_nn(a, w, out_dtype, name):
    T, K = a.shape
    N = w.shape[1]
    tm = _row_tile(T)

    def body(a_ref, w_ref, o_ref):
        o_ref[...] = _dot(a_ref[...], w_ref[...]).astype(o_ref.dtype)

    return pl.pallas_call(
        body, name=name, grid=(T // tm,),
        in_specs=[pl.BlockSpec((tm, K), lambda i: (i, 0)), pl.BlockSpec((K, N), lambda i: (0, 0))],
        out_specs=pl.BlockSpec((tm, N), lambda i: (i, 0)),
        out_shape=_sds((T, N), out_dtype),
        compiler_params=_cp(("parallel",)),
    )(a, w)


def proj_res(a, w, x, gate, name):
    T, K = a.shape
    D = w.shape[1]
    tm = _row_tile(T)

    def body(a_ref, w_ref, x_ref, g_ref, o_ref, y_ref):
        y = jnp.dot(a_ref[...], w_ref[...], preferred_element_type=F32)
        y_ref[...] = y.astype(BF16)
        o_ref[...] = x_ref[...] + g_ref[...] * y

    return pl.pallas_call(
        body, name=name, grid=(T // tm,),
        in_specs=[pl.BlockSpec((tm, K), lambda i: (i, 0)), pl.BlockSpec((K, D), lambda i: (0, 0)),
                  pl.BlockSpec((tm, D), lambda i: (i, 0)), pl.BlockSpec((1, D), lambda i: (0, 0))],
        out_specs=[pl.BlockSpec((tm, D), lambda i: (i, 0)), pl.BlockSpec((tm, D), lambda i: (i, 0))],
        out_shape=[_sds((T, D), F32), _sds((T, D), BF16)],
        compiler_params=_cp(("parallel",)),
    )(a, w, x, gate)


def bwd_out(dx, gate, y, w, tk, name):
    T, D = dx.shape
    K = w.shape[0]
    tm = _row_tile(T)

    def body(dx_ref, g_ref, y_ref, w_ref, dy_ref, da_ref, dg_ref, dy_scr):
        i, j = pl.program_id(0), pl.program_id(1)

        @pl.when(j == 0)
        def _():
            dxv = dx_ref[...]
            dy = (dxv * g_ref[...]).astype(BF16)
            dy_scr[...] = dy
            dy_ref[...] = dy
            part = jnp.sum(dxv * y_ref[...].astype(F32), axis=0, keepdims=True)

            @pl.when(i == 0)
            def _():
                dg_ref[...] = part

            @pl.when(i > 0)
            def _():
                dg_ref[...] += part

        da_ref[...] = _dot_nt(dy_scr[...], w_ref[...]).astype(BF16)

    return pl.pallas_call(
        body, name=name, grid=(T // tm, K // tk),
        in_specs=[pl.BlockSpec((tm, D), lambda i, j: (i, 0)), pl.BlockSpec((1, D), lambda i, j: (0, 0)),
                  pl.BlockSpec((tm, D), lambda i, j: (i, 0)), pl.BlockSpec((tk, D), lambda i, j: (j, 0))],
        out_specs=[pl.BlockSpec((tm, D), lambda i, j: (i, 0)), pl.BlockSpec((tm, tk), lambda i, j: (i, j)),
                   pl.BlockSpec((1, D), lambda i, j: (0, 0))],
        out_shape=[_sds((T, D), BF16), _sds((T, K), BF16), _sds((1, D), F32)],
        scratch_shapes=[pltpu.VMEM((tm, D), BF16)],
        compiler_params=_cp(("arbitrary", "arbitrary")),
    )(dx, gate, y, w)


def bwd_in(a_list, w, col_blocks, tn, acc, x, gain, sc, sh, dx_res, name, dil=1):
    T = math.prod(a_list[0].shape[:-1])
    n_a = len(a_list)
    w_list = list(w) if isinstance(w, (list, tuple)) else [w] * n_a
    D = w_list[0].shape[0]
    tm = 256 if T % 256 == 0 else T
    sub = tm // dil
    tns = [tn if tn is not None else a.shape[-1] for a in a_list]
    nsteps = a_list[0].shape[-1] // tns[0]
    assert all(a.shape[-1] == nsteps * t for a, t in zip(a_list, tns))
    final = x is not None
    has_acc = acc is not None

    def body(*refs):
        a_refs = refs[:n_a]
        w_refs = refs[n_a:2 * n_a]
        pos = 2 * n_a
        acc_ref = refs[pos] if has_acc else None
        pos += int(has_acc)
        if final:
            x_ref, g_ref, sc_ref, sh_ref, dxr_ref = refs[pos:pos + 5]
            pos += 5
            dx_ref, dg_ref, dsc_ref, dsh_ref = refs[pos:pos + 4]
            pos += 4
        else:
            dh_ref = refs[pos]
            pos += 1
        acc_scr = refs[pos]
        i, j = pl.program_id(0), pl.program_id(1)
        part = _dot_nt(a_refs[0][...].reshape(tm, tns[0]), w_refs[0][...])
        for k in range(1, n_a):
            part += _dot_nt(a_refs[k][...].reshape(tm, tns[k]), w_refs[k][...])

        @pl.when(j == 0)
        def _():
            acc_scr[...] = part

        @pl.when(j > 0)
        def _():
            acc_scr[...] += part

        @pl.when(j == nsteps - 1)
        def _():
            dh = acc_scr[...]
            if dil > 1:
                dh = _interleave(dh, refs[pos + 1], dil)
            if has_acc:
                dh = dh + acc_ref[...]
            if not final:
                dh_ref[...] = dh
                return
            _, vjp = jax.vjp(_modnorm, x_ref[...], g_ref[...], sc_ref[...], sh_ref[...])
            dxn, dg, dsc, dsh = vjp(dh)
            dx_ref[...] = dxr_ref[...] + dxn

            @pl.when(i == 0)
            def _():
                dg_ref[...] = dg
                dsc_ref[...] = dsc
                dsh_ref[...] = dsh

            @pl.when(i > 0)
            def _():
                dg_ref[...] += dg
                dsc_ref[...] += dsc
                dsh_ref[...] += dsh

    row = pl.BlockSpec((tm, D), lambda i, j: (i, 0))
    vec = pl.BlockSpec((1, D), lambda i, j: (0, 0))
    if dil == 1:
        in_specs = [pl.BlockSpec((tm, t), lambda i, j: (i, j)) for t in tns]
    else:
        in_specs = [pl.BlockSpec((dil, sub, t), lambda i, j: (0, i, j)) for t in tns]
    in_specs += [pl.BlockSpec((D, t), functools.partial(lambda i, j, c0: (0, c0 + j), c0=c0))
                 for t, c0 in zip(tns, col_blocks)]
    args = list(a_list) + w_list
    if has_acc:
        in_specs.append(row)
        args.append(acc)
    if final:
        in_specs += [row, vec, vec, vec, row]
        args += [x, gain, sc, sh, dx_res]
        out_specs = [row, vec, vec, vec]
        out_shape = [_sds((T, D), F32)] + [_sds((1, D), F32)] * 3
    else:
        out_specs = row
        out_shape = _sds((T, D), F32)
    return pl.pallas_call(
        body, name=name, grid=(T // tm, nsteps), in_specs=in_specs, out_specs=out_specs, out_shape=out_shape,
        scratch_shapes=[pltpu.VMEM((tm, D), F32)] + ([pltpu.VMEM((D // LANES, tm, LANES), F32)] if dil > 1 else []),
        compiler_params=_cp(("arbitrary", "arbitrary")),
    )(*args)


def matmul_tn(a, b, tk, tn, name, into=None, layer=0, col0=0):
    T, K = a.shape
    N = b.shape[1]
    tt = 2048 if T % 2048 == 0 else T
    nt = T // tt

    def body(a_ref, b_ref, *rest):
        o_ref, acc = rest[-2:]
        part = _dot_tn(a_ref[...], b_ref[...])

        @pl.when(pl.program_id(2) == 0)
        def _():
            acc[...] = part

        @pl.when(pl.program_id(2) > 0)
        def _():
            acc[...] += part

        @pl.when(pl.program_id(2) == nt - 1)
        def _():
            o_ref[...] = acc[...].astype(BF16)

    in_specs = [pl.BlockSpec((tt, tk), lambda k, n, t: (t, k)), pl.BlockSpec((tt, tn), lambda k, n, t: (t, n))]
    common = dict(name=name, grid=(K // tk, N // tn, nt), scratch_shapes=[pltpu.VMEM((tk, tn), F32)],
                  compiler_params=_cp(("parallel", "parallel", "arbitrary")))
    if into is None:
        return pl.pallas_call(body, in_specs=in_specs, out_specs=pl.BlockSpec((tk, tn), lambda k, n, t: (k, n)),
                              out_shape=_sds((K, N), BF16), **common)(a, b)
    return pl.pallas_call(
        body, in_specs=in_specs + [_ANY], out_specs=pl.BlockSpec((None, tk, tn), lambda k, n, t: (layer, k, col0 + n)),
        out_shape=_sds(into.shape, BF16), input_output_aliases={2: 0}, **common)(a, b, into)


def _halo_specs(tm, tc, col_of):
    per = tm // HALO

    def prev(j, i):
        return (jnp.maximum(i * per - 1, 0), col_of(j))

    def nxt(j, i, last):
        return (jnp.minimum((i + 1) * per, last), col_of(j))

    return prev, nxt, per


def _shift_down(ext, s):
    return pltpu.roll(ext, s, 0)


def _shift_up(ext, s):
    return pltpu.roll(ext, ext.shape[0] - s, 0)


def _conv_ext(ext, w):
    K = w.shape[0]
    out = w[K - 1:K] * ext
    for s in range(1, K):
        out += w[K - 1 - s:K - s] * _shift_down(ext, s)
    return out


def _conv_t_ext(dext, w):
    K = w.shape[0]
    out = w[K - 1:K] * dext
    for s in range(1, K):
        out += w[K - 1 - s:K - s] * _shift_up(dext, s)
    return out


def _conv_dw(dc, ext, K, tm):
    rowid = lax.broadcasted_iota(jnp.int32, (8, dc.shape[1]), 0)
    out = jnp.zeros((8, dc.shape[1]), F32)
    for j in range(K):
        s = K - 1 - j
        xs = ext if s == 0 else _shift_down(ext, s)
        out = jnp.where(rowid == j, jnp.sum(dc * xs[HALO:HALO + tm], axis=0, keepdims=True), out)
    return out


def _accum(ref, val, first):
    @pl.when(first)
    def _():
        ref[...] = val

    @pl.when(jnp.logical_not(first))
    def _():
        ref[...] += val


def ffn_mid_fwd(up, conv_w, conv_b, name):
    T, two_f = up.shape
    F = two_f // 2
    tm = _row_tile(T)
    tc = F // 2
    nct = F // tc
    prev, _, per = _halo_specs(tm, tc, lambda j: j)

    def body(g_ref, gp_ref, v_ref, w_ref, b_ref, a_ref):
        i = pl.program_id(1)
        gp = jnp.where(i > 0, gp_ref[...].astype(F32), 0.0)
        ext = jnp.concatenate([gp, g_ref[...].astype(F32)], axis=0)
        c = _conv_ext(ext, w_ref[...])[HALO:] + b_ref[...]
        a_ref[...] = (_silu(c) * v_ref[...].astype(F32)).astype(BF16)

    return pl.pallas_call(
        body, name=name, grid=(nct, T // tm),
        in_specs=[pl.BlockSpec((tm, tc), lambda j, i: (i, j)), pl.BlockSpec((HALO, tc), prev),
                  pl.BlockSpec((tm, tc), lambda j, i: (i, j + nct)),
                  pl.BlockSpec((FFN_CONV, tc), lambda j, i: (0, j)), pl.BlockSpec((1, tc), lambda j, i: (0, j))],
        out_specs=pl.BlockSpec((tm, tc), lambda j, i: (i, j)),
        out_shape=_sds((T, F), BF16),
        compiler_params=_cp(("parallel", "arbitrary")),
    )(up, up, up, conv_w, conv_b)


def ffn_mid_bwd(up, da, conv_w, conv_b, name):
    T, two_f = up.shape
    F = two_f // 2
    tm = _row_tile(T)
    tc = F // 2
    nct = F // tc
    nrow = T // tm
    prev, nxt, per = _halo_specs(tm, tc, lambda j: j)
    last = T // HALO - 1
    nxt_g = functools.partial(nxt, last=last)

    def nxt_v(j, i):
        return (jnp.minimum((i + 1) * per, last), j + nct)

    def body(g_ref, gp_ref, gn_ref, v_ref, vn_ref, da_ref, dan_ref, w_ref, b_ref, dg_ref, dv_ref, dw_ref, db_ref):
        i = pl.program_id(1)
        w = w_ref[...]
        gp = jnp.where(i > 0, gp_ref[...].astype(F32), 0.0)
        inside = i < nrow - 1
        ext = jnp.concatenate([gp, g_ref[...].astype(F32), gn_ref[...].astype(F32)], axis=0)
        zero = jnp.zeros((HALO, tc), F32)
        v_ext = jnp.concatenate([zero, v_ref[...].astype(F32), vn_ref[...].astype(F32)], axis=0)
        da_ext = jnp.concatenate([zero, da_ref[...].astype(F32), jnp.where(inside, dan_ref[...].astype(F32), 0.0)], axis=0)
        c = _conv_ext(ext, w) + b_ref[...]
        sg = jax.nn.sigmoid(c)
        sil = c * sg
        dc = da_ext * v_ext * (sg * (1.0 + c * (1.0 - sg)))
        dv_ref[...] = (da_ext * sil)[HALO:HALO + tm].astype(BF16)
        dg_ref[...] = _conv_t_ext(dc, w)[HALO:HALO + tm].astype(BF16)
        dcm = dc[HALO:HALO + tm]
        _accum(dw_ref, _conv_dw(dcm, ext, FFN_CONV, tm), i == 0)
        _accum(db_ref, jnp.sum(dcm, axis=0, keepdims=True), i == 0)

    main = lambda j, i: (i, j)
    main_v = lambda j, i: (i, j + nct)
    return pl.pallas_call(
        body, name=name, grid=(nct, nrow),
        in_specs=[pl.BlockSpec((tm, tc), main), pl.BlockSpec((HALO, tc), prev), pl.BlockSpec((HALO, tc), nxt_g),
                  pl.BlockSpec((tm, tc), main_v), pl.BlockSpec((HALO, tc), nxt_v),
                  pl.BlockSpec((tm, tc), main), pl.BlockSpec((HALO, tc), nxt_g),
                  pl.BlockSpec((FFN_CONV, tc), lambda j, i: (0, j)), pl.BlockSpec((1, tc), lambda j, i: (0, j))],
        out_specs=[pl.BlockSpec((tm, tc), main), pl.BlockSpec((tm, tc), main),
                   pl.BlockSpec((8, tc), lambda j, i: (0, j)), pl.BlockSpec((1, tc), lambda j, i: (0, j))],
        out_shape=[_sds((T, F), BF16), _sds((T, F), BF16), _sds((8, F), F32), _sds((1, F), F32)],
        compiler_params=_cp(("parallel", "arbitrary")),
    )(up, up, up, up, up, da, da, conv_w, conv_b)


def _l2n(x):
    return x * lax.rsqrt(jnp.sum(x * x, axis=-1, keepdims=True) + RMS_EPS)


def _qkv_tok(c, normed):
    s = _silu(c)
    n = jnp.concatenate([_l2n(s[:, h * GDN_D:(h + 1) * GDN_D]) for h in range(GDN_HEADS)], axis=1)
    return jnp.where(normed, n, s)


def gdn_prep_fwd(proj, conv_w, name):
    T = proj.shape[0]
    W = GDN_HEADS * GDN_D
    tm = _row_tile(T)
    prev, _, per = _halo_specs(tm, W, lambda j: j)

    def body(x_ref, xp_ref, w_ref, o_ref):
        j, i = pl.program_id(0), pl.program_id(1)
        xp = jnp.where(i > 0, xp_ref[...].astype(F32), 0.0)
        ext = jnp.concatenate([xp, x_ref[...].astype(F32)], axis=0)
        c = _conv_ext(ext, w_ref[...])[HALO:]
        o_ref[...] = _qkv_tok(c, j < 2)

    return pl.pallas_call(
        body, name=name, grid=(3, T // tm),
        in_specs=[pl.BlockSpec((tm, W), lambda j, i: (i, j)), pl.BlockSpec((HALO, W), prev),
                  pl.BlockSpec((GDN_CONV, W), lambda j, i: (0, j))],
        out_specs=pl.BlockSpec((tm, W), lambda j, i: (i, j)),
        out_shape=_sds((T, 3 * W), F32),
        compiler_params=_cp(("parallel", "arbitrary")),
    )(proj, proj, conv_w)


def gdn_prep_bwd(proj, dqkv, conv_w, name):
    T = proj.shape[0]
    W = GDN_HEADS * GDN_D
    tm = _row_tile(T)
    nrow = T // tm
    prev, nxt, per = _halo_specs(tm, W, lambda j: j)
    nxt = functools.partial(nxt, last=T // HALO - 1)

    def body(x_ref, xp_ref, xn_ref, d_ref, dn_ref, w_ref, dx_ref, dw_ref):
        j, i = pl.program_id(0), pl.program_id(1)
        w = w_ref[...]
        xp = jnp.where(i > 0, xp_ref[...].astype(F32), 0.0)
        ext = jnp.concatenate([xp, x_ref[...].astype(F32), xn_ref[...].astype(F32)], axis=0)
        d_ext = jnp.concatenate([jnp.zeros((HALO, W), F32), d_ref[...],
                                 jnp.where(i < nrow - 1, dn_ref[...], 0.0)], axis=0)
        c = _conv_ext(ext, w)
        _, vjp = jax.vjp(lambda cc: _qkv_tok(cc, j < 2), c)
        dc, = vjp(d_ext)
        dx_ref[...] = _conv_t_ext(dc, w)[HALO:HALO + tm].astype(BF16)
        _accum(dw_ref, _conv_dw(dc[HALO:HALO + tm], ext, GDN_CONV, tm), i == 0)

    main = lambda j, i: (i, j)
    return pl.pallas_call(
        body, name=name, grid=(3, nrow),
        in_specs=[pl.BlockSpec((tm, W), main), pl.BlockSpec((HALO, W), prev), pl.BlockSpec((HALO, W), nxt),
                  pl.BlockSpec((tm, W), main), pl.BlockSpec((HALO, W), nxt),
                  pl.BlockSpec((GDN_CONV, W), lambda j, i: (0, j))],
        out_specs=[pl.BlockSpec((tm, W), main), pl.BlockSpec((8, W), lambda j, i: (0, j))],
        out_shape=[_sds((T, 3 * W), BF16), _sds((8, 3 * W), F32)],
        compiler_params=_cp(("parallel", "arbitrary")),
    )(proj, proj, proj, dqkv, dqkv, conv_w)


def _gate_tok(ba, a_log, dt_bias):
    z = ba + dt_bias
    softplus = jnp.maximum(z, 0.0) + jnp.log(1.0 + jnp.exp(-jnp.abs(z)))
    lane = lax.broadcasted_iota(jnp.int32, ba.shape, 1)
    raw = jnp.where(lane < GDN_HEADS, jax.nn.sigmoid(ba), -jnp.exp(a_log) * softplus)
    n = ba.shape[0]
    r = lax.broadcasted_iota(jnp.int32, (n, n), 0)
    c = lax.broadcasted_iota(jnp.int32, (n, n), 1)
    in_chunk_before = (r >= c) & ((r - c) <= (r & (GDN_CHUNK - 1)))
    return jnp.where(lane < GDN_HEADS, raw, _dot_hi(in_chunk_before.astype(F32), raw))


def gate_fwd(ba, a_log, dt_bias, name):
    T, L = ba.shape
    tm = _row_tile(T)

    def body(ba_ref, al_ref, dt_ref, o_ref):
        o_ref[...] = _gate_tok(ba_ref[...], al_ref[...], dt_ref[...])

    vec = pl.BlockSpec((1, L), lambda i: (0, 0))
    return pl.pallas_call(
        body, name=name, grid=(T // tm,),
        in_specs=[pl.BlockSpec((tm, L), lambda i: (i, 0)), vec, vec],
        out_specs=pl.BlockSpec((tm, L), lambda i: (i, 0)), out_shape=_sds((T, L), F32),
        compiler_params=_cp(("parallel",)),
    )(ba, a_log, dt_bias)


def gate_bwd(ba, a_log, dt_bias, dout_a, dout_b, name):
    T, L = ba.shape
    tm = _row_tile(T)

    def body(ba_ref, al_ref, dt_ref, d_ref, d2_ref, dba_ref, dal_ref, ddt_ref):
        _, vjp = jax.vjp(_gate_tok, ba_ref[...], al_ref[...], dt_ref[...])
        dba, dal, ddt = vjp(d_ref[...] + d2_ref[...])
        dba_ref[...] = dba.astype(BF16)
        first = pl.program_id(0) == 0
        _accum(dal_ref, dal, first)
        _accum(ddt_ref, ddt, first)

    vec = pl.BlockSpec((1, L), lambda i: (0, 0))
    row = pl.BlockSpec((tm, L), lambda i: (i, 0))
    return pl.pallas_call(
        body, name=name, grid=(T // tm,),
        in_specs=[row, vec, vec, row, row], out_specs=[row, vec, vec],
        out_shape=[_sds((T, L), BF16), _sds((1, L), F32), _sds((1, L), F32)],
        compiler_params=_cp(("arbitrary",)),
    )(ba, a_log, dt_bias, dout_a, dout_b)


_B_NN = (((2,), (1,)), ((0,), (0,)))
_B_NT = (((2,), (2,)), ((0,), (0,)))
GDN_GROUP = 8


def _split_bf16(a):
    hi = a.astype(BF16)
    return hi, (a - hi.astype(F32)).astype(BF16)


def _bdot3(a, b):
    ah, al = _split_bf16(a)
    bh, bl = _split_bf16(b)
    dg = lambda x, y: lax.dot_general(x, y, _B_NN, preferred_element_type=F32)
    return dg(ah, bh) + (dg(ah, bl) + dg(al, bh))


def _exact_ones_dot(x, dims, ones_shape, ones_first):
    ones = jnp.ones(ones_shape, BF16)
    out = None
    for _ in range(3):
        piece = x.astype(BF16)
        ops = (ones, piece) if ones_first else (piece, ones)
        term = lax.dot_general(*ops, dims, preferred_element_type=F32)
        out = term if out is None else out + term
        x = x - piece.astype(F32)
    return out


@jax.custom_vjp
def _rows_from_cols(col):
    B, C, _ = col.shape
    eye = (lax.broadcasted_iota(jnp.int32, (1, C, C), 1) == lax.broadcasted_iota(jnp.int32, (1, C, C), 2)).astype(F32)
    return _exact_ones_dot(col * eye, _B_NN, (B, C, C), True)


def _rows_from_cols_fwd(col):
    return _rows_from_cols(col), None


def _rows_from_cols_bwd(_, ct):
    B, C, _ = ct.shape
    sums = _exact_ones_dot(ct, (((1,), (1,)), ((0,), (0,))), (B, C, LANES), False)
    return (sums[:, :, :1],)


_rows_from_cols.defvjp(_rows_from_cols_fwd, _rows_from_cols_bwd)


def _gdn_prep(q, k, v, gcol, bcol):
    B, C, dk = q.shape
    r = lax.broadcasted_iota(jnp.int32, (1, C, C), 1)
    cidx = lax.broadcasted_iota(jnp.int32, (1, C, C), 2)
    causal = r >= cidx
    strict = r > cidx
    decay = jnp.where(causal, jnp.exp(jnp.where(causal, gcol - _rows_from_cols(gcol), 0.0)), 0.0)
    qs = q * (dk ** -0.5)
    kb = k * bcol
    kk = lax.dot_general(kb.astype(BF16), k.astype(BF16), _B_NT, preferred_element_type=F32)
    L = jnp.where(strict, kk * decay, 0.0)
    tinv = (r == cidx).astype(F32) - L
    p = L
    n = 2
    while n < C:
        p = _bdot3(p, p)
        tinv = tinv + _bdot3(tinv, p)
        n *= 2
    egc = jnp.exp(gcol)
    u = _bdot3(tinv, v * bcol)
    w = _bdot3(tinv, kb * egc)
    qk = lax.dot_general(qs.astype(BF16), k.astype(BF16), _B_NT, preferred_element_type=F32)
    intra = jnp.where(causal, qk * decay, 0.0)
    last = lax.broadcasted_iota(jnp.int32, (1, C, 1), 1) == C - 1
    gt = jnp.sum(jnp.where(last, gcol, 0.0), axis=1, keepdims=True)
    return w, u, qs * egc, k * jnp.exp(gt - gcol), intra


def _gdn_scan(S, w, u, qg, kdec, intra, gcol):
    C = w.shape[0]
    last = lax.broadcasted_iota(jnp.int32, (C, 1), 0) == C - 1
    gt = jnp.sum(jnp.where(last, gcol, 0.0), axis=0, keepdims=True)
    v_new = u - _dot(w, S)
    o = _dot(qg, S) + _dot(intra, v_new)
    return S * jnp.exp(gt) + _dot_tn(kdec, v_new), o


def _gdn_specs(T, rev=False):
    H, D, C, B = GDN_HEADS, GDN_D, GDN_CHUNK, GDN_GROUP
    nsteps = T // (B * C)
    at = (lambda n: nsteps - 1 - n) if rev else (lambda n: n)
    rows = lambda w: pl.BlockSpec((B * C, w), lambda n: (at(n), 0))
    sq = pl.BlockSpec((B, H, C, C), lambda n: (at(n), 0, 0, 0))
    st = pl.BlockSpec((B, H, D, D), lambda n: (at(n), 0, 0, 0))
    return rows, sq, st, nsteps


def _gate_cols(gate_ref, h, rows=slice(None)):
    return gate_ref[rows, h:h + 1], gate_ref[rows, GDN_HEADS + h:GDN_HEADS + h + 1]


def gdn_prep_chunks_fwd(qkv, gate, name, gather=None):
    T = qkv.shape[0]
    H, D, C, B = GDN_HEADS, GDN_D, GDN_CHUNK, GDN_GROUP
    W = H * D
    rows, sq, _, nsteps = _gdn_specs(T)
    n_g = len(gather) if gather else 0

    def body(*refs):
        x_ref, g_ref = refs[:2]
        w_ref, u_ref, qg_ref, kd_ref, in_ref = refs[2 + n_g:7 + n_g]
        if n_g:
            exch = _Gather(refs[2:2 + n_g], refs[7 + n_g:7 + 2 * n_g], *refs[7 + 2 * n_g:])
            step = pl.program_id(0)
            pl.when(step == 0)(exch.start)
            pl.when(step == nsteps // 2)(exch.forward)
        for h in range(H):
            q, k, v = [x_ref[:, j * W + h * D:j * W + (h + 1) * D].reshape(B, C, D) for j in range(3)]
            bcol, gcol = [c.reshape(B, C, 1) for c in _gate_cols(g_ref, h)]
            outs = _gdn_prep(q, k, v, gcol, bcol)
            for ref, val in zip((w_ref, u_ref, qg_ref, kd_ref), outs[:4]):
                ref[:, h * D:(h + 1) * D] = val.reshape(B * C, D).astype(ref.dtype)
            in_ref[:, h] = outs[4].astype(BF16)
        if n_g:
            pl.when(step == nsteps - 1)(exch.finish)

    gather = list(gather) if gather else []
    return pl.pallas_call(
        body, name=name, grid=(nsteps,),
        in_specs=[rows(3 * W), rows(LANES)] + [_ANY] * n_g,
        out_specs=[rows(W), rows(W), rows(W), rows(W), sq] + [_ANY] * n_g,
        out_shape=[_sds((T, W), BF16), _sds((T, W), F32), _sds((T, W), BF16), _sds((T, W), BF16),
                   _sds((T // C, H, C, C), BF16)] + [_sds((N_DEV,) + g.shape, g.dtype) for g in gather],
        scratch_shapes=_gather_scratch(n_g) if n_g else [],
        compiler_params=_cp(("arbitrary",) if n_g else ("parallel",)),
    )(qkv, gate, *gather)


def gdn_prep_chunks_bwd(qkv, gate, dw, du, dqg, dkd, dintra, name):
    T = qkv.shape[0]
    H, D, C, B = GDN_HEADS, GDN_D, GDN_CHUNK, GDN_GROUP
    W = H * D
    rows, sq, _, nsteps = _gdn_specs(T)

    def body(x_ref, g_ref, dw_ref, du_ref, dqg_ref, dkd_ref, din_ref, dx_ref, dg_ref):
        dg_ref[...] = jnp.zeros_like(dg_ref)
        for h in range(H):
            q, k, v = [x_ref[:, j * W + h * D:j * W + (h + 1) * D].reshape(B, C, D) for j in range(3)]
            bcol, gcol = [c.reshape(B, C, 1) for c in _gate_cols(g_ref, h)]
            _, vjp = jax.vjp(_gdn_prep, q, k, v, gcol, bcol)
            cots = tuple(r[:, h * D:(h + 1) * D].reshape(B, C, D) for r in (dw_ref, du_ref, dqg_ref, dkd_ref))
            dq, dk, dv, dgc, db = vjp(cots + (din_ref[:, h],))
            for j, val in enumerate((dq, dk, dv)):
                dx_ref[:, j * W + h * D:j * W + (h + 1) * D] = val.reshape(B * C, D)
            dg_ref[:, h:h + 1] = db.reshape(B * C, 1)
            dg_ref[:, H + h:H + h + 1] = dgc.reshape(B * C, 1)

    return pl.pallas_call(
        body, name=name, grid=(nsteps,),
        in_specs=[rows(3 * W), rows(LANES), rows(W), rows(W), rows(W), rows(W), sq],
        out_specs=[rows(3 * W), rows(LANES)],
        out_shape=[_sds((T, 3 * W), F32), _sds((T, LANES), F32)],
        compiler_params=_cp(("parallel",)),
    )(qkv, gate, dw, du, dqg, dkd, dintra)


def gdn_fwd(w, u, qg, kdec, intra, gate, name):
    T = w.shape[0]
    H, D, C, B = GDN_HEADS, GDN_D, GDN_CHUNK, GDN_GROUP
    W = H * D
    rows, sq, st, nsteps = _gdn_specs(T)

    def body(w_ref, u_ref, qg_ref, kd_ref, in_ref, g_ref, o_ref, s_ref, s_scr):
        @pl.when(pl.program_id(0) == 0)
        def _():
            s_scr[...] = jnp.zeros_like(s_scr)

        def chunk(cb, carry):
            rs = pl.ds(pl.multiple_of(cb * C, C), C)
            for h in range(H):
                cs = slice(h * D, (h + 1) * D)
                S = s_scr[h]
                s_ref[cb, h] = S
                S_new, o = _gdn_scan(S, w_ref[rs, cs], u_ref[rs, cs], qg_ref[rs, cs], kd_ref[rs, cs], in_ref[cb, h],
                                     _gate_cols(g_ref, h, rs)[1])
                s_scr[h] = S_new
                o_ref[rs, cs] = o
            return carry

        lax.fori_loop(0, B, chunk, 0)

    return pl.pallas_call(
        body, name=name, grid=(nsteps,),
        in_specs=[rows(W), rows(W), rows(W), rows(W), sq, rows(LANES)],
        out_specs=[rows(W), st],
        out_shape=[_sds((T, W), F32), _sds((T // C, H, D, D), F32)],
        scratch_shapes=[pltpu.VMEM((H, D, D), F32)],
        compiler_params=_cp(("arbitrary",)),
    )(w, u, qg, kdec, intra, gate)


def gdn_bwd(w, u, qg, kdec, intra, gate, states, do, name):
    T = w.shape[0]
    H, D, C, B = GDN_HEADS, GDN_D, GDN_CHUNK, GDN_GROUP
    W = H * D
    rows, sq, st, nsteps = _gdn_specs(T, rev=True)

    def body(w_ref, u_ref, qg_ref, kd_ref, in_ref, g_ref, s_ref, do_ref,
             dw_ref, du_ref, dqg_ref, dkd_ref, din_ref, dg_ref, ds_scr):
        @pl.when(pl.program_id(0) == 0)
        def _():
            ds_scr[...] = jnp.zeros_like(ds_scr)
        dg_ref[...] = jnp.zeros_like(dg_ref)

        def chunk(t, carry):
            cb = B - 1 - t
            rs = pl.ds(pl.multiple_of(cb * C, C), C)
            for h in range(H):
                cs = slice(h * D, (h + 1) * D)
                _, vjp = jax.vjp(_gdn_scan, s_ref[cb, h], w_ref[rs, cs].astype(F32), u_ref[rs, cs],
                                 qg_ref[rs, cs].astype(F32), kd_ref[rs, cs].astype(F32), in_ref[cb, h].astype(F32),
                                 _gate_cols(g_ref, h, rs)[1])
                dS, dw, du, dqg, dkd, din, dgc = vjp((ds_scr[h], do_ref[rs, cs]))
                ds_scr[h] = dS
                dw_ref[rs, cs] = dw
                du_ref[rs, cs] = du
                dqg_ref[rs, cs] = dqg
                dkd_ref[rs, cs] = dkd
                din_ref[cb, h] = din
                dg_ref[rs, H + h:H + h + 1] = dgc
            return carry

        lax.fori_loop(0, B, chunk, 0)

    return pl.pallas_call(
        body, name=name, grid=(nsteps,),
        in_specs=[rows(W), rows(W), rows(W), rows(W), sq, rows(LANES), st, rows(W)],
        out_specs=[rows(W), rows(W), rows(W), rows(W), sq, rows(LANES)],
        out_shape=[_sds((T, W), F32)] * 4 + [_sds((T // C, H, C, C), F32), _sds((T, LANES), F32)],
        scratch_shapes=[pltpu.VMEM((H, D, D), F32)],
        compiler_params=_cp(("arbitrary",)),
    )(w, u, qg, kdec, intra, gate, states, do)


def _gated_norm(o, z, gain):
    outs = []
    for h in range(GDN_HEADS):
        oh = o[:, h * GDN_D:(h + 1) * GDN_D]
        y = oh * lax.rsqrt(jnp.mean(oh * oh, axis=-1, keepdims=True) + RMS_EPS) * gain
        outs.append(y * _silu(z[:, h * GDN_D:(h + 1) * GDN_D]))
    return jnp.concatenate(outs, axis=1)


def _window_sums(ext, shift):
    outs = []
    s = ext
    step = 1
    for gi, win in enumerate(POOL_WINDOWS):
        while step < win:
            s = s + shift(s, step)
            step *= 2
        outs.append(s[:, gi * GDN_D:(gi + 1) * GDN_D])
    return jnp.concatenate(outs, axis=1)


def _pool_counts(t0, rows, width):
    t1 = (t0 + 1 + lax.broadcasted_iota(jnp.int32, (rows, width), 0)).astype(F32)
    lane = lax.broadcasted_iota(jnp.int32, (rows, width), 1)
    win = jnp.full((rows, width), float(POOL_WINDOWS[-1]), F32)
    for gi in reversed(range(len(POOL_WINDOWS) - 1)):
        win = jnp.where(lane < (gi + 1) * GDN_D, float(POOL_WINDOWS[gi]), win)
    return jnp.minimum(t1, win)


def even_post_fwd(o, proj, gdn_norm, pool_w, pool_scale, name):
    T = o.shape[0]
    W = GDN_HEADS * GDN_D
    tm = _row_tile(T)
    per = tm // HALO

    def body(o_ref, z_ref, p_ref, pp_ref, gn_ref, pw_ref, ps_ref, out_ref):
        i = pl.program_id(0)
        out_ref[:, :W] = _gated_norm(o_ref[...], z_ref[...].astype(F32), gn_ref[...]).astype(BF16)
        pp = jnp.where(i > 0, pp_ref[...].astype(F32), 0.0)
        ext = jnp.concatenate([pp, p_ref[...].astype(F32)], axis=0)
        pooled = (_window_sums(ext, _shift_down)[HALO:] / _pool_counts(i * tm, tm, W)) - ext[HALO:]
        for gi in range(len(POOL_WINDOWS)):
            sl = slice(gi * GDN_D, (gi + 1) * GDN_D)
            y = _dot(pooled[:, sl], pw_ref[gi]) * ps_ref[:, sl]
            out_ref[:, W + gi * GDN_D:W + (gi + 1) * GDN_D] = y.astype(BF16)

    return pl.pallas_call(
        body, name=name, grid=(T // tm,),
        in_specs=[pl.BlockSpec((tm, W), lambda i: (i, 0)), pl.BlockSpec((tm, W), lambda i: (i, 3)),
                  pl.BlockSpec((tm, W), lambda i: (i, 4)),
                  pl.BlockSpec((HALO, W), lambda i: (jnp.maximum(i * per - 1, 0), 4)),
                  pl.BlockSpec((1, GDN_D), lambda i: (0, 0)),
                  pl.BlockSpec((len(POOL_WINDOWS), GDN_D, GDN_D), lambda i: (0, 0, 0)),
                  pl.BlockSpec((1, W), lambda i: (0, 0))],
        out_specs=pl.BlockSpec((tm, 2 * W), lambda i: (i, 0)),
        out_shape=_sds((T, 2 * W), BF16),
        compiler_params=_cp(("arbitrary",)),
    )(o, proj, proj, proj, gdn_norm, pool_w, pool_scale)


def even_post_bwd(o, proj, gdn_norm, pool_w, pool_scale, dcat, name):
    T = o.shape[0]
    W = GDN_HEADS * GDN_D
    G = len(POOL_WINDOWS)
    tm = _row_tile(T)
    per = tm // HALO
    nrow = T // tm
    last = T // HALO - 1

    def body(o_ref, z_ref, p_ref, pp_ref, gn_ref, pw_ref, ps_ref, d_ref, dn_ref,
             do_ref, dzp_ref, dgn_ref, dpw_ref, dps_ref):
        i = pl.program_id(0)
        first = i == 0
        _, vjp = jax.vjp(_gated_norm, o_ref[...], z_ref[...].astype(F32), gn_ref[...])
        do, dz, dgn = vjp(d_ref[:, :W].astype(F32))
        do_ref[...] = do
        dzp_ref[:, :W] = dz.astype(BF16)
        _accum(dgn_ref, dgn, first)
        pp = jnp.where(first, 0.0, pp_ref[...].astype(F32))
        ext = jnp.concatenate([pp, p_ref[...].astype(F32)], axis=0)
        pooled = (_window_sums(ext, _shift_down)[HALO:] / _pool_counts(i * tm, tm, W)) - ext[HALO:]
        dy_ext = jnp.concatenate([d_ref[:, W:].astype(F32), jnp.where(i < nrow - 1, dn_ref[...].astype(F32), 0.0)], axis=0)
        dys_ext = dy_ext * ps_ref[...]
        dpooled, dscale = [], []
        for gi in range(G):
            sl = slice(gi * GDN_D, (gi + 1) * GDN_D)
            dpooled.append(_dot_nt(dys_ext[:, sl], pw_ref[gi]))
            y = _dot(pooled[:, sl], pw_ref[gi])
            dscale.append(jnp.sum(dy_ext[:tm, sl] * y, axis=0, keepdims=True))
            _accum(dpw_ref.at[gi], _dot_tn(pooled[:, sl], dys_ext[:tm, sl]), first)
        dpooled = jnp.concatenate(dpooled, axis=1)
        _accum(dps_ref, jnp.concatenate(dscale, axis=1), first)
        dmean = dpooled / _pool_counts(i * tm, tm + HALO, W)
        dp = _window_sums(dmean, _shift_up)[:tm] - dpooled[:tm]
        dzp_ref[:, W:] = dp.astype(BF16)

    return pl.pallas_call(
        body, name=name, grid=(nrow,),
        in_specs=[pl.BlockSpec((tm, W), lambda i: (i, 0)), pl.BlockSpec((tm, W), lambda i: (i, 3)),
                  pl.BlockSpec((tm, W), lambda i: (i, 4)),
                  pl.BlockSpec((HALO, W), lambda i: (jnp.maximum(i * per - 1, 0), 4)),
                  pl.BlockSpec((1, GDN_D), lambda i: (0, 0)),
                  pl.BlockSpec((G, GDN_D, GDN_D), lambda i: (0, 0, 0)),
                  pl.BlockSpec((1, W), lambda i: (0, 0)),
                  pl.BlockSpec((tm, 2 * W), lambda i: (i, 0)),
                  pl.BlockSpec((HALO, W), lambda i: (jnp.minimum((i + 1) * per, last), 1))],
        out_specs=[pl.BlockSpec((tm, W), lambda i: (i, 0)), pl.BlockSpec((tm, 2 * W), lambda i: (i, 0)),
                   pl.BlockSpec((1, GDN_D), lambda i: (0, 0)), pl.BlockSpec((G, GDN_D, GDN_D), lambda i: (0, 0, 0)),
                   pl.BlockSpec((1, W), lambda i: (0, 0))],
        out_shape=[_sds((T, W), F32), _sds((T, 2 * W), BF16), _sds((1, GDN_D), F32), _sds((G, GDN_D, GDN_D), F32),
                   _sds((1, W), F32)],
        compiler_params=_cp(("arbitrary",)),
    )(o, proj, proj, proj, gdn_norm, pool_w, pool_scale, dcat, dcat)


def _head_norm(x, gain):
    return x * lax.rsqrt(jnp.mean(x * x, axis=-1, keepdims=True) + RMS_EPS) * gain


def _att_scores(q, k, slope, has_prev):
    B = ATT_BLOCK
    a = lax.broadcasted_iota(jnp.int32, (B, 2 * B), 0)
    j = lax.broadcasted_iota(jnp.int32, (B, 2 * B), 1)
    rel = B + a - j
    mask = (rel >= 0) & (rel <= B) & ((j >= B) | has_prev)
    s = _dot_nt(q, k) - slope * rel.astype(F32)
    return jnp.where(mask, s, NEG), mask


def _alibi_slope(h, dil):
    return dil * (2.0 ** (-8.0 * (h + 1) / ATT_HEADS))


def att_fwd(qk, qkv, name):
    dil, L, _ = qk.shape
    Wd = ATT_HEADS * ATT_DH
    nb = L // ATT_BLOCK
    B = ATT_BLOCK

    def body(q_ref, kc_ref, kp_ref, vc_ref, vp_ref, o_ref, l_ref):
        has_prev = pl.program_id(1) > 0
        for h in range(ATT_HEADS):
            sl = slice(h * ATT_DH, (h + 1) * ATT_DH)
            k = jnp.concatenate([kp_ref[:, sl], kc_ref[:, sl]], axis=0)
            v = jnp.concatenate([vp_ref[:, sl], vc_ref[:, sl]], axis=0)
            s, _ = _att_scores(q_ref[:, sl], k, _alibi_slope(h, dil), has_prev)
            m = jnp.max(s, axis=-1, keepdims=True)
            p = jnp.exp(s - m)
            l = jnp.sum(p, axis=-1, keepdims=True)
            o_ref[:, sl] = (_dot(p, v) / l).astype(BF16)
            l_ref[:, sl] = jnp.broadcast_to(m + jnp.log(l), (B, ATT_DH))

    cur = lambda t: pl.BlockSpec((None, B, Wd), lambda r, n: (r, n, t))
    prev = lambda t: pl.BlockSpec((None, B, Wd), lambda r, n: (r, jnp.maximum(n - 1, 0), t))
    out = pl.BlockSpec((None, B, Wd), lambda r, n: (r, n, 0))
    return pl.pallas_call(
        body, name=name, grid=(dil, nb),
        in_specs=[cur(0), cur(1), prev(1), cur(2), prev(2)],
        out_specs=[out, out], out_shape=[_sds((dil, L, Wd), BF16), _sds((dil, L, Wd), F32)],
        compiler_params=_cp(("parallel", "arbitrary")),
    )(qk, qk, qk, qkv, qkv)


def att_bwd(qk, qkv, lse, do, dd, name):
    dil, L, _ = qk.shape
    Wd = ATT_HEADS * ATT_DH
    nb = L // ATT_BLOCK
    B = ATT_BLOCK

    def body(q_ref, kc_ref, kp_ref, vc_ref, vp_ref, l_ref, do_ref, dd_ref, dq_ref, dk_ref, dv_ref, ck_scr, cv_scr):
        n = pl.program_id(1)
        has_prev = n > 0

        @pl.when(n < nb)
        def _():
            for h in range(ATT_HEADS):
                sl = slice(h * ATT_DH, (h + 1) * ATT_DH)
                q = q_ref[:, sl]
                k = jnp.concatenate([kp_ref[:, sl], kc_ref[:, sl]], axis=0)
                v = jnp.concatenate([vp_ref[:, sl], vc_ref[:, sl]], axis=0)
                do = do_ref[:, sl]
                s, mask = _att_scores(q, k, _alibi_slope(h, dil), has_prev)
                p = jnp.where(mask, jnp.exp(s - l_ref[:, h * ATT_DH:h * ATT_DH + 1]), 0.0)
                delta = jnp.sum(dd_ref[:, sl].astype(F32), axis=-1, keepdims=True)
                ds = p * (_dot_nt(do, v) - delta)
                dq_ref[:, sl] = _dot(ds, k).astype(BF16)
                dk = _dot_tn(ds, q)
                dv = _dot_tn(p, do)
                dk_ref[:, sl] = (jnp.where(has_prev, ck_scr[:, sl] + dk[:B], 0.0)).astype(BF16)
                dv_ref[:, sl] = (jnp.where(has_prev, cv_scr[:, sl] + dv[:B], 0.0)).astype(BF16)
                ck_scr[:, sl] = dk[B:]
                cv_scr[:, sl] = dv[B:]

        @pl.when(n == nb)
        def _():
            dk_ref[...] = ck_scr[...].astype(BF16)
            dv_ref[...] = cv_scr[...].astype(BF16)

    cur = lambda t: pl.BlockSpec((None, B, Wd), lambda r, n: (r, jnp.minimum(n, nb - 1), t))
    prev = lambda t: pl.BlockSpec((None, B, Wd), lambda r, n: (r, jnp.clip(n - 1, 0, nb - 1), t))
    kv_out = pl.BlockSpec((None, B, Wd), lambda r, n: (r, jnp.maximum(n - 1, 0), 0))
    return pl.pallas_call(
        body, name=name, grid=(dil, nb + 1),
        in_specs=[cur(0), cur(1), prev(1), cur(2), prev(2), cur(0), cur(0), cur(0)],
        out_specs=[cur(0), kv_out, kv_out],
        out_shape=[_sds((dil, L, Wd), BF16)] * 3,
        scratch_shapes=[pltpu.VMEM((B, Wd), F32), pltpu.VMEM((B, Wd), F32)],
        compiler_params=_cp(("parallel", "arbitrary")),
    )(qk, qk, qk, qkv, qkv, lse, do, dd)


def qk_norm_bwd(qkv, dq, dk, q_norm, k_norm, name):
    T = qkv.shape[0]
    Wd = dq.shape[1]
    tm = _row_tile(T)

    def body(q_ref, k_ref, dq_ref, dk_ref, qn_ref, kn_ref, oq_ref, ok_ref, dqn_ref, dkn_ref):
        first = pl.program_id(0) == 0
        for x_ref, d_ref, g_ref, o_ref, dg_ref, scale in ((q_ref, dq_ref, qn_ref, oq_ref, dqn_ref, ATT_DH ** -0.5),
                                                         (k_ref, dk_ref, kn_ref, ok_ref, dkn_ref, 1.0)):
            dg = jnp.zeros((1, ATT_DH), F32)
            for h in range(Wd // ATT_DH):
                sl = slice(h * ATT_DH, (h + 1) * ATT_DH)
                _, vjp = jax.vjp(lambda x, g: _head_norm(x, g) * scale, x_ref[:, sl].astype(F32), g_ref[...])
                dx, dg_h = vjp(d_ref[:, sl].astype(F32))
                o_ref[:, sl] = dx.astype(BF16)
                dg += dg_h
            _accum(dg_ref, dg, first)

    row = lambda t: pl.BlockSpec((tm, Wd), lambda i: (i, t))
    vec = pl.BlockSpec((1, ATT_DH), lambda i: (0, 0))
    return pl.pallas_call(
        body, name=name, grid=(T // tm,),
        in_specs=[row(0), row(1), row(0), row(0), vec, vec], out_specs=[row(0), row(0), vec, vec],
        out_shape=[_sds((T, Wd), BF16)] * 2 + [_sds((1, ATT_DH), F32)] * 2,
        compiler_params=_cp(("arbitrary",)),
    )(qkv, qkv, dq, dk, q_norm, k_norm)


def _merge_weights(l0, l1, l2):
    m = jnp.maximum(jnp.maximum(l0, l1), l2)
    e = [jnp.exp(l - m) for l in (l0, l1, l2)]
    tot = e[0] + e[1] + e[2]
    return [x / tot for x in e]


def _merge_specs(arrays, tm):
    return [pl.BlockSpec((a.shape[0], tm // a.shape[0], a.shape[2]), lambda i: (0, i, 0)) for a in arrays]


def _merge_load(refs, dils, tm, scr):
    vals = [ref[...].astype(F32).reshape(tm, ref.shape[-1]) for ref in refs]
    return [v if dl == 1 else _interleave(v, scr, dl) for v, dl in zip(vals, dils)]


def merge_fwd(outs, lses, name):
    Wd = outs[0].shape[2]
    T = outs[0].shape[0] * outs[0].shape[1]
    tm = _row_tile(T)
    dils = [a.shape[0] for a in outs] * 2

    def body(*refs):
        vals = _merge_load(refs[:6], dils, tm, refs[7])
        w = _merge_weights(*vals[3:])
        refs[6][...] = (w[0] * vals[0] + w[1] * vals[1] + w[2] * vals[2]).astype(BF16)

    return pl.pallas_call(body, name=name, grid=(T // tm,), in_specs=_merge_specs(list(outs) + list(lses), tm),
                          out_specs=pl.BlockSpec((tm, Wd), lambda i: (i, 0)), out_shape=_sds((T, Wd), BF16),
                          scratch_shapes=[pltpu.VMEM((Wd // LANES, tm, LANES), F32)],
                          compiler_params=_cp(("parallel",)))(*outs, *lses)


def merge_bwd(outs, lses, d, name):
    Wd = outs[0].shape[2]
    T = outs[0].shape[0] * outs[0].shape[1]
    tm = 256 if T % 256 == 0 else T
    dils = [a.shape[0] for a in outs] * 2

    def body(*refs):
        scr = refs[13]
        vals = _merge_load(refs[:6], dils, tm, scr)
        w = _merge_weights(*vals[3:])
        dv = refs[6][...].astype(F32)
        dvm = dv * (w[0] * vals[0] + w[1] * vals[1] + w[2] * vals[2])
        for g in range(3):
            for ref, val in ((refs[7 + g], w[g] * dv), (refs[10 + g], w[g] * dvm)):
                ref[...] = (val if dils[g] == 1 else _deinterleave(val, scr, dils[g])).reshape(ref.shape).astype(BF16)

    specs = _merge_specs(list(outs) + list(lses), tm)
    res = pl.pallas_call(body, name=name, grid=(T // tm,),
                         in_specs=specs + [pl.BlockSpec((tm, Wd), lambda i: (i, 0))], out_specs=specs,
                         out_shape=[_sds(a.shape, BF16) for a in list(outs) + list(outs)],
                         scratch_shapes=[pltpu.VMEM((Wd // LANES, tm, LANES), F32)],
                         compiler_params=_cp(("parallel",)))(*outs, *lses, d)
    return res[:3], res[3:]


def loss_head(y, target, name):
    T, D = y.shape
    tm = _row_tile(T)

    def body(y_ref, t_ref, l_ref, dy_ref):
        err = y_ref[...] - t_ref[...]
        dy_ref[...] = err * (1.0 / D)
        part = 0.5 * jnp.sum(jnp.sum(err * err, axis=1, keepdims=True) * (1.0 / D), axis=0, keepdims=True)
        _accum(l_ref, jnp.broadcast_to(part, (1, 128)), pl.program_id(0) == 0)

    row = pl.BlockSpec((tm, D), lambda i: (i, 0))
    return pl.pallas_call(body, name=name, grid=(T // tm,), in_specs=[row, row],
                          out_specs=[pl.BlockSpec((1, 128), lambda i: (0, 0)), row],
                          out_shape=[_sds((1, 128), F32), _sds((T, D), F32)],
                          compiler_params=_cp(("arbitrary",)))(y, target)


def ada_fwd(c_all, w, b, name):
    depth, D, n = w.shape

    def body(c_ref, w_ref, b_ref, o_ref):
        o_ref[...] = _dot(_silu(c_ref[...]), w_ref[...]) + b_ref[...]

    return pl.pallas_call(
        body, name=name, grid=(depth,),
        in_specs=[pl.BlockSpec((N_DEV, D), lambda i: (0, 0)), pl.BlockSpec((None, D, n), lambda i: (i, 0, 0)),
                  pl.BlockSpec((None, 1, n), lambda i: (i, 0, 0))],
        out_specs=pl.BlockSpec((None, N_DEV, n), lambda i: (i, 0, 0)),
        out_shape=_sds((depth, N_DEV, n), F32), compiler_params=_cp(("parallel",)),
    )(c_all, w, b)


def ada_bwd(c_all, dmod, name):
    depth, _, n = dmod.shape
    D = c_all.shape[1]

    def body(c_ref, d_ref, o_ref):
        o_ref[...] = _dot_tn(_silu(c_ref[...]), d_ref[...])

    return pl.pallas_call(
        body, name=name, grid=(depth,),
        in_specs=[pl.BlockSpec((N_DEV, D), lambda i: (0, 0)), pl.BlockSpec((None, N_DEV, n), lambda i: (i, 0, 0))],
        out_specs=pl.BlockSpec((None, D, n), lambda i: (i, 0, 0)),
        out_shape=_sds((depth, D, n), F32), compiler_params=_cp(("parallel",)),
    )(c_all, dmod)


def adamw(w, m, v, gparts, name):
    R, C = w.shape
    k = gparts.shape[0]
    tr = R
    for cand in (512, 256, 128, 64, 32, 16, 8):
        if R % cand == 0 and cand * C * 4 <= 2 * 1024 * 1024:
            tr = cand
            break
    bc1 = 1.0 - ADAM_B1 ** ADAM_STEP
    bc2 = 1.0 - ADAM_B2 ** ADAM_STEP

    def body(w_ref, m_ref, v_ref, gp_ref, g_ref, d_ref, nm_ref, nv_ref):
        g = gp_ref[0].astype(F32)
        for q in range(1, k):
            g = g + gp_ref[q].astype(F32)
        nm = ADAM_B1 * m_ref[...] + (1.0 - ADAM_B1) * g
        nv = ADAM_B2 * v_ref[...] + (1.0 - ADAM_B2) * (g * g)
        g_ref[...] = g
        nm_ref[...] = nm
        nv_ref[...] = nv
        d_ref[...] = -ADAM_LR * ((nm / bc1) / (jnp.sqrt(nv / bc2) + ADAM_EPS) + ADAM_WD * w_ref[...])

    row = pl.BlockSpec((tr, C), lambda i: (i, 0))
    return pl.pallas_call(
        body, name=name, grid=(R // tr,),
        in_specs=[row, row, row, pl.BlockSpec((k, tr, C), lambda i: (0, i, 0))],
        out_specs=[row] * 4, out_shape=[_sds((R, C), F32)] * 4, compiler_params=_cp(("parallel",)),
    )(w, m, v, gparts)


def _mesh_pos():
    return lax.axis_index("x"), lax.axis_index("y"), lax.axis_index("c")


def _other_chips(x, y):
    return [(1 - x, y), (x, 1 - y), (1 - x, 1 - y)]


_ANY = pl.BlockSpec(memory_space=pl.ANY)


class _Gather:
    def __init__(self, ins, outs, send_sems, recv_sems, local_sems):
        self.ins, self.outs, self.n = ins, outs, len(ins)
        self.send_sems, self.recv_sems, self.local_sems = send_sems, recv_sems, local_sems
        self.x, self.y, self.c = _mesh_pos()
        self.me, self.sibling = (self.x, self.y, self.c), (self.x, self.y, 1 - self.c)
        self.chips = _other_chips(self.x, self.y)

    def _copy(self, a, k, block, to, src=None):
        dst = self.outs[a].at[4 * block[0] + 2 * block[1] + block[2]]
        return pltpu.make_async_remote_copy(
            src_ref=dst if src is None else src, dst_ref=dst, send_sem=self.send_sems.at[a, k],
            recv_sem=self.recv_sems.at[a, k], device_id=to, device_id_type=MESH)

    def _mine(self):
        me_slot = 4 * self.x + 2 * self.y + self.c
        return [pltpu.make_async_copy(self.ins[a], self.outs[a].at[me_slot], self.local_sems.at[a]) for a in range(self.n)]

    def _first(self):
        out = []
        for a in range(self.n):
            out.append(self._copy(a, 0, self.me, self.sibling, src=self.ins[a]))
            out += [self._copy(a, 1 + j, self.me, (*chip, self.c), src=self.ins[a]) for j, chip in enumerate(self.chips)]
        return out

    def _passed(self):
        return [self._copy(a, 4 + j, (*chip, self.c), self.sibling) for j, chip in enumerate(self.chips) for a in range(self.n)]

    def start(self):
        for cp in self._mine() + self._first():
            cp.start()

    def forward(self):
        for j, chip in enumerate(self.chips):
            for a in range(self.n):
                self._copy(a, 1 + j, (*chip, self.c), self.me).wait_recv()
                self._copy(a, 4 + j, (*chip, self.c), self.sibling).start()

    def finish(self):
        for a in range(self.n):
            self._copy(a, 0, self.sibling, self.me).wait_recv()
            for j, chip in enumerate(self.chips):
                self._copy(a, 4 + j, (*chip, 1 - self.c), self.me).wait_recv()
        for cp in self._first() + self._passed():
            cp.wait_send()
        for cp in self._mine():
            cp.wait()


def _gather_scratch(n):
    return [pltpu.SemaphoreType.DMA((n, 7)), pltpu.SemaphoreType.DMA((n, 7)), pltpu.SemaphoreType.DMA((n,))]


def all_gather(shards, name):
    n = len(shards)

    def body(*refs):
        g = _Gather(refs[:n], refs[n:2 * n], *refs[2 * n:])
        g.start()
        g.forward()
        g.finish()

    return pl.pallas_call(
        body, name=name, in_specs=[_ANY] * n, out_specs=[_ANY] * n,
        out_shape=[_sds((N_DEV,) + s.shape, s.dtype) for s in shards],
        scratch_shapes=_gather_scratch(n),
        compiler_params=pltpu.CompilerParams(has_side_effects=True),
    )(*shards)


def exchange_pair(stacked, name):
    n = len(stacked)

    def body(*refs):
        ins, outs = refs[:n], refs[n:2 * n]
        send_sems, recv_sems = refs[2 * n:]
        x, y, c = _mesh_pos()
        copies = [pltpu.make_async_remote_copy(
            src_ref=ins[a].at[2 * q + (1 - c)], dst_ref=outs[a].at[q], send_sem=send_sems.at[a, q],
            recv_sem=recv_sems.at[a, q], device_id=(x, y, 1 - c), device_id_type=MESH)
            for a in range(n) for q in range(4)]
        for cp in copies:
            cp.start()
        for cp in copies:
            cp.wait()

    return pl.pallas_call(
        body, name=name, in_specs=[_ANY] * n, out_specs=[_ANY] * n,
        out_shape=[_sds((4,) + s.shape[1:], s.dtype) for s in stacked],
        scratch_shapes=[pltpu.SemaphoreType.DMA((n, 4)), pltpu.SemaphoreType.DMA((n, 4))],
        compiler_params=pltpu.CompilerParams(has_side_effects=True),
    )(*stacked)


def pair_add(stacked, got, c_idx, name):
    _, R, C = stacked.shape
    tr = R
    for cand in (512, 256, 128, 64, 32, 16):
        if R % cand == 0 and cand * C * 4 <= 2 * 1024 * 1024:
            tr = cand
            break

    def body(c_ref, s_ref, g_ref, o_ref):
        o_ref[...] = (s_ref[...].astype(F32) + g_ref[...].astype(F32)).astype(BF16)

    return pl.pallas_call(
        body, name=name,
        grid_spec=pltpu.PrefetchScalarGridSpec(
            num_scalar_prefetch=1, grid=(4, R // tr),
            in_specs=[pl.BlockSpec((None, tr, C), lambda q, i, c_ref: (2 * q + c_ref[0], i, 0)),
                      pl.BlockSpec((None, tr, C), lambda q, i, c_ref: (q, i, 0))],
            out_specs=pl.BlockSpec((None, tr, C), lambda q, i, c_ref: (q, i, 0))),
        out_shape=_sds((4, R, C), BF16),
        compiler_params=_cp(("parallel", "parallel")),
    )(c_idx, stacked, got)


def exchange_chips(parts, name):
    n = len(parts)

    def body(*refs):
        ins, outs = refs[:n], refs[n:2 * n]
        send_sems, recv_sems, local_sems = refs[2 * n:]
        x, y, c = _mesh_pos()
        myq = 2 * x + y
        mine = [pltpu.make_async_copy(ins[a].at[myq], outs[a].at[myq], local_sems.at[a]) for a in range(n)]
        for cp in mine:
            cp.start()
        copies = [pltpu.make_async_remote_copy(
            src_ref=ins[a].at[2 * chip[0] + chip[1]], dst_ref=outs[a].at[myq], send_sem=send_sems.at[a, j],
            recv_sem=recv_sems.at[a, j], device_id=(*chip, c), device_id_type=MESH)
            for a in range(n) for j, chip in enumerate(_other_chips(x, y))]
        for cp in copies:
            cp.start()
        for cp in copies:
            cp.wait()
        for cp in mine:
            cp.wait()

    return pl.pallas_call(
        body, name=name, in_specs=[_ANY] * n, out_specs=[_ANY] * n,
        out_shape=[_sds(s.shape, s.dtype) for s in parts],
        scratch_shapes=[pltpu.SemaphoreType.DMA((n, 3)), pltpu.SemaphoreType.DMA((n, 3)), pltpu.SemaphoreType.DMA((n,))],
        compiler_params=pltpu.CompilerParams(has_side_effects=True),
    )(*parts)


def _pad_lanes(v, start, width=128):
    return jnp.pad(v.astype(F32), (start, width - start - v.shape[0]))[None]


def _carry(results, n_own):
    results = list(results)
    return results[:n_own] + [results[n_own:]]


def _layer_fwd(i, x, mod, P, next_shards=None):
    T, D = x.shape
    next_shards = next_shards or {}
    next_ffn = {k: v for k, v in next_shards.items() if k.startswith("ffn")}
    next_mix = {k: v for k, v in next_shards.items() if not k.startswith("ffn")}
    sh_m, sc_m, g_m, sh_f, sc_f, g_f = [mod[k:k + 1] for k in range(6)]
    sv = {"x0": x}
    tag = f"l{i}"
    if i % 2 == 0:
        e = i // 2
        h, proj = norm_proj(x, P["norm_mix"][i:i + 1], sc_m, sh_m, P["ev_main"][e], P["ev_main"][e].shape[1], tag + "_in")
        ba = matmul_nn(h, P["ev_ba"][e], F32, tag + "_ba")
        al = _pad_lanes(P["gdn_a_log"][e], GDN_HEADS)
        dt = _pad_lanes(P["gdn_dt_bias"][e], GDN_HEADS)
        gate = gate_fwd(ba, al, dt, tag + "_gate")
        qkv = gdn_prep_fwd(proj, P["gdn_conv_w"][e], tag + "_prep")
        *chunks, got_mix = _carry(gdn_prep_chunks_fwd(qkv, gate, tag + "_chunks", gather=list(next_mix.values())), 5)
        o, states = gdn_fwd(*chunks, gate, tag + "_gdn")
        cat = even_post_fwd(o, proj, P["gdn_norm"][e:e + 1], P["pool_w"][e], P["pool_scale"][e:e + 1], tag + "_post")
        x1, y_m = proj_res(cat, P["ev_w_out"][e], x, g_m, tag + "_out")
        sv.update(h=h, proj=proj, ba=ba, al=al, dt=dt, gate=gate, qkv=qkv, chunks=chunks, o=o, states=states, a_m=cat)
    else:
        od = i // 2
        qn, kn = P["att_q_norm"][od:od + 1], P["att_k_norm"][od:od + 1]
        h, proj, qk = [], [], []
        got_mix = []
        for gi, (_, dil) in enumerate(DIL_PATTERNS):
            hg, pg, qkg, got = _carry(norm_proj(x, P["norm_mix"][i:i + 1], sc_m, sh_m, P["od_w_in"][od], 3 * D,
                                                f"{tag}_in{gi}", cols=(gi, 1), dil=dil, qk_norms=(qn, kn),
                                                gather=list(next_mix.values()) if gi == 0 else None), 3)
            got_mix += got
            h.append(hg.reshape(T, D))
            proj.append(pg)
            qk.append(qkg)
        res = [att_fwd(qk[gi], proj[gi], f"{tag}_att{gi}") for gi in range(len(DIL_PATTERNS))]
        outs, lses = [r[0] for r in res], [r[1] for r in res]
        merged = merge_fwd(outs, lses, tag + "_merge")
        x1, y_m = proj_res(merged, P["od_w_out"][od], x, g_m, tag + "_out")
        sv.update(h=h, proj=proj, qk=qk, outs=outs, lses=lses, a_m=merged)
    hf, up, got_ffn = _carry(norm_proj(x1, P["norm_ffn"][i:i + 1], sc_f, sh_f, P["ffn_w_up"][i], 1408, tag + "_up",
                                       gather=list(next_ffn.values())), 2)
    a = ffn_mid_fwd(up, P["ffn_conv_w"][i], P["ffn_conv_b"][i:i + 1], tag + "_mid")
    x2, y_f = proj_res(a, P["ffn_w_down"][i], x1, g_f, tag + "_down")
    sv.update(y_m=y_m, x1=x1, hf=hf, up=up, a_f=a, y_f=y_f)
    got = dict(zip(list(next_mix.keys()) + list(next_ffn.keys()), list(got_mix) + list(got_ffn)))
    return x2, sv, (got if next_shards else None)


def add_layer_weights(P, gathered):
    W = GDN_HEADS * GDN_D
    for k, g in gathered.items():
        if k in _COL_SHARDED:
            full = g.transpose(1, 0, 2).reshape(g.shape[1], N_DEV * g.shape[2])
        else:
            full = g.reshape(N_DEV * g.shape[1], g.shape[2])
        if k == "ev_w_in":
            P.setdefault("ev_main", []).append(jnp.concatenate([full[:, :4 * W], full[:, 4 * W + 2 * GDN_HEADS:]], axis=1))
            P.setdefault("ev_ba", []).append(jnp.pad(full[:, 4 * W:4 * W + 2 * GDN_HEADS], ((0, 0), (0, 128 - 2 * GDN_HEADS))))
        else:
            P.setdefault(k, []).append(full)


def _layer_bwd(i, dx2, sv, mod, P, bufs):
    T = dx2.shape[0]
    sh_m, sc_m, g_m, sh_f, sc_f, g_f = [mod[k:k + 1] for k in range(6)]
    tag = f"b{i}"
    G = {}
    F = P["ffn_w_down"][i].shape[0]
    dy, da, dg_f = bwd_out(dx2, g_f, sv["y_f"], P["ffn_w_down"][i], F, tag + "_down")
    bufs["ffn_w_down"] = matmul_tn(sv["a_f"], dy, F // 2, dy.shape[1], tag + "_wdown", into=bufs["ffn_w_down"], layer=i)
    dgate, dval, dcw, dcb = ffn_mid_bwd(sv["up"], da, P["ffn_conv_w"][i], P["ffn_conv_b"][i:i + 1], tag + "_mid")
    G["ffn_conv_w"], G["ffn_conv_b"] = dcw[:FFN_CONV], dcb[0]
    dx1, dnf, dsc_f, dsh_f = bwd_in([dgate, dval], P["ffn_w_up"][i], [0, 1], F, None,
                                    sv["x1"], P["norm_ffn"][i:i + 1], sc_f, sh_f, dx2, tag + "_up")
    G["norm_ffn"] = dnf[0]
    x0, h, proj = sv["x0"], sv["h"], sv["proj"]
    D = x0.shape[1]
    for k, d in enumerate((dgate, dval)):
        bufs["ffn_w_up"] = matmul_tn(sv["hf"], d, D, F // 2, f"{tag}_wup{k}", into=bufs["ffn_w_up"], layer=i, col0=2 * k)
    gain = P["norm_mix"][i:i + 1]
    if i % 2 == 0:
        e = i // 2
        W = GDN_HEADS * GDN_D
        dy, dcat, dg_m = bwd_out(dx1, g_m, sv["y_m"], P["ev_w_out"][e], 1024, tag + "_out")
        bufs["ev_w_out"] = matmul_tn(sv["a_m"], dy, 1024, D, tag + "_wout", into=bufs["ev_w_out"], layer=e)
        do, dzp, dgn, dpw, dps = even_post_bwd(sv["o"], proj, P["gdn_norm"][e:e + 1], P["pool_w"][e],
                                               P["pool_scale"][e:e + 1], dcat, tag + "_post")
        G["gdn_norm"], G["pool_w"], G["pool_scale"] = dgn[0], dpw, dps[0]
        *dchunks, dgate_scan = gdn_bwd(*sv["chunks"], sv["gate"], sv["states"], do, tag + "_gdn")
        dqkv, dgate_prep = gdn_prep_chunks_bwd(sv["qkv"], sv["gate"], *dchunks, tag + "_chunks")
        dba, dal, ddt = gate_bwd(sv["ba"], sv["al"], sv["dt"], dgate_prep, dgate_scan, tag + "_gate")
        G["gdn_a_log"], G["gdn_dt_bias"] = dal[0, GDN_HEADS:2 * GDN_HEADS], ddt[0, GDN_HEADS:2 * GDN_HEADS]
        dqkv_raw, dconv = gdn_prep_bwd(proj, dqkv, P["gdn_conv_w"][e], tag + "_prep")
        G["gdn_conv_w"] = dconv[:GDN_CONV]
        w_main = P["ev_main"][e]
        dx0, dnm, dsc_m, dsh_m = bwd_in([dba, dqkv_raw, dzp], [P["ev_ba"][e], w_main[:, :3 * W], w_main[:, 3 * W:]],
                                        [0, 0, 0], None, None, x0, gain, sc_m, sh_m, dx1, tag + "_in")
        gw_qkv = matmul_tn(h, dqkv_raw, 1024, W, tag + "_win1")
        gw_zp = matmul_tn(h, dzp, 1024, W, tag + "_win2")
        gw_ba = matmul_tn(h, dba, 1024, 128, tag + "_win0")
        G["ev_w_in"] = jnp.concatenate([gw_qkv, gw_zp[:, :W], gw_ba[:, :2 * GDN_HEADS], gw_zp[:, W:]], axis=1)
    else:
        od = i // 2
        qn, kn = P["att_q_norm"][od:od + 1], P["att_k_norm"][od:od + 1]
        dy, dmerged, dg_m = bwd_out(dx1, g_m, sv["y_m"], P["od_w_out"][od], 1024, tag + "_out")
        bufs["od_w_out"] = matmul_tn(sv["a_m"], dy, 1024, D, tag + "_wout", into=bufs["od_w_out"], layer=od)
        douts, dds = merge_bwd(sv["outs"], sv["lses"], dmerged, tag + "_merge")
        dh = None
        dqn_sum, dkn_sum = 0.0, 0.0
        ng = len(DIL_PATTERNS)
        for gi, (_, dil) in enumerate(DIL_PATTERNS):
            dqs, dks, dv = att_bwd(sv["qk"][gi], proj[gi], sv["lses"][gi], douts[gi], dds[gi], f"{tag}_att{gi}")
            dq, dk, dqn, dkn = qk_norm_bwd(proj[gi].reshape(T, 3 * D), dqs.reshape(T, D), dks.reshape(T, D), qn, kn,
                                           f"{tag}_qkn{gi}")
            dqn_sum, dkn_sum = dqn_sum + dqn[0], dkn_sum + dkn[0]
            for k, d in enumerate((dq, dk, dv)):
                bufs["od_w_in"] = matmul_tn(h[gi], d.reshape(T, D), 1024, D, f"{tag}_win{gi}{k}", into=bufs["od_w_in"],
                                            layer=od, col0=3 * gi + k)
            cols = [3 * gi, 3 * gi + 1, 3 * gi + 2]
            ops = [d.reshape(dil, T // dil, D) if dil > 1 else d.reshape(T, D) for d in (dq, dk, dv)]
            if gi < ng - 1:
                dh = bwd_in(ops, P["od_w_in"][od], cols, D, dh, None, None, None, None, None, f"{tag}_in{gi}", dil=dil)
            else:
                dx0, dnm, dsc_m, dsh_m = bwd_in(ops, P["od_w_in"][od], cols, D, dh, x0, gain, sc_m, sh_m, dx1,
                                                f"{tag}_in{gi}", dil=dil)
        G["att_q_norm"], G["att_k_norm"] = dqn_sum, dkn_sum
    G["norm_mix"] = dnm[0]
    dmod = jnp.concatenate([dsh_m, dsc_m, dg_m, dsh_f, dsc_f, dg_f], axis=0)
    return dx0, dmod, G


_PER_LAYER = ("norm_mix", "norm_ffn", "ffn_conv_w", "ffn_conv_b")
_PER_EVEN = ("ev_w_in", "gdn_conv_w", "gdn_a_log", "gdn_dt_bias", "gdn_norm", "pool_w", "pool_scale")
_PER_ODD = ("att_q_norm", "att_k_norm")
_IN_PLACE = ("ffn_w_up", "ffn_w_down", "ev_w_out", "od_w_in", "od_w_out")


def device_step(x, mod, target, P, layer_shards=None):
    saved = []
    for i in range(DEPTH):
        nxt = layer_shards[i + 1] if (layer_shards is not None and i + 1 < DEPTH) else None
        x, sv, got = _layer_fwd(i, x, mod[i], P, nxt)
        if got is not None:
            add_layer_weights(P, got)
        saved.append(sv)
    loss, dx = loss_head(x, target, "loss")
    layer_grads, dmods = [None] * DEPTH, [None] * DEPTH
    bufs = {k: lax.empty((len(P[k]),) + tuple(P[k][0].shape), BF16) for k in _IN_PLACE}
    for i in reversed(range(DEPTH)):
        dx, dmods[i], layer_grads[i] = _layer_bwd(i, dx, saved[i], mod[i], P, bufs)
    G = dict(bufs)
    G.update({k: jnp.stack([layer_grads[i][k] for i in range(DEPTH)]) for k in _PER_LAYER})
    G.update({k: jnp.stack([layer_grads[i][k] for i in range(0, DEPTH, 2)]) for k in _PER_EVEN})
    G.update({k: jnp.stack([layer_grads[i][k] for i in range(1, DEPTH, 2)]) for k in _PER_ODD})
    return loss, dx, jnp.stack(dmods), G


_WEIGHTS = ("ada_w", "ada_b", "norm_mix", "norm_ffn", "ev_w_in", "ev_w_out", "gdn_conv_w", "gdn_a_log", "gdn_dt_bias",
            "gdn_norm", "pool_w", "pool_scale", "od_w_in", "od_w_out", "att_q_norm", "att_k_norm", "ffn_w_up",
            "ffn_conv_w", "ffn_conv_b", "ffn_w_down")
_COL_SHARDED = ("ev_w_in", "od_w_in", "ffn_w_up")
_ROW_SHARDED = ("ev_w_out", "od_w_out", "ffn_w_down")
_SMALL_SHARDED = ("gdn_conv_w", "ffn_conv_w")
_REPLICATED = ("ada_b", "norm_mix", "norm_ffn", "gdn_a_log", "gdn_dt_bias", "gdn_norm", "pool_w", "pool_scale",
               "att_q_norm", "att_k_norm", "ffn_conv_b")
PACK_LANES = 128
PACK_ROWS = 8


def _pack_rows(n):
    unit = PACK_LANES * PACK_ROWS
    return -(-n // unit) * PACK_ROWS


def _pack(arrays):
    parts = []
    for a in arrays:
        flat = a.reshape(-1).astype(F32)
        rows = _pack_rows(flat.shape[0])
        parts.append(jnp.pad(flat, (0, rows * PACK_LANES - flat.shape[0])).reshape(rows, PACK_LANES))
    return jnp.concatenate(parts, axis=0)


def _unpack(packed, shapes, lead=()):
    out, row = [], 0
    for s in shapes:
        n = math.prod(s)
        rows = _pack_rows(n)
        blk = packed[..., row:row + rows, :].reshape(lead + (rows * PACK_LANES,))
        out.append(blk[..., :n].reshape(lead + tuple(s)))
        row += rows
    return out


def _unshard_cols(g):
    _, L, R, n = g.shape
    return g.transpose(1, 2, 0, 3).reshape(L, R, N_DEV * n)


def _shard_cols(full):
    L, R, N = full.shape
    n = N // N_DEV
    return full.reshape(L * R, N_DEV, n).transpose(1, 0, 2)


def _shard_rows(full):
    L, R, C = full.shape
    r = R // N_DEV
    return full.reshape(L, N_DEV, r, C).transpose(1, 0, 2, 3).reshape(N_DEV, L * r, C)


def kernel(x, c, ada_w, ada_b, norm_mix, norm_ffn, ev_w_in, ev_w_out, gdn_conv_w, gdn_a_log, gdn_dt_bias, gdn_norm, pool_w, pool_scale, od_w_in, od_w_out, att_q_norm, att_k_norm, ffn_w_up, ffn_conv_w, ffn_conv_b, ffn_w_down, loss_target, m_ada_w, m_ada_b, m_norm_mix, m_norm_ffn, m_ev_w_in, m_ev_w_out, m_gdn_conv_w, m_gdn_a_log, m_gdn_dt_bias, m_gdn_norm, m_pool_w, m_pool_scale, m_od_w_in, m_od_w_out, m_att_q_norm, m_att_k_norm, m_ffn_w_up, m_ffn_conv_w, m_ffn_conv_b, m_ffn_w_down, v_ada_w, v_ada_b, v_norm_mix, v_norm_ffn, v_ev_w_in, v_ev_w_out, v_gdn_conv_w, v_gdn_a_log, v_gdn_dt_bias, v_gdn_norm, v_pool_w, v_pool_scale, v_od_w_in, v_od_w_out, v_att_q_norm, v_att_k_norm, v_ffn_w_up, v_ffn_conv_w, v_ffn_conv_b, v_ffn_w_down):
    args = locals()
    Wl = {k: args[k] for k in _WEIGHTS}
    Ml = {k: args["m_" + k] for k in _WEIGHTS}
    Vl = {k: args["v_" + k] for k in _WEIGHTS}
    mx, my, mc = _mesh_pos()
    dev = 4 * mx + 2 * my + mc
    T, D = x.shape[1], x.shape[2]
    x2d, tgt = x.reshape(T, D), loss_target.reshape(T, D)

    small_shapes = [c.shape] + [Wl[k].shape for k in _SMALL_SHARDED]
    big = list(_COL_SHARDED + _ROW_SHARDED)
    layer_shards = []
    for i in range(DEPTH):
        mixer = ("ev_w_in", "ev_w_out") if i % 2 == 0 else ("od_w_in", "od_w_out")
        shards = {k: Wl[k][i // 2].astype(BF16) for k in mixer}
        shards.update({k: Wl[k][i].astype(BF16) for k in ("ffn_w_up", "ffn_w_down")})
        layer_shards.append(shards)
    gathered = all_gather([_pack([c] + [Wl[k] for k in _SMALL_SHARDED])] + list(layer_shards[0].values()), "gather_w0")
    c_all, conv_g, conv_f = _unpack(gathered[0], small_shapes, lead=(N_DEV,))
    c_all = c_all.reshape(N_DEV, D)
    P = {k: Wl[k] for k in _REPLICATED}
    P["gdn_conv_w"], P["ffn_conv_w"] = _unshard_cols(conv_g), _unshard_cols(conv_f)
    add_layer_weights(P, dict(zip(layer_shards[0].keys(), gathered[1:])))

    n_ada = ada_w.shape[2]
    b_cols = lax.dynamic_slice_in_dim(ada_b, dev * n_ada, n_ada, axis=1)
    mod_cols = ada_fwd(c_all, ada_w, b_cols[:, None, :], "ada_fwd")
    mod_all, = all_gather([mod_cols], "gather_mod")
    mod = lax.dynamic_index_in_dim(mod_all, dev, axis=2, keepdims=False)
    mod = mod.transpose(1, 0, 2).reshape(DEPTH, 6, D)

    loss, dx, dmod, G = device_step(x2d, mod, tgt, P, layer_shards)
    loss = lax.psum(loss[0, 0], ("x", "y", "c"))

    G["ada_b"] = dmod.reshape(DEPTH, 6 * D)
    small = list(_REPLICATED) + list(_SMALL_SHARDED)
    parts_all, = all_gather([_pack([G[k] for k in small])], "gather_small")
    zeros = {k: jnp.zeros_like(G[k]) for k in _SMALL_SHARDED}
    packs = [_pack([src[k] for k in _REPLICATED] + [zeros[k] for k in _SMALL_SHARDED]) for src in (Wl, Ml, Vl)]
    res = adamw(*packs, parts_all, "adamw_small")
    shapes = [G[k].shape for k in small]
    out_g, out_d, out_m, out_v = ({k: a for k, a in zip(small, _unpack(r, shapes))} for r in res)
    dmod_all = _unpack(parts_all, shapes, lead=(N_DEV,))[0].reshape(N_DEV, DEPTH, 6 * D)
    dm_cols = lax.dynamic_slice_in_dim(dmod_all, dev * n_ada, n_ada, axis=2).transpose(1, 0, 2)
    g_ada = ada_bwd(c_all, dm_cols, "ada_bwd")

    def flat2(a):
        return a.reshape(-1, a.shape[-1])

    loc = {k: lax.dynamic_slice_in_dim(out_g[k], dev * Wl[k].shape[-1], Wl[k].shape[-1], axis=out_g[k].ndim - 1)
           for k in _SMALL_SHARDED}
    res = adamw(*[_pack([src[k] for k in _SMALL_SHARDED]) for src in (Wl, Ml, Vl)],
                _pack([loc[k] for k in _SMALL_SHARDED])[None], "adamw_conv")
    for dst, r in zip((out_g, out_d, out_m, out_v), res):
        dst.update(zip(_SMALL_SHARDED, _unpack(r, [Wl[k].shape for k in _SMALL_SHARDED])))
    res = adamw(flat2(ada_w), flat2(m_ada_w), flat2(v_ada_w), flat2(g_ada)[None], "adamw_ada")
    for dst, r in zip((out_g, out_d, out_m, out_v), res):
        dst["ada_w"] = r.reshape(ada_w.shape)

    stacked = [_shard_cols(G[k]) for k in _COL_SHARDED] + [_shard_rows(G[k]) for k in _ROW_SHARDED]
    got = exchange_pair(stacked, "rs_pair")
    c_idx = jnp.reshape(mc, (1,)).astype(jnp.int32)
    chip_parts = [pair_add(s, g, c_idx, f"rs_add_{k}") for s, g, k in zip(stacked, got, big)]
    parts = exchange_chips(chip_parts, "rs_chips")
    for k, p in zip(big, parts):
        res = adamw(flat2(Wl[k]), flat2(Ml[k]), flat2(Vl[k]), p, "adamw_" + k)
        for dst, r in zip((out_g, out_d, out_m, out_v), res):
            dst[k] = r.reshape(Wl[k].shape)

    return (loss, dx.reshape(x.shape), *[out_g[k] for k in _WEIGHTS], *[out_d[k] for k in _WEIGHTS],
            *[out_m[k] for k in _WEIGHTS], *[out_v[k] for k in _WEIGHTS])
```

```python
import functools
import math

import jax
import jax.numpy as jnp
from jax import lax
from jax.experimental import pallas as pl
from jax.experimental.pallas import tpu as pltpu

F32 = jnp.float32
BF16 = jnp.bfloat16
HI = lax.Precision.HIGHEST
MESH = pl.DeviceIdType.MESH

N_DEV = 8
RMS_EPS = 1e-6
DEPTH = 4
GDN_HEADS = 4
GDN_D = 128
GDN_CHUNK = 64
GDN_CONV = 4
POOL_WINDOWS = (2, 4, 8, 16)
DIL_PATTERNS = ((128, 1), (512, 4), (2048, 16))
ATT_HEADS = 8
ATT_DH = 128
ATT_BLOCK = 128
FFN_CONV = 3
ADAM_LR, ADAM_B1, ADAM_B2, ADAM_EPS, ADAM_WD, ADAM_STEP = 0.001, 0.9, 0.999, 1e-08, 0.01, 10

HALO = 16
NEG = -1e30
VMEM_LIMIT_BYTES = 56 * 1024 * 1024


def _cp(sem=None, **kw):
    return pltpu.CompilerParams(dimension_semantics=sem, vmem_limit_bytes=VMEM_LIMIT_BYTES, **kw)


def _sds(shape, dtype):
    return jax.ShapeDtypeStruct(tuple(shape), dtype)


def _dot(a, b):
    return jnp.dot(a.astype(BF16), b.astype(BF16), preferred_element_type=F32)


def _dot_nt(a, b):
    return lax.dot_general(a.astype(BF16), b.astype(BF16), (((1,), (1,)), ((), ())), preferred_element_type=F32)


def _dot_tn(a, b):
    return lax.dot_general(a.astype(BF16), b.astype(BF16), (((0,), (0,)), ((), ())), preferred_element_type=F32)


def _dot_hi(a, b):
    return jnp.dot(a, b, preferred_element_type=F32, precision=HI)


def _silu(x):
    return x * jax.nn.sigmoid(x)


def _modnorm(x, gain, sc, sh):
    y = x * lax.rsqrt(jnp.mean(x * x, axis=-1, keepdims=True) + RMS_EPS)
    return y * gain * (1.0 + sc) + sh


def _row_tile(T):
    return 512 if T % 512 == 0 else T


LANES = 128


def _deinterleave(val, scr, dil):
    tm, width = val.shape
    sub = tm // dil
    ncb = width // LANES
    for cb in range(ncb):
        scr[cb] = val[:, cb * LANES:(cb + 1) * LANES]
    return jnp.concatenate([jnp.concatenate([scr.at[cb][pl.ds(r, sub, stride=dil), :] for cb in range(ncb)], axis=1)
                            for r in range(dil)], axis=0)


def _interleave(val, scr, dil):
    tm, width = val.shape
    sub = tm // dil
    ncb = width // LANES
    for r in range(dil):
        for cb in range(ncb):
            scr.at[cb][pl.ds(r, sub, stride=dil), :] = val[r * sub:(r + 1) * sub, cb * LANES:(cb + 1) * LANES]
    return jnp.concatenate([scr[cb] for cb in range(ncb)], axis=1)


def norm_proj(x, gain, sc, sh, w, tn, name, cols=None, dil=1, qk_norms=None, gather=None):
    T, D = x.shape
    c0, ncol = cols if cols is not None else (0, w.shape[1] // tn)
    N = ncol * tn
    tm = 1024 if (ncol > 1 and dil == 1 and T % 1024 == 0) else _row_tile(T)
    sub = tm // dil
    n_g = len(gather) if gather else 0
    n_own_in = 5 + (2 if qk_norms is not None else 0)
    n_own_out = 2 + (1 if qk_norms is not None else 0)
    n_in, n_out = n_own_in + n_g, n_own_out + n_g
    n_scr = 1 + (1 if dil > 1 else 0)
    nsteps = (T // tm) * ncol
    Wd = tn // 3

    def body(*refs):
        x_ref, g_ref, sc_ref, sh_ref, w_ref = refs[:5]
        h_ref, o_ref = refs[n_in:n_in + 2]
        h_scr = refs[n_in + n_out]
        step = pl.program_id(0) * ncol + pl.program_id(1)
        if n_g:
            exch = _Gather(refs[n_own_in:n_in], refs[n_in + n_own_out:n_in + n_out], *refs[n_in + n_out + n_scr:])
            pl.when(step == 0)(exch.start)
            pl.when(step == nsteps // 2)(exch.forward)

        @pl.when(pl.program_id(1) == 0)
        def _():
            h = _modnorm(x_ref[...], g_ref[...], sc_ref[...], sh_ref[...])
            if dil > 1:
                h = _deinterleave(h, refs[n_in + n_out + 1], dil)
            h_scr[...] = h.astype(BF16)
            h_ref[...] = h.reshape(h_ref.shape).astype(BF16)
        res = jnp.dot(h_scr[...], w_ref[...], preferred_element_type=F32)
        o_ref[...] = res.reshape(o_ref.shape).astype(BF16)
        if qk_norms is not None:
            qk_ref = refs[n_in + 2]
            parts = []
            for t, scale in enumerate((ATT_DH ** -0.5, 1.0)):
                for hd in range(Wd // ATT_DH):
                    c = t * Wd + hd * ATT_DH
                    parts.append(_head_norm(res[:, c:c + ATT_DH], refs[5 + t][...]) * scale)
            qk_ref[...] = jnp.concatenate(parts, axis=1).reshape(qk_ref.shape).astype(BF16)
        if n_g:
            pl.when(step == nsteps - 1)(exch.finish)

    vec = pl.BlockSpec((1, D), lambda i, j: (0, 0))
    in_specs = [pl.BlockSpec((tm, D), lambda i, j: (i, 0)), vec, vec, vec, pl.BlockSpec((D, tn), lambda i, j: (0, c0 + j))]
    args = [x, gain, sc, sh, w]
    out_specs = [pl.BlockSpec((dil, sub, D), lambda i, j: (0, i, 0)), pl.BlockSpec((dil, sub, tn), lambda i, j: (0, i, j))]
    out_shape = [_sds((dil, T // dil, D), BF16), _sds((dil, T // dil, N), BF16)]
    if qk_norms is not None:
        assert ncol == 1
        in_specs += [pl.BlockSpec((1, ATT_DH), lambda i, j: (0, 0))] * 2
        args += list(qk_norms)
        out_specs.append(pl.BlockSpec((dil, sub, 2 * Wd), lambda i, j: (0, i, 0)))
        out_shape.append(_sds((dil, T // dil, 2 * Wd), BF16))
    scratch = [pltpu.VMEM((tm, D), BF16)] + ([pltpu.VMEM((D // LANES, tm, LANES), F32)] if dil > 1 else [])
    if n_g:
        in_specs += [_ANY] * n_g
        args += list(gather)
        out_specs += [_ANY] * n_g
        out_shape += [_sds((N_DEV,) + g.shape, g.dtype) for g in gather]
        scratch += _gather_scratch(n_g)
    res = pl.pallas_call(
        body, name=name, grid=(T // tm, ncol), in_specs=in_specs, out_specs=out_specs, out_shape=out_shape,
        scratch_shapes=scratch, compiler_params=_cp(("arbitrary", "arbitrary") if n_g else ("parallel", "arbitrary")),
    )(*args)
    if dil == 1 and qk_norms is None:
        return [r.reshape(T, -1) for r in res[:2]] + list(res[2:])
    return res


def matmul_nn(a, w, out_dtype, name):
    T, K = a.shape
    N = w.shape[1]
    tm = _row_tile(T)

    def body(a_ref, w_ref, o_ref):
        o_ref[...] = _dot(a_ref[...], w_ref[...]).astype(o_ref.dtype)

    return pl.pallas_call(
        body, name=name, grid=(T // tm,),
        in_specs=[pl.BlockSpec((tm, K), lambda i: (i, 0)), pl.BlockSpec((K, N), lambda i: (0, 0))],
        out_specs=pl.BlockSpec((tm, N), lambda i: (i, 0)),
        out_shape=_sds((T, N), out_dtype),
        compiler_params=_cp(("parallel",)),
    )(a, w)


def proj_res(a, w, x, gate, name):
    T, K = a.shape
    D = w.shape[1]
    tm = _row_tile(T)

    def body(a_ref, w_ref, x_ref, g_ref, o_ref, y_ref):
        y = jnp.dot(a_ref[...], w_ref[...], preferred_element_type=F32)
        y_ref[...] = y.astype(BF16)
        o_ref[...] = x_ref[...] + g_ref[...] * y

    return pl.pallas_call(
        body, name=name, grid=(T // tm,),
        in_specs=[pl.BlockSpec((tm, K), lambda i: (i, 0)), pl.BlockSpec((K, D), lambda i: (0, 0)),
                  pl.BlockSpec((tm, D), lambda i: (i, 0)), pl.BlockSpec((1, D), lambda i: (0, 0))],
        out_specs=[pl.BlockSpec((tm, D), lambda i: (i, 0)), pl.BlockSpec((tm, D), lambda i: (i, 0))],
        out_shape=[_sds((T, D), F32), _sds((T, D), BF16)],
        compiler_params=_cp(("parallel",)),
    )(a, w, x, gate)


def bwd_out(dx, gate, y, w, tk, name):
    T, D = dx.shape
    K = w.shape[0]
    tm = _row_tile(T)

    def body(dx_ref, g_ref, y_ref, w_ref, dy_ref, da_ref, dg_ref, dy_scr):
        i, j = pl.program_id(0), pl.program_id(1)

        @pl.when(j == 0)
        def _():
            dxv = dx_ref[...]
            dy = (dxv * g_ref[...]).astype(BF16)
            dy_scr[...] = dy
            dy_ref[...] = dy
            part = jnp.sum(dxv * y_ref[...].astype(F32), axis=0, keepdims=True)

            @pl.when(i == 0)
            def _():
                dg_ref[...] = part

            @pl.when(i > 0)
            def _():
                dg_ref[...] += part

        da_ref[...] = _dot_nt(dy_scr[...], w_ref[...]).astype(BF16)

    return pl.pallas_call(
        body, name=name, grid=(T // tm, K // tk),
        in_specs=[pl.BlockSpec((tm, D), lambda i, j: (i, 0)), pl.BlockSpec((1, D), lambda i, j: (0, 0)),
                  pl.BlockSpec((tm, D), lambda i, j: (i, 0)), pl.BlockSpec((tk, D), lambda i, j: (j, 0))],
        out_specs=[pl.BlockSpec((tm, D), lambda i, j: (i, 0)), pl.BlockSpec((tm, tk), lambda i, j: (i, j)),
                   pl.BlockSpec((1, D), lambda i, j: (0, 0))],
        out_shape=[_sds((T, D), BF16), _sds((T, K), BF16), _sds((1, D), F32)],
        scratch_shapes=[pltpu.VMEM((tm, D), BF16)],
        compiler_params=_cp(("arbitrary", "arbitrary")),
    )(dx, gate, y, w)


def bwd_in(a_list, w, col_blocks, tn, acc, x, gain, sc, sh, dx_res, name, dil=1):
    T = math.prod(a_list[0].shape[:-1])
    n_a = len(a_list)
    w_list = list(w) if isinstance(w, (list, tuple)) else [w] * n_a
    D = w_list[0].shape[0]
    tm = 256 if T % 256 == 0 else T
    sub = tm // dil
    tns = [tn if tn is not None else a.shape[-1] for a in a_list]
    nsteps = a_list[0].shape[-1] // tns[0]
    assert all(a.shape[-1] == nsteps * t for a, t in zip(a_list, tns))
    final = x is not None
    has_acc = acc is not None

    def body(*refs):
        a_refs = refs[:n_a]
        w_refs = refs[n_a:2 * n_a]
        pos = 2 * n_a
        acc_ref = refs[pos] if has_acc else None
        pos += int(has_acc)
        if final:
            x_ref, g_ref, sc_ref, sh_ref, dxr_ref = refs[pos:pos + 5]
            pos += 5
            dx_ref, dg_ref, dsc_ref, dsh_ref = refs[pos:pos + 4]
            pos += 4
        else:
            dh_ref = refs[pos]
            pos += 1
        acc_scr = refs[pos]
        i, j = pl.program_id(0), pl.program_id(1)
        part = _dot_nt(a_refs[0][...].reshape(tm, tns[0]), w_refs[0][...])
        for k in range(1, n_a):
            part += _dot_nt(a_refs[k][...].reshape(tm, tns[k]), w_refs[k][...])

        @pl.when(j == 0)
        def _():
            acc_scr[...] = part

        @pl.when(j > 0)
        def _():
            acc_scr[...] += part

        @pl.when(j == nsteps - 1)
        def _():
            dh = acc_scr[...]
            if dil > 1:
                dh = _interleave(dh, refs[pos + 1], dil)
            if has_acc:
                dh = dh + acc_ref[...]
            if not final:
                dh_ref[...] = dh
                return
            _, vjp = jax.vjp(_modnorm, x_ref[...], g_ref[...], sc_ref[...], sh_ref[...])
            dxn, dg, dsc, dsh = vjp(dh)
            dx_ref[...] = dxr_ref[...] + dxn

            @pl.when(i == 0)
            def _():
                dg_ref[...] = dg
                dsc_ref[...] = dsc
                dsh_ref[...] = dsh

            @pl.when(i > 0)
            def _():
                dg_ref[...] += dg
                dsc_ref[...] += dsc
                dsh_ref[...] += dsh

    row = pl.BlockSpec((tm, D), lambda i, j: (i, 0))
    vec = pl.BlockSpec((1, D), lambda i, j: (0, 0))
    if dil == 1:
        in_specs = [pl.BlockSpec((tm, t), lambda i, j: (i, j)) for t in tns]
    else:
        in_specs = [pl.BlockSpec((dil, sub, t), lambda i, j: (0, i, j)) for t in tns]
    in_specs += [pl.BlockSpec((D, t), functools.partial(lambda i, j, c0: (0, c0 + j), c0=c0))
                 for t, c0 in zip(tns, col_blocks)]
    args = list(a_list) + w_list
    if has_acc:
        in_specs.append(row)
        args.append(acc)
    if final:
        in_specs += [row, vec, vec, vec, row]
        args += [x, gain, sc, sh, dx_res]
        out_specs = [row, vec, vec, vec]
        out_shape = [_sds((T, D), F32)] + [_sds((1, D), F32)] * 3
    else:
        out_specs = row
        out_shape = _sds((T, D), F32)
    return pl.pallas_call(
        body, name=name, grid=(T // tm, nsteps), in_specs=in_specs, out_specs=out_specs, out_shape=out_shape,
        scratch_shapes=[pltpu.VMEM((tm, D), F32)] + ([pltpu.VMEM((D // LANES, tm, LANES), F32)] if dil > 1 else []),
        compiler_params=_cp(("arbitrary", "arbitrary")),
    )(*args)


def matmul_tn(a, b, tk, tn, name, into=None, layer=0, col0=0):
    T, K = a.shape
    N = b.shape[1]
    tt = 2048 if T % 2048 == 0 else T
    nt = T // tt

    def body(a_ref, b_ref, *rest):
        o_ref, acc = rest[-2:]
        part = _dot_tn(a_ref[...], b_ref[...])

        @pl.when(pl.program_id(2) == 0)
        def _():
            acc[...] = part

        @pl.when(pl.program_id(2) > 0)
        def _():
            acc[...] += part

        @pl.when(pl.program_id(2) == nt - 1)
        def _():
            o_ref[...] = acc[...].astype(BF16)

    in_specs = [pl.BlockSpec((tt, tk), lambda k, n, t: (t, k)), pl.BlockSpec((tt, tn), lambda k, n, t: (t, n))]
    common = dict(name=name, grid=(K // tk, N // tn, nt), scratch_shapes=[pltpu.VMEM((tk, tn), F32)],
                  compiler_params=_cp(("parallel", "parallel", "arbitrary")))
    if into is None:
        return pl.pallas_call(body, in_specs=in_specs, out_specs=pl.BlockSpec((tk, tn), lambda k, n, t: (k, n)),
                              out_shape=_sds((K, N), BF16), **common)(a, b)
    return pl.pallas_call(
        body, in_specs=in_specs + [_ANY], out_specs=pl.BlockSpec((None, tk, tn), lambda k, n, t: (layer, k, col0 + n)),
        out_shape=_sds(into.shape, BF16), input_output_aliases={2: 0}, **common)(a, b, into)


def _halo_specs(tm, tc, col_of):
    per = tm // HALO

    def prev(j, i):
        return (jnp.maximum(i * per - 1, 0), col_of(j))

    def nxt(j, i, last):
        return (jnp.minimum((i + 1) * per, last), col_of(j))

    return prev, nxt, per


def _shift_down(ext, s):
    return pltpu.roll(ext, s, 0)


def _shift_up(ext, s):
    return pltpu.roll(ext, ext.shape[0] - s, 0)


def _conv_ext(ext, w):
    K = w.shape[0]
    out = w[K - 1:K] * ext
    for s in range(1, K):
        out += w[K - 1 - s:K - s] * _shift_down(ext, s)
    return out


def _conv_t_ext(dext, w):
    K = w.shape[0]
    out = w[K - 1:K] * dext
    for s in range(1, K):
        out += w[K - 1 - s:K - s] * _shift_up(dext, s)
    return out


def _conv_dw(dc, ext, K, tm):
    rowid = lax.broadcasted_iota(jnp.int32, (8, dc.shape[1]), 0)
    out = jnp.zeros((8, dc.shape[1]), F32)
    for j in range(K):
        s = K - 1 - j
        xs = ext if s == 0 else _shift_down(ext, s)
        out = jnp.where(rowid == j, jnp.sum(dc * xs[HALO:HALO + tm], axis=0, keepdims=True), out)
    return out


def _accum(ref, val, first):
    @pl.when(first)
    def _():
        ref[...] = val

    @pl.when(jnp.logical_not(first))
    def _():
        ref[...] += val


def ffn_mid_fwd(up, conv_w, conv_b, name):
    T, two_f = up.shape
    F = two_f // 2
    tm = _row_tile(T)
    tc = F // 2
    nct = F // tc
    prev, _, per = _halo_specs(tm, tc, lambda j: j)

    def body(g_ref, gp_ref, v_ref, w_ref, b_ref, a_ref):
        i = pl.program_id(1)
        gp = jnp.where(i > 0, gp_ref[...].astype(F32), 0.0)
        ext = jnp.concatenate([gp, g_ref[...].astype(F32)], axis=0)
        c = _conv_ext(ext, w_ref[...])[HALO:] + b_ref[...]
        a_ref[...] = (_silu(c) * v_ref[...].astype(F32)).astype(BF16)

    return pl.pallas_call(
        body, name=name, grid=(nct, T // tm),
        in_specs=[pl.BlockSpec((tm, tc), lambda j, i: (i, j)), pl.BlockSpec((HALO, tc), prev),
                  pl.BlockSpec((tm, tc), lambda j, i: (i, j + nct)),
                  pl.BlockSpec((FFN_CONV, tc), lambda j, i: (0, j)), pl.BlockSpec((1, tc), lambda j, i: (0, j))],
        out_specs=pl.BlockSpec((tm, tc), lambda j, i: (i, j)),
        out_shape=_sds((T, F), BF16),
        compiler_params=_cp(("parallel", "arbitrary")),
    )(up, up, up, conv_w, conv_b)


def ffn_mid_bwd(up, da, conv_w, conv_b, name):
    T, two_f = up.shape
    F = two_f // 2
    tm = _row_tile(T)
    tc = F // 2
    nct = F // tc
    nrow = T // tm
    prev, nxt, per = _halo_specs(tm, tc, lambda j: j)
    last = T // HALO - 1
    nxt_g = functools.partial(nxt, last=last)

    def nxt_v(j, i):
        return (jnp.minimum((i + 1) * per, last), j + nct)

    def body(g_ref, gp_ref, gn_ref, v_ref, vn_ref, da_ref, dan_ref, w_ref, b_ref, dg_ref, dv_ref, dw_ref, db_ref):
        i = pl.program_id(1)
        w = w_ref[...]
        gp = jnp.where(i > 0, gp_ref[...].astype(F32), 0.0)
        inside = i < nrow - 1
        ext = jnp.concatenate([gp, g_ref[...].astype(F32), gn_ref[...].astype(F32)], axis=0)
        zero = jnp.zeros((HALO, tc), F32)
        v_ext = jnp.concatenate([zero, v_ref[...].astype(F32), vn_ref[...].astype(F32)], axis=0)
        da_ext = jnp.concatenate([zero, da_ref[...].astype(F32), jnp.where(inside, dan_ref[...].astype(F32), 0.0)], axis=0)
        c = _conv_ext(ext, w) + b_ref[...]
        sg = jax.nn.sigmoid(c)
        sil = c * sg
        dc = da_ext * v_ext * (sg * (1.0 + c * (1.0 - sg)))
        dv_ref[...] = (da_ext * sil)[HALO:HALO + tm].astype(BF16)
        dg_ref[...] = _conv_t_ext(dc, w)[HALO:HALO + tm].astype(BF16)
        dcm = dc[HALO:HALO + tm]
        _accum(dw_ref, _conv_dw(dcm, ext, FFN_CONV, tm), i == 0)
        _accum(db_ref, jnp.sum(dcm, axis=0, keepdims=True), i == 0)

    main = lambda j, i: (i, j)
    main_v = lambda j, i: (i, j + nct)
    return pl.pallas_call(
        body, name=name, grid=(nct, nrow),
        in_specs=[pl.BlockSpec((tm, tc), main), pl.BlockSpec((HALO, tc), prev), pl.BlockSpec((HALO, tc), nxt_g),
                  pl.BlockSpec((tm, tc), main_v), pl.BlockSpec((HALO, tc), nxt_v),
                  pl.BlockSpec((tm, tc), main), pl.BlockSpec((HALO, tc), nxt_g),
                  pl.BlockSpec((FFN_CONV, tc), lambda j, i: (0, j)), pl.BlockSpec((1, tc), lambda j, i: (0, j))],
        out_specs=[pl.BlockSpec((tm, tc), main), pl.BlockSpec((tm, tc), main),
                   pl.BlockSpec((8, tc), lambda j, i: (0, j)), pl.BlockSpec((1, tc), lambda j, i: (0, j))],
        out_shape=[_sds((T, F), BF16), _sds((T, F), BF16), _sds((8, F), F32), _sds((1, F), F32)],
        compiler_params=_cp(("parallel", "arbitrary")),
    )(up, up, up, up, up, da, da, conv_w, conv_b)


def _l2n(x):
    return x * lax.rsqrt(jnp.sum(x * x, axis=-1, keepdims=True) + RMS_EPS)


def _qkv_tok(c, normed):
    s = _silu(c)
    n = jnp.concatenate([_l2n(s[:, h * GDN_D:(h + 1) * GDN_D]) for h in range(GDN_HEADS)], axis=1)
    return jnp.where(normed, n, s)


def gdn_prep_fwd(proj, conv_w, name):
    T = proj.shape[0]
    W = GDN_HEADS * GDN_D
    tm = _row_tile(T)
    prev, _, per = _halo_specs(tm, W, lambda j: j)

    def body(x_ref, xp_ref, w_ref, o_ref):
        j, i = pl.program_id(0), pl.program_id(1)
        xp = jnp.where(i > 0, xp_ref[...].astype(F32), 0.0)
        ext = jnp.concatenate([xp, x_ref[...].astype(F32)], axis=0)
        c = _conv_ext(ext, w_ref[...])[HALO:]
        o_ref[...] = _qkv_tok(c, j < 2)

    return pl.pallas_call(
        body, name=name, grid=(3, T // tm),
        in_specs=[pl.BlockSpec((tm, W), lambda j, i: (i, j)), pl.BlockSpec((HALO, W), prev),
                  pl.BlockSpec((GDN_CONV, W), lambda j, i: (0, j))],
        out_specs=pl.BlockSpec((tm, W), lambda j, i: (i, j)),
        out_shape=_sds((T, 3 * W), F32),
        compiler_params=_cp(("parallel", "arbitrary")),
    )(proj, proj, conv_w)


def gdn_prep_bwd(proj, dqkv, conv_w, name):
    T = proj.shape[0]
    W = GDN_HEADS * GDN_D
    tm = _row_tile(T)
    nrow = T // tm
    prev, nxt, per = _halo_specs(tm, W, lambda j: j)
    nxt = functools.partial(nxt, last=T // HALO - 1)

    def body(x_ref, xp_ref, xn_ref, d_ref, dn_ref, w_ref, dx_ref, dw_ref):
        j, i = pl.program_id(0), pl.program_id(1)
        w = w_ref[...]
        xp = jnp.where(i > 0, xp_ref[...].astype(F32), 0.0)
        ext = jnp.concatenate([xp, x_ref[...].astype(F32), xn_ref[...].astype(F32)], axis=0)
        d_ext = jnp.concatenate([jnp.zeros((HALO, W), F32), d_ref[...],
                                 jnp.where(i < nrow - 1, dn_ref[...], 0.0)], axis=0)
        c = _conv_ext(ext, w)
        _, vjp = jax.vjp(lambda cc: _qkv_tok(cc, j < 2), c)
        dc, = vjp(d_ext)
        dx_ref[...] = _conv_t_ext(dc, w)[HALO:HALO + tm].astype(BF16)
        _accum(dw_ref, _conv_dw(dc[HALO:HALO + tm], ext, GDN_CONV, tm), i == 0)

    main = lambda j, i: (i, j)
    return pl.pallas_call(
        body, name=name, grid=(3, nrow),
        in_specs=[pl.BlockSpec((tm, W), main), pl.BlockSpec((HALO, W), prev), pl.BlockSpec((HALO, W), nxt),
                  pl.BlockSpec((tm, W), main), pl.BlockSpec((HALO, W), nxt),
                  pl.BlockSpec((GDN_CONV, W), lambda j, i: (0, j))],
        out_specs=[pl.BlockSpec((tm, W), main), pl.BlockSpec((8, W), lambda j, i: (0, j))],
        out_shape=[_sds((T, 3 * W), BF16), _sds((8, 3 * W), F32)],
        compiler_params=_cp(("parallel", "arbitrary")),
    )(proj, proj, proj, dqkv, dqkv, conv_w)


def _gate_tok(ba, a_log, dt_bias):
    z = ba + dt_bias
    softplus = jnp.maximum(z, 0.0) + jnp.log(1.0 + jnp.exp(-jnp.abs(z)))
    lane = lax.broadcasted_iota(jnp.int32, ba.shape, 1)
    raw = jnp.where(lane < GDN_HEADS, jax.nn.sigmoid(ba), -jnp.exp(a_log) * softplus)
    n = ba.shape[0]
    r = lax.broadcasted_iota(jnp.int32, (n, n), 0)
    c = lax.broadcasted_iota(jnp.int32, (n, n), 1)
    in_chunk_before = (r >= c) & ((r - c) <= (r & (GDN_CHUNK - 1)))
    return jnp.where(lane < GDN_HEADS, raw, _dot_hi(in_chunk_before.astype(F32), raw))


def gate_fwd(ba, a_log, dt_bias, name):
    T, L = ba.shape
    tm = _row_tile(T)

    def body(ba_ref, al_ref, dt_ref, o_ref):
        o_ref[...] = _gate_tok(ba_ref[...], al_ref[...], dt_ref[...])

    vec = pl.BlockSpec((1, L), lambda i: (0, 0))
    return pl.pallas_call(
        body, name=name, grid=(T // tm,),
        in_specs=[pl.BlockSpec((tm, L), lambda i: (i, 0)), vec, vec],
        out_specs=pl.BlockSpec((tm, L), lambda i: (i, 0)), out_shape=_sds((T, L), F32),
        compiler_params=_cp(("parallel",)),
    )(ba, a_log, dt_bias)


def gate_bwd(ba, a_log, dt_bias, dout_a, dout_b, name):
    T, L = ba.shape
    tm = _row_tile(T)

    def body(ba_ref, al_ref, dt_ref, d_ref, d2_ref, dba_ref, dal_ref, ddt_ref):
        _, vjp = jax.vjp(_gate_tok, ba_ref[...], al_ref[...], dt_ref[...])
        dba, dal, ddt = vjp(d_ref[...] + d2_ref[...])
        dba_ref[...] = dba.astype(BF16)
        first = pl.program_id(0) == 0
        _accum(dal_ref, dal, first)
        _accum(ddt_ref, ddt, first)

    vec = pl.BlockSpec((1, L), lambda i: (0, 0))
    row = pl.BlockSpec((tm, L), lambda i: (i, 0))
    return pl.pallas_call(
        body, name=name, grid=(T // tm,),
        in_specs=[row, vec, vec, row, row], out_specs=[row, vec, vec],
        out_shape=[_sds((T, L), BF16), _sds((1, L), F32), _sds((1, L), F32)],
        compiler_params=_cp(("arbitrary",)),
    )(ba, a_log, dt_bias, dout_a, dout_b)


_B_NN = (((2,), (1,)), ((0,), (0,)))
_B_NT = (((2,), (2,)), ((0,), (0,)))
GDN_GROUP = 8


def _split_bf16(a):
    hi = a.astype(BF16)
    return hi, (a - hi.astype(F32)).astype(BF16)


def _bdot3(a, b):
    ah, al = _split_bf16(a)
    bh, bl = _split_bf16(b)
    dg = lambda x, y: lax.dot_general(x, y, _B_NN, preferred_element_type=F32)
    return dg(ah, bh) + (dg(ah, bl) + dg(al, bh))


def _exact_ones_dot(x, dims, ones_shape, ones_first):
    ones = jnp.ones(ones_shape, BF16)
    out = None
    for _ in range(3):
        piece = x.astype(BF16)
        ops = (ones, piece) if ones_first else (piece, ones)
        term = lax.dot_general(*ops, dims, preferred_element_type=F32)
        out = term if out is None else out + term
        x = x - piece.astype(F32)
    return out


@jax.custom_vjp
def _rows_from_cols(col):
    B, C, _ = col.shape
    eye = (lax.broadcasted_iota(jnp.int32, (1, C, C), 1) == lax.broadcasted_iota(jnp.int32, (1, C, C), 2)).astype(F32)
    return _exact_ones_dot(col * eye, _B_NN, (B, C, C), True)


def _rows_from_cols_fwd(col):
    return _rows_from_cols(col), None


def _rows_from_cols_bwd(_, ct):
    B, C, _ = ct.shape
    sums = _exact_ones_dot(ct, (((1,), (1,)), ((0,), (0,))), (B, C, LANES), False)
    return (sums[:, :, :1],)


_rows_from_cols.defvjp(_rows_from_cols_fwd, _rows_from_cols_bwd)


def _gdn_prep(q, k, v, gcol, bcol):
    B, C, dk = q.shape
    r = lax.broadcasted_iota(jnp.int32, (1, C, C), 1)
    cidx = lax.broadcasted_iota(jnp.int32, (1, C, C), 2)
    causal = r >= cidx
    strict = r > cidx
    decay = jnp.where(causal, jnp.exp(jnp.where(causal, gcol - _rows_from_cols(gcol), 0.0)), 0.0)
    qs = q * (dk ** -0.5)
    kb = k * bcol
    kk = lax.dot_general(kb.astype(BF16), k.astype(BF16), _B_NT, preferred_element_type=F32)
    L = jnp.where(strict, kk * decay, 0.0)
    tinv = (r == cidx).astype(F32) - L
    p = L
    n = 2
    while n < C:
        p = _bdot3(p, p)
        tinv = tinv + _bdot3(tinv, p)
        n *= 2
    egc = jnp.exp(gcol)
    u = _bdot3(tinv, v * bcol)
    w = _bdot3(tinv, kb * egc)
    qk = lax.dot_general(qs.astype(BF16), k.astype(BF16), _B_NT, preferred_element_type=F32)
    intra = jnp.where(causal, qk * decay, 0.0)
    last = lax.broadcasted_iota(jnp.int32, (1, C, 1), 1) == C - 1
    gt = jnp.sum(jnp.where(last, gcol, 0.0), axis=1, keepdims=True)
    return w, u, qs * egc, k * jnp.exp(gt - gcol), intra


def _gdn_scan(S, w, u, qg, kdec, intra, gcol):
    C = w.shape[0]
    last = lax.broadcasted_iota(jnp.int32, (C, 1), 0) == C - 1
    gt = jnp.sum(jnp.where(last, gcol, 0.0), axis=0, keepdims=True)
    v_new = u - _dot(w, S)
    o = _dot(qg, S) + _dot(intra, v_new)
    return S * jnp.exp(gt) + _dot_tn(kdec, v_new), o


def _gdn_specs(T, rev=False):
    H, D, C, B = GDN_HEADS, GDN_D, GDN_CHUNK, GDN_GROUP
    nsteps = T // (B * C)
    at = (lambda n: nsteps - 1 - n) if rev else (lambda n: n)
    rows = lambda w: pl.BlockSpec((B * C, w), lambda n: (at(n), 0))
    sq = pl.BlockSpec((B, H, C, C), lambda n: (at(n), 0, 0, 0))
    st = pl.BlockSpec((B, H, D, D), lambda n: (at(n), 0, 0, 0))
    return rows, sq, st, nsteps


def _gate_cols(gate_ref, h, rows=slice(None)):
    return gate_ref[rows, h:h + 1], gate_ref[rows, GDN_HEADS + h:GDN_HEADS + h + 1]


def gdn_prep_chunks_fwd(qkv, gate, name, gather=None):
    T = qkv.shape[0]
    H, D, C, B = GDN_HEADS, GDN_D, GDN_CHUNK, GDN_GROUP
    W = H * D
    rows, sq, _, nsteps = _gdn_specs(T)
    n_g = len(gather) if gather else 0

    def body(*refs):
        x_ref, g_ref = refs[:2]
        w_ref, u_ref, qg_ref, kd_ref, in_ref = refs[2 + n_g:7 + n_g]
        if n_g:
            exch = _Gather(refs[2:2 + n_g], refs[7 + n_g:7 + 2 * n_g], *refs[7 + 2 * n_g:])
            step = pl.program_id(0)
            pl.when(step == 0)(exch.start)
            pl.when(step == nsteps // 2)(exch.forward)
        for h in range(H):
            q, k, v = [x_ref[:, j * W + h * D:j * W + (h + 1) * D].reshape(B, C, D) for j in range(3)]
            bcol, gcol = [c.reshape(B, C, 1) for c in _gate_cols(g_ref, h)]
            outs = _gdn_prep(q, k, v, gcol, bcol)
            for ref, val in zip((w_ref, u_ref, qg_ref, kd_ref), outs[:4]):
                ref[:, h * D:(h + 1) * D] = val.reshape(B * C, D).astype(ref.dtype)
            in_ref[:, h] = outs[4].astype(BF16)
        if n_g:
            pl.when(step == nsteps - 1)(exch.finish)

    gather = list(gather) if gather else []
    return pl.pallas_call(
        body, name=name, grid=(nsteps,),
        in_specs=[rows(3 * W), rows(LANES)] + [_ANY] * n_g,
        out_specs=[rows(W), rows(W), rows(W), rows(W), sq] + [_ANY] * n_g,
        out_shape=[_sds((T, W), BF16), _sds((T, W), F32), _sds((T, W), BF16), _sds((T, W), BF16),
                   _sds((T // C, H, C, C), BF16)] + [_sds((N_DEV,) + g.shape, g.dtype) for g in gather],
        scratch_shapes=_gather_scratch(n_g) if n_g else [],
        compiler_params=_cp(("arbitrary",) if n_g else ("parallel",)),
    )(qkv, gate, *gather)


def gdn_prep_chunks_bwd(qkv, gate, dw, du, dqg, dkd, dintra, name):
    T = qkv.shape[0]
    H, D, C, B = GDN_HEADS, GDN_D, GDN_CHUNK, GDN_GROUP
    W = H * D
    rows, sq, _, nsteps = _gdn_specs(T)

    def body(x_ref, g_ref, dw_ref, du_ref, dqg_ref, dkd_ref, din_ref, dx_ref, dg_ref):
        dg_ref[...] = jnp.zeros_like(dg_ref)
        for h in range(H):
            q, k, v = [x_ref[:, j * W + h * D:j * W + (h + 1) * D].reshape(B, C, D) for j in range(3)]
            bcol, gcol = [c.reshape(B, C, 1) for c in _gate_cols(g_ref, h)]
            _, vjp = jax.vjp(_gdn_prep, q, k, v, gcol, bcol)
            cots = tuple(r[:, h * D:(h + 1) * D].reshape(B, C, D) for r in (dw_ref, du_ref, dqg_ref, dkd_ref))
            dq, dk, dv, dgc, db = vjp(cots + (din_ref[:, h],))
            for j, val in enumerate((dq, dk, dv)):
                dx_ref[:, j * W + h * D:j * W + (h + 1) * D] = val.reshape(B * C, D)
            dg_ref[:, h:h + 1] = db.reshape(B * C, 1)
            dg_ref[:, H + h:H + h + 1] = dgc.reshape(B * C, 1)

    return pl.pallas_call(
        body, name=name, grid=(nsteps,),
        in_specs=[rows(3 * W), rows(LANES), rows(W), rows(W), rows(W), rows(W), sq],
        out_specs=[rows(3 * W), rows(LANES)],
        out_shape=[_sds((T, 3 * W), F32), _sds((T, LANES), F32)],
        compiler_params=_cp(("parallel",)),
    )(qkv, gate, dw, du, dqg, dkd, dintra)


def gdn_fwd(w, u, qg, kdec, intra, gate, name):
    T = w.shape[0]
    H, D, C, B = GDN_HEADS, GDN_D, GDN_CHUNK, GDN_GROUP
    W = H * D
    rows, sq, st, nsteps = _gdn_specs(T)

    def body(w_ref, u_ref, qg_ref, kd_ref, in_ref, g_ref, o_ref, s_ref, s_scr):
        @pl.when(pl.program_id(0) == 0)
        def _():
            s_scr[...] = jnp.zeros_like(s_scr)

        def chunk(cb, carry):
            rs = pl.ds(pl.multiple_of(cb * C, C), C)
            for h in range(H):
                cs = slice(h * D, (h + 1) * D)
                S = s_scr[h]
                s_ref[cb, h] = S
                S_new, o = _gdn_scan(S, w_ref[rs, cs], u_ref[rs, cs], qg_ref[rs, cs], kd_ref[rs, cs], in_ref[cb, h],
                                     _gate_cols(g_ref, h, rs)[1])
                s_scr[h] = S_new
                o_ref[rs, cs] = o
            return carry

        lax.fori_loop(0, B, chunk, 0)

    return pl.pallas_call(
        body, name=name, grid=(nsteps,),
        in_specs=[rows(W), rows(W), rows(W), rows(W), sq, rows(LANES)],
        out_specs=[rows(W), st],
        out_shape=[_sds((T, W), F32), _sds((T // C, H, D, D), F32)],
        scratch_shapes=[pltpu.VMEM((H, D, D), F32)],
        compiler_params=_cp(("arbitrary",)),
    )(w, u, qg, kdec, intra, gate)


def gdn_bwd(w, u, qg, kdec, intra, gate, states, do, name):
    T = w.shape[0]
    H, D, C, B = GDN_HEADS, GDN_D, GDN_CHUNK, GDN_GROUP
    W = H * D
    rows, sq, st, nsteps = _gdn_specs(T, rev=True)

    def body(w_ref, u_ref, qg_ref, kd_ref, in_ref, g_ref, s_ref, do_ref,
             dw_ref, du_ref, dqg_ref, dkd_ref, din_ref, dg_ref, ds_scr):
        @pl.when(pl.program_id(0) == 0)
        def _():
            ds_scr[...] = jnp.zeros_like(ds_scr)
        dg_ref[...] = jnp.zeros_like(dg_ref)

        def chunk(t, carry):
            cb = B - 1 - t
            rs = pl.ds(pl.multiple_of(cb * C, C), C)
            for h in range(H):
                cs = slice(h * D, (h + 1) * D)
                _, vjp = jax.vjp(_gdn_scan, s_ref[cb, h], w_ref[rs, cs].astype(F32), u_ref[rs, cs],
                                 qg_ref[rs, cs].astype(F32), kd_ref[rs, cs].astype(F32), in_ref[cb, h].astype(F32),
                                 _gate_cols(g_ref, h, rs)[1])
                dS, dw, du, dqg, dkd, din, dgc = vjp((ds_scr[h], do_ref[rs, cs]))
                ds_scr[h] = dS
                dw_ref[rs, cs] = dw
                du_ref[rs, cs] = du
                dqg_ref[rs, cs] = dqg
                dkd_ref[rs, cs] = dkd
                din_ref[cb, h] = din
                dg_ref[rs, H + h:H + h + 1] = dgc
            return carry

        lax.fori_loop(0, B, chunk, 0)

    return pl.pallas_call(
        body, name=name, grid=(nsteps,),
        in_specs=[rows(W), rows(W), rows(W), rows(W), sq, rows(LANES), st, rows(W)],
        out_specs=[rows(W), rows(W), rows(W), rows(W), sq, rows(LANES)],
        out_shape=[_sds((T, W), F32)] * 4 + [_sds((T // C, H, C, C), F32), _sds((T, LANES), F32)],
        scratch_shapes=[pltpu.VMEM((H, D, D), F32)],
        compiler_params=_cp(("arbitrary",)),
    )(w, u, qg, kdec, intra, gate, states, do)


def _gated_norm(o, z, gain):
    outs = []
    for h in range(GDN_HEADS):
        oh = o[:, h * GDN_D:(h + 1) * GDN_D]
        y = oh * lax.rsqrt(jnp.mean(oh * oh, axis=-1, keepdims=True) + RMS_EPS) * gain
        outs.append(y * _silu(z[:, h * GDN_D:(h + 1) * GDN_D]))
    return jnp.concatenate(outs, axis=1)


def _window_sums(ext, shift):
    outs = []
    s = ext
    step = 1
    for gi, win in enumerate(POOL_WINDOWS):
        while step < win:
            s = s + shift(s, step)
            step *= 2
        outs.append(s[:, gi * GDN_D:(gi + 1) * GDN_D])
    return jnp.concatenate(outs, axis=1)


def _pool_counts(t0, rows, width):
    t1 = (t0 + 1 + lax.broadcasted_iota(jnp.int32, (rows, width), 0)).astype(F32)
    lane = lax.broadcasted_iota(jnp.int32, (rows, width), 1)
    win = jnp.full((rows, width), float(POOL_WINDOWS[-1]), F32)
    for gi in reversed(range(len(POOL_WINDOWS) - 1)):
        win = jnp.where(lane < (gi + 1) * GDN_D, float(POOL_WINDOWS[gi]), win)
    return jnp.minimum(t1, win)


def even_post_fwd(o, proj, gdn_norm, pool_w, pool_scale, name):
    T = o.shape[0]
    W = GDN_HEADS * GDN_D
    tm = _row_tile(T)
    per = tm // HALO

    def body(o_ref, z_ref, p_ref, pp_ref, gn_ref, pw_ref, ps_ref, out_ref):
        i = pl.program_id(0)
        out_ref[:, :W] = _gated_norm(o_ref[...], z_ref[...].astype(F32), gn_ref[...]).astype(BF16)
        pp = jnp.where(i > 0, pp_ref[...].astype(F32), 0.0)
        ext = jnp.concatenate([pp, p_ref[...].astype(F32)], axis=0)
        pooled = (_window_sums(ext, _shift_down)[HALO:] / _pool_counts(i * tm, tm, W)) - ext[HALO:]
        for gi in range(len(POOL_WINDOWS)):
            sl = slice(gi * GDN_D, (gi + 1) * GDN_D)
            y = _dot(pooled[:, sl], pw_ref[gi]) * ps_ref[:, sl]
            out_ref[:, W + gi * GDN_D:W + (gi + 1) * GDN_D] = y.astype(BF16)

    return pl.pallas_call(
        body, name=name, grid=(T // tm,),
        in_specs=[pl.BlockSpec((tm, W), lambda i: (i, 0)), pl.BlockSpec((tm, W), lambda i: (i, 3)),
                  pl.BlockSpec((tm, W), lambda i: (i, 4)),
                  pl.BlockSpec((HALO, W), lambda i: (jnp.maximum(i * per - 1, 0), 4)),
                  pl.BlockSpec((1, GDN_D), lambda i: (0, 0)),
                  pl.BlockSpec((len(POOL_WINDOWS), GDN_D, GDN_D), lambda i: (0, 0, 0)),
                  pl.BlockSpec((1, W), lambda i: (0, 0))],
        out_specs=pl.BlockSpec((tm, 2 * W), lambda i: (i, 0)),
        out_shape=_sds((T, 2 * W), BF16),
        compiler_params=_cp(("arbitrary",)),
    )(o, proj, proj, proj, gdn_norm, pool_w, pool_scale)


def even_post_bwd(o, proj, gdn_norm, pool_w, pool_scale, dcat, name):
    T = o.shape[0]
    W = GDN_HEADS * GDN_D
    G = len(POOL_WINDOWS)
    tm = _row_tile(T)
    per = tm // HALO
    nrow = T // tm
    last = T // HALO - 1

    def body(o_ref, z_ref, p_ref, pp_ref, gn_ref, pw_ref, ps_ref, d_ref, dn_ref,
             do_ref, dzp_ref, dgn_ref, dpw_ref, dps_ref):
        i = pl.program_id(0)
        first = i == 0
        _, vjp = jax.vjp(_gated_norm, o_ref[...], z_ref[...].astype(F32), gn_ref[...])
        do, dz, dgn = vjp(d_ref[:, :W].astype(F32))
        do_ref[...] = do
        dzp_ref[:, :W] = dz.astype(BF16)
        _accum(dgn_ref, dgn, first)
        pp = jnp.where(first, 0.0, pp_ref[...].astype(F32))
        ext = jnp.concatenate([pp, p_ref[...].astype(F32)], axis=0)
        pooled = (_window_sums(ext, _shift_down)[HALO:] / _pool_counts(i * tm, tm, W)) - ext[HALO:]
        dy_ext = jnp.concatenate([d_ref[:, W:].astype(F32), jnp.where(i < nrow - 1, dn_ref[...].astype(F32), 0.0)], axis=0)
        dys_ext = dy_ext * ps_ref[...]
        dpooled, dscale = [], []
        for gi in range(G):
            sl = slice(gi * GDN_D, (gi + 1) * GDN_D)
            dpooled.append(_dot_nt(dys_ext[:, sl], pw_ref[gi]))
            y = _dot(pooled[:, sl], pw_ref[gi])
            dscale.append(jnp.sum(dy_ext[:tm, sl] * y, axis=0, keepdims=True))
            _accum(dpw_ref.at[gi], _dot_tn(pooled[:, sl], dys_ext[:tm, sl]), first)
        dpooled = jnp.concatenate(dpooled, axis=1)
        _accum(dps_ref, jnp.concatenate(dscale, axis=1), first)
        dmean = dpooled / _pool_counts(i * tm, tm + HALO, W)
        dp = _window_sums(dmean, _shift_up)[:tm] - dpooled[:tm]
        dzp_ref[:, W:] = dp.astype(BF16)

    return pl.pallas_call(
        body, name=name, grid=(nrow,),
        in_specs=[pl.BlockSpec((tm, W), lambda i: (i, 0)), pl.BlockSpec((tm, W), lambda i: (i, 3)),
                  pl.BlockSpec((tm, W), lambda i: (i, 4)),
                  pl.BlockSpec((HALO, W), lambda i: (jnp.maximum(i * per - 1, 0), 4)),
                  pl.BlockSpec((1, GDN_D), lambda i: (0, 0)),
                  pl.BlockSpec((G, GDN_D, GDN_D), lambda i: (0, 0, 0)),
                  pl.BlockSpec((1, W), lambda i: (0, 0)),
                  pl.BlockSpec((tm, 2 * W), lambda i: (i, 0)),
                  pl.BlockSpec((HALO, W), lambda i: (jnp.minimum((i + 1) * per, last), 1))],
        out_specs=[pl.BlockSpec((tm, W), lambda i: (i, 0)), pl.BlockSpec((tm, 2 * W), lambda i: (i, 0)),
                   pl.BlockSpec((1, GDN_D), lambda i: (0, 0)), pl.BlockSpec((G, GDN_D, GDN_D), lambda i: (0, 0, 0)),
                   pl.BlockSpec((1, W), lambda i: (0, 0))],
        out_shape=[_sds((T, W), F32), _sds((T, 2 * W), BF16), _sds((1, GDN_D), F32), _sds((G, GDN_D, GDN_D), F32),
                   _sds((1, W), F32)],
        compiler_params=_cp(("arbitrary",)),
    )(o, proj, proj, proj, gdn_norm, pool_w, pool_scale, dcat, dcat)


def _head_norm(x, gain):
    return x * lax.rsqrt(jnp.mean(x * x, axis=-1, keepdims=True) + RMS_EPS) * gain


def _att_scores(q, k, slope, has_prev):
    B = ATT_BLOCK
    a = lax.broadcasted_iota(jnp.int32, (B, 2 * B), 0)
    j = lax.broadcasted_iota(jnp.int32, (B, 2 * B), 1)
    rel = B + a - j
    mask = (rel >= 0) & (rel <= B) & ((j >= B) | has_prev)
    s = _dot_nt(q, k) - slope * rel.astype(F32)
    return jnp.where(mask, s, NEG), mask


def _alibi_slope(h, dil):
    return dil * (2.0 ** (-8.0 * (h + 1) / ATT_HEADS))


def att_fwd(qk, qkv, name):
    dil, L, _ = qk.shape
    Wd = ATT_HEADS * ATT_DH
    nb = L // ATT_BLOCK
    B = ATT_BLOCK

    def body(q_ref, kc_ref, kp_ref, vc_ref, vp_ref, o_ref, l_ref):
        has_prev = pl.program_id(1) > 0
        for h in range(ATT_HEADS):
            sl = slice(h * ATT_DH, (h + 1) * ATT_DH)
            k = jnp.concatenate([kp_ref[:, sl], kc_ref[:, sl]], axis=0)
            v = jnp.concatenate([vp_ref[:, sl], vc_ref[:, sl]], axis=0)
            s, _ = _att_scores(q_ref[:, sl], k, _alibi_slope(h, dil), has_prev)
            m = jnp.max(s, axis=-1, keepdims=True)
            p = jnp.exp(s - m)
            l = jnp.sum(p, axis=-1, keepdims=True)
            o_ref[:, sl] = (_dot(p, v) / l).astype(BF16)
            l_ref[:, sl] = jnp.broadcast_to(m + jnp.log(l), (B, ATT_DH))

    cur = lambda t: pl.BlockSpec((None, B, Wd), lambda r, n: (r, n, t))
    prev = lambda t: pl.BlockSpec((None, B, Wd), lambda r, n: (r, jnp.maximum(n - 1, 0), t))
    out = pl.BlockSpec((None, B, Wd), lambda r, n: (r, n, 0))
    return pl.pallas_call(
        body, name=name, grid=(dil, nb),
        in_specs=[cur(0), cur(1), prev(1), cur(2), prev(2)],
        out_specs=[out, out], out_shape=[_sds((dil, L, Wd), BF16), _sds((dil, L, Wd), F32)],
        compiler_params=_cp(("parallel", "arbitrary")),
    )(qk, qk, qk, qkv, qkv)


def att_bwd(qk, qkv, lse, do, dd, name, carry=None):
    dil, L, _ = qk.shape
    Wd = ATT_HEADS * ATT_DH
    nb = L // ATT_BLOCK
    B = ATT_BLOCK
    carry = list(carry) if carry else []
    n_c = len(carry)

    def body(*refs):
        q_ref, kc_ref, kp_ref, vc_ref, vp_ref, l_ref, do_ref, dd_ref = refs[:8]
        dq_ref, dk_ref, dv_ref = refs[8 + n_c:11 + n_c]
        ck_scr, cv_scr = refs[11 + 2 * n_c:13 + 2 * n_c]
        r, n = pl.program_id(0), pl.program_id(1)
        has_prev = n > 0
        if n_c:
            ex = _ChipExchange(refs[8:8 + n_c], refs[11 + n_c:11 + 2 * n_c], *refs[13 + 2 * n_c:])
            pl.when((r == 0) & (n == 0))(ex.start)

        @pl.when(n < nb)
        def _():
            for h in range(ATT_HEADS):
                sl = slice(h * ATT_DH, (h + 1) * ATT_DH)
                q = q_ref[:, sl]
                k = jnp.concatenate([kp_ref[:, sl], kc_ref[:, sl]], axis=0)
                v = jnp.concatenate([vp_ref[:, sl], vc_ref[:, sl]], axis=0)
                do = do_ref[:, sl]
                s, mask = _att_scores(q, k, _alibi_slope(h, dil), has_prev)
                p = jnp.where(mask, jnp.exp(s - l_ref[:, h * ATT_DH:h * ATT_DH + 1]), 0.0)
                delta = jnp.sum(dd_ref[:, sl].astype(F32), axis=-1, keepdims=True)
                ds = p * (_dot_nt(do, v) - delta)
                dq_ref[:, sl] = _dot(ds, k).astype(BF16)
                dk = _dot_tn(ds, q)
                dv = _dot_tn(p, do)
                dk_ref[:, sl] = (jnp.where(has_prev, ck_scr[:, sl] + dk[:B], 0.0)).astype(BF16)
                dv_ref[:, sl] = (jnp.where(has_prev, cv_scr[:, sl] + dv[:B], 0.0)).astype(BF16)
                ck_scr[:, sl] = dk[B:]
                cv_scr[:, sl] = dv[B:]

        @pl.when(n == nb)
        def _():
            dk_ref[...] = ck_scr[...].astype(BF16)
            dv_ref[...] = cv_scr[...].astype(BF16)

        if n_c:
            pl.when((r == dil - 1) & (n == nb))(ex.finish)

    cur = lambda t: pl.BlockSpec((None, B, Wd), lambda r, n: (r, jnp.minimum(n, nb - 1), t))
    prev = lambda t: pl.BlockSpec((None, B, Wd), lambda r, n: (r, jnp.clip(n - 1, 0, nb - 1), t))
    kv_out = pl.BlockSpec((None, B, Wd), lambda r, n: (r, jnp.maximum(n - 1, 0), 0))
    return pl.pallas_call(
        body, name=name, grid=(dil, nb + 1),
        in_specs=[cur(0), cur(1), prev(1), cur(2), prev(2), cur(0), cur(0), cur(0)] + [_ANY] * n_c,
        out_specs=[cur(0), kv_out, kv_out] + [_ANY] * n_c,
        out_shape=[_sds((dil, L, Wd), BF16)] * 3 + [_sds(a.shape, a.dtype) for a in carry],
        scratch_shapes=[pltpu.VMEM((B, Wd), F32), pltpu.VMEM((B, Wd), F32)] + (_chip_exchange_scratch(n_c) if n_c else []),
        compiler_params=_cp(("arbitrary", "arbitrary") if n_c else ("parallel", "arbitrary")),
    )(qk, qk, qk, qkv, qkv, lse, do, dd, *carry)


def qk_norm_bwd(qkv, dq, dk, q_norm, k_norm, name):
    T = qkv.shape[0]
    Wd = dq.shape[1]
    tm = _row_tile(T)

    def body(q_ref, k_ref, dq_ref, dk_ref, qn_ref, kn_ref, oq_ref, ok_ref, dqn_ref, dkn_ref):
        first = pl.program_id(0) == 0
        for x_ref, d_ref, g_ref, o_ref, dg_ref, scale in ((q_ref, dq_ref, qn_ref, oq_ref, dqn_ref, ATT_DH ** -0.5),
                                                         (k_ref, dk_ref, kn_ref, ok_ref, dkn_ref, 1.0)):
            dg = jnp.zeros((1, ATT_DH), F32)
            for h in range(Wd // ATT_DH):
                sl = slice(h * ATT_DH, (h + 1) * ATT_DH)
                _, vjp = jax.vjp(lambda x, g: _head_norm(x, g) * scale, x_ref[:, sl].astype(F32), g_ref[...])
                dx, dg_h = vjp(d_ref[:, sl].astype(F32))
                o_ref[:, sl] = dx.astype(BF16)
                dg += dg_h
            _accum(dg_ref, dg, first)

    row = lambda t: pl.BlockSpec((tm, Wd), lambda i: (i, t))
    vec = pl.BlockSpec((1, ATT_DH), lambda i: (0, 0))
    return pl.pallas_call(
        body, name=name, grid=(T // tm,),
        in_specs=[row(0), row(1), row(0), row(0), vec, vec], out_specs=[row(0), row(0), vec, vec],
        out_shape=[_sds((T, Wd), BF16)] * 2 + [_sds((1, ATT_DH), F32)] * 2,
        compiler_params=_cp(("arbitrary",)),
    )(qkv, qkv, dq, dk, q_norm, k_norm)


def _merge_weights(l0, l1, l2):
    m = jnp.maximum(jnp.maximum(l0, l1), l2)
    e = [jnp.exp(l - m) for l in (l0, l1, l2)]
    tot = e[0] + e[1] + e[2]
    return [x / tot for x in e]


def _merge_specs(arrays, tm):
    return [pl.BlockSpec((a.shape[0], tm // a.shape[0], a.shape[2]), lambda i: (0, i, 0)) for a in arrays]


def _merge_load(refs, dils, tm, scr):
    vals = [ref[...].astype(F32).reshape(tm, ref.shape[-1]) for ref in refs]
    return [v if dl == 1 else _interleave(v, scr, dl) for v, dl in zip(vals, dils)]


def merge_fwd(outs, lses, name):
    Wd = outs[0].shape[2]
    T = outs[0].shape[0] * outs[0].shape[1]
    tm = _row_tile(T)
    dils = [a.shape[0] for a in outs] * 2

    def body(*refs):
        vals = _merge_load(refs[:6], dils, tm, refs[7])
        w = _merge_weights(*vals[3:])
        refs[6][...] = (w[0] * vals[0] + w[1] * vals[1] + w[2] * vals[2]).astype(BF16)

    return pl.pallas_call(body, name=name, grid=(T // tm,), in_specs=_merge_specs(list(outs) + list(lses), tm),
                          out_specs=pl.BlockSpec((tm, Wd), lambda i: (i, 0)), out_shape=_sds((T, Wd), BF16),
                          scratch_shapes=[pltpu.VMEM((Wd // LANES, tm, LANES), F32)],
                          compiler_params=_cp(("parallel",)))(*outs, *lses)


def merge_bwd(outs, lses, d, name):
    Wd = outs[0].shape[2]
    T = outs[0].shape[0] * outs[0].shape[1]
    tm = 256 if T % 256 == 0 else T
    dils = [a.shape[0] for a in outs] * 2

    def body(*refs):
        scr = refs[13]
        vals = _merge_load(refs[:6], dils, tm, scr)
        w = _merge_weights(*vals[3:])
        dv = refs[6][...].astype(F32)
        dvm = dv * (w[0] * vals[0] + w[1] * vals[1] + w[2] * vals[2])
        for g in range(3):
            for ref, val in ((refs[7 + g], w[g] * dv), (refs[10 + g], w[g] * dvm)):
                ref[...] = (val if dils[g] == 1 else _deinterleave(val, scr, dils[g])).reshape(ref.shape).astype(BF16)

    specs = _merge_specs(list(outs) + list(lses), tm)
    res = pl.pallas_call(body, name=name, grid=(T // tm,),
                         in_specs=specs + [pl.BlockSpec((tm, Wd), lambda i: (i, 0))], out_specs=specs,
                         out_shape=[_sds(a.shape, BF16) for a in list(outs) + list(outs)],
                         scratch_shapes=[pltpu.VMEM((Wd // LANES, tm, LANES), F32)],
                         compiler_params=_cp(("parallel",)))(*outs, *lses, d)
    return res[:3], res[3:]


def loss_head(y, target, name):
    T, D = y.shape
    tm = _row_tile(T)

    def body(y_ref, t_ref, l_ref, dy_ref):
        err = y_ref[...] - t_ref[...]
        dy_ref[...] = err * (1.0 / D)
        part = 0.5 * jnp.sum(jnp.sum(err * err, axis=1, keepdims=True) * (1.0 / D), axis=0, keepdims=True)
        _accum(l_ref, jnp.broadcast_to(part, (1, 128)), pl.program_id(0) == 0)

    row = pl.BlockSpec((tm, D), lambda i: (i, 0))
    return pl.pallas_call(body, name=name, grid=(T // tm,), in_specs=[row, row],
                          out_specs=[pl.BlockSpec((1, 128), lambda i: (0, 0)), row],
                          out_shape=[_sds((1, 128), F32), _sds((T, D), F32)],
                          compiler_params=_cp(("arbitrary",)))(y, target)


def ada_fwd(c_all, w, b, name):
    depth, D, n = w.shape

    def body(c_ref, w_ref, b_ref, o_ref):
        o_ref[...] = _dot(_silu(c_ref[...]), w_ref[...]) + b_ref[...]

    return pl.pallas_call(
        body, name=name, grid=(depth,),
        in_specs=[pl.BlockSpec((N_DEV, D), lambda i: (0, 0)), pl.BlockSpec((None, D, n), lambda i: (i, 0, 0)),
                  pl.BlockSpec((None, 1, n), lambda i: (i, 0, 0))],
        out_specs=pl.BlockSpec((None, N_DEV, n), lambda i: (i, 0, 0)),
        out_shape=_sds((depth, N_DEV, n), F32), compiler_params=_cp(("parallel",)),
    )(c_all, w, b)


def ada_bwd(c_all, dmod, name):
    depth, _, n = dmod.shape
    D = c_all.shape[1]

    def body(c_ref, d_ref, o_ref):
        o_ref[...] = _dot_tn(_silu(c_ref[...]), d_ref[...])

    return pl.pallas_call(
        body, name=name, grid=(depth,),
        in_specs=[pl.BlockSpec((N_DEV, D), lambda i: (0, 0)), pl.BlockSpec((None, N_DEV, n), lambda i: (i, 0, 0))],
        out_specs=pl.BlockSpec((None, D, n), lambda i: (i, 0, 0)),
        out_shape=_sds((depth, D, n), F32), compiler_params=_cp(("parallel",)),
    )(c_all, dmod)


def adamw(w, m, v, gparts, name):
    R, C = w.shape
    k = gparts.shape[0]
    tr = R
    for cand in (512, 256, 128, 64, 32, 16, 8):
        if R % cand == 0 and cand * C * 4 <= 2 * 1024 * 1024:
            tr = cand
            break
    bc1 = 1.0 - ADAM_B1 ** ADAM_STEP
    bc2 = 1.0 - ADAM_B2 ** ADAM_STEP

    def body(w_ref, m_ref, v_ref, gp_ref, g_ref, d_ref, nm_ref, nv_ref):
        g = gp_ref[0].astype(F32)
        for q in range(1, k):
            g = g + gp_ref[q].astype(F32)
        nm = ADAM_B1 * m_ref[...] + (1.0 - ADAM_B1) * g
        nv = ADAM_B2 * v_ref[...] + (1.0 - ADAM_B2) * (g * g)
        g_ref[...] = g
        nm_ref[...] = nm
        nv_ref[...] = nv
        d_ref[...] = -ADAM_LR * ((nm / bc1) / (jnp.sqrt(nv / bc2) + ADAM_EPS) + ADAM_WD * w_ref[...])

    row = pl.BlockSpec((tr, C), lambda i: (i, 0))
    return pl.pallas_call(
        body, name=name, grid=(R // tr,),
        in_specs=[row, row, row, pl.BlockSpec((k, tr, C), lambda i: (0, i, 0))],
        out_specs=[row] * 4, out_shape=[_sds((R, C), F32)] * 4, compiler_params=_cp(("parallel",)),
    )(w, m, v, gparts)


def _mesh_pos():
    return lax.axis_index("x"), lax.axis_index("y"), lax.axis_index("c")


def _other_chips(x, y):
    return [(1 - x, y), (x, 1 - y), (1 - x, 1 - y)]


_ANY = pl.BlockSpec(memory_space=pl.ANY)


class _Gather:
    def __init__(self, ins, outs, send_sems, recv_sems, local_sems):
        self.ins, self.outs, self.n = ins, outs, len(ins)
        self.send_sems, self.recv_sems, self.local_sems = send_sems, recv_sems, local_sems
        self.x, self.y, self.c = _mesh_pos()
        self.me, self.sibling = (self.x, self.y, self.c), (self.x, self.y, 1 - self.c)
        self.chips = _other_chips(self.x, self.y)

    def _copy(self, a, k, block, to, src=None):
        dst = self.outs[a].at[4 * block[0] + 2 * block[1] + block[2]]
        return pltpu.make_async_remote_copy(
            src_ref=dst if src is None else src, dst_ref=dst, send_sem=self.send_sems.at[a, k],
            recv_sem=self.recv_sems.at[a, k], device_id=to, device_id_type=MESH)

    def _mine(self):
        me_slot = 4 * self.x + 2 * self.y + self.c
        return [pltpu.make_async_copy(self.ins[a], self.outs[a].at[me_slot], self.local_sems.at[a]) for a in range(self.n)]

    def _first(self):
        out = []
        for a in range(self.n):
            out.append(self._copy(a, 0, self.me, self.sibling, src=self.ins[a]))
            out += [self._copy(a, 1 + j, self.me, (*chip, self.c), src=self.ins[a]) for j, chip in enumerate(self.chips)]
        return out

    def _passed(self):
        return [self._copy(a, 4 + j, (*chip, self.c), self.sibling) for j, chip in enumerate(self.chips) for a in range(self.n)]

    def start(self):
        for cp in self._mine() + self._first():
            cp.start()

    def forward(self):
        for j, chip in enumerate(self.chips):
            for a in range(self.n):
                self._copy(a, 1 + j, (*chip, self.c), self.me).wait_recv()
                self._copy(a, 4 + j, (*chip, self.c), self.sibling).start()

    def finish(self):
        for a in range(self.n):
            self._copy(a, 0, self.sibling, self.me).wait_recv()
            for j, chip in enumerate(self.chips):
                self._copy(a, 4 + j, (*chip, 1 - self.c), self.me).wait_recv()
        for cp in self._first() + self._passed():
            cp.wait_send()
        for cp in self._mine():
            cp.wait()


def _gather_scratch(n):
    return [pltpu.SemaphoreType.DMA((n, 7)), pltpu.SemaphoreType.DMA((n, 7)), pltpu.SemaphoreType.DMA((n,))]


def all_gather(shards, name):
    n = len(shards)

    def body(*refs):
        g = _Gather(refs[:n], refs[n:2 * n], *refs[2 * n:])
        g.start()
        g.forward()
        g.finish()

    return pl.pallas_call(
        body, name=name, in_specs=[_ANY] * n, out_specs=[_ANY] * n,
        out_shape=[_sds((N_DEV,) + s.shape, s.dtype) for s in shards],
        scratch_shapes=_gather_scratch(n),
        compiler_params=pltpu.CompilerParams(has_side_effects=True),
    )(*shards)


def exchange_pair(stacked, name):
    n = len(stacked)

    def body(*refs):
        ins, outs = refs[:n], refs[n:2 * n]
        send_sems, recv_sems = refs[2 * n:]
        x, y, c = _mesh_pos()
        copies = [pltpu.make_async_remote_copy(
            src_ref=ins[a].at[2 * q + (1 - c)], dst_ref=outs[a].at[q], send_sem=send_sems.at[a, q],
            recv_sem=recv_sems.at[a, q], device_id=(x, y, 1 - c), device_id_type=MESH)
            for a in range(n) for q in range(4)]
        for cp in copies:
            cp.start()
        for cp in copies:
            cp.wait()

    return pl.pallas_call(
        body, name=name, in_specs=[_ANY] * n, out_specs=[_ANY] * n,
        out_shape=[_sds((4,) + s.shape[1:], s.dtype) for s in stacked],
        scratch_shapes=[pltpu.SemaphoreType.DMA((n, 4)), pltpu.SemaphoreType.DMA((n, 4))],
        compiler_params=pltpu.CompilerParams(has_side_effects=True),
    )(*stacked)


def pair_add(stacked, got, c_idx, name):
    _, R, C = stacked.shape
    tr = R
    for cand in (512, 256, 128, 64, 32, 16):
        if R % cand == 0 and cand * C * 4 <= 2 * 1024 * 1024:
            tr = cand
            break

    def body(c_ref, s_ref, g_ref, o_ref):
        o_ref[...] = (s_ref[...].astype(F32) + g_ref[...].astype(F32)).astype(BF16)

    return pl.pallas_call(
        body, name=name,
        grid_spec=pltpu.PrefetchScalarGridSpec(
            num_scalar_prefetch=1, grid=(4, R // tr),
            in_specs=[pl.BlockSpec((None, tr, C), lambda q, i, c_ref: (2 * q + c_ref[0], i, 0)),
                      pl.BlockSpec((None, tr, C), lambda q, i, c_ref: (q, i, 0))],
            out_specs=pl.BlockSpec((None, tr, C), lambda q, i, c_ref: (q, i, 0))),
        out_shape=_sds((4, R, C), BF16),
        compiler_params=_cp(("parallel", "parallel")),
    )(c_idx, stacked, got)


def exchange_chips(parts, name):
    n = len(parts)

    def body(*refs):
        ex = _ChipExchange(refs[:n], refs[n:2 * n], *refs[2 * n:])
        ex.start()
        ex.finish()

    return pl.pallas_call(
        body, name=name, in_specs=[_ANY] * n, out_specs=[_ANY] * n,
        out_shape=[_sds(s.shape, s.dtype) for s in parts],
        scratch_shapes=_chip_exchange_scratch(n),
        compiler_params=pltpu.CompilerParams(has_side_effects=True),
    )(*parts)


class _ChipExchange:
    def __init__(self, ins, outs, send_sems, recv_sems, local_sems):
        self.ins, self.outs, self.n = ins, outs, len(ins)
        self.send_sems, self.recv_sems, self.local_sems = send_sems, recv_sems, local_sems
        self.x, self.y, self.c = _mesh_pos()

    def _copies(self):
        myq = 2 * self.x + self.y
        mine = [pltpu.make_async_copy(self.ins[a].at[myq], self.outs[a].at[myq], self.local_sems.at[a])
                for a in range(self.n)]
        remote = [pltpu.make_async_remote_copy(
            src_ref=self.ins[a].at[2 * chip[0] + chip[1]], dst_ref=self.outs[a].at[myq], send_sem=self.send_sems.at[a, j],
            recv_sem=self.recv_sems.at[a, j], device_id=(*chip, self.c), device_id_type=MESH)
            for a in range(self.n) for j, chip in enumerate(_other_chips(self.x, self.y))]
        return mine, remote

    def start(self):
        mine, remote = self._copies()
        for cp in mine + remote:
            cp.start()

    def finish(self):
        mine, remote = self._copies()
        for cp in remote + mine:
            cp.wait()


def _chip_exchange_scratch(n):
    return [pltpu.SemaphoreType.DMA((n, 3)), pltpu.SemaphoreType.DMA((n, 3)), pltpu.SemaphoreType.DMA((n,))]


def _pad_lanes(v, start, width=128):
    return jnp.pad(v.astype(F32), (start, width - start - v.shape[0]))[None]


def _carry(results, n_own):
    results = list(results)
    return results[:n_own] + [results[n_own:]]


def _layer_fwd(i, x, mod, P, next_shards=None):
    T, D = x.shape
    next_shards = next_shards or {}
    next_ffn = {k: v for k, v in next_shards.items() if k.startswith("ffn")}
    next_mix = {k: v for k, v in next_shards.items() if not k.startswith("ffn")}
    sh_m, sc_m, g_m, sh_f, sc_f, g_f = [mod[k:k + 1] for k in range(6)]
    sv = {"x0": x}
    tag = f"l{i}"
    if i % 2 == 0:
        e = i // 2
        h, proj = norm_proj(x, P["norm_mix"][i:i + 1], sc_m, sh_m, P["ev_main"][e], P["ev_main"][e].shape[1], tag + "_in")
        ba = matmul_nn(h, P["ev_ba"][e], F32, tag + "_ba")
        al = _pad_lanes(P["gdn_a_log"][e], GDN_HEADS)
        dt = _pad_lanes(P["gdn_dt_bias"][e], GDN_HEADS)
        gate = gate_fwd(ba, al, dt, tag + "_gate")
        qkv = gdn_prep_fwd(proj, P["gdn_conv_w"][e], tag + "_prep")
        *chunks, got_mix = _carry(gdn_prep_chunks_fwd(qkv, gate, tag + "_chunks", gather=list(next_mix.values())), 5)
        o, states = gdn_fwd(*chunks, gate, tag + "_gdn")
        cat = even_post_fwd(o, proj, P["gdn_norm"][e:e + 1], P["pool_w"][e], P["pool_scale"][e:e + 1], tag + "_post")
        x1, y_m = proj_res(cat, P["ev_w_out"][e], x, g_m, tag + "_out")
        sv.update(h=h, proj=proj, ba=ba, al=al, dt=dt, gate=gate, qkv=qkv, chunks=chunks, o=o, states=states, a_m=cat)
    else:
        od = i // 2
        qn, kn = P["att_q_norm"][od:od + 1], P["att_k_norm"][od:od + 1]
        h, proj, qk = [], [], []
        got_mix = []
        for gi, (_, dil) in enumerate(DIL_PATTERNS):
            hg, pg, qkg, got = _carry(norm_proj(x, P["norm_mix"][i:i + 1], sc_m, sh_m, P["od_w_in"][od], 3 * D,
                                                f"{tag}_in{gi}", cols=(gi, 1), dil=dil, qk_norms=(qn, kn),
                                                gather=list(next_mix.values()) if gi == 0 else None), 3)
            got_mix += got
            h.append(hg.reshape(T, D))
            proj.append(pg)
            qk.append(qkg)
        res = [att_fwd(qk[gi], proj[gi], f"{tag}_att{gi}") for gi in range(len(DIL_PATTERNS))]
        outs, lses = [r[0] for r in res], [r[1] for r in res]
        merged = merge_fwd(outs, lses, tag + "_merge")
        x1, y_m = proj_res(merged, P["od_w_out"][od], x, g_m, tag + "_out")
        sv.update(h=h, proj=proj, qk=qk, outs=outs, lses=lses, a_m=merged)
    hf, up, got_ffn = _carry(norm_proj(x1, P["norm_ffn"][i:i + 1], sc_f, sh_f, P["ffn_w_up"][i], 1408, tag + "_up",
                                       gather=list(next_ffn.values())), 2)
    a = ffn_mid_fwd(up, P["ffn_conv_w"][i], P["ffn_conv_b"][i:i + 1], tag + "_mid")
    x2, y_f = proj_res(a, P["ffn_w_down"][i], x1, g_f, tag + "_down")
    sv.update(y_m=y_m, x1=x1, hf=hf, up=up, a_f=a, y_f=y_f)
    got = dict(zip(list(next_mix.keys()) + list(next_ffn.keys()), list(got_mix) + list(got_ffn)))
    return x2, sv, (got if next_shards else None)


def add_layer_weights(P, gathered):
    W = GDN_HEADS * GDN_D
    for k, g in gathered.items():
        if k in _COL_SHARDED:
            full = g.transpose(1, 0, 2).reshape(g.shape[1], N_DEV * g.shape[2])
        else:
            full = g.reshape(N_DEV * g.shape[1], g.shape[2])
        if k == "ev_w_in":
            P.setdefault("ev_main", []).append(jnp.concatenate([full[:, :4 * W], full[:, 4 * W + 2 * GDN_HEADS:]], axis=1))
            P.setdefault("ev_ba", []).append(jnp.pad(full[:, 4 * W:4 * W + 2 * GDN_HEADS], ((0, 0), (0, 128 - 2 * GDN_HEADS))))
        else:
            P.setdefault(k, []).append(full)


def _put(bufs, k, li, a, b, tk, tn, name, col0=0):
    half = bufs[k][0].shape[0]
    g, loc = divmod(li, half)
    bufs[k][g] = matmul_tn(a, b, tk, tn, name, into=bufs[k][g], layer=loc, col0=col0)


CARRY_SPLIT = ((1, 3, 4, 0), (2, 5))


def _layer_bwd(i, dx2, sv, mod, P, bufs, carry=None):
    T = dx2.shape[0]
    sh_m, sc_m, g_m, sh_f, sc_f, g_f = [mod[k:k + 1] for k in range(6)]
    tag = f"b{i}"
    G = {}
    F = P["ffn_w_down"][i].shape[0]
    dy, da, dg_f = bwd_out(dx2, g_f, sv["y_f"], P["ffn_w_down"][i], F, tag + "_down")
    _put(bufs, "ffn_w_down", i, sv["a_f"], dy, F // 2, dy.shape[1], tag + "_wdown")
    dgate, dval, dcw, dcb = ffn_mid_bwd(sv["up"], da, P["ffn_conv_w"][i], P["ffn_conv_b"][i:i + 1], tag + "_mid")
    G["ffn_conv_w"], G["ffn_conv_b"] = dcw[:FFN_CONV], dcb[0]
    dx1, dnf, dsc_f, dsh_f = bwd_in([dgate, dval], P["ffn_w_up"][i], [0, 1], F, None,
                                    sv["x1"], P["norm_ffn"][i:i + 1], sc_f, sh_f, dx2, tag + "_up")
    G["norm_ffn"] = dnf[0]
    x0, h, proj = sv["x0"], sv["h"], sv["proj"]
    D = x0.shape[1]
    for k, d in enumerate((dgate, dval)):
        _put(bufs, "ffn_w_up", i, sv["hf"], d, D, F // 2, f"{tag}_wup{k}", col0=2 * k)
    gain = P["norm_mix"][i:i + 1]
    if i % 2 == 0:
        e = i // 2
        W = GDN_HEADS * GDN_D
        dy, dcat, dg_m = bwd_out(dx1, g_m, sv["y_m"], P["ev_w_out"][e], 1024, tag + "_out")
        _put(bufs, "ev_w_out", e, sv["a_m"], dy, 1024, D, tag + "_wout")
        do, dzp, dgn, dpw, dps = even_post_bwd(sv["o"], proj, P["gdn_norm"][e:e + 1], P["pool_w"][e],
                                               P["pool_scale"][e:e + 1], dcat, tag + "_post")
        G["gdn_norm"], G["pool_w"], G["pool_scale"] = dgn[0], dpw, dps[0]
        *dchunks, dgate_scan = gdn_bwd(*sv["chunks"], sv["gate"], sv["states"], do, tag + "_gdn")
        dqkv, dgate_prep = gdn_prep_chunks_bwd(sv["qkv"], sv["gate"], *dchunks, tag + "_chunks")
        dba, dal, ddt = gate_bwd(sv["ba"], sv["al"], sv["dt"], dgate_prep, dgate_scan, tag + "_gate")
        G["gdn_a_log"], G["gdn_dt_bias"] = dal[0, GDN_HEADS:2 * GDN_HEADS], ddt[0, GDN_HEADS:2 * GDN_HEADS]
        dqkv_raw, dconv = gdn_prep_bwd(proj, dqkv, P["gdn_conv_w"][e], tag + "_prep")
        G["gdn_conv_w"] = dconv[:GDN_CONV]
        w_main = P["ev_main"][e]
        dx0, dnm, dsc_m, dsh_m = bwd_in([dba, dqkv_raw, dzp], [P["ev_ba"][e], w_main[:, :3 * W], w_main[:, 3 * W:]],
                                        [0, 0, 0], None, None, x0, gain, sc_m, sh_m, dx1, tag + "_in")
        gw_qkv = matmul_tn(h, dqkv_raw, 1024, W, tag + "_win1")
        gw_zp = matmul_tn(h, dzp, 1024, W, tag + "_win2")
        gw_ba = matmul_tn(h, dba, 1024, 128, tag + "_win0")
        G["ev_w_in"] = jnp.concatenate([gw_qkv, gw_zp[:, :W], gw_ba[:, :2 * GDN_HEADS], gw_zp[:, W:]], axis=1)
    else:
        od = i // 2
        qn, kn = P["att_q_norm"][od:od + 1], P["att_k_norm"][od:od + 1]
        dy, dmerged, dg_m = bwd_out(dx1, g_m, sv["y_m"], P["od_w_out"][od], 1024, tag + "_out")
        _put(bufs, "od_w_out", od, sv["a_m"], dy, 1024, D, tag + "_wout")
        douts, dds = merge_bwd(sv["outs"], sv["lses"], dmerged, tag + "_merge")
        dh = None
        exchanged = {}
        dqn_sum, dkn_sum = 0.0, 0.0
        ng = len(DIL_PATTERNS)
        for gi, (_, dil) in enumerate(DIL_PATTERNS):
            riding = [carry[j] for j in CARRY_SPLIT[gi]] if (carry is not None and gi < len(CARRY_SPLIT)) else None
            dqs, dks, dv, *got = att_bwd(sv["qk"][gi], proj[gi], sv["lses"][gi], douts[gi], dds[gi], f"{tag}_att{gi}",
                                         carry=riding)
            if riding:
                exchanged.update(zip(CARRY_SPLIT[gi], got))
            dq, dk, dqn, dkn = qk_norm_bwd(proj[gi].reshape(T, 3 * D), dqs.reshape(T, D), dks.reshape(T, D), qn, kn,
                                           f"{tag}_qkn{gi}")
            dqn_sum, dkn_sum = dqn_sum + dqn[0], dkn_sum + dkn[0]
            for k, d in enumerate((dq, dk, dv)):
                _put(bufs, "od_w_in", od, h[gi], d.reshape(T, D), 1024, D, f"{tag}_win{gi}{k}", col0=3 * gi + k)
            cols = [3 * gi, 3 * gi + 1, 3 * gi + 2]
            ops = [d.reshape(dil, T // dil, D) if dil > 1 else d.reshape(T, D) for d in (dq, dk, dv)]
            if gi < ng - 1:
                dh = bwd_in(ops, P["od_w_in"][od], cols, D, dh, None, None, None, None, None, f"{tag}_in{gi}", dil=dil)
            else:
                dx0, dnm, dsc_m, dsh_m = bwd_in(ops, P["od_w_in"][od], cols, D, dh, x0, gain, sc_m, sh_m, dx1,
                                                f"{tag}_in{gi}", dil=dil)
        G["att_q_norm"], G["att_k_norm"] = dqn_sum, dkn_sum
    G["norm_mix"] = dnm[0]
    dmod = jnp.concatenate([dsh_m, dsc_m, dg_m, dsh_f, dsc_f, dg_f], axis=0)
    return dx0, dmod, G, ([exchanged[j] for j in range(len(carry))] if carry is not None else None)


_PER_LAYER = ("norm_mix", "norm_ffn", "ffn_conv_w", "ffn_conv_b")
_PER_EVEN = ("ev_w_in", "gdn_conv_w", "gdn_a_log", "gdn_dt_bias", "gdn_norm", "pool_w", "pool_scale")
_PER_ODD = ("att_q_norm", "att_k_norm")
_IN_PLACE = ("ffn_w_up", "ffn_w_down", "ev_w_out", "od_w_in", "od_w_out")


def device_step(x, mod, target, P, layer_shards=None, early_exchange=None):
    saved = []
    for i in range(DEPTH):
        nxt = layer_shards[i + 1] if (layer_shards is not None and i + 1 < DEPTH) else None
        x, sv, got = _layer_fwd(i, x, mod[i], P, nxt)
        if got is not None:
            add_layer_weights(P, got)
        saved.append(sv)
    loss, dx = loss_head(x, target, "loss")
    layer_grads, dmods = [None] * DEPTH, [None] * DEPTH
    bufs = {k: [lax.empty((len(P[k]) // 2,) + tuple(P[k][0].shape), BF16) for _ in range(2)] for k in _IN_PLACE}
    early_parts = None
    for i in reversed(range(DEPTH)):
        carry = None
        if i == DEPTH // 2 - 1 and early_exchange is not None:
            done = {k: bufs[k][1] for k in _IN_PLACE}
            done["ev_w_in"] = layer_grads[DEPTH // 2]["ev_w_in"][None]
            carry = early_exchange(done)
        dx, dmods[i], layer_grads[i], got = _layer_bwd(i, dx, saved[i], mod[i], P, bufs, carry)
        early_parts = got if got is not None else early_parts
    G = dict(bufs)
    G["ev_w_in"] = [layer_grads[0]["ev_w_in"][None], layer_grads[DEPTH // 2]["ev_w_in"][None]]
    G.update({k: jnp.stack([layer_grads[i][k] for i in range(DEPTH)]) for k in _PER_LAYER})
    G.update({k: jnp.stack([layer_grads[i][k] for i in range(0, DEPTH, 2)]) for k in _PER_EVEN if k != "ev_w_in"})
    G.update({k: jnp.stack([layer_grads[i][k] for i in range(1, DEPTH, 2)]) for k in _PER_ODD})
    return loss, dx, jnp.stack(dmods), G, early_parts


_WEIGHTS = ("ada_w", "ada_b", "norm_mix", "norm_ffn", "ev_w_in", "ev_w_out", "gdn_conv_w", "gdn_a_log", "gdn_dt_bias",
            "gdn_norm", "pool_w", "pool_scale", "od_w_in", "od_w_out", "att_q_norm", "att_k_norm", "ffn_w_up",
            "ffn_conv_w", "ffn_conv_b", "ffn_w_down")
_COL_SHARDED = ("ev_w_in", "od_w_in", "ffn_w_up")
_ROW_SHARDED = ("ev_w_out", "od_w_out", "ffn_w_down")
_SMALL_SHARDED = ("gdn_conv_w", "ffn_conv_w")
_REPLICATED = ("ada_b", "norm_mix", "norm_ffn", "gdn_a_log", "gdn_dt_bias", "gdn_norm", "pool_w", "pool_scale",
               "att_q_norm", "att_k_norm", "ffn_conv_b")
PACK_LANES = 128
PACK_ROWS = 8


def _pack_rows(n):
    unit = PACK_LANES * PACK_ROWS
    return -(-n // unit) * PACK_ROWS


def _pack(arrays):
    parts = []
    for a in arrays:
        flat = a.reshape(-1).astype(F32)
        rows = _pack_rows(flat.shape[0])
        parts.append(jnp.pad(flat, (0, rows * PACK_LANES - flat.shape[0])).reshape(rows, PACK_LANES))
    return jnp.concatenate(parts, axis=0)


def _unpack(packed, shapes, lead=()):
    out, row = [], 0
    for s in shapes:
        n = math.prod(s)
        rows = _pack_rows(n)
        blk = packed[..., row:row + rows, :].reshape(lead + (rows * PACK_LANES,))
        out.append(blk[..., :n].reshape(lead + tuple(s)))
        row += rows
    return out


def _unshard_cols(g):
    _, L, R, n = g.shape
    return g.transpose(1, 2, 0, 3).reshape(L, R, N_DEV * n)


def _shard_cols(full):
    L, R, N = full.shape
    n = N // N_DEV
    return full.reshape(L * R, N_DEV, n).transpose(1, 0, 2)


def _shard_rows(full):
    L, R, C = full.shape
    r = R // N_DEV
    return full.reshape(L, N_DEV, r, C).transpose(1, 0, 2, 3).reshape(N_DEV, L * r, C)


def kernel(x, c, ada_w, ada_b, norm_mix, norm_ffn, ev_w_in, ev_w_out, gdn_conv_w, gdn_a_log, gdn_dt_bias, gdn_norm, pool_w, pool_scale, od_w_in, od_w_out, att_q_norm, att_k_norm, ffn_w_up, ffn_conv_w, ffn_conv_b, ffn_w_down, loss_target, m_ada_w, m_ada_b, m_norm_mix, m_norm_ffn, m_ev_w_in, m_ev_w_out, m_gdn_conv_w, m_gdn_a_log, m_gdn_dt_bias, m_gdn_norm, m_pool_w, m_pool_scale, m_od_w_in, m_od_w_out, m_att_q_norm, m_att_k_norm, m_ffn_w_up, m_ffn_conv_w, m_ffn_conv_b, m_ffn_w_down, v_ada_w, v_ada_b, v_norm_mix, v_norm_ffn, v_ev_w_in, v_ev_w_out, v_gdn_conv_w, v_gdn_a_log, v_gdn_dt_bias, v_gdn_norm, v_pool_w, v_pool_scale, v_od_w_in, v_od_w_out, v_att_q_norm, v_att_k_norm, v_ffn_w_up, v_ffn_conv_w, v_ffn_conv_b, v_ffn_w_down):
    args = locals()
    Wl = {k: args[k] for k in _WEIGHTS}
    Ml = {k: args["m_" + k] for k in _WEIGHTS}
    Vl = {k: args["v_" + k] for k in _WEIGHTS}
    mx, my, mc = _mesh_pos()
    dev = 4 * mx + 2 * my + mc
    T, D = x.shape[1], x.shape[2]
    x2d, tgt = x.reshape(T, D), loss_target.reshape(T, D)

    small_shapes = [c.shape] + [Wl[k].shape for k in _SMALL_SHARDED]
    big = list(_COL_SHARDED + _ROW_SHARDED)
    layer_shards = []
    for i in range(DEPTH):
        mixer = ("ev_w_in", "ev_w_out") if i % 2 == 0 else ("od_w_in", "od_w_out")
        shards = {k: Wl[k][i // 2].astype(BF16) for k in mixer}
        shards.update({k: Wl[k][i].astype(BF16) for k in ("ffn_w_up", "ffn_w_down")})
        layer_shards.append(shards)
    gathered = all_gather([_pack([c] + [Wl[k] for k in _SMALL_SHARDED])] + list(layer_shards[0].values()), "gather_w0")
    c_all, conv_g, conv_f = _unpack(gathered[0], small_shapes, lead=(N_DEV,))
    c_all = c_all.reshape(N_DEV, D)
    P = {k: Wl[k] for k in _REPLICATED}
    P["gdn_conv_w"], P["ffn_conv_w"] = _unshard_cols(conv_g), _unshard_cols(conv_f)
    add_layer_weights(P, dict(zip(layer_shards[0].keys(), gathered[1:])))

    n_ada = ada_w.shape[2]
    b_cols = lax.dynamic_slice_in_dim(ada_b, dev * n_ada, n_ada, axis=1)
    mod_cols = ada_fwd(c_all, ada_w, b_cols[:, None, :], "ada_fwd")
    mod_all, = all_gather([mod_cols], "gather_mod")
    mod = lax.dynamic_index_in_dim(mod_all, dev, axis=2, keepdims=False)
    mod = mod.transpose(1, 0, 2).reshape(DEPTH, 6, D)

    c_idx = jnp.reshape(mc, (1,)).astype(jnp.int32)

    def chip_partials(grads, tag):
        stacked = [_shard_cols(grads[k]) for k in _COL_SHARDED] + [_shard_rows(grads[k]) for k in _ROW_SHARDED]
        got = exchange_pair(stacked, "rs_pair" + tag)
        return [pair_add(s, g, c_idx, f"rs_add_{k}{tag}") for s, g, k in zip(stacked, got, big)]

    loss, dx, dmod, G, early_parts = device_step(x2d, mod, tgt, P, layer_shards, lambda g: chip_partials(g, "_b"))
    loss = lax.psum(loss[0, 0], ("x", "y", "c"))

    G["ada_b"] = dmod.reshape(DEPTH, 6 * D)
    small = list(_REPLICATED) + list(_SMALL_SHARDED)
    parts_all, = all_gather([_pack([G[k] for k in small])], "gather_small")
    zeros = {k: jnp.zeros_like(G[k]) for k in _SMALL_SHARDED}
    packs = [_pack([src[k] for k in _REPLICATED] + [zeros[k] for k in _SMALL_SHARDED]) for src in (Wl, Ml, Vl)]
    res = adamw(*packs, parts_all, "adamw_small")
    shapes = [G[k].shape for k in small]
    out_g, out_d, out_m, out_v = ({k: a for k, a in zip(small, _unpack(r, shapes))} for r in res)
    dmod_all = _unpack(parts_all, shapes, lead=(N_DEV,))[0].reshape(N_DEV, DEPTH, 6 * D)
    dm_cols = lax.dynamic_slice_in_dim(dmod_all, dev * n_ada, n_ada, axis=2).transpose(1, 0, 2)
    g_ada = ada_bwd(c_all, dm_cols, "ada_bwd")

    def flat2(a):
        return a.reshape(-1, a.shape[-1])

    loc = {k: lax.dynamic_slice_in_dim(out_g[k], dev * Wl[k].shape[-1], Wl[k].shape[-1], axis=out_g[k].ndim - 1)
           for k in _SMALL_SHARDED}
    res = adamw(*[_pack([src[k] for k in _SMALL_SHARDED]) for src in (Wl, Ml, Vl)],
                _pack([loc[k] for k in _SMALL_SHARDED])[None], "adamw_conv")
    for dst, r in zip((out_g, out_d, out_m, out_v), res):
        dst.update(zip(_SMALL_SHARDED, _unpack(r, [Wl[k].shape for k in _SMALL_SHARDED])))
    res = adamw(flat2(ada_w), flat2(m_ada_w), flat2(v_ada_w), flat2(g_ada)[None], "adamw_ada")
    for dst, r in zip((out_g, out_d, out_m, out_v), res):
        dst["ada_w"] = r.reshape(ada_w.shape)

    parts = exchange_chips(chip_partials({k: G[k][0] for k in big}, "_a"), "rs_chips")
    for k, p, q in zip(big, parts, early_parts):
        res = adamw(flat2(Wl[k]), flat2(Ml[k]), flat2(Vl[k]), jnp.concatenate([p, q], axis=1), "adamw_" + k)
        for dst, r in zip((out_g, out_d, out_m, out_v), res):
            dst[k] = r.reshape(Wl[k].shape)

    return (loss, dx.reshape(x.shape), *[out_g[k] for k in _WEIGHTS], *[out_d[k] for k in _WEIGHTS],
            *[out_m[k] for k in _WEIGHTS], *[out_v[k] for k in _WEIGHTS])
```

```python
import functools
import math

import jax
import jax.numpy as jnp
from jax import lax
from jax.experimental import pallas as pl
from jax.experimental.pallas import tpu as pltpu

F32 = jnp.float32
BF16 = jnp.bfloat16
HI = lax.Precision.HIGHEST
MESH = pl.DeviceIdType.MESH

N_DEV = 8
RMS_EPS = 1e-6
DEPTH = 4
GDN_HEADS = 4
GDN_D = 128
GDN_CHUNK = 64
GDN_CONV = 4
POOL_WINDOWS = (2, 4, 8, 16)
DIL_PATTERNS = ((128, 1), (512, 4), (2048, 16))
ATT_HEADS = 8
ATT_DH = 128
ATT_BLOCK = 128
FFN_CONV = 3
ADAM_LR, ADAM_B1, ADAM_B2, ADAM_EPS, ADAM_WD, ADAM_STEP = 0.001, 0.9, 0.999, 1e-08, 0.01, 10

HALO = 16
NEG = -1e30
VMEM_LIMIT_BYTES = 56 * 1024 * 1024


def _cp(sem=None, **kw):
    return pltpu.CompilerParams(dimension_semantics=sem, vmem_limit_bytes=VMEM_LIMIT_BYTES, **kw)


def _sds(shape, dtype):
    return jax.ShapeDtypeStruct(tuple(shape), dtype)


def _dot(a, b):
    return jnp.dot(a.astype(BF16), b.astype(BF16), preferred_element_type=F32)


def _dot_nt(a, b):
    return lax.dot_general(a.astype(BF16), b.astype(BF16), (((1,), (1,)), ((), ())), preferred_element_type=F32)


def _dot_tn(a, b):
    return lax.dot_general(a.astype(BF16), b.astype(BF16), (((0,), (0,)), ((), ())), preferred_element_type=F32)


def _dot_hi(a, b):
    return jnp.dot(a, b, preferred_element_type=F32, precision=HI)


def _silu(x):
    return x * jax.nn.sigmoid(x)


def _modnorm(x, gain, sc, sh):
    y = x * lax.rsqrt(jnp.mean(x * x, axis=-1, keepdims=True) + RMS_EPS)
    return y * gain * (1.0 + sc) + sh


def _row_tile(T):
    return 512 if T % 512 == 0 else T


LANES = 128


def _deinterleave(val, scr, dil):
    tm, width = val.shape
    sub = tm // dil
    ncb = width // LANES
    for cb in range(ncb):
        scr[cb] = val[:, cb * LANES:(cb + 1) * LANES]
    return jnp.concatenate([jnp.concatenate([scr.at[cb][pl.ds(r, sub, stride=dil), :] for cb in range(ncb)], axis=1)
                            for r in range(dil)], axis=0)


def _interleave(val, scr, dil):
    tm, width = val.shape
    sub = tm // dil
    ncb = width // LANES
    for r in range(dil):
        for cb in range(ncb):
            scr.at[cb][pl.ds(r, sub, stride=dil), :] = val[r * sub:(r + 1) * sub, cb * LANES:(cb + 1) * LANES]
    return jnp.concatenate([scr[cb] for cb in range(ncb)], axis=1)


def norm_proj(x, gain, sc, sh, w, tn, name, cols=None, dil=1, qk_norms=None, gather=None):
    T, D = x.shape
    c0, ncol = cols if cols is not None else (0, w.shape[1] // tn)
    N = ncol * tn
    tm = 1024 if (ncol > 1 and dil == 1 and T % 1024 == 0) else _row_tile(T)
    sub = tm // dil
    n_g = len(gather) if gather else 0
    n_own_in = 5 + (2 if qk_norms is not None else 0)
    n_own_out = 2 + (1 if qk_norms is not None else 0)
    n_in, n_out = n_own_in + n_g, n_own_out + n_g
    n_scr = 1 + (1 if dil > 1 else 0)
    nsteps = (T // tm) * ncol
    Wd = tn // 3

    def body(*refs):
        x_ref, g_ref, sc_ref, sh_ref, w_ref = refs[:5]
        h_ref, o_ref = refs[n_in:n_in + 2]
        h_scr = refs[n_in + n_out]
        step = pl.program_id(0) * ncol + pl.program_id(1)
        if n_g:
            exch = _Gather(refs[n_own_in:n_in], refs[n_in + n_own_out:n_in + n_out], *refs[n_in + n_out + n_scr:])
            pl.when(step == 0)(exch.start)
            pl.when(step == nsteps // 2)(exch.forward)

        @pl.when(pl.program_id(1) == 0)
        def _():
            h = _modnorm(x_ref[...], g_ref[...], sc_ref[...], sh_ref[...])
            if dil > 1:
                h = _deinterleave(h, refs[n_in + n_out + 1], dil)
            h_scr[...] = h.astype(BF16)
            h_ref[...] = h.reshape(h_ref.shape).astype(BF16)
        res = jnp.dot(h_scr[...], w_ref[...], preferred_element_type=F32)
        o_ref[...] = res.reshape(o_ref.shape).astype(BF16)
        if qk_norms is not None:
            qk_ref = refs[n_in + 2]
            parts = []
            for t, scale in enumerate((ATT_DH ** -0.5, 1.0)):
                for hd in range(Wd // ATT_DH):
                    c = t * Wd + hd * ATT_DH
                    parts.append(_head_norm(res[:, c:c + ATT_DH], refs[5 + t][...]) * scale)
            qk_ref[...] = jnp.concatenate(parts, axis=1).reshape(qk_ref.shape).astype(BF16)
        if n_g:
            pl.when(step == nsteps - 1)(exch.finish)

    vec = pl.BlockSpec((1, D), lambda i, j: (0, 0))
    in_specs = [pl.BlockSpec((tm, D), lambda i, j: (i, 0)), vec, vec, vec, pl.BlockSpec((D, tn), lambda i, j: (0, c0 + j))]
    args = [x, gain, sc, sh, w]
    out_specs = [pl.BlockSpec((dil, sub, D), lambda i, j: (0, i, 0)), pl.BlockSpec((dil, sub, tn), lambda i, j: (0, i, j))]
    out_shape = [_sds((dil, T // dil, D), BF16), _sds((dil, T // dil, N), BF16)]
    if qk_norms is not None:
        assert ncol == 1
        in_specs += [pl.BlockSpec((1, ATT_DH), lambda i, j: (0, 0))] * 2
        args += list(qk_norms)
        out_specs.append(pl.BlockSpec((dil, sub, 2 * Wd), lambda i, j: (0, i, 0)))
        out_shape.append(_sds((dil, T // dil, 2 * Wd), BF16))
    scratch = [pltpu.VMEM((tm, D), BF16)] + ([pltpu.VMEM((D // LANES, tm, LANES), F32)] if dil > 1 else [])
    if n_g:
        in_specs += [_ANY] * n_g
        args += list(gather)
        out_specs += [_ANY] * n_g
        out_shape += [_sds((N_DEV,) + g.shape, g.dtype) for g in gather]
        scratch += _gather_scratch(n_g)
    res = pl.pallas_call(
        body, name=name, grid=(T // tm, ncol), in_specs=in_specs, out_specs=out_specs, out_shape=out_shape,
        scratch_shapes=scratch, compiler_params=_cp(("arbitrary", "arbitrary") if n_g else ("parallel", "arbitrary")),
    )(*args)
    if dil == 1 and qk_norms is None:
        return [r.reshape(T, -1) for r in res[:2]] + list(res[2:])
    return res


def matmul_nn(a, w, out_dtype, name):
    T, K = a.shape
    N = w.shape[1]
    tm = _row_tile(T)

    def body(a_ref, w_ref, o_ref):
        o_ref[...] = _dot(a_ref[...], w_ref[...]).astype(o_ref.dtype)

    return pl.pallas_call(
        body, name=name, grid=(T // tm,),
        in_specs=[pl.BlockSpec((tm, K), lambda i: (i, 0)), pl.BlockSpec((K, N), lambda i: (0, 0))],
        out_specs=pl.BlockSpec((tm, N), lambda i: (i, 0)),
        out_shape=_sds((T, N), out_dtype),
        compiler_params=_cp(("parallel",)),
    )(a, w)


def proj_res(a, w, x, gate, name):
    T, K = a.shape
    D = w.shape[1]
    tm = _row_tile(T)

    def body(a_ref, w_ref, x_ref, g_ref, o_ref, y_ref):
        y = jnp.dot(a_ref[...], w_ref[...], preferred_element_type=F32)
        y_ref[...] = y.astype(BF16)
        o_ref[...] = x_ref[...] + g_ref[...] * y

    return pl.pallas_call(
        body, name=name, grid=(T // tm,),
        in_specs=[pl.BlockSpec((tm, K), lambda i: (i, 0)), pl.BlockSpec((K, D), lambda i: (0, 0)),
                  pl.BlockSpec((tm, D), lambda i: (i, 0)), pl.BlockSpec((1, D), lambda i: (0, 0))],
        out_specs=[pl.BlockSpec((tm, D), lambda i: (i, 0)), pl.BlockSpec((tm, D), lambda i: (i, 0))],
        out_shape=[_sds((T, D), F32), _sds((T, D), BF16)],
        compiler_params=_cp(("parallel",)),
    )(a, w, x, gate)


def bwd_out(dx, gate, y, w, tk, name):
    T, D = dx.shape
    K = w.shape[0]
    tm = _row_tile(T)

    def body(dx_ref, g_ref, y_ref, w_ref, dy_ref, da_ref, dg_ref, dy_scr):
        i, j = pl.program_id(0), pl.program_id(1)

        @pl.when(j == 0)
        def _():
            dxv = dx_ref[...]
            dy = (dxv * g_ref[...]).astype(BF16)
            dy_scr[...] = dy
            dy_ref[...] = dy
            part = jnp.sum(dxv * y_ref[...].astype(F32), axis=0, keepdims=True)

            @pl.when(i == 0)
            def _():
                dg_ref[...] = part

            @pl.when(i > 0)
            def _():
                dg_ref[...] += part

        da_ref[...] = _dot_nt(dy_scr[...], w_ref[...]).astype(BF16)

    return pl.pallas_call(
        body, name=name, grid=(T // tm, K // tk),
        in_specs=[pl.BlockSpec((tm, D), lambda i, j: (i, 0)), pl.BlockSpec((1, D), lambda i, j: (0, 0)),
                  pl.BlockSpec((tm, D), lambda i, j: (i, 0)), pl.BlockSpec((tk, D), lambda i, j: (j, 0))],
        out_specs=[pl.BlockSpec((tm, D), lambda i, j: (i, 0)), pl.BlockSpec((tm, tk), lambda i, j: (i, j)),
                   pl.BlockSpec((1, D), lambda i, j: (0, 0))],
        out_shape=[_sds((T, D), BF16), _sds((T, K), BF16), _sds((1, D), F32)],
        scratch_shapes=[pltpu.VMEM((tm, D), BF16)],
        compiler_params=_cp(("arbitrary", "arbitrary")),
    )(dx, gate, y, w)


def bwd_in(a_list, w, col_blocks, tn, acc, x, gain, sc, sh, dx_res, name, dil=1):
    T = math.prod(a_list[0].shape[:-1])
    n_a = len(a_list)
    w_list = list(w) if isinstance(w, (list, tuple)) else [w] * n_a
    D = w_list[0].shape[0]
    tm = 256 if T % 256 == 0 else T
    sub = tm // dil
    tns = [tn if tn is not None else a.shape[-1] for a in a_list]
    nsteps = a_list[0].shape[-1] // tns[0]
    assert all(a.shape[-1] == nsteps * t for a, t in zip(a_list, tns))
    final = x is not None
    has_acc = acc is not None

    def body(*refs):
        a_refs = refs[:n_a]
        w_refs = refs[n_a:2 * n_a]
        pos = 2 * n_a
        acc_ref = refs[pos] if has_acc else None
        pos += int(has_acc)
        if final:
            x_ref, g_ref, sc_ref, sh_ref, dxr_ref = refs[pos:pos + 5]
            pos += 5
            dx_ref, dg_ref, dsc_ref, dsh_ref = refs[pos:pos + 4]
            pos += 4
        else:
            dh_ref = refs[pos]
            pos += 1
        acc_scr = refs[pos]
        i, j = pl.program_id(0), pl.program_id(1)
        part = _dot_nt(a_refs[0][...].reshape(tm, tns[0]), w_refs[0][...])
        for k in range(1, n_a):
            part += _dot_nt(a_refs[k][...].reshape(tm, tns[k]), w_refs[k][...])

        @pl.when(j == 0)
        def _():
            acc_scr[...] = part

        @pl.when(j > 0)
        def _():
            acc_scr[...] += part

        @pl.when(j == nsteps - 1)
        def _():
            dh = acc_scr[...]
            if dil > 1:
                dh = _interleave(dh, refs[pos + 1], dil)
            if has_acc:
                dh = dh + acc_ref[...]
            if not final:
                dh_ref[...] = dh
                return
            _, vjp = jax.vjp(_modnorm, x_ref[...], g_ref[...], sc_ref[...], sh_ref[...])
            dxn, dg, dsc, dsh = vjp(dh)
            dx_ref[...] = dxr_ref[...] + dxn

            @pl.when(i == 0)
            def _():
                dg_ref[...] = dg
                dsc_ref[...] = dsc
                dsh_ref[...] = dsh

            @pl.when(i > 0)
            def _():
                dg_ref[...] += dg
                dsc_ref[...] += dsc
                dsh_ref[...] += dsh

    row = pl.BlockSpec((tm, D), lambda i, j: (i, 0))
    vec = pl.BlockSpec((1, D), lambda i, j: (0, 0))
    if dil == 1:
        in_specs = [pl.BlockSpec((tm, t), lambda i, j: (i, j)) for t in tns]
    else:
        in_specs = [pl.BlockSpec((dil, sub, t), lambda i, j: (0, i, j)) for t in tns]
    in_specs += [pl.BlockSpec((D, t), functools.partial(lambda i, j, c0: (0, c0 + j), c0=c0))
                 for t, c0 in zip(tns, col_blocks)]
    args = list(a_list) + w_list
    if has_acc:
        in_specs.append(row)
        args.append(acc)
    if final:
        in_specs += [row, vec, vec, vec, row]
        args += [x, gain, sc, sh, dx_res]
        out_specs = [row, vec, vec, vec]
        out_shape = [_sds((T, D), F32)] + [_sds((1, D), F32)] * 3
    else:
        out_specs = row
        out_shape = _sds((T, D), F32)
    return pl.pallas_call(
        body, name=name, grid=(T // tm, nsteps), in_specs=in_specs, out_specs=out_specs, out_shape=out_shape,
        scratch_shapes=[pltpu.VMEM((tm, D), F32)] + ([pltpu.VMEM((D // LANES, tm, LANES), F32)] if dil > 1 else []),
        compiler_params=_cp(("arbitrary", "arbitrary")),
    )(*args)


def matmul_tn(a, b, tk, tn, name, into=None, layer=0, col0=0):
    T, K = a.shape
    N = b.shape[1]
    tt = 4096 if (max(tk, tn) <= 1024 and T % 4096 == 0) else (2048 if T % 2048 == 0 else T)
    nt = T // tt

    def body(a_ref, b_ref, *rest):
        o_ref, acc = rest[-2:]
        part = _dot_tn(a_ref[...], b_ref[...])

        @pl.when(pl.program_id(2) == 0)
        def _():
            acc[...] = part

        @pl.when(pl.program_id(2) > 0)
        def _():
            acc[...] += part

        @pl.when(pl.program_id(2) == nt - 1)
        def _():
            o_ref[...] = acc[...].astype(BF16)

    in_specs = [pl.BlockSpec((tt, tk), lambda k, n, t: (t, k)), pl.BlockSpec((tt, tn), lambda k, n, t: (t, n))]
    common = dict(name=name, grid=(K // tk, N // tn, nt), scratch_shapes=[pltpu.VMEM((tk, tn), F32)],
                  compiler_params=_cp(("parallel", "parallel", "arbitrary")))
    if into is None:
        return pl.pallas_call(body, in_specs=in_specs, out_specs=pl.BlockSpec((tk, tn), lambda k, n, t: (k, n)),
                              out_shape=_sds((K, N), BF16), **common)(a, b)
    return pl.pallas_call(
        body, in_specs=in_specs + [_ANY], out_specs=pl.BlockSpec((None, tk, tn), lambda k, n, t: (layer, k, col0 + n)),
        out_shape=_sds(into.shape, BF16), input_output_aliases={2: 0}, **common)(a, b, into)


def _halo_specs(tm, tc, col_of):
    per = tm // HALO

    def prev(j, i):
        return (jnp.maximum(i * per - 1, 0), col_of(j))

    def nxt(j, i, last):
        return (jnp.minimum((i + 1) * per, last), col_of(j))

    return prev, nxt, per


def _shift_down(ext, s):
    return pltpu.roll(ext, s, 0)


def _shift_up(ext, s):
    return pltpu.roll(ext, ext.shape[0] - s, 0)


def _conv_ext(ext, w):
    K = w.shape[0]
    out = w[K - 1:K] * ext
    for s in range(1, K):
        out += w[K - 1 - s:K - s] * _shift_down(ext, s)
    return out


def _conv_t_ext(dext, w):
    K = w.shape[0]
    out = w[K - 1:K] * dext
    for s in range(1, K):
        out += w[K - 1 - s:K - s] * _shift_up(dext, s)
    return out


def _conv_dw(dc, ext, K, tm):
    rowid = lax.broadcasted_iota(jnp.int32, (8, dc.shape[1]), 0)
    out = jnp.zeros((8, dc.shape[1]), F32)
    for j in range(K):
        s = K - 1 - j
        xs = ext if s == 0 else _shift_down(ext, s)
        out = jnp.where(rowid == j, jnp.sum(dc * xs[HALO:HALO + tm], axis=0, keepdims=True), out)
    return out


def _accum(ref, val, first):
    @pl.when(first)
    def _():
        ref[...] = val

    @pl.when(jnp.logical_not(first))
    def _():
        ref[...] += val


def ffn_mid_fwd(up, conv_w, conv_b, name):
    T, two_f = up.shape
    F = two_f // 2
    tm = _row_tile(T)
    tc = F // 2
    nct = F // tc
    prev, _, per = _halo_specs(tm, tc, lambda j: j)

    def body(g_ref, gp_ref, v_ref, w_ref, b_ref, a_ref):
        i = pl.program_id(1)
        gp = jnp.where(i > 0, gp_ref[...].astype(F32), 0.0)
        ext = jnp.concatenate([gp, g_ref[...].astype(F32)], axis=0)
        c = _conv_ext(ext, w_ref[...])[HALO:] + b_ref[...]
        a_ref[...] = (_silu(c) * v_ref[...].astype(F32)).astype(BF16)

    return pl.pallas_call(
        body, name=name, grid=(nct, T // tm),
        in_specs=[pl.BlockSpec((tm, tc), lambda j, i: (i, j)), pl.BlockSpec((HALO, tc), prev),
                  pl.BlockSpec((tm, tc), lambda j, i: (i, j + nct)),
                  pl.BlockSpec((FFN_CONV, tc), lambda j, i: (0, j)), pl.BlockSpec((1, tc), lambda j, i: (0, j))],
        out_specs=pl.BlockSpec((tm, tc), lambda j, i: (i, j)),
        out_shape=_sds((T, F), BF16),
        compiler_params=_cp(("parallel", "arbitrary")),
    )(up, up, up, conv_w, conv_b)


def ffn_mid_bwd(up, da, conv_w, conv_b, name):
    T, two_f = up.shape
    F = two_f // 2
    tm = _row_tile(T)
    tc = F // 2
    nct = F // tc
    nrow = T // tm
    prev, nxt, per = _halo_specs(tm, tc, lambda j: j)
    last = T // HALO - 1
    nxt_g = functools.partial(nxt, last=last)

    def nxt_v(j, i):
        return (jnp.minimum((i + 1) * per, last), j + nct)

    def body(g_ref, gp_ref, gn_ref, v_ref, vn_ref, da_ref, dan_ref, w_ref, b_ref, dg_ref, dv_ref, dw_ref, db_ref):
        i = pl.program_id(1)
        w = w_ref[...]
        gp = jnp.where(i > 0, gp_ref[...].astype(F32), 0.0)
        inside = i < nrow - 1
        ext = jnp.concatenate([gp, g_ref[...].astype(F32), gn_ref[...].astype(F32)], axis=0)
        zero = jnp.zeros((HALO, tc), F32)
        v_ext = jnp.concatenate([zero, v_ref[...].astype(F32), vn_ref[...].astype(F32)], axis=0)
        da_ext = jnp.concatenate([zero, da_ref[...].astype(F32), jnp.where(inside, dan_ref[...].astype(F32), 0.0)], axis=0)
        c = _conv_ext(ext, w) + b_ref[...]
        sg = jax.nn.sigmoid(c)
        sil = c * sg
        dc = da_ext * v_ext * (sg * (1.0 + c * (1.0 - sg)))
        dv_ref[...] = (da_ext * sil)[HALO:HALO + tm].astype(BF16)
        dg_ref[...] = _conv_t_ext(dc, w)[HALO:HALO + tm].astype(BF16)
        dcm = dc[HALO:HALO + tm]
        _accum(dw_ref, _conv_dw(dcm, ext, FFN_CONV, tm), i == 0)
        _accum(db_ref, jnp.sum(dcm, axis=0, keepdims=True), i == 0)

    main = lambda j, i: (i, j)
    main_v = lambda j, i: (i, j + nct)
    return pl.pallas_call(
        body, name=name, grid=(nct, nrow),
        in_specs=[pl.BlockSpec((tm, tc), main), pl.BlockSpec((HALO, tc), prev), pl.BlockSpec((HALO, tc), nxt_g),
                  pl.BlockSpec((tm, tc), main_v), pl.BlockSpec((HALO, tc), nxt_v),
                  pl.BlockSpec((tm, tc), main), pl.BlockSpec((HALO, tc), nxt_g),
                  pl.BlockSpec((FFN_CONV, tc), lambda j, i: (0, j)), pl.BlockSpec((1, tc), lambda j, i: (0, j))],
        out_specs=[pl.BlockSpec((tm, tc), main), pl.BlockSpec((tm, tc), main),
                   pl.BlockSpec((8, tc), lambda j, i: (0, j)), pl.BlockSpec((1, tc), lambda j, i: (0, j))],
        out_shape=[_sds((T, F), BF16), _sds((T, F), BF16), _sds((8, F), F32), _sds((1, F), F32)],
        compiler_params=_cp(("parallel", "arbitrary")),
    )(up, up, up, up, up, da, da, conv_w, conv_b)


def _l2n(x):
    return x * lax.rsqrt(jnp.sum(x * x, axis=-1, keepdims=True) + RMS_EPS)


def _qkv_tok(c, normed):
    s = _silu(c)
    n = jnp.concatenate([_l2n(s[:, h * GDN_D:(h + 1) * GDN_D]) for h in range(GDN_HEADS)], axis=1)
    return jnp.where(normed, n, s)


def gdn_prep_fwd(proj, conv_w, name):
    T = proj.shape[0]
    W = GDN_HEADS * GDN_D
    tm = _row_tile(T)
    prev, _, per = _halo_specs(tm, W, lambda j: j)

    def body(x_ref, xp_ref, w_ref, o_ref):
        j, i = pl.program_id(0), pl.program_id(1)
        xp = jnp.where(i > 0, xp_ref[...].astype(F32), 0.0)
        ext = jnp.concatenate([xp, x_ref[...].astype(F32)], axis=0)
        c = _conv_ext(ext, w_ref[...])[HALO:]
        o_ref[...] = _qkv_tok(c, j < 2)

    return pl.pallas_call(
        body, name=name, grid=(3, T // tm),
        in_specs=[pl.BlockSpec((tm, W), lambda j, i: (i, j)), pl.BlockSpec((HALO, W), prev),
                  pl.BlockSpec((GDN_CONV, W), lambda j, i: (0, j))],
        out_specs=pl.BlockSpec((tm, W), lambda j, i: (i, j)),
        out_shape=_sds((T, 3 * W), F32),
        compiler_params=_cp(("parallel", "arbitrary")),
    )(proj, proj, conv_w)


def gdn_prep_bwd(proj, dqkv, conv_w, name):
    T = proj.shape[0]
    W = GDN_HEADS * GDN_D
    tm = _row_tile(T)
    nrow = T // tm
    prev, nxt, per = _halo_specs(tm, W, lambda j: j)
    nxt = functools.partial(nxt, last=T // HALO - 1)

    def body(x_ref, xp_ref, xn_ref, d_ref, dn_ref, w_ref, dx_ref, dw_ref):
        j, i = pl.program_id(0), pl.program_id(1)
        w = w_ref[...]
        xp = jnp.where(i > 0, xp_ref[...].astype(F32), 0.0)
        ext = jnp.concatenate([xp, x_ref[...].astype(F32), xn_ref[...].astype(F32)], axis=0)
        d_ext = jnp.concatenate([jnp.zeros((HALO, W), F32), d_ref[...],
                                 jnp.where(i < nrow - 1, dn_ref[...], 0.0)], axis=0)
        c = _conv_ext(ext, w)
        _, vjp = jax.vjp(lambda cc: _qkv_tok(cc, j < 2), c)
        dc, = vjp(d_ext)
        dx_ref[...] = _conv_t_ext(dc, w)[HALO:HALO + tm].astype(BF16)
        _accum(dw_ref, _conv_dw(dc[HALO:HALO + tm], ext, GDN_CONV, tm), i == 0)

    main = lambda j, i: (i, j)
    return pl.pallas_call(
        body, name=name, grid=(3, nrow),
        in_specs=[pl.BlockSpec((tm, W), main), pl.BlockSpec((HALO, W), prev), pl.BlockSpec((HALO, W), nxt),
                  pl.BlockSpec((tm, W), main), pl.BlockSpec((HALO, W), nxt),
                  pl.BlockSpec((GDN_CONV, W), lambda j, i: (0, j))],
        out_specs=[pl.BlockSpec((tm, W), main), pl.BlockSpec((8, W), lambda j, i: (0, j))],
        out_shape=[_sds((T, 3 * W), BF16), _sds((8, 3 * W), F32)],
        compiler_params=_cp(("parallel", "arbitrary")),
    )(proj, proj, proj, dqkv, dqkv, conv_w)


def _gate_tok(ba, a_log, dt_bias):
    z = ba + dt_bias
    softplus = jnp.maximum(z, 0.0) + jnp.log(1.0 + jnp.exp(-jnp.abs(z)))
    lane = lax.broadcasted_iota(jnp.int32, ba.shape, 1)
    raw = jnp.where(lane < GDN_HEADS, jax.nn.sigmoid(ba), -jnp.exp(a_log) * softplus)
    n = ba.shape[0]
    r = lax.broadcasted_iota(jnp.int32, (n, n), 0)
    c = lax.broadcasted_iota(jnp.int32, (n, n), 1)
    in_chunk_before = (r >= c) & ((r - c) <= (r & (GDN_CHUNK - 1)))
    return jnp.where(lane < GDN_HEADS, raw, _dot_hi(in_chunk_before.astype(F32), raw))


def gate_fwd(ba, a_log, dt_bias, name):
    T, L = ba.shape
    tm = _row_tile(T)

    def body(ba_ref, al_ref, dt_ref, o_ref):
        o_ref[...] = _gate_tok(ba_ref[...], al_ref[...], dt_ref[...])

    vec = pl.BlockSpec((1, L), lambda i: (0, 0))
    return pl.pallas_call(
        body, name=name, grid=(T // tm,),
        in_specs=[pl.BlockSpec((tm, L), lambda i: (i, 0)), vec, vec],
        out_specs=pl.BlockSpec((tm, L), lambda i: (i, 0)), out_shape=_sds((T, L), F32),
        compiler_params=_cp(("parallel",)),
    )(ba, a_log, dt_bias)


def gate_bwd(ba, a_log, dt_bias, dout_a, dout_b, name):
    T, L = ba.shape
    tm = _row_tile(T)

    def body(ba_ref, al_ref, dt_ref, d_ref, d2_ref, dba_ref, dal_ref, ddt_ref):
        _, vjp = jax.vjp(_gate_tok, ba_ref[...], al_ref[...], dt_ref[...])
        dba, dal, ddt = vjp(d_ref[...] + d2_ref[...])
        dba_ref[...] = dba.astype(BF16)
        first = pl.program_id(0) == 0
        _accum(dal_ref, dal, first)
        _accum(ddt_ref, ddt, first)

    vec = pl.BlockSpec((1, L), lambda i: (0, 0))
    row = pl.BlockSpec((tm, L), lambda i: (i, 0))
    return pl.pallas_call(
        body, name=name, grid=(T // tm,),
        in_specs=[row, vec, vec, row, row], out_specs=[row, vec, vec],
        out_shape=[_sds((T, L), BF16), _sds((1, L), F32), _sds((1, L), F32)],
        compiler_params=_cp(("arbitrary",)),
    )(ba, a_log, dt_bias, dout_a, dout_b)


_B_NN = (((2,), (1,)), ((0,), (0,)))
_B_NT = (((2,), (2,)), ((0,), (0,)))
GDN_GROUP = 8


def _split_bf16(a):
    hi = a.astype(BF16)
    return hi, (a - hi.astype(F32)).astype(BF16)


@jax.custom_vjp
def _bdot3(a, b):
    ah, al = _split_bf16(a)
    bh, bl = _split_bf16(b)
    dg = lambda x, y: lax.dot_general(x, y, _B_NN, preferred_element_type=F32)
    return dg(ah, bh) + (dg(ah, bl) + dg(al, bh))


def _bdot3_fwd(a, b):
    return _bdot3(a, b), (a, b)


def _bdot3_bwd(res, ct):
    a, b = res
    c16 = ct.astype(BF16)
    da = lax.dot_general(c16, b.astype(BF16), _B_NT, preferred_element_type=F32)
    db = lax.dot_general(a.astype(BF16), c16, (((1,), (1,)), ((0,), (0,))), preferred_element_type=F32)
    return da, db


_bdot3.defvjp(_bdot3_fwd, _bdot3_bwd)


def _exact_ones_dot(x, dims, ones_shape, ones_first):
    ones = jnp.ones(ones_shape, BF16)
    out = None
    for _ in range(3):
        piece = x.astype(BF16)
        ops = (ones, piece) if ones_first else (piece, ones)
        term = lax.dot_general(*ops, dims, preferred_element_type=F32)
        out = term if out is None else out + term
        x = x - piece.astype(F32)
    return out


@jax.custom_vjp
def _rows_from_cols(col):
    B, C, _ = col.shape
    eye = (lax.broadcasted_iota(jnp.int32, (1, C, C), 1) == lax.broadcasted_iota(jnp.int32, (1, C, C), 2)).astype(F32)
    return _exact_ones_dot(col * eye, _B_NN, (B, C, C), True)


def _rows_from_cols_fwd(col):
    return _rows_from_cols(col), None


def _rows_from_cols_bwd(_, ct):
    B, C, _ = ct.shape
    sums = _exact_ones_dot(ct, (((1,), (1,)), ((0,), (0,))), (B, C, LANES), False)
    return (sums[:, :, :1],)


_rows_from_cols.defvjp(_rows_from_cols_fwd, _rows_from_cols_bwd)


def _gdn_prep(q, k, v, gcol, bcol):
    B, C, dk = q.shape
    r = lax.broadcasted_iota(jnp.int32, (1, C, C), 1)
    cidx = lax.broadcasted_iota(jnp.int32, (1, C, C), 2)
    causal = r >= cidx
    strict = r > cidx
    decay = jnp.where(causal, jnp.exp(jnp.where(causal, gcol - _rows_from_cols(gcol), 0.0)), 0.0)
    qs = q * (dk ** -0.5)
    kb = k * bcol
    kk = lax.dot_general(kb.astype(BF16), k.astype(BF16), _B_NT, preferred_element_type=F32)
    L = jnp.where(strict, kk * decay, 0.0)
    tinv = (r == cidx).astype(F32) - L
    p = L
    n = 2
    while n < C:
        p = _bdot3(p, p)
        tinv = tinv + _bdot3(tinv, p)
        n *= 2
    egc = jnp.exp(gcol)
    u = _bdot3(tinv, v * bcol)
    w = _bdot3(tinv, kb * egc)
    qk = lax.dot_general(qs.astype(BF16), k.astype(BF16), _B_NT, preferred_element_type=F32)
    intra = jnp.where(causal, qk * decay, 0.0)
    last = lax.broadcasted_iota(jnp.int32, (1, C, 1), 1) == C - 1
    gt = jnp.sum(jnp.where(last, gcol, 0.0), axis=1, keepdims=True)
    return w, u, qs * egc, k * jnp.exp(gt - gcol), intra


def _gdn_scan(S, w, u, qg, kdec, intra, gcol):
    C = w.shape[0]
    last = lax.broadcasted_iota(jnp.int32, (C, 1), 0) == C - 1
    gt = jnp.sum(jnp.where(last, gcol, 0.0), axis=0, keepdims=True)
    v_new = u - _dot(w, S)
    o = _dot(qg, S) + _dot(intra, v_new)
    return S * jnp.exp(gt) + _dot_tn(kdec, v_new), o


def _gdn_specs(T, rev=False):
    H, D, C, B = GDN_HEADS, GDN_D, GDN_CHUNK, GDN_GROUP
    nsteps = T // (B * C)
    at = (lambda n: nsteps - 1 - n) if rev else (lambda n: n)
    rows = lambda w: pl.BlockSpec((B * C, w), lambda n: (at(n), 0))
    sq = pl.BlockSpec((B, H, C, C), lambda n: (at(n), 0, 0, 0))
    st = pl.BlockSpec((B, H, D, D), lambda n: (at(n), 0, 0, 0))
    return rows, sq, st, nsteps


def _gate_cols(gate_ref, h, rows=slice(None)):
    return gate_ref[rows, h:h + 1], gate_ref[rows, GDN_HEADS + h:GDN_HEADS + h + 1]


def gdn_prep_chunks_fwd(qkv, gate, name, gather=None):
    T = qkv.shape[0]
    H, D, C, B = GDN_HEADS, GDN_D, GDN_CHUNK, GDN_GROUP
    W = H * D
    rows, sq, _, nsteps = _gdn_specs(T)
    n_g = len(gather) if gather else 0

    def body(*refs):
        x_ref, g_ref = refs[:2]
        w_ref, u_ref, qg_ref, kd_ref, in_ref = refs[2 + n_g:7 + n_g]
        if n_g:
            exch = _Gather(refs[2:2 + n_g], refs[7 + n_g:7 + 2 * n_g], *refs[7 + 2 * n_g:])
            step = pl.program_id(0)
            pl.when(step == 0)(exch.start)
            pl.when(step == nsteps // 2)(exch.forward)
        for h in range(H):
            q, k, v = [x_ref[:, j * W + h * D:j * W + (h + 1) * D].reshape(B, C, D) for j in range(3)]
            bcol, gcol = [c.reshape(B, C, 1) for c in _gate_cols(g_ref, h)]
            outs = _gdn_prep(q, k, v, gcol, bcol)
            for ref, val in zip((w_ref, u_ref, qg_ref, kd_ref), outs[:4]):
                ref[:, h * D:(h + 1) * D] = val.reshape(B * C, D).astype(ref.dtype)
            in_ref[:, h] = outs[4].astype(BF16)
        if n_g:
            pl.when(step == nsteps - 1)(exch.finish)

    gather = list(gather) if gather else []
    return pl.pallas_call(
        body, name=name, grid=(nsteps,),
        in_specs=[rows(3 * W), rows(LANES)] + [_ANY] * n_g,
        out_specs=[rows(W), rows(W), rows(W), rows(W), sq] + [_ANY] * n_g,
        out_shape=[_sds((T, W), BF16), _sds((T, W), F32), _sds((T, W), BF16), _sds((T, W), BF16),
                   _sds((T // C, H, C, C), BF16)] + [_sds((N_DEV,) + g.shape, g.dtype) for g in gather],
        scratch_shapes=_gather_scratch(n_g) if n_g else [],
        compiler_params=_cp(("arbitrary",) if n_g else ("parallel",)),
    )(qkv, gate, *gather)


def gdn_prep_chunks_bwd(qkv, gate, dw, du, dqg, dkd, dintra, name):
    T = qkv.shape[0]
    H, D, C, B = GDN_HEADS, GDN_D, GDN_CHUNK, GDN_GROUP
    W = H * D
    rows, sq, _, nsteps = _gdn_specs(T)

    def body(x_ref, g_ref, dw_ref, du_ref, dqg_ref, dkd_ref, din_ref, dx_ref, dg_ref):
        dg_ref[...] = jnp.zeros_like(dg_ref)
        for h in range(H):
            q, k, v = [x_ref[:, j * W + h * D:j * W + (h + 1) * D].reshape(B, C, D) for j in range(3)]
            bcol, gcol = [c.reshape(B, C, 1) for c in _gate_cols(g_ref, h)]
            _, vjp = jax.vjp(_gdn_prep, q, k, v, gcol, bcol)
            cots = tuple(r[:, h * D:(h + 1) * D].reshape(B, C, D) for r in (dw_ref, du_ref, dqg_ref, dkd_ref))
            dq, dk, dv, dgc, db = vjp(cots + (din_ref[:, h],))
            for j, val in enumerate((dq, dk, dv)):
                dx_ref[:, j * W + h * D:j * W + (h + 1) * D] = val.reshape(B * C, D)
            dg_ref[:, h:h + 1] = db.reshape(B * C, 1)
            dg_ref[:, H + h:H + h + 1] = dgc.reshape(B * C, 1)

    return pl.pallas_call(
        body, name=name, grid=(nsteps,),
        in_specs=[rows(3 * W), rows(LANES), rows(W), rows(W), rows(W), rows(W), sq],
        out_specs=[rows(3 * W), rows(LANES)],
        out_shape=[_sds((T, 3 * W), F32), _sds((T, LANES), F32)],
        compiler_params=_cp(("parallel",)),
    )(qkv, gate, dw, du, dqg, dkd, dintra)


def gdn_fwd(w, u, qg, kdec, intra, gate, name):
    T = w.shape[0]
    H, D, C, B = GDN_HEADS, GDN_D, GDN_CHUNK, GDN_GROUP
    W = H * D
    rows, sq, st, nsteps = _gdn_specs(T)

    def body(w_ref, u_ref, qg_ref, kd_ref, in_ref, g_ref, o_ref, s_ref, s_scr):
        @pl.when(pl.program_id(0) == 0)
        def _():
            s_scr[...] = jnp.zeros_like(s_scr)

        def chunk(cb, carry):
            rs = pl.ds(pl.multiple_of(cb * C, C), C)
            for h in range(H):
                cs = slice(h * D, (h + 1) * D)
                S = s_scr[h]
                s_ref[cb, h] = S
                S_new, o = _gdn_scan(S, w_ref[rs, cs], u_ref[rs, cs], qg_ref[rs, cs], kd_ref[rs, cs], in_ref[cb, h],
                                     _gate_cols(g_ref, h, rs)[1])
                s_scr[h] = S_new
                o_ref[rs, cs] = o
            return carry

        lax.fori_loop(0, B, chunk, 0)

    return pl.pallas_call(
        body, name=name, grid=(nsteps,),
        in_specs=[rows(W), rows(W), rows(W), rows(W), sq, rows(LANES)],
        out_specs=[rows(W), st],
        out_shape=[_sds((T, W), F32), _sds((T // C, H, D, D), F32)],
        scratch_shapes=[pltpu.VMEM((H, D, D), F32)],
        compiler_params=_cp(("arbitrary",)),
    )(w, u, qg, kdec, intra, gate)


def gdn_bwd(w, u, qg, kdec, intra, gate, states, do, name):
    T = w.shape[0]
    H, D, C, B = GDN_HEADS, GDN_D, GDN_CHUNK, GDN_GROUP
    W = H * D
    rows, sq, st, nsteps = _gdn_specs(T, rev=True)

    def body(w_ref, u_ref, qg_ref, kd_ref, in_ref, g_ref, s_ref, do_ref,
             dw_ref, du_ref, dqg_ref, dkd_ref, din_ref, dg_ref, ds_scr):
        @pl.when(pl.program_id(0) == 0)
        def _():
            ds_scr[...] = jnp.zeros_like(ds_scr)
        dg_ref[...] = jnp.zeros_like(dg_ref)

        def chunk(t, carry):
            cb = B - 1 - t
            rs = pl.ds(pl.multiple_of(cb * C, C), C)
            for h in range(H):
                cs = slice(h * D, (h + 1) * D)
                _, vjp = jax.vjp(_gdn_scan, s_ref[cb, h], w_ref[rs, cs].astype(F32), u_ref[rs, cs],
                                 qg_ref[rs, cs].astype(F32), kd_ref[rs, cs].astype(F32), in_ref[cb, h].astype(F32),
                                 _gate_cols(g_ref, h, rs)[1])
                dS, dw, du, dqg, dkd, din, dgc = vjp((ds_scr[h], do_ref[rs, cs]))
                ds_scr[h] = dS
                dw_ref[rs, cs] = dw
                du_ref[rs, cs] = du
                dqg_ref[rs, cs] = dqg
                dkd_ref[rs, cs] = dkd
                din_ref[cb, h] = din
                dg_ref[rs, H + h:H + h + 1] = dgc
            return carry

        lax.fori_loop(0, B, chunk, 0)

    return pl.pallas_call(
        body, name=name, grid=(nsteps,),
        in_specs=[rows(W), rows(W), rows(W), rows(W), sq, rows(LANES), st, rows(W)],
        out_specs=[rows(W), rows(W), rows(W), rows(W), sq, rows(LANES)],
        out_shape=[_sds((T, W), F32)] * 4 + [_sds((T // C, H, C, C), F32), _sds((T, LANES), F32)],
        scratch_shapes=[pltpu.VMEM((H, D, D), F32)],
        compiler_params=_cp(("arbitrary",)),
    )(w, u, qg, kdec, intra, gate, states, do)


def _gated_norm(o, z, gain):
    outs = []
    for h in range(GDN_HEADS):
        oh = o[:, h * GDN_D:(h + 1) * GDN_D]
        y = oh * lax.rsqrt(jnp.mean(oh * oh, axis=-1, keepdims=True) + RMS_EPS) * gain
        outs.append(y * _silu(z[:, h * GDN_D:(h + 1) * GDN_D]))
    return jnp.concatenate(outs, axis=1)


def _window_sums(ext, shift):
    outs = []
    s = ext
    step = 1
    for gi, win in enumerate(POOL_WINDOWS):
        while step < win:
            s = s + shift(s, step)
            step *= 2
        outs.append(s[:, gi * GDN_D:(gi + 1) * GDN_D])
    return jnp.concatenate(outs, axis=1)


def _pool_counts(t0, rows, width):
    t1 = (t0 + 1 + lax.broadcasted_iota(jnp.int32, (rows, width), 0)).astype(F32)
    lane = lax.broadcasted_iota(jnp.int32, (rows, width), 1)
    win = jnp.full((rows, width), float(POOL_WINDOWS[-1]), F32)
    for gi in reversed(range(len(POOL_WINDOWS) - 1)):
        win = jnp.where(lane < (gi + 1) * GDN_D, float(POOL_WINDOWS[gi]), win)
    return jnp.minimum(t1, win)


def even_post_fwd(o, proj, gdn_norm, pool_w, pool_scale, name):
    T = o.shape[0]
    W = GDN_HEADS * GDN_D
    tm = _row_tile(T)
    per = tm // HALO

    def body(o_ref, z_ref, p_ref, pp_ref, gn_ref, pw_ref, ps_ref, out_ref):
        i = pl.program_id(0)
        out_ref[:, :W] = _gated_norm(o_ref[...], z_ref[...].astype(F32), gn_ref[...]).astype(BF16)
        pp = jnp.where(i > 0, pp_ref[...].astype(F32), 0.0)
        ext = jnp.concatenate([pp, p_ref[...].astype(F32)], axis=0)
        pooled = (_window_sums(ext, _shift_down)[HALO:] / _pool_counts(i * tm, tm, W)) - ext[HALO:]
        for gi in range(len(POOL_WINDOWS)):
            sl = slice(gi * GDN_D, (gi + 1) * GDN_D)
            y = _dot(pooled[:, sl], pw_ref[gi]) * ps_ref[:, sl]
            out_ref[:, W + gi * GDN_D:W + (gi + 1) * GDN_D] = y.astype(BF16)

    return pl.pallas_call(
        body, name=name, grid=(T // tm,),
        in_specs=[pl.BlockSpec((tm, W), lambda i: (i, 0)), pl.BlockSpec((tm, W), lambda i: (i, 3)),
                  pl.BlockSpec((tm, W), lambda i: (i, 4)),
                  pl.BlockSpec((HALO, W), lambda i: (jnp.maximum(i * per - 1, 0), 4)),
                  pl.BlockSpec((1, GDN_D), lambda i: (0, 0)),
                  pl.BlockSpec((len(POOL_WINDOWS), GDN_D, GDN_D), lambda i: (0, 0, 0)),
                  pl.BlockSpec((1, W), lambda i: (0, 0))],
        out_specs=pl.BlockSpec((tm, 2 * W), lambda i: (i, 0)),
        out_shape=_sds((T, 2 * W), BF16),
        compiler_params=_cp(("arbitrary",)),
    )(o, proj, proj, proj, gdn_norm, pool_w, pool_scale)


def even_post_bwd(o, proj, gdn_norm, pool_w, pool_scale, dcat, name):
    T = o.shape[0]
    W = GDN_HEADS * GDN_D
    G = len(POOL_WINDOWS)
    tm = _row_tile(T)
    per = tm // HALO
    nrow = T // tm
    last = T // HALO - 1

    def body(o_ref, z_ref, p_ref, pp_ref, gn_ref, pw_ref, ps_ref, d_ref, dn_ref,
             do_ref, dzp_ref, dgn_ref, dpw_ref, dps_ref):
        i = pl.program_id(0)
        first = i == 0
        _, vjp = jax.vjp(_gated_norm, o_ref[...], z_ref[...].astype(F32), gn_ref[...])
        do, dz, dgn = vjp(d_ref[:, :W].astype(F32))
        do_ref[...] = do
        dzp_ref[:, :W] = dz.astype(BF16)
        _accum(dgn_ref, dgn, first)
        pp = jnp.where(first, 0.0, pp_ref[...].astype(F32))
        ext = jnp.concatenate([pp, p_ref[...].astype(F32)], axis=0)
        pooled = (_window_sums(ext, _shift_down)[HALO:] / _pool_counts(i * tm, tm, W)) - ext[HALO:]
        dy_ext = jnp.concatenate([d_ref[:, W:].astype(F32), jnp.where(i < nrow - 1, dn_ref[...].astype(F32), 0.0)], axis=0)
        dys_ext = dy_ext * ps_ref[...]
        dpooled, dscale = [], []
        for gi in range(G):
            sl = slice(gi * GDN_D, (gi + 1) * GDN_D)
            dpooled.append(_dot_nt(dys_ext[:, sl], pw_ref[gi]))
            y = _dot(pooled[:, sl], pw_ref[gi])
            dscale.append(jnp.sum(dy_ext[:tm, sl] * y, axis=0, keepdims=True))
            _accum(dpw_ref.at[gi], _dot_tn(pooled[:, sl], dys_ext[:tm, sl]), first)
        dpooled = jnp.concatenate(dpooled, axis=1)
        _accum(dps_ref, jnp.concatenate(dscale, axis=1), first)
        dmean = dpooled / _pool_counts(i * tm, tm + HALO, W)
        dp = _window_sums(dmean, _shift_up)[:tm] - dpooled[:tm]
        dzp_ref[:, W:] = dp.astype(BF16)

    return pl.pallas_call(
        body, name=name, grid=(nrow,),
        in_specs=[pl.BlockSpec((tm, W), lambda i: (i, 0)), pl.BlockSpec((tm, W), lambda i: (i, 3)),
                  pl.BlockSpec((tm, W), lambda i: (i, 4)),
                  pl.BlockSpec((HALO, W), lambda i: (jnp.maximum(i * per - 1, 0), 4)),
                  pl.BlockSpec((1, GDN_D), lambda i: (0, 0)),
                  pl.BlockSpec((G, GDN_D, GDN_D), lambda i: (0, 0, 0)),
                  pl.BlockSpec((1, W), lambda i: (0, 0)),
                  pl.BlockSpec((tm, 2 * W), lambda i: (i, 0)),
                  pl.BlockSpec((HALO, W), lambda i: (jnp.minimum((i + 1) * per, last), 1))],
        out_specs=[pl.BlockSpec((tm, W), lambda i: (i, 0)), pl.BlockSpec((tm, 2 * W), lambda i: (i, 0)),
                   pl.BlockSpec((1, GDN_D), lambda i: (0, 0)), pl.BlockSpec((G, GDN_D, GDN_D), lambda i: (0, 0, 0)),
                   pl.BlockSpec((1, W), lambda i: (0, 0))],
        out_shape=[_sds((T, W), F32), _sds((T, 2 * W), BF16), _sds((1, GDN_D), F32), _sds((G, GDN_D, GDN_D), F32),
                   _sds((1, W), F32)],
        compiler_params=_cp(("arbitrary",)),
    )(o, proj, proj, proj, gdn_norm, pool_w, pool_scale, dcat, dcat)


def _head_norm(x, gain):
    return x * lax.rsqrt(jnp.mean(x * x, axis=-1, keepdims=True) + RMS_EPS) * gain


def _att_scores(q, k, slope, has_prev):
    B = ATT_BLOCK
    a = lax.broadcasted_iota(jnp.int32, (B, 2 * B), 0)
    j = lax.broadcasted_iota(jnp.int32, (B, 2 * B), 1)
    rel = B + a - j
    mask = (rel >= 0) & (rel <= B) & ((j >= B) | has_prev)
    s = _dot_nt(q, k) - slope * rel.astype(F32)
    return jnp.where(mask, s, NEG), mask


def _alibi_slope(h, dil):
    return dil * (2.0 ** (-8.0 * (h + 1) / ATT_HEADS))


def att_fwd(qk, qkv, name):
    dil, L, _ = qk.shape
    Wd = ATT_HEADS * ATT_DH
    nb = L // ATT_BLOCK
    B = ATT_BLOCK

    def body(q_ref, kc_ref, kp_ref, vc_ref, vp_ref, o_ref, l_ref):
        has_prev = pl.program_id(1) > 0
        for h in range(ATT_HEADS):
            sl = slice(h * ATT_DH, (h + 1) * ATT_DH)
            k = jnp.concatenate([kp_ref[:, sl], kc_ref[:, sl]], axis=0)
            v = jnp.concatenate([vp_ref[:, sl], vc_ref[:, sl]], axis=0)
            s, _ = _att_scores(q_ref[:, sl], k, _alibi_slope(h, dil), has_prev)
            m = jnp.max(s, axis=-1, keepdims=True)
            p = jnp.exp(s - m)
            l = jnp.sum(p, axis=-1, keepdims=True)
            o_ref[:, sl] = (_dot(p, v) / l).astype(BF16)
            l_ref[:, sl] = jnp.broadcast_to(m + jnp.log(l), (B, ATT_DH))

    cur = lambda t: pl.BlockSpec((None, B, Wd), lambda r, n: (r, n, t))
    prev = lambda t: pl.BlockSpec((None, B, Wd), lambda r, n: (r, jnp.maximum(n - 1, 0), t))
    out = pl.BlockSpec((None, B, Wd), lambda r, n: (r, n, 0))
    return pl.pallas_call(
        body, name=name, grid=(dil, nb),
        in_specs=[cur(0), cur(1), prev(1), cur(2), prev(2)],
        out_specs=[out, out], out_shape=[_sds((dil, L, Wd), BF16), _sds((dil, L, Wd), F32)],
        compiler_params=_cp(("parallel", "arbitrary")),
    )(qk, qk, qk, qkv, qkv)


def att_bwd(qk, qkv, lse, do, dd, name, carry=None):
    dil, L, _ = qk.shape
    Wd = ATT_HEADS * ATT_DH
    nb = L // ATT_BLOCK
    B = ATT_BLOCK
    carry = list(carry) if carry else []
    n_c = len(carry)

    def body(*refs):
        q_ref, kc_ref, kp_ref, vc_ref, vp_ref, l_ref, do_ref, dd_ref = refs[:8]
        dq_ref, dk_ref, dv_ref = refs[8 + n_c:11 + n_c]
        ck_scr, cv_scr = refs[11 + 2 * n_c:13 + 2 * n_c]
        r, n = pl.program_id(0), pl.program_id(1)
        has_prev = n > 0
        if n_c:
            ex = _ChipExchange(refs[8:8 + n_c], refs[11 + n_c:11 + 2 * n_c], *refs[13 + 2 * n_c:])
            pl.when((r == 0) & (n == 0))(ex.start)

        @pl.when(n < nb)
        def _():
            for h in range(ATT_HEADS):
                sl = slice(h * ATT_DH, (h + 1) * ATT_DH)
                q = q_ref[:, sl]
                k = jnp.concatenate([kp_ref[:, sl], kc_ref[:, sl]], axis=0)
                v = jnp.concatenate([vp_ref[:, sl], vc_ref[:, sl]], axis=0)
                do = do_ref[:, sl]
                s, mask = _att_scores(q, k, _alibi_slope(h, dil), has_prev)
                p = jnp.where(mask, jnp.exp(s - l_ref[:, h * ATT_DH:h * ATT_DH + 1]), 0.0)
                delta = jnp.sum(dd_ref[:, sl].astype(F32), axis=-1, keepdims=True)
                ds = p * (_dot_nt(do, v) - delta)
                dq_ref[:, sl] = _dot(ds, k).astype(BF16)
                dk = _dot_tn(ds, q)
                dv = _dot_tn(p, do)
                dk_ref[:, sl] = (jnp.where(has_prev, ck_scr[:, sl] + dk[:B], 0.0)).astype(BF16)
                dv_ref[:, sl] = (jnp.where(has_prev, cv_scr[:, sl] + dv[:B], 0.0)).astype(BF16)
                ck_scr[:, sl] = dk[B:]
                cv_scr[:, sl] = dv[B:]

        @pl.when(n == nb)
        def _():
            dk_ref[...] = ck_scr[...].astype(BF16)
            dv_ref[...] = cv_scr[...].astype(BF16)

        if n_c:
            pl.when((r == dil - 1) & (n == nb))(ex.finish)

    cur = lambda t: pl.BlockSpec((None, B, Wd), lambda r, n: (r, jnp.minimum(n, nb - 1), t))
    prev = lambda t: pl.BlockSpec((None, B, Wd), lambda r, n: (r, jnp.clip(n - 1, 0, nb - 1), t))
    kv_out = pl.BlockSpec((None, B, Wd), lambda r, n: (r, jnp.maximum(n - 1, 0), 0))
    return pl.pallas_call(
        body, name=name, grid=(dil, nb + 1),
        in_specs=[cur(0), cur(1), prev(1), cur(2), prev(2), cur(0), cur(0), cur(0)] + [_ANY] * n_c,
        out_specs=[cur(0), kv_out, kv_out] + [_ANY] * n_c,
        out_shape=[_sds((dil, L, Wd), BF16)] * 3 + [_sds(a.shape, a.dtype) for a in carry],
        scratch_shapes=[pltpu.VMEM((B, Wd), F32), pltpu.VMEM((B, Wd), F32)] + (_chip_exchange_scratch(n_c) if n_c else []),
        compiler_params=_cp(("arbitrary", "arbitrary") if n_c else ("parallel", "arbitrary")),
    )(qk, qk, qk, qkv, qkv, lse, do, dd, *carry)


def qk_norm_bwd(qkv, dq, dk, q_norm, k_norm, name):
    T = qkv.shape[0]
    Wd = dq.shape[1]
    tm = _row_tile(T)

    def body(q_ref, k_ref, dq_ref, dk_ref, qn_ref, kn_ref, oq_ref, ok_ref, dqn_ref, dkn_ref):
        first = pl.program_id(0) == 0
        for x_ref, d_ref, g_ref, o_ref, dg_ref, scale in ((q_ref, dq_ref, qn_ref, oq_ref, dqn_ref, ATT_DH ** -0.5),
                                                         (k_ref, dk_ref, kn_ref, ok_ref, dkn_ref, 1.0)):
            dg = jnp.zeros((1, ATT_DH), F32)
            for h in range(Wd // ATT_DH):
                sl = slice(h * ATT_DH, (h + 1) * ATT_DH)
                _, vjp = jax.vjp(lambda x, g: _head_norm(x, g) * scale, x_ref[:, sl].astype(F32), g_ref[...])
                dx, dg_h = vjp(d_ref[:, sl].astype(F32))
                o_ref[:, sl] = dx.astype(BF16)
                dg += dg_h
            _accum(dg_ref, dg, first)

    row = lambda t: pl.BlockSpec((tm, Wd), lambda i: (i, t))
    vec = pl.BlockSpec((1, ATT_DH), lambda i: (0, 0))
    return pl.pallas_call(
        body, name=name, grid=(T // tm,),
        in_specs=[row(0), row(1), row(0), row(0), vec, vec], out_specs=[row(0), row(0), vec, vec],
        out_shape=[_sds((T, Wd), BF16)] * 2 + [_sds((1, ATT_DH), F32)] * 2,
        compiler_params=_cp(("arbitrary",)),
    )(qkv, qkv, dq, dk, q_norm, k_norm)


def _merge_weights(l0, l1, l2):
    m = jnp.maximum(jnp.maximum(l0, l1), l2)
    e = [jnp.exp(l - m) for l in (l0, l1, l2)]
    tot = e[0] + e[1] + e[2]
    return [x / tot for x in e]


def _merge_specs(arrays, tm):
    return [pl.BlockSpec((a.shape[0], tm // a.shape[0], a.shape[2]), lambda i: (0, i, 0)) for a in arrays]


def _merge_load(refs, dils, tm, scr):
    vals = [ref[...].astype(F32).reshape(tm, ref.shape[-1]) for ref in refs]
    return [v if dl == 1 else _interleave(v, scr, dl) for v, dl in zip(vals, dils)]


def merge_fwd(outs, lses, name):
    Wd = outs[0].shape[2]
    T = outs[0].shape[0] * outs[0].shape[1]
    tm = _row_tile(T)
    dils = [a.shape[0] for a in outs] * 2

    def body(*refs):
        vals = _merge_load(refs[:6], dils, tm, refs[7])
        w = _merge_weights(*vals[3:])
        refs[6][...] = (w[0] * vals[0] + w[1] * vals[1] + w[2] * vals[2]).astype(BF16)

    return pl.pallas_call(body, name=name, grid=(T // tm,), in_specs=_merge_specs(list(outs) + list(lses), tm),
                          out_specs=pl.BlockSpec((tm, Wd), lambda i: (i, 0)), out_shape=_sds((T, Wd), BF16),
                          scratch_shapes=[pltpu.VMEM((Wd // LANES, tm, LANES), F32)],
                          compiler_params=_cp(("parallel",)))(*outs, *lses)


def merge_bwd(outs, lses, d, name):
    Wd = outs[0].shape[2]
    T = outs[0].shape[0] * outs[0].shape[1]
    tm = 256 if T % 256 == 0 else T
    dils = [a.shape[0] for a in outs] * 2

    def body(*refs):
        scr = refs[13]
        vals = _merge_load(refs[:6], dils, tm, scr)
        w = _merge_weights(*vals[3:])
        dv = refs[6][...].astype(F32)
        dvm = dv * (w[0] * vals[0] + w[1] * vals[1] + w[2] * vals[2])
        for g in range(3):
            for ref, val in ((refs[7 + g], w[g] * dv), (refs[10 + g], w[g] * dvm)):
                ref[...] = (val if dils[g] == 1 else _deinterleave(val, scr, dils[g])).reshape(ref.shape).astype(BF16)

    specs = _merge_specs(list(outs) + list(lses), tm)
    res = pl.pallas_call(body, name=name, grid=(T // tm,),
                         in_specs=specs + [pl.BlockSpec((tm, Wd), lambda i: (i, 0))], out_specs=specs,
                         out_shape=[_sds(a.shape, BF16) for a in list(outs) + list(outs)],
                         scratch_shapes=[pltpu.VMEM((Wd // LANES, tm, LANES), F32)],
                         compiler_params=_cp(("parallel",)))(*outs, *lses, d)
    return res[:3], res[3:]


def loss_head(y, target, name):
    T, D = y.shape
    tm = _row_tile(T)

    def body(y_ref, t_ref, l_ref, dy_ref):
        err = y_ref[...] - t_ref[...]
        dy_ref[...] = err * (1.0 / D)
        part = 0.5 * jnp.sum(jnp.sum(err * err, axis=1, keepdims=True) * (1.0 / D), axis=0, keepdims=True)
        _accum(l_ref, jnp.broadcast_to(part, (1, 128)), pl.program_id(0) == 0)

    row = pl.BlockSpec((tm, D), lambda i: (i, 0))
    return pl.pallas_call(body, name=name, grid=(T // tm,), in_specs=[row, row],
                          out_specs=[pl.BlockSpec((1, 128), lambda i: (0, 0)), row],
                          out_shape=[_sds((1, 128), F32), _sds((T, D), F32)],
                          compiler_params=_cp(("arbitrary",)))(y, target)


def ada_fwd(c_all, w, b, name):
    depth, D, n = w.shape

    def body(c_ref, w_ref, b_ref, o_ref):
        o_ref[...] = _dot(_silu(c_ref[...]), w_ref[...]) + b_ref[...]

    return pl.pallas_call(
        body, name=name, grid=(depth,),
        in_specs=[pl.BlockSpec((N_DEV, D), lambda i: (0, 0)), pl.BlockSpec((None, D, n), lambda i: (i, 0, 0)),
                  pl.BlockSpec((None, 1, n), lambda i: (i, 0, 0))],
        out_specs=pl.BlockSpec((None, N_DEV, n), lambda i: (i, 0, 0)),
        out_shape=_sds((depth, N_DEV, n), F32), compiler_params=_cp(("parallel",)),
    )(c_all, w, b)


def ada_bwd(c_all, dmod, name):
    depth, _, n = dmod.shape
    D = c_all.shape[1]

    def body(c_ref, d_ref, o_ref):
        o_ref[...] = _dot_tn(_silu(c_ref[...]), d_ref[...])

    return pl.pallas_call(
        body, name=name, grid=(depth,),
        in_specs=[pl.BlockSpec((N_DEV, D), lambda i: (0, 0)), pl.BlockSpec((None, N_DEV, n), lambda i: (i, 0, 0))],
        out_specs=pl.BlockSpec((None, D, n), lambda i: (i, 0, 0)),
        out_shape=_sds((depth, D, n), F32), compiler_params=_cp(("parallel",)),
    )(c_all, dmod)


def adamw(w, m, v, gparts, name):
    R, C = w.shape
    k = gparts.shape[0]
    tr = R
    for cand in (512, 256, 128, 64, 32, 16, 8):
        if R % cand == 0 and cand * C * 4 <= 2 * 1024 * 1024:
            tr = cand
            break
    bc1 = 1.0 - ADAM_B1 ** ADAM_STEP
    bc2 = 1.0 - ADAM_B2 ** ADAM_STEP

    def body(w_ref, m_ref, v_ref, gp_ref, g_ref, d_ref, nm_ref, nv_ref):
        g = gp_ref[0].astype(F32)
        for q in range(1, k):
            g = g + gp_ref[q].astype(F32)
        nm = ADAM_B1 * m_ref[...] + (1.0 - ADAM_B1) * g
        nv = ADAM_B2 * v_ref[...] + (1.0 - ADAM_B2) * (g * g)
        g_ref[...] = g
        nm_ref[...] = nm
        nv_ref[...] = nv
        d_ref[...] = -ADAM_LR * ((nm / bc1) / (jnp.sqrt(nv / bc2) + ADAM_EPS) + ADAM_WD * w_ref[...])

    row = pl.BlockSpec((tr, C), lambda i: (i, 0))
    return pl.pallas_call(
        body, name=name, grid=(R // tr,),
        in_specs=[row, row, row, pl.BlockSpec((k, tr, C), lambda i: (0, i, 0))],
        out_specs=[row] * 4, out_shape=[_sds((R, C), F32)] * 4, compiler_params=_cp(("parallel",)),
    )(w, m, v, gparts)


def _mesh_pos():
    return lax.axis_index("x"), lax.axis_index("y"), lax.axis_index("c")


def _other_chips(x, y):
    return [(1 - x, y), (x, 1 - y), (1 - x, 1 - y)]


_ANY = pl.BlockSpec(memory_space=pl.ANY)


class _Gather:
    def __init__(self, ins, outs, send_sems, recv_sems, local_sems):
        self.ins, self.outs, self.n = ins, outs, len(ins)
        self.send_sems, self.recv_sems, self.local_sems = send_sems, recv_sems, local_sems
        self.x, self.y, self.c = _mesh_pos()
        self.me, self.sibling = (self.x, self.y, self.c), (self.x, self.y, 1 - self.c)
        self.chips = _other_chips(self.x, self.y)

    def _copy(self, a, k, block, to, src=None):
        dst = self.outs[a].at[4 * block[0] + 2 * block[1] + block[2]]
        return pltpu.make_async_remote_copy(
            src_ref=dst if src is None else src, dst_ref=dst, send_sem=self.send_sems.at[a, k],
            recv_sem=self.recv_sems.at[a, k], device_id=to, device_id_type=MESH)

    def _mine(self):
        me_slot = 4 * self.x + 2 * self.y + self.c
        return [pltpu.make_async_copy(self.ins[a], self.outs[a].at[me_slot], self.local_sems.at[a]) for a in range(self.n)]

    def _first(self):
        out = []
        for a in range(self.n):
            out.append(self._copy(a, 0, self.me, self.sibling, src=self.ins[a]))
            out += [self._copy(a, 1 + j, self.me, (*chip, self.c), src=self.ins[a]) for j, chip in enumerate(self.chips)]
        return out

    def _passed(self):
        return [self._copy(a, 4 + j, (*chip, self.c), self.sibling) for j, chip in enumerate(self.chips) for a in range(self.n)]

    def start(self):
        for cp in self._mine() + self._first():
            cp.start()

    def forward(self):
        for j, chip in enumerate(self.chips):
            for a in range(self.n):
                self._copy(a, 1 + j, (*chip, self.c), self.me).wait_recv()
                self._copy(a, 4 + j, (*chip, self.c), self.sibling).start()

    def finish(self):
        for a in range(self.n):
            self._copy(a, 0, self.sibling, self.me).wait_recv()
            for j, chip in enumerate(self.chips):
                self._copy(a, 4 + j, (*chip, 1 - self.c), self.me).wait_recv()
        for cp in self._first() + self._passed():
            cp.wait_send()
        for cp in self._mine():
            cp.wait()


def _gather_scratch(n):
    return [pltpu.SemaphoreType.DMA((n, 7)), pltpu.SemaphoreType.DMA((n, 7)), pltpu.SemaphoreType.DMA((n,))]


def all_gather(shards, name):
    n = len(shards)

    def body(*refs):
        g = _Gather(refs[:n], refs[n:2 * n], *refs[2 * n:])
        g.start()
        g.forward()
        g.finish()

    return pl.pallas_call(
        body, name=name, in_specs=[_ANY] * n, out_specs=[_ANY] * n,
        out_shape=[_sds((N_DEV,) + s.shape, s.dtype) for s in shards],
        scratch_shapes=_gather_scratch(n),
        compiler_params=pltpu.CompilerParams(has_side_effects=True),
    )(*shards)


def exchange_pair(stacked, name):
    n = len(stacked)

    def body(*refs):
        ins, outs = refs[:n], refs[n:2 * n]
        send_sems, recv_sems = refs[2 * n:]
        x, y, c = _mesh_pos()
        copies = [pltpu.make_async_remote_copy(
            src_ref=ins[a].at[2 * q + (1 - c)], dst_ref=outs[a].at[q], send_sem=send_sems.at[a, q],
            recv_sem=recv_sems.at[a, q], device_id=(x, y, 1 - c), device_id_type=MESH)
            for a in range(n) for q in range(4)]
        for cp in copies:
            cp.start()
        for cp in copies:
            cp.wait()

    return pl.pallas_call(
        body, name=name, in_specs=[_ANY] * n, out_specs=[_ANY] * n,
        out_shape=[_sds((4,) + s.shape[1:], s.dtype) for s in stacked],
        scratch_shapes=[pltpu.SemaphoreType.DMA((n, 4)), pltpu.SemaphoreType.DMA((n, 4))],
        compiler_params=pltpu.CompilerParams(has_side_effects=True),
    )(*stacked)


def pair_add(stacked, got, c_idx, name):
    _, R, C = stacked.shape
    tr = R
    for cand in (512, 256, 128, 64, 32, 16):
        if R % cand == 0 and cand * C * 4 <= 2 * 1024 * 1024:
            tr = cand
            break

    def body(c_ref, s_ref, g_ref, o_ref):
        o_ref[...] = (s_ref[...].astype(F32) + g_ref[...].astype(F32)).astype(BF16)

    return pl.pallas_call(
        body, name=name,
        grid_spec=pltpu.PrefetchScalarGridSpec(
            num_scalar_prefetch=1, grid=(4, R // tr),
            in_specs=[pl.BlockSpec((None, tr, C), lambda q, i, c_ref: (2 * q + c_ref[0], i, 0)),
                      pl.BlockSpec((None, tr, C), lambda q, i, c_ref: (q, i, 0))],
            out_specs=pl.BlockSpec((None, tr, C), lambda q, i, c_ref: (q, i, 0))),
        out_shape=_sds((4, R, C), BF16),
        compiler_params=_cp(("parallel", "parallel")),
    )(c_idx, stacked, got)


def exchange_chips(parts, name):
    n = len(parts)

    def body(*refs):
        ex = _ChipExchange(refs[:n], refs[n:2 * n], *refs[2 * n:])
        ex.start()
        ex.finish()

    return pl.pallas_call(
        body, name=name, in_specs=[_ANY] * n, out_specs=[_ANY] * n,
        out_shape=[_sds(s.shape, s.dtype) for s in parts],
        scratch_shapes=_chip_exchange_scratch(n),
        compiler_params=pltpu.CompilerParams(has_side_effects=True),
    )(*parts)


class _ChipExchange:
    def __init__(self, ins, outs, send_sems, recv_sems, local_sems):
        self.ins, self.outs, self.n = ins, outs, len(ins)
        self.send_sems, self.recv_sems, self.local_sems = send_sems, recv_sems, local_sems
        self.x, self.y, self.c = _mesh_pos()

    def _copies(self):
        myq = 2 * self.x + self.y
        mine = [pltpu.make_async_copy(self.ins[a].at[myq], self.outs[a].at[myq], self.local_sems.at[a])
                for a in range(self.n)]
        remote = [pltpu.make_async_remote_copy(
            src_ref=self.ins[a].at[2 * chip[0] + chip[1]], dst_ref=self.outs[a].at[myq], send_sem=self.send_sems.at[a, j],
            recv_sem=self.recv_sems.at[a, j], device_id=(*chip, self.c), device_id_type=MESH)
            for a in range(self.n) for j, chip in enumerate(_other_chips(self.x, self.y))]
        return mine, remote

    def start(self):
        mine, remote = self._copies()
        for cp in mine + remote:
            cp.start()

    def finish(self):
        mine, remote = self._copies()
        for cp in remote + mine:
            cp.wait()


def _chip_exchange_scratch(n):
    return [pltpu.SemaphoreType.DMA((n, 3)), pltpu.SemaphoreType.DMA((n, 3)), pltpu.SemaphoreType.DMA((n,))]


def _pad_lanes(v, start, width=128):
    return jnp.pad(v.astype(F32), (start, width - start - v.shape[0]))[None]


def _carry(results, n_own):
    results = list(results)
    return results[:n_own] + [results[n_own:]]


def _layer_fwd(i, x, mod, P, next_shards=None):
    T, D = x.shape
    next_shards = next_shards or {}
    next_ffn = {k: v for k, v in next_shards.items() if k.startswith("ffn")}
    next_mix = {k: v for k, v in next_shards.items() if not k.startswith("ffn")}
    sh_m, sc_m, g_m, sh_f, sc_f, g_f = [mod[k:k + 1] for k in range(6)]
    sv = {"x0": x}
    tag = f"l{i}"
    if i % 2 == 0:
        e = i // 2
        h, proj = norm_proj(x, P["norm_mix"][i:i + 1], sc_m, sh_m, P["ev_main"][e], P["ev_main"][e].shape[1], tag + "_in")
        ba = matmul_nn(h, P["ev_ba"][e], F32, tag + "_ba")
        al = _pad_lanes(P["gdn_a_log"][e], GDN_HEADS)
        dt = _pad_lanes(P["gdn_dt_bias"][e], GDN_HEADS)
        gate = gate_fwd(ba, al, dt, tag + "_gate")
        qkv = gdn_prep_fwd(proj, P["gdn_conv_w"][e], tag + "_prep")
        *chunks, got_mix = _carry(gdn_prep_chunks_fwd(qkv, gate, tag + "_chunks", gather=list(next_mix.values())), 5)
        o, states = gdn_fwd(*chunks, gate, tag + "_gdn")
        cat = even_post_fwd(o, proj, P["gdn_norm"][e:e + 1], P["pool_w"][e], P["pool_scale"][e:e + 1], tag + "_post")
        x1, y_m = proj_res(cat, P["ev_w_out"][e], x, g_m, tag + "_out")
        sv.update(h=h, proj=proj, ba=ba, al=al, dt=dt, gate=gate, qkv=qkv, chunks=chunks, o=o, states=states, a_m=cat)
    else:
        od = i // 2
        qn, kn = P["att_q_norm"][od:od + 1], P["att_k_norm"][od:od + 1]
        h, proj, qk = [], [], []
        got_mix = []
        for gi, (_, dil) in enumerate(DIL_PATTERNS):
            hg, pg, qkg, got = _carry(norm_proj(x, P["norm_mix"][i:i + 1], sc_m, sh_m, P["od_w_in"][od], 3 * D,
                                                f"{tag}_in{gi}", cols=(gi, 1), dil=dil, qk_norms=(qn, kn),
                                                gather=list(next_mix.values()) if gi == 0 else None), 3)
            got_mix += got
            h.append(hg.reshape(T, D))
            proj.append(pg)
            qk.append(qkg)
        res = [att_fwd(qk[gi], proj[gi], f"{tag}_att{gi}") for gi in range(len(DIL_PATTERNS))]
        outs, lses = [r[0] for r in res], [r[1] for r in res]
        merged = merge_fwd(outs, lses, tag + "_merge")
        x1, y_m = proj_res(merged, P["od_w_out"][od], x, g_m, tag + "_out")
        sv.update(h=h, proj=proj, qk=qk, outs=outs, lses=lses, a_m=merged)
    hf, up, got_ffn = _carry(norm_proj(x1, P["norm_ffn"][i:i + 1], sc_f, sh_f, P["ffn_w_up"][i], 1408, tag + "_up",
                                       gather=list(next_ffn.values())), 2)
    a = ffn_mid_fwd(up, P["ffn_conv_w"][i], P["ffn_conv_b"][i:i + 1], tag + "_mid")
    x2, y_f = proj_res(a, P["ffn_w_down"][i], x1, g_f, tag + "_down")
    sv.update(y_m=y_m, x1=x1, hf=hf, up=up, a_f=a, y_f=y_f)
    got = dict(zip(list(next_mix.keys()) + list(next_ffn.keys()), list(got_mix) + list(got_ffn)))
    return x2, sv, (got if next_shards else None)


def add_layer_weights(P, gathered):
    W = GDN_HEADS * GDN_D
    for k, g in gathered.items():
        if k in _COL_SHARDED:
            full = g.transpose(1, 0, 2).reshape(g.shape[1], N_DEV * g.shape[2])
        else:
            full = g.reshape(N_DEV * g.shape[1], g.shape[2])
        if k == "ev_w_in":
            P.setdefault("ev_main", []).append(jnp.concatenate([full[:, :4 * W], full[:, 4 * W + 2 * GDN_HEADS:]], axis=1))
            P.setdefault("ev_ba", []).append(jnp.pad(full[:, 4 * W:4 * W + 2 * GDN_HEADS], ((0, 0), (0, 128 - 2 * GDN_HEADS))))
        else:
            P.setdefault(k, []).append(full)


def _put(bufs, k, li, a, b, tk, tn, name, col0=0):
    half = bufs[k][0].shape[0]
    g, loc = divmod(li, half)
    bufs[k][g] = matmul_tn(a, b, tk, tn, name, into=bufs[k][g], layer=loc, col0=col0)


CARRY_SPLIT = ((1, 3, 4, 0), (2, 5))


def _layer_bwd(i, dx2, sv, mod, P, bufs, carry=None):
    T = dx2.shape[0]
    sh_m, sc_m, g_m, sh_f, sc_f, g_f = [mod[k:k + 1] for k in range(6)]
    tag = f"b{i}"
    G = {}
    F = P["ffn_w_down"][i].shape[0]
    dy, da, dg_f = bwd_out(dx2, g_f, sv["y_f"], P["ffn_w_down"][i], F, tag + "_down")
    _put(bufs, "ffn_w_down", i, sv["a_f"], dy, F // 2, dy.shape[1], tag + "_wdown")
    dgate, dval, dcw, dcb = ffn_mid_bwd(sv["up"], da, P["ffn_conv_w"][i], P["ffn_conv_b"][i:i + 1], tag + "_mid")
    G["ffn_conv_w"], G["ffn_conv_b"] = dcw[:FFN_CONV], dcb[0]
    dx1, dnf, dsc_f, dsh_f = bwd_in([dgate, dval], P["ffn_w_up"][i], [0, 1], F, None,
                                    sv["x1"], P["norm_ffn"][i:i + 1], sc_f, sh_f, dx2, tag + "_up")
    G["norm_ffn"] = dnf[0]
    x0, h, proj = sv["x0"], sv["h"], sv["proj"]
    D = x0.shape[1]
    for k, d in enumerate((dgate, dval)):
        _put(bufs, "ffn_w_up", i, sv["hf"], d, D, F // 2, f"{tag}_wup{k}", col0=2 * k)
    gain = P["norm_mix"][i:i + 1]
    if i % 2 == 0:
        e = i // 2
        W = GDN_HEADS * GDN_D
        dy, dcat, dg_m = bwd_out(dx1, g_m, sv["y_m"], P["ev_w_out"][e], 1024, tag + "_out")
        _put(bufs, "ev_w_out", e, sv["a_m"], dy, 1024, D, tag + "_wout")
        do, dzp, dgn, dpw, dps = even_post_bwd(sv["o"], proj, P["gdn_norm"][e:e + 1], P["pool_w"][e],
                                               P["pool_scale"][e:e + 1], dcat, tag + "_post")
        G["gdn_norm"], G["pool_w"], G["pool_scale"] = dgn[0], dpw, dps[0]
        *dchunks, dgate_scan = gdn_bwd(*sv["chunks"], sv["gate"], sv["states"], do, tag + "_gdn")
        dqkv, dgate_prep = gdn_prep_chunks_bwd(sv["qkv"], sv["gate"], *dchunks, tag + "_chunks")
        dba, dal, ddt = gate_bwd(sv["ba"], sv["al"], sv["dt"], dgate_prep, dgate_scan, tag + "_gate")
        G["gdn_a_log"], G["gdn_dt_bias"] = dal[0, GDN_HEADS:2 * GDN_HEADS], ddt[0, GDN_HEADS:2 * GDN_HEADS]
        dqkv_raw, dconv = gdn_prep_bwd(proj, dqkv, P["gdn_conv_w"][e], tag + "_prep")
        G["gdn_conv_w"] = dconv[:GDN_CONV]
        w_main = P["ev_main"][e]
        dx0, dnm, dsc_m, dsh_m = bwd_in([dba, dqkv_raw, dzp], [P["ev_ba"][e], w_main[:, :3 * W], w_main[:, 3 * W:]],
                                        [0, 0, 0], None, None, x0, gain, sc_m, sh_m, dx1, tag + "_in")
        gw_qkv = matmul_tn(h, dqkv_raw, 1024, W, tag + "_win1")
        gw_zp = matmul_tn(h, dzp, 1024, W, tag + "_win2")
        gw_ba = matmul_tn(h, dba, 1024, 128, tag + "_win0")
        G["ev_w_in"] = jnp.concatenate([gw_qkv, gw_zp[:, :W], gw_ba[:, :2 * GDN_HEADS], gw_zp[:, W:]], axis=1)
    else:
        od = i // 2
        qn, kn = P["att_q_norm"][od:od + 1], P["att_k_norm"][od:od + 1]
        dy, dmerged, dg_m = bwd_out(dx1, g_m, sv["y_m"], P["od_w_out"][od], 1024, tag + "_out")
        _put(bufs, "od_w_out", od, sv["a_m"], dy, 1024, D, tag + "_wout")
        douts, dds = merge_bwd(sv["outs"], sv["lses"], dmerged, tag + "_merge")
        dh = None
        exchanged = {}
        dqn_sum, dkn_sum = 0.0, 0.0
        ng = len(DIL_PATTERNS)
        for gi, (_, dil) in enumerate(DIL_PATTERNS):
            riding = [carry[j] for j in CARRY_SPLIT[gi]] if (carry is not None and gi < len(CARRY_SPLIT)) else None
            dqs, dks, dv, *got = att_bwd(sv["qk"][gi], proj[gi], sv["lses"][gi], douts[gi], dds[gi], f"{tag}_att{gi}",
                                         carry=riding)
            if riding:
                exchanged.update(zip(CARRY_SPLIT[gi], got))
            dq, dk, dqn, dkn = qk_norm_bwd(proj[gi].reshape(T, 3 * D), dqs.reshape(T, D), dks.reshape(T, D), qn, kn,
                                           f"{tag}_qkn{gi}")
            dqn_sum, dkn_sum = dqn_sum + dqn[0], dkn_sum + dkn[0]
            for k, d in enumerate((dq, dk, dv)):
                _put(bufs, "od_w_in", od, h[gi], d.reshape(T, D), 1024, D, f"{tag}_win{gi}{k}", col0=3 * gi + k)
            cols = [3 * gi, 3 * gi + 1, 3 * gi + 2]
            ops = [d.reshape(dil, T // dil, D) if dil > 1 else d.reshape(T, D) for d in (dq, dk, dv)]
            if gi < ng - 1:
                dh = bwd_in(ops, P["od_w_in"][od], cols, D, dh, None, None, None, None, None, f"{tag}_in{gi}", dil=dil)
            else:
                dx0, dnm, dsc_m, dsh_m = bwd_in(ops, P["od_w_in"][od], cols, D, dh, x0, gain, sc_m, sh_m, dx1,
                                                f"{tag}_in{gi}", dil=dil)
        G["att_q_norm"], G["att_k_norm"] = dqn_sum, dkn_sum
    G["norm_mix"] = dnm[0]
    dmod = jnp.concatenate([dsh_m, dsc_m, dg_m, dsh_f, dsc_f, dg_f], axis=0)
    return dx0, dmod, G, ([exchanged[j] for j in range(len(carry))] if carry is not None else None)


_PER_LAYER = ("norm_mix", "norm_ffn", "ffn_conv_w", "ffn_conv_b")
_PER_EVEN = ("ev_w_in", "gdn_conv_w", "gdn_a_log", "gdn_dt_bias", "gdn_norm", "pool_w", "pool_scale")
_PER_ODD = ("att_q_norm", "att_k_norm")
_IN_PLACE = ("ffn_w_up", "ffn_w_down", "ev_w_out", "od_w_in", "od_w_out")


def device_step(x, mod, target, P, layer_shards=None, early_exchange=None):
    saved = []
    for i in range(DEPTH):
        nxt = layer_shards[i + 1] if (layer_shards is not None and i + 1 < DEPTH) else None
        x, sv, got = _layer_fwd(i, x, mod[i], P, nxt)
        if got is not None:
            add_layer_weights(P, got)
        saved.append(sv)
    loss, dx = loss_head(x, target, "loss")
    layer_grads, dmods = [None] * DEPTH, [None] * DEPTH
    bufs = {k: [lax.empty((len(P[k]) // 2,) + tuple(P[k][0].shape), BF16) for _ in range(2)] for k in _IN_PLACE}
    early_parts = None
    for i in reversed(range(DEPTH)):
        carry = None
        if i == DEPTH // 2 - 1 and early_exchange is not None:
            done = {k: bufs[k][1] for k in _IN_PLACE}
            done["ev_w_in"] = layer_grads[DEPTH // 2]["ev_w_in"][None]
            carry = early_exchange(done)
        dx, dmods[i], layer_grads[i], got = _layer_bwd(i, dx, saved[i], mod[i], P, bufs, carry)
        early_parts = got if got is not None else early_parts
    G = dict(bufs)
    G["ev_w_in"] = [layer_grads[0]["ev_w_in"][None], layer_grads[DEPTH // 2]["ev_w_in"][None]]
    G.update({k: jnp.stack([layer_grads[i][k] for i in range(DEPTH)]) for k in _PER_LAYER})
    G.update({k: jnp.stack([layer_grads[i][k] for i in range(0, DEPTH, 2)]) for k in _PER_EVEN if k != "ev_w_in"})
    G.update({k: jnp.stack([layer_grads[i][k] for i in range(1, DEPTH, 2)]) for k in _PER_ODD})
    return loss, dx, jnp.stack(dmods), G, early_parts


_WEIGHTS = ("ada_w", "ada_b", "norm_mix", "norm_ffn", "ev_w_in", "ev_w_out", "gdn_conv_w", "gdn_a_log", "gdn_dt_bias",
            "gdn_norm", "pool_w", "pool_scale", "od_w_in", "od_w_out", "att_q_norm", "att_k_norm", "ffn_w_up",
            "ffn_conv_w", "ffn_conv_b", "ffn_w_down")
_COL_SHARDED = ("ev_w_in", "od_w_in", "ffn_w_up")
_ROW_SHARDED = ("ev_w_out", "od_w_out", "ffn_w_down")
_SMALL_SHARDED = ("gdn_conv_w", "ffn_conv_w")
_REPLICATED = ("ada_b", "norm_mix", "norm_ffn", "gdn_a_log", "gdn_dt_bias", "gdn_norm", "pool_w", "pool_scale",
               "att_q_norm", "att_k_norm", "ffn_conv_b")
PACK_LANES = 128
PACK_ROWS = 8


def _pack_rows(n):
    unit = PACK_LANES * PACK_ROWS
    return -(-n // unit) * PACK_ROWS


def _pack(arrays):
    parts = []
    for a in arrays:
        flat = a.reshape(-1).astype(F32)
        rows = _pack_rows(flat.shape[0])
        parts.append(jnp.pad(flat, (0, rows * PACK_LANES - flat.shape[0])).reshape(rows, PACK_LANES))
    return jnp.concatenate(parts, axis=0)


def _unpack(packed, shapes, lead=()):
    out, row = [], 0
    for s in shapes:
        n = math.prod(s)
        rows = _pack_rows(n)
        blk = packed[..., row:row + rows, :].reshape(lead + (rows * PACK_LANES,))
        out.append(blk[..., :n].reshape(lead + tuple(s)))
        row += rows
    return out


def _unshard_cols(g):
    _, L, R, n = g.shape
    return g.transpose(1, 2, 0, 3).reshape(L, R, N_DEV * n)


def _shard_cols(full):
    L, R, N = full.shape
    n = N // N_DEV
    return full.reshape(L * R, N_DEV, n).transpose(1, 0, 2)


def _shard_rows(full):
    L, R, C = full.shape
    r = R // N_DEV
    return full.reshape(L, N_DEV, r, C).transpose(1, 0, 2, 3).reshape(N_DEV, L * r, C)


def kernel(x, c, ada_w, ada_b, norm_mix, norm_ffn, ev_w_in, ev_w_out, gdn_conv_w, gdn_a_log, gdn_dt_bias, gdn_norm, pool_w, pool_scale, od_w_in, od_w_out, att_q_norm, att_k_norm, ffn_w_up, ffn_conv_w, ffn_conv_b, ffn_w_down, loss_target, m_ada_w, m_ada_b, m_norm_mix, m_norm_ffn, m_ev_w_in, m_ev_w_out, m_gdn_conv_w, m_gdn_a_log, m_gdn_dt_bias, m_gdn_norm, m_pool_w, m_pool_scale, m_od_w_in, m_od_w_out, m_att_q_norm, m_att_k_norm, m_ffn_w_up, m_ffn_conv_w, m_ffn_conv_b, m_ffn_w_down, v_ada_w, v_ada_b, v_norm_mix, v_norm_ffn, v_ev_w_in, v_ev_w_out, v_gdn_conv_w, v_gdn_a_log, v_gdn_dt_bias, v_gdn_norm, v_pool_w, v_pool_scale, v_od_w_in, v_od_w_out, v_att_q_norm, v_att_k_norm, v_ffn_w_up, v_ffn_conv_w, v_ffn_conv_b, v_ffn_w_down):
    args = locals()
    Wl = {k: args[k] for k in _WEIGHTS}
    Ml = {k: args["m_" + k] for k in _WEIGHTS}
    Vl = {k: args["v_" + k] for k in _WEIGHTS}
    mx, my, mc = _mesh_pos()
    dev = 4 * mx + 2 * my + mc
    T, D = x.shape[1], x.shape[2]
    x2d, tgt = x.reshape(T, D), loss_target.reshape(T, D)

    small_shapes = [c.shape] + [Wl[k].shape for k in _SMALL_SHARDED]
    big = list(_COL_SHARDED + _ROW_SHARDED)
    layer_shards = []
    for i in range(DEPTH):
        mixer = ("ev_w_in", "ev_w_out") if i % 2 == 0 else ("od_w_in", "od_w_out")
        shards = {k: Wl[k][i // 2].astype(BF16) for k in mixer}
        shards.update({k: Wl[k][i].astype(BF16) for k in ("ffn_w_up", "ffn_w_down")})
        layer_shards.append(shards)
    gathered = all_gather([_pack([c] + [Wl[k] for k in _SMALL_SHARDED])] + list(layer_shards[0].values()), "gather_w0")
    c_all, conv_g, conv_f = _unpack(gathered[0], small_shapes, lead=(N_DEV,))
    c_all = c_all.reshape(N_DEV, D)
    P = {k: Wl[k] for k in _REPLICATED}
    P["gdn_conv_w"], P["ffn_conv_w"] = _unshard_cols(conv_g), _unshard_cols(conv_f)
    add_layer_weights(P, dict(zip(layer_shards[0].keys(), gathered[1:])))

    n_ada = ada_w.shape[2]
    b_cols = lax.dynamic_slice_in_dim(ada_b, dev * n_ada, n_ada, axis=1)
    mod_cols = ada_fwd(c_all, ada_w, b_cols[:, None, :], "ada_fwd")
    mod_all, = all_gather([mod_cols], "gather_mod")
    mod = lax.dynamic_index_in_dim(mod_all, dev, axis=2, keepdims=False)
    mod = mod.transpose(1, 0, 2).reshape(DEPTH, 6, D)

    c_idx = jnp.reshape(mc, (1,)).astype(jnp.int32)

    def chip_partials(grads, tag):
        stacked = [_shard_cols(grads[k]) for k in _COL_SHARDED] + [_shard_rows(grads[k]) for k in _ROW_SHARDED]
        got = exchange_pair(stacked, "rs_pair" + tag)
        return [pair_add(s, g, c_idx, f"rs_add_{k}{tag}") for s, g, k in zip(stacked, got, big)]

    loss, dx, dmod, G, early_parts = device_step(x2d, mod, tgt, P, layer_shards, lambda g: chip_partials(g, "_b"))
    loss = lax.psum(loss[0, 0], ("x", "y", "c"))

    G["ada_b"] = dmod.reshape(DEPTH, 6 * D)
    small = list(_REPLICATED) + list(_SMALL_SHARDED)
    parts_all, = all_gather([_pack([G[k] for k in small])], "gather_small")
    zeros = {k: jnp.zeros_like(G[k]) for k in _SMALL_SHARDED}
    packs = [_pack([src[k] for k in _REPLICATED] + [zeros[k] for k in _SMALL_SHARDED]) for src in (Wl, Ml, Vl)]
    res = adamw(*packs, parts_all, "adamw_small")
    shapes = [G[k].shape for k in small]
    out_g, out_d, out_m, out_v = ({k: a for k, a in zip(small, _unpack(r, shapes))} for r in res)
    dmod_all = _unpack(parts_all, shapes, lead=(N_DEV,))[0].reshape(N_DEV, DEPTH, 6 * D)
    dm_cols = lax.dynamic_slice_in_dim(dmod_all, dev * n_ada, n_ada, axis=2).transpose(1, 0, 2)
    g_ada = ada_bwd(c_all, dm_cols, "ada_bwd")

    def flat2(a):
        return a.reshape(-1, a.shape[-1])

    loc = {k: lax.dynamic_slice_in_dim(out_g[k], dev * Wl[k].shape[-1], Wl[k].shape[-1], axis=out_g[k].ndim - 1)
           for k in _SMALL_SHARDED}
    res = adamw(*[_pack([src[k] for k in _SMALL_SHARDED]) for src in (Wl, Ml, Vl)],
                _pack([loc[k] for k in _SMALL_SHARDED])[None], "adamw_conv")
    for dst, r in zip((out_g, out_d, out_m, out_v), res):
        dst.update(zip(_SMALL_SHARDED, _unpack(r, [Wl[k].shape for k in _SMALL_SHARDED])))
    res = adamw(flat2(ada_w), flat2(m_ada_w), flat2(v_ada_w), flat2(g_ada)[None], "adamw_ada")
    for dst, r in zip((out_g, out_d, out_m, out_v), res):
        dst["ada_w"] = r.reshape(ada_w.shape)

    parts = exchange_chips(chip_partials({k: G[k][0] for k in big}, "_a"), "rs_chips")
    for k, p, q in zip(big, parts, early_parts):
        res = adamw(flat2(Wl[k]), flat2(Ml[k]), flat2(Vl[k]), jnp.concatenate([p, q], axis=1), "adamw_" + k)
        for dst, r in zip((out_g, out_d, out_m, out_v), res):
            dst[k] = r.reshape(Wl[k].shape)

    return (loss, dx.reshape(x.shape), *[out_g[k] for k in _WEIGHTS], *[out_d[k] for k in _WEIGHTS],
            *[out_m[k] for k in _WEIGHTS], *[out_v[k] for k in _WEIGHTS])
```

```python
import functools
import math

import jax
import jax.numpy as jnp
from jax import lax
from jax.experimental import pallas as pl
from jax.experimental.pallas import tpu as pltpu

F32 = jnp.float32
BF16 = jnp.bfloat16
HI = lax.Precision.HIGHEST
MESH = pl.DeviceIdType.MESH

N_DEV = 8
RMS_EPS = 1e-6
DEPTH = 4
GDN_HEADS = 4
GDN_D = 128
GDN_CHUNK = 64
GDN_CONV = 4
POOL_WINDOWS = (2, 4, 8, 16)
DIL_PATTERNS = ((128, 1), (512, 4), (2048, 16))
ATT_HEADS = 8
ATT_DH = 128
ATT_BLOCK = 128
FFN_CONV = 3
ADAM_LR, ADAM_B1, ADAM_B2, ADAM_EPS, ADAM_WD, ADAM_STEP = 0.001, 0.9, 0.999, 1e-08, 0.01, 10

HALO = 16
NEG = -1e30
VMEM_LIMIT_BYTES = 56 * 1024 * 1024


def _cp(sem=None, **kw):
    return pltpu.CompilerParams(dimension_semantics=sem, vmem_limit_bytes=VMEM_LIMIT_BYTES, **kw)


def _sds(shape, dtype):
    return jax.ShapeDtypeStruct(tuple(shape), dtype)


def _dot(a, b):
    return jnp.dot(a.astype(BF16), b.astype(BF16), preferred_element_type=F32)


def _dot_nt(a, b):
    return lax.dot_general(a.astype(BF16), b.astype(BF16), (((1,), (1,)), ((), ())), preferred_element_type=F32)


def _dot_tn(a, b):
    return lax.dot_general(a.astype(BF16), b.astype(BF16), (((0,), (0,)), ((), ())), preferred_element_type=F32)


def _dot_hi(a, b):
    return jnp.dot(a, b, preferred_element_type=F32, precision=HI)


def _silu(x):
    return x * jax.nn.sigmoid(x)


def _modnorm(x, gain, sc, sh):
    y = x * lax.rsqrt(jnp.mean(x * x, axis=-1, keepdims=True) + RMS_EPS)
    return y * gain * (1.0 + sc) + sh


def _row_tile(T):
    return 512 if T % 512 == 0 else T


LANES = 128


def _deinterleave(val, scr, dil):
    tm, width = val.shape
    sub = tm // dil
    ncb = width // LANES
    for cb in range(ncb):
        scr[cb] = val[:, cb * LANES:(cb + 1) * LANES]
    return jnp.concatenate([jnp.concatenate([scr.at[cb][pl.ds(r, sub, stride=dil), :] for cb in range(ncb)], axis=1)
                            for r in range(dil)], axis=0)


def _interleave(val, scr, dil):
    tm, width = val.shape
    sub = tm // dil
    ncb = width // LANES
    for r in range(dil):
        for cb in range(ncb):
            scr.at[cb][pl.ds(r, sub, stride=dil), :] = val[r * sub:(r + 1) * sub, cb * LANES:(cb + 1) * LANES]
    return jnp.concatenate([scr[cb] for cb in range(ncb)], axis=1)


def norm_proj(x, gain, sc, sh, w, tn, name, cols=None, dil=1, qk_norms=None, gather=None):
    T, D = x.shape
    c0, ncol = cols if cols is not None else (0, w.shape[1] // tn)
    N = ncol * tn
    tm = 1024 if (ncol > 1 and dil == 1 and T % 1024 == 0) else _row_tile(T)
    sub = tm // dil
    n_g = len(gather) if gather else 0
    n_own_in = 5 + (2 if qk_norms is not None else 0)
    n_own_out = 2 + (1 if qk_norms is not None else 0)
    n_in, n_out = n_own_in + n_g, n_own_out + n_g
    n_scr = 1 + (1 if dil > 1 else 0)
    nsteps = (T // tm) * ncol
    Wd = tn // 3

    def body(*refs):
        x_ref, g_ref, sc_ref, sh_ref, w_ref = refs[:5]
        h_ref, o_ref = refs[n_in:n_in + 2]
        h_scr = refs[n_in + n_out]
        step = pl.program_id(0) * ncol + pl.program_id(1)
        if n_g:
            exch = _Gather(refs[n_own_in:n_in], refs[n_in + n_own_out:n_in + n_out], *refs[n_in + n_out + n_scr:])
            pl.when(step == 0)(exch.start)
            pl.when(step == nsteps // 2)(exch.forward)

        @pl.when(pl.program_id(1) == 0)
        def _():
            h = _modnorm(x_ref[...], g_ref[...], sc_ref[...], sh_ref[...])
            if dil > 1:
                h = _deinterleave(h, refs[n_in + n_out + 1], dil)
            h_scr[...] = h.astype(BF16)
            h_ref[...] = h.reshape(h_ref.shape).astype(BF16)
        res = jnp.dot(h_scr[...], w_ref[...], preferred_element_type=F32)
        o_ref[...] = res.reshape(o_ref.shape).astype(BF16)
        if qk_norms is not None:
            qk_ref = refs[n_in + 2]
            parts = []
            for t, scale in enumerate((ATT_DH ** -0.5, 1.0)):
                for hd in range(Wd // ATT_DH):
                    c = t * Wd + hd * ATT_DH
                    parts.append(_head_norm(res[:, c:c + ATT_DH], refs[5 + t][...]) * scale)
            qk_ref[...] = jnp.concatenate(parts, axis=1).reshape(qk_ref.shape).astype(BF16)
        if n_g:
            pl.when(step == nsteps - 1)(exch.finish)

    vec = pl.BlockSpec((1, D), lambda i, j: (0, 0))
    in_specs = [pl.BlockSpec((tm, D), lambda i, j: (i, 0)), vec, vec, vec, pl.BlockSpec((D, tn), lambda i, j: (0, c0 + j))]
    args = [x, gain, sc, sh, w]
    out_specs = [pl.BlockSpec((dil, sub, D), lambda i, j: (0, i, 0)), pl.BlockSpec((dil, sub, tn), lambda i, j: (0, i, j))]
    out_shape = [_sds((dil, T // dil, D), BF16), _sds((dil, T // dil, N), BF16)]
    if qk_norms is not None:
        assert ncol == 1
        in_specs += [pl.BlockSpec((1, ATT_DH), lambda i, j: (0, 0))] * 2
        args += list(qk_norms)
        out_specs.append(pl.BlockSpec((dil, sub, 2 * Wd), lambda i, j: (0, i, 0)))
        out_shape.append(_sds((dil, T // dil, 2 * Wd), BF16))
    scratch = [pltpu.VMEM((tm, D), BF16)] + ([pltpu.VMEM((D // LANES, tm, LANES), F32)] if dil > 1 else [])
    if n_g:
        in_specs += [_ANY] * n_g
        args += list(gather)
        out_specs += [_ANY] * n_g
        out_shape += [_sds((N_DEV,) + g.shape, g.dtype) for g in gather]
        scratch += _gather_scratch(n_g)
    res = pl.pallas_call(
        body, name=name, grid=(T // tm, ncol), in_specs=in_specs, out_specs=out_specs, out_shape=out_shape,
        scratch_shapes=scratch, compiler_params=_cp(("arbitrary", "arbitrary") if n_g else ("parallel", "arbitrary")),
    )(*args)
    if dil == 1 and qk_norms is None:
        return [r.reshape(T, -1) for r in res[:2]] + list(res[2:])
    return res


def matmul_nn(a, w, out_dtype, name):
    T, K = a.shape
    N = w.shape[1]
    tm = _row_tile(T)

    def body(a_ref, w_ref, o_ref):
        o_ref[...] = _dot(a_ref[...], w_ref[...]).astype(o_ref.dtype)

    return pl.pallas_call(
        body, name=name, grid=(T // tm,),
        in_specs=[pl.BlockSpec((tm, K), lambda i: (i, 0)), pl.BlockSpec((K, N), lambda i: (0, 0))],
        out_specs=pl.BlockSpec((tm, N), lambda i: (i, 0)),
        out_shape=_sds((T, N), out_dtype),
        compiler_params=_cp(("parallel",)),
    )(a, w)


def proj_res(a, w, x, gate, name):
    T, K = a.shape
    D = w.shape[1]
    tm = _row_tile(T)

    def body(a_ref, w_ref, x_ref, g_ref, o_ref, y_ref):
        y = jnp.dot(a_ref[...], w_ref[...], preferred_element_type=F32)
        y_ref[...] = y.astype(BF16)
        o_ref[...] = x_ref[...] + g_ref[...] * y

    return pl.pallas_call(
        body, name=name, grid=(T // tm,),
        in_specs=[pl.BlockSpec((tm, K), lambda i: (i, 0)), pl.BlockSpec((K, D), lambda i: (0, 0)),
                  pl.BlockSpec((tm, D), lambda i: (i, 0)), pl.BlockSpec((1, D), lambda i: (0, 0))],
        out_specs=[pl.BlockSpec((tm, D), lambda i: (i, 0)), pl.BlockSpec((tm, D), lambda i: (i, 0))],
        out_shape=[_sds((T, D), F32), _sds((T, D), BF16)],
        compiler_params=_cp(("parallel",)),
    )(a, w, x, gate)


def bwd_out(dx, gate, y, w, tk, name):
    T, D = dx.shape
    K = w.shape[0]
    tm = _row_tile(T)

    def body(dx_ref, g_ref, y_ref, w_ref, dy_ref, da_ref, dg_ref, dy_scr):
        i, j = pl.program_id(0), pl.program_id(1)

        @pl.when(j == 0)
        def _():
            dxv = dx_ref[...]
            dy = (dxv * g_ref[...]).astype(BF16)
            dy_scr[...] = dy
            dy_ref[...] = dy
            part = jnp.sum(dxv * y_ref[...].astype(F32), axis=0, keepdims=True)

            @pl.when(i == 0)
            def _():
                dg_ref[...] = part

            @pl.when(i > 0)
            def _():
                dg_ref[...] += part

        da_ref[...] = _dot_nt(dy_scr[...], w_ref[...]).astype(BF16)

    return pl.pallas_call(
        body, name=name, grid=(T // tm, K // tk),
        in_specs=[pl.BlockSpec((tm, D), lambda i, j: (i, 0)), pl.BlockSpec((1, D), lambda i, j: (0, 0)),
                  pl.BlockSpec((tm, D), lambda i, j: (i, 0)), pl.BlockSpec((tk, D), lambda i, j: (j, 0))],
        out_specs=[pl.BlockSpec((tm, D), lambda i, j: (i, 0)), pl.BlockSpec((tm, tk), lambda i, j: (i, j)),
                   pl.BlockSpec((1, D), lambda i, j: (0, 0))],
        out_shape=[_sds((T, D), BF16), _sds((T, K), BF16), _sds((1, D), F32)],
        scratch_shapes=[pltpu.VMEM((tm, D), BF16)],
        compiler_params=_cp(("arbitrary", "arbitrary")),
    )(dx, gate, y, w)


def bwd_in(a_list, w, col_blocks, tn, acc, x, gain, sc, sh, dx_res, name, dil=1):
    T = math.prod(a_list[0].shape[:-1])
    n_a = len(a_list)
    w_list = list(w) if isinstance(w, (list, tuple)) else [w] * n_a
    D = w_list[0].shape[0]
    tm = 256 if T % 256 == 0 else T
    sub = tm // dil
    tns = [tn if tn is not None else a.shape[-1] for a in a_list]
    nsteps = a_list[0].shape[-1] // tns[0]
    assert all(a.shape[-1] == nsteps * t for a, t in zip(a_list, tns))
    final = x is not None
    has_acc = acc is not None

    def body(*refs):
        a_refs = refs[:n_a]
        w_refs = refs[n_a:2 * n_a]
        pos = 2 * n_a
        acc_ref = refs[pos] if has_acc else None
        pos += int(has_acc)
        if final:
            x_ref, g_ref, sc_ref, sh_ref, dxr_ref = refs[pos:pos + 5]
            pos += 5
            dx_ref, dg_ref, dsc_ref, dsh_ref = refs[pos:pos + 4]
            pos += 4
        else:
            dh_ref = refs[pos]
            pos += 1
        acc_scr = refs[pos]
        i, j = pl.program_id(0), pl.program_id(1)
        part = _dot_nt(a_refs[0][...].reshape(tm, tns[0]), w_refs[0][...])
        for k in range(1, n_a):
            part += _dot_nt(a_refs[k][...].reshape(tm, tns[k]), w_refs[k][...])

        @pl.when(j == 0)
        def _():
            acc_scr[...] = part

        @pl.when(j > 0)
        def _():
            acc_scr[...] += part

        @pl.when(j == nsteps - 1)
        def _():
            dh = acc_scr[...]
            if dil > 1:
                dh = _interleave(dh, refs[pos + 1], dil)
            if has_acc:
                dh = dh + acc_ref[...]
            if not final:
                dh_ref[...] = dh
                return
            _, vjp = jax.vjp(_modnorm, x_ref[...], g_ref[...], sc_ref[...], sh_ref[...])
            dxn, dg, dsc, dsh = vjp(dh)
            dx_ref[...] = dxr_ref[...] + dxn

            @pl.when(i == 0)
            def _():
                dg_ref[...] = dg
                dsc_ref[...] = dsc
                dsh_ref[...] = dsh

            @pl.when(i > 0)
            def _():
                dg_ref[...] += dg
                dsc_ref[...] += dsc
                dsh_ref[...] += dsh

    row = pl.BlockSpec((tm, D), lambda i, j: (i, 0))
    vec = pl.BlockSpec((1, D), lambda i, j: (0, 0))
    if dil == 1:
        in_specs = [pl.BlockSpec((tm, t), lambda i, j: (i, j)) for t in tns]
    else:
        in_specs = [pl.BlockSpec((dil, sub, t), lambda i, j: (0, i, j)) for t in tns]
    in_specs += [pl.BlockSpec((D, t), functools.partial(lambda i, j, c0: (0, c0 + j), c0=c0))
                 for t, c0 in zip(tns, col_blocks)]
    args = list(a_list) + w_list
    if has_acc:
        in_specs.append(row)
        args.append(acc)
    if final:
        in_specs += [row, vec, vec, vec, row]
        args += [x, gain, sc, sh, dx_res]
        out_specs = [row, vec, vec, vec]
        out_shape = [_sds((T, D), F32)] + [_sds((1, D), F32)] * 3
    else:
        out_specs = row
        out_shape = _sds((T, D), F32)
    return pl.pallas_call(
        body, name=name, grid=(T // tm, nsteps), in_specs=in_specs, out_specs=out_specs, out_shape=out_shape,
        scratch_shapes=[pltpu.VMEM((tm, D), F32)] + ([pltpu.VMEM((D // LANES, tm, LANES), F32)] if dil > 1 else []),
        compiler_params=_cp(("arbitrary", "arbitrary")),
    )(*args)


def matmul_tn(a, b, tk, tn, name, into=None, layer=0, col0=0):
    T, K = a.shape
    N = b.shape[1]
    tt = 2048 if T % 2048 == 0 else T
    nt = T // tt

    def body(a_ref, b_ref, *rest):
        o_ref, acc = rest[-2:]
        part = _dot_tn(a_ref[...], b_ref[...])

        @pl.when(pl.program_id(2) == 0)
        def _():
            acc[...] = part

        @pl.when(pl.program_id(2) > 0)
        def _():
            acc[...] += part

        @pl.when(pl.program_id(2) == nt - 1)
        def _():
            o_ref[...] = acc[...].astype(BF16)

    in_specs = [pl.BlockSpec((tt, tk), lambda k, n, t: (t, k)), pl.BlockSpec((tt, tn), lambda k, n, t: (t, n))]
    common = dict(name=name, grid=(K // tk, N // tn, nt), scratch_shapes=[pltpu.VMEM((tk, tn), F32)],
                  compiler_params=_cp(("parallel", "parallel", "arbitrary")))
    if into is None:
        return pl.pallas_call(body, in_specs=in_specs, out_specs=pl.BlockSpec((tk, tn), lambda k, n, t: (k, n)),
                              out_shape=_sds((K, N), BF16), **common)(a, b)
    return pl.pallas_call(
        body, in_specs=in_specs + [_ANY], out_specs=pl.BlockSpec((None, tk, tn), lambda k, n, t: (layer, k, col0 + n)),
        out_shape=_sds(into.shape, BF16), input_output_aliases={2: 0}, **common)(a, b, into)


def _halo_specs(tm, tc, col_of):
    per = tm // HALO

    def prev(j, i):
        return (jnp.maximum(i * per - 1, 0), col_of(j))

    def nxt(j, i, last):
        return (jnp.minimum((i + 1) * per, last), col_of(j))

    return prev, nxt, per


def _shift_down(ext, s):
    return pltpu.roll(ext, s, 0)


def _shift_up(ext, s):
    return pltpu.roll(ext, ext.shape[0] - s, 0)


def _conv_ext(ext, w):
    K = w.shape[0]
    out = w[K - 1:K] * ext
    for s in range(1, K):
        out += w[K - 1 - s:K - s] * _shift_down(ext, s)
    return out


def _conv_t_ext(dext, w):
    K = w.shape[0]
    out = w[K - 1:K] * dext
    for s in range(1, K):
        out += w[K - 1 - s:K - s] * _shift_up(dext, s)
    return out


def _conv_dw(dc, ext, K, tm):
    rowid = lax.broadcasted_iota(jnp.int32, (8, dc.shape[1]), 0)
    out = jnp.zeros((8, dc.shape[1]), F32)
    for j in range(K):
        s = K - 1 - j
        xs = ext if s == 0 else _shift_down(ext, s)
        out = jnp.where(rowid == j, jnp.sum(dc * xs[HALO:HALO + tm], axis=0, keepdims=True), out)
    return out


def _accum(ref, val, first):
    @pl.when(first)
    def _():
        ref[...] = val

    @pl.when(jnp.logical_not(first))
    def _():
        ref[...] += val


def ffn_mid_fwd(up, conv_w, conv_b, name):
    T, two_f = up.shape
    F = two_f // 2
    tm = _row_tile(T)
    tc = F // 2
    nct = F // tc
    prev, _, per = _halo_specs(tm, tc, lambda j: j)

    def body(g_ref, gp_ref, v_ref, w_ref, b_ref, a_ref):
        i = pl.program_id(1)
        gp = jnp.where(i > 0, gp_ref[...].astype(F32), 0.0)
        ext = jnp.concatenate([gp, g_ref[...].astype(F32)], axis=0)
        c = _conv_ext(ext, w_ref[...])[HALO:] + b_ref[...]
        a_ref[...] = (_silu(c) * v_ref[...].astype(F32)).astype(BF16)

    return pl.pallas_call(
        body, name=name, grid=(nct, T // tm),
        in_specs=[pl.BlockSpec((tm, tc), lambda j, i: (i, j)), pl.BlockSpec((HALO, tc), prev),
                  pl.BlockSpec((tm, tc), lambda j, i: (i, j + nct)),
                  pl.BlockSpec((FFN_CONV, tc), lambda j, i: (0, j)), pl.BlockSpec((1, tc), lambda j, i: (0, j))],
        out_specs=pl.BlockSpec((tm, tc), lambda j, i: (i, j)),
        out_shape=_sds((T, F), BF16),
        compiler_params=_cp(("parallel", "arbitrary")),
    )(up, up, up, conv_w, conv_b)


def ffn_mid_bwd(up, da, conv_w, conv_b, name):
    T, two_f = up.shape
    F = two_f // 2
    tm = _row_tile(T)
    tc = F // 2
    nct = F // tc
    nrow = T // tm
    prev, nxt, per = _halo_specs(tm, tc, lambda j: j)
    last = T // HALO - 1
    nxt_g = functools.partial(nxt, last=last)

    def nxt_v(j, i):
        return (jnp.minimum((i + 1) * per, last), j + nct)

    def body(g_ref, gp_ref, gn_ref, v_ref, vn_ref, da_ref, dan_ref, w_ref, b_ref, dg_ref, dv_ref, dw_ref, db_ref):
        i = pl.program_id(1)
        w = w_ref[...]
        gp = jnp.where(i > 0, gp_ref[...].astype(F32), 0.0)
        inside = i < nrow - 1
        ext = jnp.concatenate([gp, g_ref[...].astype(F32), gn_ref[...].astype(F32)], axis=0)
        zero = jnp.zeros((HALO, tc), F32)
        v_ext = jnp.concatenate([zero, v_ref[...].astype(F32), vn_ref[...].astype(F32)], axis=0)
        da_ext = jnp.concatenate([zero, da_ref[...].astype(F32), jnp.where(inside, dan_ref[...].astype(F32), 0.0)], axis=0)
        c = _conv_ext(ext, w) + b_ref[...]
        sg = jax.nn.sigmoid(c)
        sil = c * sg
        dc = da_ext * v_ext * (sg * (1.0 + c * (1.0 - sg)))
        dv_ref[...] = (da_ext * sil)[HALO:HALO + tm].astype(BF16)
        dg_ref[...] = _conv_t_ext(dc, w)[HALO:HALO + tm].astype(BF16)
        dcm = dc[HALO:HALO + tm]
        _accum(dw_ref, _conv_dw(dcm, ext, FFN_CONV, tm), i == 0)
        _accum(db_ref, jnp.sum(dcm, axis=0, keepdims=True), i == 0)

    main = lambda j, i: (i, j)
    main_v = lambda j, i: (i, j + nct)
    return pl.pallas_call(
        body, name=name, grid=(nct, nrow),
        in_specs=[pl.BlockSpec((tm, tc), main), pl.BlockSpec((HALO, tc), prev), pl.BlockSpec((HALO, tc), nxt_g),
                  pl.BlockSpec((tm, tc), main_v), pl.BlockSpec((HALO, tc), nxt_v),
                  pl.BlockSpec((tm, tc), main), pl.BlockSpec((HALO, tc), nxt_g),
                  pl.BlockSpec((FFN_CONV, tc), lambda j, i: (0, j)), pl.BlockSpec((1, tc), lambda j, i: (0, j))],
        out_specs=[pl.BlockSpec((tm, tc), main), pl.BlockSpec((tm, tc), main),
                   pl.BlockSpec((8, tc), lambda j, i: (0, j)), pl.BlockSpec((1, tc), lambda j, i: (0, j))],
        out_shape=[_sds((T, F), BF16), _sds((T, F), BF16), _sds((8, F), F32), _sds((1, F), F32)],
        compiler_params=_cp(("parallel", "arbitrary")),
    )(up, up, up, up, up, da, da, conv_w, conv_b)


def _l2n(x):
    return x * lax.rsqrt(jnp.sum(x * x, axis=-1, keepdims=True) + RMS_EPS)


def _qkv_tok(c, normed):
    s = _silu(c)
    n = jnp.concatenate([_l2n(s[:, h * GDN_D:(h + 1) * GDN_D]) for h in range(GDN_HEADS)], axis=1)
    return jnp.where(normed, n, s)


def gdn_prep_fwd(proj, conv_w, name):
    T = proj.shape[0]
    W = GDN_HEADS * GDN_D
    tm = _row_tile(T)
    prev, _, per = _halo_specs(tm, W, lambda j: j)

    def body(x_ref, xp_ref, w_ref, o_ref):
        j, i = pl.program_id(0), pl.program_id(1)
        xp = jnp.where(i > 0, xp_ref[...].astype(F32), 0.0)
        ext = jnp.concatenate([xp, x_ref[...].astype(F32)], axis=0)
        c = _conv_ext(ext, w_ref[...])[HALO:]
        o_ref[...] = _qkv_tok(c, j < 2)

    return pl.pallas_call(
        body, name=name, grid=(3, T // tm),
        in_specs=[pl.BlockSpec((tm, W), lambda j, i: (i, j)), pl.BlockSpec((HALO, W), prev),
                  pl.BlockSpec((GDN_CONV, W), lambda j, i: (0, j))],
        out_specs=pl.BlockSpec((tm, W), lambda j, i: (i, j)),
        out_shape=_sds((T, 3 * W), F32),
        compiler_params=_cp(("parallel", "arbitrary")),
    )(proj, proj, conv_w)


def gdn_prep_bwd(proj, dqkv, conv_w, name):
    T = proj.shape[0]
    W = GDN_HEADS * GDN_D
    tm = _row_tile(T)
    nrow = T // tm
    prev, nxt, per = _halo_specs(tm, W, lambda j: j)
    nxt = functools.partial(nxt, last=T // HALO - 1)

    def body(x_ref, xp_ref, xn_ref, d_ref, dn_ref, w_ref, dx_ref, dw_ref):
        j, i = pl.program_id(0), pl.program_id(1)
        w = w_ref[...]
        xp = jnp.where(i > 0, xp_ref[...].astype(F32), 0.0)
        ext = jnp.concatenate([xp, x_ref[...].astype(F32), xn_ref[...].astype(F32)], axis=0)
        d_ext = jnp.concatenate([jnp.zeros((HALO, W), F32), d_ref[...],
                                 jnp.where(i < nrow - 1, dn_ref[...], 0.0)], axis=0)
        c = _conv_ext(ext, w)
        _, vjp = jax.vjp(lambda cc: _qkv_tok(cc, j < 2), c)
        dc, = vjp(d_ext)
        dx_ref[...] = _conv_t_ext(dc, w)[HALO:HALO + tm].astype(BF16)
        _accum(dw_ref, _conv_dw(dc[HALO:HALO + tm], ext, GDN_CONV, tm), i == 0)

    main = lambda j, i: (i, j)
    return pl.pallas_call(
        body, name=name, grid=(3, nrow),
        in_specs=[pl.BlockSpec((tm, W), main), pl.BlockSpec((HALO, W), prev), pl.BlockSpec((HALO, W), nxt),
                  pl.BlockSpec((tm, W), main), pl.BlockSpec((HALO, W), nxt),
                  pl.BlockSpec((GDN_CONV, W), lambda j, i: (0, j))],
        out_specs=[pl.BlockSpec((tm, W), main), pl.BlockSpec((8, W), lambda j, i: (0, j))],
        out_shape=[_sds((T, 3 * W), BF16), _sds((8, 3 * W), F32)],
        compiler_params=_cp(("parallel", "arbitrary")),
    )(proj, proj, proj, dqkv, dqkv, conv_w)


def _gate_tok(ba, a_log, dt_bias):
    z = ba + dt_bias
    softplus = jnp.maximum(z, 0.0) + jnp.log(1.0 + jnp.exp(-jnp.abs(z)))
    lane = lax.broadcasted_iota(jnp.int32, ba.shape, 1)
    raw = jnp.where(lane < GDN_HEADS, jax.nn.sigmoid(ba), -jnp.exp(a_log) * softplus)
    n = ba.shape[0]
    r = lax.broadcasted_iota(jnp.int32, (n, n), 0)
    c = lax.broadcasted_iota(jnp.int32, (n, n), 1)
    in_chunk_before = (r >= c) & ((r - c) <= (r & (GDN_CHUNK - 1)))
    return jnp.where(lane < GDN_HEADS, raw, _dot_hi(in_chunk_before.astype(F32), raw))


def gate_fwd(ba, a_log, dt_bias, name):
    T, L = ba.shape
    tm = _row_tile(T)

    def body(ba_ref, al_ref, dt_ref, o_ref):
        o_ref[...] = _gate_tok(ba_ref[...], al_ref[...], dt_ref[...])

    vec = pl.BlockSpec((1, L), lambda i: (0, 0))
    return pl.pallas_call(
        body, name=name, grid=(T // tm,),
        in_specs=[pl.BlockSpec((tm, L), lambda i: (i, 0)), vec, vec],
        out_specs=pl.BlockSpec((tm, L), lambda i: (i, 0)), out_shape=_sds((T, L), F32),
        compiler_params=_cp(("parallel",)),
    )(ba, a_log, dt_bias)


def gate_bwd(ba, a_log, dt_bias, dout_a, dout_b, name):
    T, L = ba.shape
    tm = _row_tile(T)

    def body(ba_ref, al_ref, dt_ref, d_ref, d2_ref, dba_ref, dal_ref, ddt_ref):
        _, vjp = jax.vjp(_gate_tok, ba_ref[...], al_ref[...], dt_ref[...])
        dba, dal, ddt = vjp(d_ref[...] + d2_ref[...])
        dba_ref[...] = dba.astype(BF16)
        first = pl.program_id(0) == 0
        _accum(dal_ref, dal, first)
        _accum(ddt_ref, ddt, first)

    vec = pl.BlockSpec((1, L), lambda i: (0, 0))
    row = pl.BlockSpec((tm, L), lambda i: (i, 0))
    return pl.pallas_call(
        body, name=name, grid=(T // tm,),
        in_specs=[row, vec, vec, row, row], out_specs=[row, vec, vec],
        out_shape=[_sds((T, L), BF16), _sds((1, L), F32), _sds((1, L), F32)],
        compiler_params=_cp(("arbitrary",)),
    )(ba, a_log, dt_bias, dout_a, dout_b)


_B_NN = (((2,), (1,)), ((0,), (0,)))
_B_NT = (((2,), (2,)), ((0,), (0,)))
GDN_GROUP = 8


def _split_bf16(a):
    hi = a.astype(BF16)
    return hi, (a - hi.astype(F32)).astype(BF16)


@jax.custom_vjp
def _bdot3(a, b):
    ah, al = _split_bf16(a)
    bh, bl = _split_bf16(b)
    dg = lambda x, y: lax.dot_general(x, y, _B_NN, preferred_element_type=F32)
    return dg(ah, bh) + (dg(ah, bl) + dg(al, bh))


def _bdot3_fwd(a, b):
    return _bdot3(a, b), (a, b)


def _bdot3_bwd(res, ct):
    a, b = res
    c16 = ct.astype(BF16)
    da = lax.dot_general(c16, b.astype(BF16), _B_NT, preferred_element_type=F32)
    db = lax.dot_general(a.astype(BF16), c16, (((1,), (1,)), ((0,), (0,))), preferred_element_type=F32)
    return da, db


_bdot3.defvjp(_bdot3_fwd, _bdot3_bwd)


def _exact_ones_dot(x, dims, ones_shape, ones_first):
    ones = jnp.ones(ones_shape, BF16)
    out = None
    for _ in range(3):
        piece = x.astype(BF16)
        ops = (ones, piece) if ones_first else (piece, ones)
        term = lax.dot_general(*ops, dims, preferred_element_type=F32)
        out = term if out is None else out + term
        x = x - piece.astype(F32)
    return out


@jax.custom_vjp
def _rows_from_cols(col):
    B, C, _ = col.shape
    eye = (lax.broadcasted_iota(jnp.int32, (1, C, C), 1) == lax.broadcasted_iota(jnp.int32, (1, C, C), 2)).astype(F32)
    return _exact_ones_dot(col * eye, _B_NN, (B, C, C), True)


def _rows_from_cols_fwd(col):
    return _rows_from_cols(col), None


def _rows_from_cols_bwd(_, ct):
    B, C, _ = ct.shape
    sums = _exact_ones_dot(ct, (((1,), (1,)), ((0,), (0,))), (B, C, LANES), False)
    return (sums[:, :, :1],)


_rows_from_cols.defvjp(_rows_from_cols_fwd, _rows_from_cols_bwd)


def _gdn_prep(q, k, v, gcol, bcol):
    B, C, dk = q.shape
    r = lax.broadcasted_iota(jnp.int32, (1, C, C), 1)
    cidx = lax.broadcasted_iota(jnp.int32, (1, C, C), 2)
    causal = r >= cidx
    strict = r > cidx
    decay = jnp.where(causal, jnp.exp(jnp.where(causal, gcol - _rows_from_cols(gcol), 0.0)), 0.0)
    qs = q * (dk ** -0.5)
    kb = k * bcol
    kk = lax.dot_general(kb.astype(BF16), k.astype(BF16), _B_NT, preferred_element_type=F32)
    L = jnp.where(strict, kk * decay, 0.0)
    tinv = (r == cidx).astype(F32) - L
    p = L
    n = 2
    while n < C:
        p = _bdot3(p, p)
        tinv = tinv + _bdot3(tinv, p)
        n *= 2
    egc = jnp.exp(gcol)
    u = _bdot3(tinv, v * bcol)
    w = _bdot3(tinv, kb * egc)
    qk = lax.dot_general(qs.astype(BF16), k.astype(BF16), _B_NT, preferred_element_type=F32)
    intra = jnp.where(causal, qk * decay, 0.0)
    last = lax.broadcasted_iota(jnp.int32, (1, C, 1), 1) == C - 1
    gt = jnp.sum(jnp.where(last, gcol, 0.0), axis=1, keepdims=True)
    return w, u, qs * egc, k * jnp.exp(gt - gcol), intra


def _gdn_scan(S, w, u, qg, kdec, intra, gcol):
    C = w.shape[0]
    last = lax.broadcasted_iota(jnp.int32, (C, 1), 0) == C - 1
    gt = jnp.sum(jnp.where(last, gcol, 0.0), axis=0, keepdims=True)
    v_new = u - _dot(w, S)
    o = _dot(qg, S) + _dot(intra, v_new)
    return S * jnp.exp(gt) + _dot_tn(kdec, v_new), o


def _gdn_specs(T, rev=False):
    H, D, C, B = GDN_HEADS, GDN_D, GDN_CHUNK, GDN_GROUP
    nsteps = T // (B * C)
    at = (lambda n: nsteps - 1 - n) if rev else (lambda n: n)
    rows = lambda w: pl.BlockSpec((B * C, w), lambda n: (at(n), 0))
    sq = pl.BlockSpec((B, H, C, C), lambda n: (at(n), 0, 0, 0))
    st = pl.BlockSpec((B, H, D, D), lambda n: (at(n), 0, 0, 0))
    return rows, sq, st, nsteps


def _gate_cols(gate_ref, h, rows=slice(None)):
    return gate_ref[rows, h:h + 1], gate_ref[rows, GDN_HEADS + h:GDN_HEADS + h + 1]


def gdn_prep_chunks_fwd(qkv, gate, name, gather=None):
    T = qkv.shape[0]
    H, D, C, B = GDN_HEADS, GDN_D, GDN_CHUNK, GDN_GROUP
    W = H * D
    rows, sq, _, nsteps = _gdn_specs(T)
    n_g = len(gather) if gather else 0

    def body(*refs):
        x_ref, g_ref = refs[:2]
        w_ref, u_ref, qg_ref, kd_ref, in_ref = refs[2 + n_g:7 + n_g]
        if n_g:
            exch = _Gather(refs[2:2 + n_g], refs[7 + n_g:7 + 2 * n_g], *refs[7 + 2 * n_g:])
            step = pl.program_id(0)
            pl.when(step == 0)(exch.start)
            pl.when(step == nsteps // 2)(exch.forward)
        for h in range(H):
            q, k, v = [x_ref[:, j * W + h * D:j * W + (h + 1) * D].reshape(B, C, D) for j in range(3)]
            bcol, gcol = [c.reshape(B, C, 1) for c in _gate_cols(g_ref, h)]
            outs = _gdn_prep(q, k, v, gcol, bcol)
            for ref, val in zip((w_ref, u_ref, qg_ref, kd_ref), outs[:4]):
                ref[:, h * D:(h + 1) * D] = val.reshape(B * C, D).astype(ref.dtype)
            in_ref[:, h] = outs[4].astype(BF16)
        if n_g:
            pl.when(step == nsteps - 1)(exch.finish)

    gather = list(gather) if gather else []
    return pl.pallas_call(
        body, name=name, grid=(nsteps,),
        in_specs=[rows(3 * W), rows(LANES)] + [_ANY] * n_g,
        out_specs=[rows(W), rows(W), rows(W), rows(W), sq] + [_ANY] * n_g,
        out_shape=[_sds((T, W), BF16), _sds((T, W), F32), _sds((T, W), BF16), _sds((T, W), BF16),
                   _sds((T // C, H, C, C), BF16)] + [_sds((N_DEV,) + g.shape, g.dtype) for g in gather],
        scratch_shapes=_gather_scratch(n_g) if n_g else [],
        compiler_params=_cp(("arbitrary",) if n_g else ("parallel",)),
    )(qkv, gate, *gather)


def gdn_prep_chunks_bwd(qkv, gate, dw, du, dqg, dkd, dintra, name):
    T = qkv.shape[0]
    H, D, C, B = GDN_HEADS, GDN_D, GDN_CHUNK, GDN_GROUP
    W = H * D
    rows, sq, _, nsteps = _gdn_specs(T)

    def body(x_ref, g_ref, dw_ref, du_ref, dqg_ref, dkd_ref, din_ref, dx_ref, dg_ref):
        dg_ref[...] = jnp.zeros_like(dg_ref)
        for h in range(H):
            q, k, v = [x_ref[:, j * W + h * D:j * W + (h + 1) * D].reshape(B, C, D) for j in range(3)]
            bcol, gcol = [c.reshape(B, C, 1) for c in _gate_cols(g_ref, h)]
            _, vjp = jax.vjp(_gdn_prep, q, k, v, gcol, bcol)
            cots = tuple(r[:, h * D:(h + 1) * D].reshape(B, C, D) for r in (dw_ref, du_ref, dqg_ref, dkd_ref))
            dq, dk, dv, dgc, db = vjp(cots + (din_ref[:, h],))
            for j, val in enumerate((dq, dk, dv)):
                dx_ref[:, j * W + h * D:j * W + (h + 1) * D] = val.reshape(B * C, D)
            dg_ref[:, h:h + 1] = db.reshape(B * C, 1)
            dg_ref[:, H + h:H + h + 1] = dgc.reshape(B * C, 1)

    return pl.pallas_call(
        body, name=name, grid=(nsteps,),
        in_specs=[rows(3 * W), rows(LANES), rows(W), rows(W), rows(W), rows(W), sq],
        out_specs=[rows(3 * W), rows(LANES)],
        out_shape=[_sds((T, 3 * W), F32), _sds((T, LANES), F32)],
        compiler_params=_cp(("parallel",)),
    )(qkv, gate, dw, du, dqg, dkd, dintra)


def gdn_fwd(w, u, qg, kdec, intra, gate, name):
    T = w.shape[0]
    H, D, C, B = GDN_HEADS, GDN_D, GDN_CHUNK, GDN_GROUP
    W = H * D
    rows, sq, st, nsteps = _gdn_specs(T)

    def body(w_ref, u_ref, qg_ref, kd_ref, in_ref, g_ref, o_ref, s_ref, s_scr):
        @pl.when(pl.program_id(0) == 0)
        def _():
            s_scr[...] = jnp.zeros_like(s_scr)

        def chunk(cb, carry):
            rs = pl.ds(pl.multiple_of(cb * C, C), C)
            for h in range(H):
                cs = slice(h * D, (h + 1) * D)
                S = s_scr[h]
                s_ref[cb, h] = S
                S_new, o = _gdn_scan(S, w_ref[rs, cs], u_ref[rs, cs], qg_ref[rs, cs], kd_ref[rs, cs], in_ref[cb, h],
                                     _gate_cols(g_ref, h, rs)[1])
                s_scr[h] = S_new
                o_ref[rs, cs] = o
            return carry

        lax.fori_loop(0, B, chunk, 0)

    return pl.pallas_call(
        body, name=name, grid=(nsteps,),
        in_specs=[rows(W), rows(W), rows(W), rows(W), sq, rows(LANES)],
        out_specs=[rows(W), st],
        out_shape=[_sds((T, W), F32), _sds((T // C, H, D, D), F32)],
        scratch_shapes=[pltpu.VMEM((H, D, D), F32)],
        compiler_params=_cp(("arbitrary",)),
    )(w, u, qg, kdec, intra, gate)


def gdn_bwd(w, u, qg, kdec, intra, gate, states, do, name):
    T = w.shape[0]
    H, D, C, B = GDN_HEADS, GDN_D, GDN_CHUNK, GDN_GROUP
    W = H * D
    rows, sq, st, nsteps = _gdn_specs(T, rev=True)

    def body(w_ref, u_ref, qg_ref, kd_ref, in_ref, g_ref, s_ref, do_ref,
             dw_ref, du_ref, dqg_ref, dkd_ref, din_ref, dg_ref, ds_scr):
        @pl.when(pl.program_id(0) == 0)
        def _():
            ds_scr[...] = jnp.zeros_like(ds_scr)
        dg_ref[...] = jnp.zeros_like(dg_ref)

        def chunk(t, carry):
            cb = B - 1 - t
            rs = pl.ds(pl.multiple_of(cb * C, C), C)
            for h in range(H):
                cs = slice(h * D, (h + 1) * D)
                _, vjp = jax.vjp(_gdn_scan, s_ref[cb, h], w_ref[rs, cs].astype(F32), u_ref[rs, cs],
                                 qg_ref[rs, cs].astype(F32), kd_ref[rs, cs].astype(F32), in_ref[cb, h].astype(F32),
                                 _gate_cols(g_ref, h, rs)[1])
                dS, dw, du, dqg, dkd, din, dgc = vjp((ds_scr[h], do_ref[rs, cs]))
                ds_scr[h] = dS
                dw_ref[rs, cs] = dw
                du_ref[rs, cs] = du
                dqg_ref[rs, cs] = dqg
                dkd_ref[rs, cs] = dkd
                din_ref[cb, h] = din
                dg_ref[rs, H + h:H + h + 1] = dgc
            return carry

        lax.fori_loop(0, B, chunk, 0)

    return pl.pallas_call(
        body, name=name, grid=(nsteps,),
        in_specs=[rows(W), rows(W), rows(W), rows(W), sq, rows(LANES), st, rows(W)],
        out_specs=[rows(W), rows(W), rows(W), rows(W), sq, rows(LANES)],
        out_shape=[_sds((T, W), F32)] * 4 + [_sds((T // C, H, C, C), F32), _sds((T, LANES), F32)],
        scratch_shapes=[pltpu.VMEM((H, D, D), F32)],
        compiler_params=_cp(("arbitrary",)),
    )(w, u, qg, kdec, intra, gate, states, do)


def _gated_norm(o, z, gain):
    outs = []
    for h in range(GDN_HEADS):
        oh = o[:, h * GDN_D:(h + 1) * GDN_D]
        y = oh * lax.rsqrt(jnp.mean(oh * oh, axis=-1, keepdims=True) + RMS_EPS) * gain
        outs.append(y * _silu(z[:, h * GDN_D:(h + 1) * GDN_D]))
    return jnp.concatenate(outs, axis=1)


def _window_sums(ext, shift):
    outs = []
    s = ext
    step = 1
    for gi, win in enumerate(POOL_WINDOWS):
        while step < win:
            s = s + shift(s, step)
            step *= 2
        outs.append(s[:, gi * GDN_D:(gi + 1) * GDN_D])
    return jnp.concatenate(outs, axis=1)


def _pool_counts(t0, rows, width):
    t1 = (t0 + 1 + lax.broadcasted_iota(jnp.int32, (rows, width), 0)).astype(F32)
    lane = lax.broadcasted_iota(jnp.int32, (rows, width), 1)
    win = jnp.full((rows, width), float(POOL_WINDOWS[-1]), F32)
    for gi in reversed(range(len(POOL_WINDOWS) - 1)):
        win = jnp.where(lane < (gi + 1) * GDN_D, float(POOL_WINDOWS[gi]), win)
    return jnp.minimum(t1, win)


def even_post_fwd(o, proj, gdn_norm, pool_w, pool_scale, name):
    T = o.shape[0]
    W = GDN_HEADS * GDN_D
    tm = _row_tile(T)
    per = tm // HALO

    def body(o_ref, z_ref, p_ref, pp_ref, gn_ref, pw_ref, ps_ref, out_ref):
        i = pl.program_id(0)
        out_ref[:, :W] = _gated_norm(o_ref[...], z_ref[...].astype(F32), gn_ref[...]).astype(BF16)
        pp = jnp.where(i > 0, pp_ref[...].astype(F32), 0.0)
        ext = jnp.concatenate([pp, p_ref[...].astype(F32)], axis=0)
        pooled = (_window_sums(ext, _shift_down)[HALO:] / _pool_counts(i * tm, tm, W)) - ext[HALO:]
        for gi in range(len(POOL_WINDOWS)):
            sl = slice(gi * GDN_D, (gi + 1) * GDN_D)
            y = _dot(pooled[:, sl], pw_ref[gi]) * ps_ref[:, sl]
            out_ref[:, W + gi * GDN_D:W + (gi + 1) * GDN_D] = y.astype(BF16)

    return pl.pallas_call(
        body, name=name, grid=(T // tm,),
        in_specs=[pl.BlockSpec((tm, W), lambda i: (i, 0)), pl.BlockSpec((tm, W), lambda i: (i, 3)),
                  pl.BlockSpec((tm, W), lambda i: (i, 4)),
                  pl.BlockSpec((HALO, W), lambda i: (jnp.maximum(i * per - 1, 0), 4)),
                  pl.BlockSpec((1, GDN_D), lambda i: (0, 0)),
                  pl.BlockSpec((len(POOL_WINDOWS), GDN_D, GDN_D), lambda i: (0, 0, 0)),
                  pl.BlockSpec((1, W), lambda i: (0, 0))],
        out_specs=pl.BlockSpec((tm, 2 * W), lambda i: (i, 0)),
        out_shape=_sds((T, 2 * W), BF16),
        compiler_params=_cp(("arbitrary",)),
    )(o, proj, proj, proj, gdn_norm, pool_w, pool_scale)


def even_post_bwd(o, proj, gdn_norm, pool_w, pool_scale, dcat, name):
    T = o.shape[0]
    W = GDN_HEADS * GDN_D
    G = len(POOL_WINDOWS)
    tm = _row_tile(T)
    per = tm // HALO
    nrow = T // tm
    last = T // HALO - 1

    def body(o_ref, z_ref, p_ref, pp_ref, gn_ref, pw_ref, ps_ref, d_ref, dn_ref,
             do_ref, dzp_ref, dgn_ref, dpw_ref, dps_ref):
        i = pl.program_id(0)
        first = i == 0
        _, vjp = jax.vjp(_gated_norm, o_ref[...], z_ref[...].astype(F32), gn_ref[...])
        do, dz, dgn = vjp(d_ref[:, :W].astype(F32))
        do_ref[...] = do
        dzp_ref[:, :W] = dz.astype(BF16)
        _accum(dgn_ref, dgn, first)
        pp = jnp.where(first, 0.0, pp_ref[...].astype(F32))
        ext = jnp.concatenate([pp, p_ref[...].astype(F32)], axis=0)
        pooled = (_window_sums(ext, _shift_down)[HALO:] / _pool_counts(i * tm, tm, W)) - ext[HALO:]
        dy_ext = jnp.concatenate([d_ref[:, W:].astype(F32), jnp.where(i < nrow - 1, dn_ref[...].astype(F32), 0.0)], axis=0)
        dys_ext = dy_ext * ps_ref[...]
        dpooled, dscale = [], []
        for gi in range(G):
            sl = slice(gi * GDN_D, (gi + 1) * GDN_D)
            dpooled.append(_dot_nt(dys_ext[:, sl], pw_ref[gi]))
            y = _dot(pooled[:, sl], pw_ref[gi])
            dscale.append(jnp.sum(dy_ext[:tm, sl] * y, axis=0, keepdims=True))
            _accum(dpw_ref.at[gi], _dot_tn(pooled[:, sl], dys_ext[:tm, sl]), first)
        dpooled = jnp.concatenate(dpooled, axis=1)
        _accum(dps_ref, jnp.concatenate(dscale, axis=1), first)
        dmean = dpooled / _pool_counts(i * tm, tm + HALO, W)
        dp = _window_sums(dmean, _shift_up)[:tm] - dpooled[:tm]
        dzp_ref[:, W:] = dp.astype(BF16)

    return pl.pallas_call(
        body, name=name, grid=(nrow,),
        in_specs=[pl.BlockSpec((tm, W), lambda i: (i, 0)), pl.BlockSpec((tm, W), lambda i: (i, 3)),
                  pl.BlockSpec((tm, W), lambda i: (i, 4)),
                  pl.BlockSpec((HALO, W), lambda i: (jnp.maximum(i * per - 1, 0), 4)),
                  pl.BlockSpec((1, GDN_D), lambda i: (0, 0)),
                  pl.BlockSpec((G, GDN_D, GDN_D), lambda i: (0, 0, 0)),
                  pl.BlockSpec((1, W), lambda i: (0, 0)),
                  pl.BlockSpec((tm, 2 * W), lambda i: (i, 0)),
                  pl.BlockSpec((HALO, W), lambda i: (jnp.minimum((i + 1) * per, last), 1))],
        out_specs=[pl.BlockSpec((tm, W), lambda i: (i, 0)), pl.BlockSpec((tm, 2 * W), lambda i: (i, 0)),
                   pl.BlockSpec((1, GDN_D), lambda i: (0, 0)), pl.BlockSpec((G, GDN_D, GDN_D), lambda i: (0, 0, 0)),
                   pl.BlockSpec((1, W), lambda i: (0, 0))],
        out_shape=[_sds((T, W), F32), _sds((T, 2 * W), BF16), _sds((1, GDN_D), F32), _sds((G, GDN_D, GDN_D), F32),
                   _sds((1, W), F32)],
        compiler_params=_cp(("arbitrary",)),
    )(o, proj, proj, proj, gdn_norm, pool_w, pool_scale, dcat, dcat)


def _head_norm(x, gain):
    return x * lax.rsqrt(jnp.mean(x * x, axis=-1, keepdims=True) + RMS_EPS) * gain


def _att_scores(q, k, slope, has_prev):
    B = ATT_BLOCK
    a = lax.broadcasted_iota(jnp.int32, (B, 2 * B), 0)
    j = lax.broadcasted_iota(jnp.int32, (B, 2 * B), 1)
    rel = B + a - j
    mask = (rel >= 0) & (rel <= B) & ((j >= B) | has_prev)
    s = _dot_nt(q, k) - slope * rel.astype(F32)
    return jnp.where(mask, s, NEG), mask


def _alibi_slope(h, dil):
    return dil * (2.0 ** (-8.0 * (h + 1) / ATT_HEADS))


def att_fwd(qk, qkv, name):
    dil, L, _ = qk.shape
    Wd = ATT_HEADS * ATT_DH
    nb = L // ATT_BLOCK
    B = ATT_BLOCK

    def body(q_ref, kc_ref, kp_ref, vc_ref, vp_ref, o_ref, l_ref):
        has_prev = pl.program_id(1) > 0
        for h in range(ATT_HEADS):
            sl = slice(h * ATT_DH, (h + 1) * ATT_DH)
            k = jnp.concatenate([kp_ref[:, sl], kc_ref[:, sl]], axis=0)
            v = jnp.concatenate([vp_ref[:, sl], vc_ref[:, sl]], axis=0)
            s, _ = _att_scores(q_ref[:, sl], k, _alibi_slope(h, dil), has_prev)
            m = jnp.max(s, axis=-1, keepdims=True)
            p = jnp.exp(s - m)
            l = jnp.sum(p, axis=-1, keepdims=True)
            o_ref[:, sl] = (_dot(p, v) / l).astype(BF16)
            l_ref[:, sl] = jnp.broadcast_to(m + jnp.log(l), (B, ATT_DH))

    cur = lambda t: pl.BlockSpec((None, B, Wd), lambda r, n: (r, n, t))
    prev = lambda t: pl.BlockSpec((None, B, Wd), lambda r, n: (r, jnp.maximum(n - 1, 0), t))
    out = pl.BlockSpec((None, B, Wd), lambda r, n: (r, n, 0))
    return pl.pallas_call(
        body, name=name, grid=(dil, nb),
        in_specs=[cur(0), cur(1), prev(1), cur(2), prev(2)],
        out_specs=[out, out], out_shape=[_sds((dil, L, Wd), BF16), _sds((dil, L, Wd), F32)],
        compiler_params=_cp(("parallel", "arbitrary")),
    )(qk, qk, qk, qkv, qkv)


def att_bwd(qk, qkv, lse, do, dd, name, carry=None):
    dil, L, _ = qk.shape
    Wd = ATT_HEADS * ATT_DH
    nb = L // ATT_BLOCK
    B = ATT_BLOCK
    carry = list(carry) if carry else []
    n_c = len(carry)

    def body(*refs):
        q_ref, kc_ref, kp_ref, vc_ref, vp_ref, l_ref, do_ref, dd_ref = refs[:8]
        dq_ref, dk_ref, dv_ref = refs[8 + n_c:11 + n_c]
        ck_scr, cv_scr = refs[11 + 2 * n_c:13 + 2 * n_c]
        r, n = pl.program_id(0), pl.program_id(1)
        has_prev = n > 0
        if n_c:
            ex = _ChipExchange(refs[8:8 + n_c], refs[11 + n_c:11 + 2 * n_c], *refs[13 + 2 * n_c:])
            pl.when((r == 0) & (n == 0))(ex.start)

        @pl.when(n < nb)
        def _():
            for h in range(ATT_HEADS):
                sl = slice(h * ATT_DH, (h + 1) * ATT_DH)
                q = q_ref[:, sl]
                k = jnp.concatenate([kp_ref[:, sl], kc_ref[:, sl]], axis=0)
                v = jnp.concatenate([vp_ref[:, sl], vc_ref[:, sl]], axis=0)
                do = do_ref[:, sl]
                s, mask = _att_scores(q, k, _alibi_slope(h, dil), has_prev)
                p = jnp.where(mask, jnp.exp(s - l_ref[:, h * ATT_DH:h * ATT_DH + 1]), 0.0)
                delta = jnp.sum(dd_ref[:, sl].astype(F32), axis=-1, keepdims=True)
                ds = p * (_dot_nt(do, v) - delta)
                dq_ref[:, sl] = _dot(ds, k).astype(BF16)
                dk = _dot_tn(ds, q)
                dv = _dot_tn(p, do)
                dk_ref[:, sl] = (jnp.where(has_prev, ck_scr[:, sl] + dk[:B], 0.0)).astype(BF16)
                dv_ref[:, sl] = (jnp.where(has_prev, cv_scr[:, sl] + dv[:B], 0.0)).astype(BF16)
                ck_scr[:, sl] = dk[B:]
                cv_scr[:, sl] = dv[B:]

        @pl.when(n == nb)
        def _():
            dk_ref[...] = ck_scr[...].astype(BF16)
            dv_ref[...] = cv_scr[...].astype(BF16)

        if n_c:
            pl.when((r == dil - 1) & (n == nb))(ex.finish)

    cur = lambda t: pl.BlockSpec((None, B, Wd), lambda r, n: (r, jnp.minimum(n, nb - 1), t))
    prev = lambda t: pl.BlockSpec((None, B, Wd), lambda r, n: (r, jnp.clip(n - 1, 0, nb - 1), t))
    kv_out = pl.BlockSpec((None, B, Wd), lambda r, n: (r, jnp.maximum(n - 1, 0), 0))
    return pl.pallas_call(
        body, name=name, grid=(dil, nb + 1),
        in_specs=[cur(0), cur(1), prev(1), cur(2), prev(2), cur(0), cur(0), cur(0)] + [_ANY] * n_c,
        out_specs=[cur(0), kv_out, kv_out] + [_ANY] * n_c,
        out_shape=[_sds((dil, L, Wd), BF16)] * 3 + [_sds(a.shape, a.dtype) for a in carry],
        scratch_shapes=[pltpu.VMEM((B, Wd), F32), pltpu.VMEM((B, Wd), F32)] + (_chip_exchange_scratch(n_c) if n_c else []),
        compiler_params=_cp(("arbitrary", "arbitrary") if n_c else ("parallel", "arbitrary")),
    )(qk, qk, qk, qkv, qkv, lse, do, dd, *carry)


def qk_norm_bwd(qkv, dq, dk, q_norm, k_norm, name):
    T = qkv.shape[0]
    Wd = dq.shape[1]
    tm = _row_tile(T)

    def body(q_ref, k_ref, dq_ref, dk_ref, qn_ref, kn_ref, oq_ref, ok_ref, dqn_ref, dkn_ref):
        first = pl.program_id(0) == 0
        for x_ref, d_ref, g_ref, o_ref, dg_ref, scale in ((q_ref, dq_ref, qn_ref, oq_ref, dqn_ref, ATT_DH ** -0.5),
                                                         (k_ref, dk_ref, kn_ref, ok_ref, dkn_ref, 1.0)):
            dg = jnp.zeros((1, ATT_DH), F32)
            for h in range(Wd // ATT_DH):
                sl = slice(h * ATT_DH, (h + 1) * ATT_DH)
                _, vjp = jax.vjp(lambda x, g: _head_norm(x, g) * scale, x_ref[:, sl].astype(F32), g_ref[...])
                dx, dg_h = vjp(d_ref[:, sl].astype(F32))
                o_ref[:, sl] = dx.astype(BF16)
                dg += dg_h
            _accum(dg_ref, dg, first)

    row = lambda t: pl.BlockSpec((tm, Wd), lambda i: (i, t))
    vec = pl.BlockSpec((1, ATT_DH), lambda i: (0, 0))
    return pl.pallas_call(
        body, name=name, grid=(T // tm,),
        in_specs=[row(0), row(1), row(0), row(0), vec, vec], out_specs=[row(0), row(0), vec, vec],
        out_shape=[_sds((T, Wd), BF16)] * 2 + [_sds((1, ATT_DH), F32)] * 2,
        compiler_params=_cp(("arbitrary",)),
    )(qkv, qkv, dq, dk, q_norm, k_norm)


def _merge_weights(l0, l1, l2):
    m = jnp.maximum(jnp.maximum(l0, l1), l2)
    e = [jnp.exp(l - m) for l in (l0, l1, l2)]
    tot = e[0] + e[1] + e[2]
    return [x / tot for x in e]


def _merge_specs(arrays, tm):
    return [pl.BlockSpec((a.shape[0], tm // a.shape[0], a.shape[2]), lambda i: (0, i, 0)) for a in arrays]


def _merge_load(refs, dils, tm, scr):
    vals = [ref[...].astype(F32).reshape(tm, ref.shape[-1]) for ref in refs]
    return [v if dl == 1 else _interleave(v, scr, dl) for v, dl in zip(vals, dils)]


def merge_fwd(outs, lses, name):
    Wd = outs[0].shape[2]
    T = outs[0].shape[0] * outs[0].shape[1]
    tm = _row_tile(T)
    dils = [a.shape[0] for a in outs] * 2

    def body(*refs):
        vals = _merge_load(refs[:6], dils, tm, refs[7])
        w = _merge_weights(*vals[3:])
        refs[6][...] = (w[0] * vals[0] + w[1] * vals[1] + w[2] * vals[2]).astype(BF16)

    return pl.pallas_call(body, name=name, grid=(T // tm,), in_specs=_merge_specs(list(outs) + list(lses), tm),
                          out_specs=pl.BlockSpec((tm, Wd), lambda i: (i, 0)), out_shape=_sds((T, Wd), BF16),
                          scratch_shapes=[pltpu.VMEM((Wd // LANES, tm, LANES), F32)],
                          compiler_params=_cp(("parallel",)))(*outs, *lses)


def merge_bwd(outs, lses, d, name):
    Wd = outs[0].shape[2]
    T = outs[0].shape[0] * outs[0].shape[1]
    tm = 256 if T % 256 == 0 else T
    dils = [a.shape[0] for a in outs] * 2

    def body(*refs):
        scr = refs[13]
        vals = _merge_load(refs[:6], dils, tm, scr)
        w = _merge_weights(*vals[3:])
        dv = refs[6][...].astype(F32)
        dvm = dv * (w[0] * vals[0] + w[1] * vals[1] + w[2] * vals[2])
        for g in range(3):
            for ref, val in ((refs[7 + g], w[g] * dv), (refs[10 + g], w[g] * dvm)):
                ref[...] = (val if dils[g] == 1 else _deinterleave(val, scr, dils[g])).reshape(ref.shape).astype(BF16)

    specs = _merge_specs(list(outs) + list(lses), tm)
    res = pl.pallas_call(body, name=name, grid=(T // tm,),
                         in_specs=specs + [pl.BlockSpec((tm, Wd), lambda i: (i, 0))], out_specs=specs,
                         out_shape=[_sds(a.shape, BF16) for a in list(outs) + list(outs)],
                         scratch_shapes=[pltpu.VMEM((Wd // LANES, tm, LANES), F32)],
                         compiler_params=_cp(("parallel",)))(*outs, *lses, d)
    return res[:3], res[3:]


def loss_head(y, target, name):
    T, D = y.shape
    tm = _row_tile(T)

    def body(y_ref, t_ref, l_ref, dy_ref):
        err = y_ref[...] - t_ref[...]
        dy_ref[...] = err * (1.0 / D)
        part = 0.5 * jnp.sum(jnp.sum(err * err, axis=1, keepdims=True) * (1.0 / D), axis=0, keepdims=True)
        _accum(l_ref, jnp.broadcast_to(part, (1, 128)), pl.program_id(0) == 0)

    row = pl.BlockSpec((tm, D), lambda i: (i, 0))
    return pl.pallas_call(body, name=name, grid=(T // tm,), in_specs=[row, row],
                          out_specs=[pl.BlockSpec((1, 128), lambda i: (0, 0)), row],
                          out_shape=[_sds((1, 128), F32), _sds((T, D), F32)],
                          compiler_params=_cp(("arbitrary",)))(y, target)


def ada_fwd(c_all, w, b, name):
    depth, D, n = w.shape

    def body(c_ref, w_ref, b_ref, o_ref):
        o_ref[...] = _dot(_silu(c_ref[...]), w_ref[...]) + b_ref[...]

    return pl.pallas_call(
        body, name=name, grid=(depth,),
        in_specs=[pl.BlockSpec((N_DEV, D), lambda i: (0, 0)), pl.BlockSpec((None, D, n), lambda i: (i, 0, 0)),
                  pl.BlockSpec((None, 1, n), lambda i: (i, 0, 0))],
        out_specs=pl.BlockSpec((None, N_DEV, n), lambda i: (i, 0, 0)),
        out_shape=_sds((depth, N_DEV, n), F32), compiler_params=_cp(("parallel",)),
    )(c_all, w, b)


def ada_bwd(c_all, dmod, name):
    depth, _, n = dmod.shape
    D = c_all.shape[1]

    def body(c_ref, d_ref, o_ref):
        o_ref[...] = _dot_tn(_silu(c_ref[...]), d_ref[...])

    return pl.pallas_call(
        body, name=name, grid=(depth,),
        in_specs=[pl.BlockSpec((N_DEV, D), lambda i: (0, 0)), pl.BlockSpec((None, N_DEV, n), lambda i: (i, 0, 0))],
        out_specs=pl.BlockSpec((None, D, n), lambda i: (i, 0, 0)),
        out_shape=_sds((depth, D, n), F32), compiler_params=_cp(("parallel",)),
    )(c_all, dmod)


def adamw(w, m, v, gparts, name):
    R, C = w.shape
    k = gparts.shape[0]
    tr = R
    for cand in (512, 256, 128, 64, 32, 16, 8):
        if R % cand == 0 and cand * C * 4 <= 2 * 1024 * 1024:
            tr = cand
            break
    bc1 = 1.0 - ADAM_B1 ** ADAM_STEP
    bc2 = 1.0 - ADAM_B2 ** ADAM_STEP

    def body(w_ref, m_ref, v_ref, gp_ref, g_ref, d_ref, nm_ref, nv_ref):
        g = gp_ref[0].astype(F32)
        for q in range(1, k):
            g = g + gp_ref[q].astype(F32)
        nm = ADAM_B1 * m_ref[...] + (1.0 - ADAM_B1) * g
        nv = ADAM_B2 * v_ref[...] + (1.0 - ADAM_B2) * (g * g)
        g_ref[...] = g
        nm_ref[...] = nm
        nv_ref[...] = nv
        d_ref[...] = -ADAM_LR * ((nm / bc1) / (jnp.sqrt(nv / bc2) + ADAM_EPS) + ADAM_WD * w_ref[...])

    row = pl.BlockSpec((tr, C), lambda i: (i, 0))
    return pl.pallas_call(
        body, name=name, grid=(R // tr,),
        in_specs=[row, row, row, pl.BlockSpec((k, tr, C), lambda i: (0, i, 0))],
        out_specs=[row] * 4, out_shape=[_sds((R, C), F32)] * 4, compiler_params=_cp(("parallel",)),
    )(w, m, v, gparts)


def _mesh_pos():
    return lax.axis_index("x"), lax.axis_index("y"), lax.axis_index("c")


def _other_chips(x, y):
    return [(1 - x, y), (x, 1 - y), (1 - x, 1 - y)]


_ANY = pl.BlockSpec(memory_space=pl.ANY)


class _Gather:
    def __init__(self, ins, outs, send_sems, recv_sems, local_sems):
        self.ins, self.outs, self.n = ins, outs, len(ins)
        self.send_sems, self.recv_sems, self.local_sems = send_sems, recv_sems, local_sems
        self.x, self.y, self.c = _mesh_pos()
        self.me, self.sibling = (self.x, self.y, self.c), (self.x, self.y, 1 - self.c)
        self.chips = _other_chips(self.x, self.y)

    def _copy(self, a, k, block, to, src=None):
        dst = self.outs[a].at[4 * block[0] + 2 * block[1] + block[2]]
        return pltpu.make_async_remote_copy(
            src_ref=dst if src is None else src, dst_ref=dst, send_sem=self.send_sems.at[a, k],
            recv_sem=self.recv_sems.at[a, k], device_id=to, device_id_type=MESH)

    def _mine(self):
        me_slot = 4 * self.x + 2 * self.y + self.c
        return [pltpu.make_async_copy(self.ins[a], self.outs[a].at[me_slot], self.local_sems.at[a]) for a in range(self.n)]

    def _first(self):
        out = []
        for a in range(self.n):
            out.append(self._copy(a, 0, self.me, self.sibling, src=self.ins[a]))
            out += [self._copy(a, 1 + j, self.me, (*chip, self.c), src=self.ins[a]) for j, chip in enumerate(self.chips)]
        return out

    def _passed(self):
        return [self._copy(a, 4 + j, (*chip, self.c), self.sibling) for j, chip in enumerate(self.chips) for a in range(self.n)]

    def start(self):
        for cp in self._mine() + self._first():
            cp.start()

    def forward(self):
        for j, chip in enumerate(self.chips):
            for a in range(self.n):
                self._copy(a, 1 + j, (*chip, self.c), self.me).wait_recv()
                self._copy(a, 4 + j, (*chip, self.c), self.sibling).start()

    def finish(self):
        for a in range(self.n):
            self._copy(a, 0, self.sibling, self.me).wait_recv()
            for j, chip in enumerate(self.chips):
                self._copy(a, 4 + j, (*chip, 1 - self.c), self.me).wait_recv()
        for cp in self._first() + self._passed():
            cp.wait_send()
        for cp in self._mine():
            cp.wait()


def _gather_scratch(n):
    return [pltpu.SemaphoreType.DMA((n, 7)), pltpu.SemaphoreType.DMA((n, 7)), pltpu.SemaphoreType.DMA((n,))]


def all_gather(shards, name):
    n = len(shards)

    def body(*refs):
        g = _Gather(refs[:n], refs[n:2 * n], *refs[2 * n:])
        g.start()
        g.forward()
        g.finish()

    return pl.pallas_call(
        body, name=name, in_specs=[_ANY] * n, out_specs=[_ANY] * n,
        out_shape=[_sds((N_DEV,) + s.shape, s.dtype) for s in shards],
        scratch_shapes=_gather_scratch(n),
        compiler_params=pltpu.CompilerParams(has_side_effects=True),
    )(*shards)


def exchange_pair(stacked, name):
    n = len(stacked)

    def body(*refs):
        ins, outs = refs[:n], refs[n:2 * n]
        send_sems, recv_sems = refs[2 * n:]
        x, y, c = _mesh_pos()
        copies = [pltpu.make_async_remote_copy(
            src_ref=ins[a].at[2 * q + (1 - c)], dst_ref=outs[a].at[q], send_sem=send_sems.at[a, q],
            recv_sem=recv_sems.at[a, q], device_id=(x, y, 1 - c), device_id_type=MESH)
            for a in range(n) for q in range(4)]
        for cp in copies:
            cp.start()
        for cp in copies:
            cp.wait()

    return pl.pallas_call(
        body, name=name, in_specs=[_ANY] * n, out_specs=[_ANY] * n,
        out_shape=[_sds((4,) + s.shape[1:], s.dtype) for s in stacked],
        scratch_shapes=[pltpu.SemaphoreType.DMA((n, 4)), pltpu.SemaphoreType.DMA((n, 4))],
        compiler_params=pltpu.CompilerParams(has_side_effects=True),
    )(*stacked)


def pair_add(stacked, got, c_idx, name):
    _, R, C = stacked.shape
    tr = R
    for cand in (512, 256, 128, 64, 32, 16):
        if R % cand == 0 and cand * C * 4 <= 2 * 1024 * 1024:
            tr = cand
            break

    def body(c_ref, s_ref, g_ref, o_ref):
        o_ref[...] = (s_ref[...].astype(F32) + g_ref[...].astype(F32)).astype(BF16)

    return pl.pallas_call(
        body, name=name,
        grid_spec=pltpu.PrefetchScalarGridSpec(
            num_scalar_prefetch=1, grid=(4, R // tr),
            in_specs=[pl.BlockSpec((None, tr, C), lambda q, i, c_ref: (2 * q + c_ref[0], i, 0)),
                      pl.BlockSpec((None, tr, C), lambda q, i, c_ref: (q, i, 0))],
            out_specs=pl.BlockSpec((None, tr, C), lambda q, i, c_ref: (q, i, 0))),
        out_shape=_sds((4, R, C), BF16),
        compiler_params=_cp(("parallel", "parallel")),
    )(c_idx, stacked, got)


def exchange_chips(parts, name):
    n = len(parts)

    def body(*refs):
        ex = _ChipExchange(refs[:n], refs[n:2 * n], *refs[2 * n:])
        ex.start()
        ex.finish()

    return pl.pallas_call(
        body, name=name, in_specs=[_ANY] * n, out_specs=[_ANY] * n,
        out_shape=[_sds(s.shape, s.dtype) for s in parts],
        scratch_shapes=_chip_exchange_scratch(n),
        compiler_params=pltpu.CompilerParams(has_side_effects=True),
    )(*parts)


class _ChipExchange:
    def __init__(self, ins, outs, send_sems, recv_sems, local_sems):
        self.ins, self.outs, self.n = ins, outs, len(ins)
        self.send_sems, self.recv_sems, self.local_sems = send_sems, recv_sems, local_sems
        self.x, self.y, self.c = _mesh_pos()

    def _copies(self):
        myq = 2 * self.x + self.y
        mine = [pltpu.make_async_copy(self.ins[a].at[myq], self.outs[a].at[myq], self.local_sems.at[a])
                for a in range(self.n)]
        remote = [pltpu.make_async_remote_copy(
            src_ref=self.ins[a].at[2 * chip[0] + chip[1]], dst_ref=self.outs[a].at[myq], send_sem=self.send_sems.at[a, j],
            recv_sem=self.recv_sems.at[a, j], device_id=(*chip, self.c), device_id_type=MESH)
            for a in range(self.n) for j, chip in enumerate(_other_chips(self.x, self.y))]
        return mine, remote

    def start(self):
        mine, remote = self._copies()
        for cp in mine + remote:
            cp.start()

    def finish(self):
        mine, remote = self._copies()
        for cp in remote + mine:
            cp.wait()


def _chip_exchange_scratch(n):
    return [pltpu.SemaphoreType.DMA((n, 3)), pltpu.SemaphoreType.DMA((n, 3)), pltpu.SemaphoreType.DMA((n,))]


def _pad_lanes(v, start, width=128):
    return jnp.pad(v.astype(F32), (start, width - start - v.shape[0]))[None]


def _carry(results, n_own):
    results = list(results)
    return results[:n_own] + [results[n_own:]]


def _layer_fwd(i, x, mod, P, next_shards=None):
    T, D = x.shape
    next_shards = next_shards or {}
    next_ffn = {k: v for k, v in next_shards.items() if k.startswith("ffn")}
    next_mix = {k: v for k, v in next_shards.items() if not k.startswith("ffn")}
    sh_m, sc_m, g_m, sh_f, sc_f, g_f = [mod[k:k + 1] for k in range(6)]
    sv = {"x0": x}
    tag = f"l{i}"
    if i % 2 == 0:
        e = i // 2
        h, proj = norm_proj(x, P["norm_mix"][i:i + 1], sc_m, sh_m, P["ev_main"][e], P["ev_main"][e].shape[1], tag + "_in")
        ba = matmul_nn(h, P["ev_ba"][e], F32, tag + "_ba")
        al = _pad_lanes(P["gdn_a_log"][e], GDN_HEADS)
        dt = _pad_lanes(P["gdn_dt_bias"][e], GDN_HEADS)
        gate = gate_fwd(ba, al, dt, tag + "_gate")
        qkv = gdn_prep_fwd(proj, P["gdn_conv_w"][e], tag + "_prep")
        *chunks, got_mix = _carry(gdn_prep_chunks_fwd(qkv, gate, tag + "_chunks", gather=list(next_mix.values())), 5)
        o, states = gdn_fwd(*chunks, gate, tag + "_gdn")
        cat = even_post_fwd(o, proj, P["gdn_norm"][e:e + 1], P["pool_w"][e], P["pool_scale"][e:e + 1], tag + "_post")
        x1, y_m = proj_res(cat, P["ev_w_out"][e], x, g_m, tag + "_out")
        sv.update(h=h, proj=proj, ba=ba, al=al, dt=dt, gate=gate, qkv=qkv, chunks=chunks, o=o, states=states, a_m=cat)
    else:
        od = i // 2
        qn, kn = P["att_q_norm"][od:od + 1], P["att_k_norm"][od:od + 1]
        h, proj, qk = [], [], []
        got_mix = []
        for gi, (_, dil) in enumerate(DIL_PATTERNS):
            hg, pg, qkg, got = _carry(norm_proj(x, P["norm_mix"][i:i + 1], sc_m, sh_m, P["od_w_in"][od], 3 * D,
                                                f"{tag}_in{gi}", cols=(gi, 1), dil=dil, qk_norms=(qn, kn),
                                                gather=list(next_mix.values()) if gi == 0 else None), 3)
            got_mix += got
            h.append(hg.reshape(T, D))
            proj.append(pg)
            qk.append(qkg)
        res = [att_fwd(qk[gi], proj[gi], f"{tag}_att{gi}") for gi in range(len(DIL_PATTERNS))]
        outs, lses = [r[0] for r in res], [r[1] for r in res]
        merged = merge_fwd(outs, lses, tag + "_merge")
        x1, y_m = proj_res(merged, P["od_w_out"][od], x, g_m, tag + "_out")
        sv.update(h=h, proj=proj, qk=qk, outs=outs, lses=lses, a_m=merged)
    hf, up, got_ffn = _carry(norm_proj(x1, P["norm_ffn"][i:i + 1], sc_f, sh_f, P["ffn_w_up"][i], 1408, tag + "_up",
                                       gather=list(next_ffn.values())), 2)
    a = ffn_mid_fwd(up, P["ffn_conv_w"][i], P["ffn_conv_b"][i:i + 1], tag + "_mid")
    x2, y_f = proj_res(a, P["ffn_w_down"][i], x1, g_f, tag + "_down")
    sv.update(y_m=y_m, x1=x1, hf=hf, up=up, a_f=a, y_f=y_f)
    got = dict(zip(list(next_mix.keys()) + list(next_ffn.keys()), list(got_mix) + list(got_ffn)))
    return x2, sv, (got if next_shards else None)


def add_layer_weights(P, gathered):
    W = GDN_HEADS * GDN_D
    for k, g in gathered.items():
        if k in _COL_SHARDED:
            full = g.transpose(1, 0, 2).reshape(g.shape[1], N_DEV * g.shape[2])
        else:
            full = g.reshape(N_DEV * g.shape[1], g.shape[2])
        if k == "ev_w_in":
            P.setdefault("ev_main", []).append(jnp.concatenate([full[:, :4 * W], full[:, 4 * W + 2 * GDN_HEADS:]], axis=1))
            P.setdefault("ev_ba", []).append(jnp.pad(full[:, 4 * W:4 * W + 2 * GDN_HEADS], ((0, 0), (0, 128 - 2 * GDN_HEADS))))
        else:
            P.setdefault(k, []).append(full)


def _put(bufs, k, li, a, b, tk, tn, name, col0=0):
    half = bufs[k][0].shape[0]
    g, loc = divmod(li, half)
    bufs[k][g] = matmul_tn(a, b, tk, tn, name, into=bufs[k][g], layer=loc, col0=col0)


CARRY_SPLIT = ((1, 3, 4, 0), (2,), (5,))


def _layer_bwd(i, dx2, sv, mod, P, bufs, carry=None):
    T = dx2.shape[0]
    sh_m, sc_m, g_m, sh_f, sc_f, g_f = [mod[k:k + 1] for k in range(6)]
    tag = f"b{i}"
    G = {}
    F = P["ffn_w_down"][i].shape[0]
    dy, da, dg_f = bwd_out(dx2, g_f, sv["y_f"], P["ffn_w_down"][i], F, tag + "_down")
    _put(bufs, "ffn_w_down", i, sv["a_f"], dy, F // 2, dy.shape[1], tag + "_wdown")
    dgate, dval, dcw, dcb = ffn_mid_bwd(sv["up"], da, P["ffn_conv_w"][i], P["ffn_conv_b"][i:i + 1], tag + "_mid")
    G["ffn_conv_w"], G["ffn_conv_b"] = dcw[:FFN_CONV], dcb[0]
    dx1, dnf, dsc_f, dsh_f = bwd_in([dgate, dval], P["ffn_w_up"][i], [0, 1], F, None,
                                    sv["x1"], P["norm_ffn"][i:i + 1], sc_f, sh_f, dx2, tag + "_up")
    G["norm_ffn"] = dnf[0]
    x0, h, proj = sv["x0"], sv["h"], sv["proj"]
    D = x0.shape[1]
    for k, d in enumerate((dgate, dval)):
        _put(bufs, "ffn_w_up", i, sv["hf"], d, D, F // 2, f"{tag}_wup{k}", col0=2 * k)
    gain = P["norm_mix"][i:i + 1]
    if i % 2 == 0:
        e = i // 2
        W = GDN_HEADS * GDN_D
        dy, dcat, dg_m = bwd_out(dx1, g_m, sv["y_m"], P["ev_w_out"][e], 1024, tag + "_out")
        _put(bufs, "ev_w_out", e, sv["a_m"], dy, 1024, D, tag + "_wout")
        do, dzp, dgn, dpw, dps = even_post_bwd(sv["o"], proj, P["gdn_norm"][e:e + 1], P["pool_w"][e],
                                               P["pool_scale"][e:e + 1], dcat, tag + "_post")
        G["gdn_norm"], G["pool_w"], G["pool_scale"] = dgn[0], dpw, dps[0]
        *dchunks, dgate_scan = gdn_bwd(*sv["chunks"], sv["gate"], sv["states"], do, tag + "_gdn")
        dqkv, dgate_prep = gdn_prep_chunks_bwd(sv["qkv"], sv["gate"], *dchunks, tag + "_chunks")
        dba, dal, ddt = gate_bwd(sv["ba"], sv["al"], sv["dt"], dgate_prep, dgate_scan, tag + "_gate")
        G["gdn_a_log"], G["gdn_dt_bias"] = dal[0, GDN_HEADS:2 * GDN_HEADS], ddt[0, GDN_HEADS:2 * GDN_HEADS]
        dqkv_raw, dconv = gdn_prep_bwd(proj, dqkv, P["gdn_conv_w"][e], tag + "_prep")
        G["gdn_conv_w"] = dconv[:GDN_CONV]
        w_main = P["ev_main"][e]
        dx0, dnm, dsc_m, dsh_m = bwd_in([dba, dqkv_raw, dzp], [P["ev_ba"][e], w_main[:, :3 * W], w_main[:, 3 * W:]],
                                        [0, 0, 0], None, None, x0, gain, sc_m, sh_m, dx1, tag + "_in")
        gw_qkv = matmul_tn(h, dqkv_raw, 1024, W, tag + "_win1")
        gw_zp = matmul_tn(h, dzp, 1024, W, tag + "_win2")
        gw_ba = matmul_tn(h, dba, 1024, 128, tag + "_win0")
        G["ev_w_in"] = jnp.concatenate([gw_qkv, gw_zp[:, :W], gw_ba[:, :2 * GDN_HEADS], gw_zp[:, W:]], axis=1)
    else:
        od = i // 2
        qn, kn = P["att_q_norm"][od:od + 1], P["att_k_norm"][od:od + 1]
        dy, dmerged, dg_m = bwd_out(dx1, g_m, sv["y_m"], P["od_w_out"][od], 1024, tag + "_out")
        _put(bufs, "od_w_out", od, sv["a_m"], dy, 1024, D, tag + "_wout")
        douts, dds = merge_bwd(sv["outs"], sv["lses"], dmerged, tag + "_merge")
        dh = None
        exchanged = {}
        dqn_sum, dkn_sum = 0.0, 0.0
        ng = len(DIL_PATTERNS)
        for gi, (_, dil) in enumerate(DIL_PATTERNS):
            riding = [carry[j] for j in CARRY_SPLIT[gi]] if (carry is not None and gi < len(CARRY_SPLIT)) else None
            dqs, dks, dv, *got = att_bwd(sv["qk"][gi], proj[gi], sv["lses"][gi], douts[gi], dds[gi], f"{tag}_att{gi}",
                                         carry=riding)
            if riding:
                exchanged.update(zip(CARRY_SPLIT[gi], got))
            dq, dk, dqn, dkn = qk_norm_bwd(proj[gi].reshape(T, 3 * D), dqs.reshape(T, D), dks.reshape(T, D), qn, kn,
                                           f"{tag}_qkn{gi}")
            dqn_sum, dkn_sum = dqn_sum + dqn[0], dkn_sum + dkn[0]
            for k, d in enumerate((dq, dk, dv)):
                _put(bufs, "od_w_in", od, h[gi], d.reshape(T, D), 1024, D, f"{tag}_win{gi}{k}", col0=3 * gi + k)
            cols = [3 * gi, 3 * gi + 1, 3 * gi + 2]
            ops = [d.reshape(dil, T // dil, D) if dil > 1 else d.reshape(T, D) for d in (dq, dk, dv)]
            if gi < ng - 1:
                dh = bwd_in(ops, P["od_w_in"][od], cols, D, dh, None, None, None, None, None, f"{tag}_in{gi}", dil=dil)
            else:
                dx0, dnm, dsc_m, dsh_m = bwd_in(ops, P["od_w_in"][od], cols, D, dh, x0, gain, sc_m, sh_m, dx1,
                                                f"{tag}_in{gi}", dil=dil)
        G["att_q_norm"], G["att_k_norm"] = dqn_sum, dkn_sum
    G["norm_mix"] = dnm[0]
    dmod = jnp.concatenate([dsh_m, dsc_m, dg_m, dsh_f, dsc_f, dg_f], axis=0)
    return dx0, dmod, G, ([exchanged[j] for j in range(len(carry))] if carry is not None else None)


_PER_LAYER = ("norm_mix", "norm_ffn", "ffn_conv_w", "ffn_conv_b")
_PER_EVEN = ("ev_w_in", "gdn_conv_w", "gdn_a_log", "gdn_dt_bias", "gdn_norm", "pool_w", "pool_scale")
_PER_ODD = ("att_q_norm", "att_k_norm")
_IN_PLACE = ("ffn_w_up", "ffn_w_down", "ev_w_out", "od_w_in", "od_w_out")


def device_step(x, mod, target, P, layer_shards=None, early_exchange=None):
    saved = []
    for i in range(DEPTH):
        nxt = layer_shards[i + 1] if (layer_shards is not None and i + 1 < DEPTH) else None
        x, sv, got = _layer_fwd(i, x, mod[i], P, nxt)
        if got is not None:
            add_layer_weights(P, got)
        saved.append(sv)
    loss, dx = loss_head(x, target, "loss")
    layer_grads, dmods = [None] * DEPTH, [None] * DEPTH
    bufs = {k: [lax.empty((len(P[k]) // 2,) + tuple(P[k][0].shape), BF16) for _ in range(2)] for k in _IN_PLACE}
    early_parts = None
    for i in reversed(range(DEPTH)):
        carry = None
        if i == DEPTH // 2 - 1 and early_exchange is not None:
            done = {k: bufs[k][1] for k in _IN_PLACE}
            done["ev_w_in"] = layer_grads[DEPTH // 2]["ev_w_in"][None]
            carry = early_exchange(done)
        dx, dmods[i], layer_grads[i], got = _layer_bwd(i, dx, saved[i], mod[i], P, bufs, carry)
        early_parts = got if got is not None else early_parts
    G = dict(bufs)
    G["ev_w_in"] = [layer_grads[0]["ev_w_in"][None], layer_grads[DEPTH // 2]["ev_w_in"][None]]
    G.update({k: jnp.stack([layer_grads[i][k] for i in range(DEPTH)]) for k in _PER_LAYER})
    G.update({k: jnp.stack([layer_grads[i][k] for i in range(0, DEPTH, 2)]) for k in _PER_EVEN if k != "ev_w_in"})
    G.update({k: jnp.stack([layer_grads[i][k] for i in range(1, DEPTH, 2)]) for k in _PER_ODD})
    return loss, dx, jnp.stack(dmods), G, early_parts


_WEIGHTS = ("ada_w", "ada_b", "norm_mix", "norm_ffn", "ev_w_in", "ev_w_out", "gdn_conv_w", "gdn_a_log", "gdn_dt_bias",
            "gdn_norm", "pool_w", "pool_scale", "od_w_in", "od_w_out", "att_q_norm", "att_k_norm", "ffn_w_up",
            "ffn_conv_w", "ffn_conv_b", "ffn_w_down")
_COL_SHARDED = ("ev_w_in", "od_w_in", "ffn_w_up")
_ROW_SHARDED = ("ev_w_out", "od_w_out", "ffn_w_down")
_SMALL_SHARDED = ("gdn_conv_w", "ffn_conv_w")
_REPLICATED = ("ada_b", "norm_mix", "norm_ffn", "gdn_a_log", "gdn_dt_bias", "gdn_norm", "pool_w", "pool_scale",
               "att_q_norm", "att_k_norm", "ffn_conv_b")
PACK_LANES = 128
PACK_ROWS = 8


def _pack_rows(n):
    unit = PACK_LANES * PACK_ROWS
    return -(-n // unit) * PACK_ROWS


def _pack(arrays):
    parts = []
    for a in arrays:
        flat = a.reshape(-1).astype(F32)
        rows = _pack_rows(flat.shape[0])
        parts.append(jnp.pad(flat, (0, rows * PACK_LANES - flat.shape[0])).reshape(rows, PACK_LANES))
    return jnp.concatenate(parts, axis=0)


def _unpack(packed, shapes, lead=()):
    out, row = [], 0
    for s in shapes:
        n = math.prod(s)
        rows = _pack_rows(n)
        blk = packed[..., row:row + rows, :].reshape(lead + (rows * PACK_LANES,))
        out.append(blk[..., :n].reshape(lead + tuple(s)))
        row += rows
    return out


def _unshard_cols(g):
    _, L, R, n = g.shape
    return g.transpose(1, 2, 0, 3).reshape(L, R, N_DEV * n)


def _shard_cols(full):
    L, R, N = full.shape
    n = N // N_DEV
    return full.reshape(L * R, N_DEV, n).transpose(1, 0, 2)


def _shard_rows(full):
    L, R, C = full.shape
    r = R // N_DEV
    return full.reshape(L, N_DEV, r, C).transpose(1, 0, 2, 3).reshape(N_DEV, L * r, C)


def kernel(x, c, ada_w, ada_b, norm_mix, norm_ffn, ev_w_in, ev_w_out, gdn_conv_w, gdn_a_log, gdn_dt_bias, gdn_norm, pool_w, pool_scale, od_w_in, od_w_out, att_q_norm, att_k_norm, ffn_w_up, ffn_conv_w, ffn_conv_b, ffn_w_down, loss_target, m_ada_w, m_ada_b, m_norm_mix, m_norm_ffn, m_ev_w_in, m_ev_w_out, m_gdn_conv_w, m_gdn_a_log, m_gdn_dt_bias, m_gdn_norm, m_pool_w, m_pool_scale, m_od_w_in, m_od_w_out, m_att_q_norm, m_att_k_norm, m_ffn_w_up, m_ffn_conv_w, m_ffn_conv_b, m_ffn_w_down, v_ada_w, v_ada_b, v_norm_mix, v_norm_ffn, v_ev_w_in, v_ev_w_out, v_gdn_conv_w, v_gdn_a_log, v_gdn_dt_bias, v_gdn_norm, v_pool_w, v_pool_scale, v_od_w_in, v_od_w_out, v_att_q_norm, v_att_k_norm, v_ffn_w_up, v_ffn_conv_w, v_ffn_conv_b, v_ffn_w_down):
    args = locals()
    Wl = {k: args[k] for k in _WEIGHTS}
    Ml = {k: args["m_" + k] for k in _WEIGHTS}
    Vl = {k: args["v_" + k] for k in _WEIGHTS}
    mx, my, mc = _mesh_pos()
    dev = 4 * mx + 2 * my + mc
    T, D = x.shape[1], x.shape[2]
    x2d, tgt = x.reshape(T, D), loss_target.reshape(T, D)

    small_shapes = [c.shape] + [Wl[k].shape for k in _SMALL_SHARDED]
    big = list(_COL_SHARDED + _ROW_SHARDED)
    layer_shards = []
    for i in range(DEPTH):
        mixer = ("ev_w_in", "ev_w_out") if i % 2 == 0 else ("od_w_in", "od_w_out")
        shards = {k: Wl[k][i // 2].astype(BF16) for k in mixer}
        shards.update({k: Wl[k][i].astype(BF16) for k in ("ffn_w_up", "ffn_w_down")})
        layer_shards.append(shards)
    gathered = all_gather([_pack([c] + [Wl[k] for k in _SMALL_SHARDED])] + list(layer_shards[0].values()), "gather_w0")
    c_all, conv_g, conv_f = _unpack(gathered[0], small_shapes, lead=(N_DEV,))
    c_all = c_all.reshape(N_DEV, D)
    P = {k: Wl[k] for k in _REPLICATED}
    P["gdn_conv_w"], P["ffn_conv_w"] = _unshard_cols(conv_g), _unshard_cols(conv_f)
    add_layer_weights(P, dict(zip(layer_shards[0].keys(), gathered[1:])))

    n_ada = ada_w.shape[2]
    b_cols = lax.dynamic_slice_in_dim(ada_b, dev * n_ada, n_ada, axis=1)
    mod_cols = ada_fwd(c_all, ada_w, b_cols[:, None, :], "ada_fwd")
    mod_all, = all_gather([mod_cols], "gather_mod")
    mod = lax.dynamic_index_in_dim(mod_all, dev, axis=2, keepdims=False)
    mod = mod.transpose(1, 0, 2).reshape(DEPTH, 6, D)

    c_idx = jnp.reshape(mc, (1,)).astype(jnp.int32)

    def chip_partials(grads, tag):
        stacked = [_shard_cols(grads[k]) for k in _COL_SHARDED] + [_shard_rows(grads[k]) for k in _ROW_SHARDED]
        got = exchange_pair(stacked, "rs_pair" + tag)
        return [pair_add(s, g, c_idx, f"rs_add_{k}{tag}") for s, g, k in zip(stacked, got, big)]

    loss, dx, dmod, G, early_parts = device_step(x2d, mod, tgt, P, layer_shards, lambda g: chip_partials(g, "_b"))
    loss = lax.psum(loss[0, 0], ("x", "y", "c"))

    G["ada_b"] = dmod.reshape(DEPTH, 6 * D)
    small = list(_REPLICATED) + list(_SMALL_SHARDED)
    parts_all, = all_gather([_pack([G[k] for k in small])], "gather_small")
    zeros = {k: jnp.zeros_like(G[k]) for k in _SMALL_SHARDED}
    packs = [_pack([src[k] for k in _REPLICATED] + [zeros[k] for k in _SMALL_SHARDED]) for src in (Wl, Ml, Vl)]
    res = adamw(*packs, parts_all, "adamw_small")
    shapes = [G[k].shape for k in small]
    out_g, out_d, out_m, out_v = ({k: a for k, a in zip(small, _unpack(r, shapes))} for r in res)
    dmod_all = _unpack(parts_all, shapes, lead=(N_DEV,))[0].reshape(N_DEV, DEPTH, 6 * D)
    dm_cols = lax.dynamic_slice_in_dim(dmod_all, dev * n_ada, n_ada, axis=2).transpose(1, 0, 2)
    g_ada = ada_bwd(c_all, dm_cols, "ada_bwd")

    def flat2(a):
        return a.reshape(-1, a.shape[-1])

    loc = {k: lax.dynamic_slice_in_dim(out_g[k], dev * Wl[k].shape[-1], Wl[k].shape[-1], axis=out_g[k].ndim - 1)
           for k in _SMALL_SHARDED}
    res = adamw(*[_pack([src[k] for k in _SMALL_SHARDED]) for src in (Wl, Ml, Vl)],
                _pack([loc[k] for k in _SMALL_SHARDED])[None], "adamw_conv")
    for dst, r in zip((out_g, out_d, out_m, out_v), res):
        dst.update(zip(_SMALL_SHARDED, _unpack(r, [Wl[k].shape for k in _SMALL_SHARDED])))
    res = adamw(flat2(ada_w), flat2(m_ada_w), flat2(v_ada_w), flat2(g_ada)[None], "adamw_ada")
    for dst, r in zip((out_g, out_d, out_m, out_v), res):
        dst["ada_w"] = r.reshape(ada_w.shape)

    parts = exchange_chips(chip_partials({k: G[k][0] for k in big}, "_a"), "rs_chips")
    for k, p, q in zip(big, parts, early_parts):
        res = adamw(flat2(Wl[k]), flat2(Ml[k]), flat2(Vl[k]), jnp.concatenate([p, q], axis=1), "adamw_" + k)
        for dst, r in zip((out_g, out_d, out_m, out_v), res):
            dst[k] = r.reshape(Wl[k].shape)

    return (loss, dx.reshape(x.shape), *[out_g[k] for k in _WEIGHTS], *[out_d[k] for k in _WEIGHTS],
            *[out_m[k] for k in _WEIGHTS], *[out_v[k] for k in _WEIGHTS])
```
